```python
import jax, jax.numpy as jnp
from jax import lax
import numpy as np

D_MODEL = 1024
BATCH = 8
SEQ = 2048
DEPTH = 1
DEC_BATCH = 128
DEC_SEQ = 4
PAST_LEN = 16384
PAGE_SIZE = 128

N_META = 16
D_MIX = D_MODEL
D_TM = D_MIX // 2
TM_HEAD = 64
TM_HEADS = D_TM // TM_HEAD
DECAY_RANK = 64
AAA_RANK = 64
GATE_RANK = 128
D_TM_PROJ = 3 * D_TM + DECAY_RANK + AAA_RANK + GATE_RANK
D_LRU = D_MIX - D_TM
LRU_BLOCKS = 8
LRU_BLOCK = D_LRU // LRU_BLOCKS
LRU_CONV_W = 4
LRU_C = 8.0
D_IN_PROJ = D_TM_PROJ + 2 * D_LRU
D_FF = 3 * D_MODEL
FFN_CONV_W = 3
EPS = 1e-6
GN_EPS = 64e-5

kernel_name = 'hymba_rwkv7_rglru_convffn_step'


def _rms(x, g):
    xf = x.astype(jnp.float32)
    y = xf * lax.rsqrt(jnp.mean(xf * xf, -1, keepdims=True) + EPS)
    return (y * g.astype(jnp.float32)).astype(x.dtype)


def _causal_dwconv(x, buf, w, b):
    width = w.shape[0]
    seq = x.shape[1]
    xc = jnp.concatenate([buf.astype(x.dtype), x], axis=1)
    y = b + xc[:, 0:seq] * w[0]
    for j in range(1, width):
        y = y + xc[:, j:j + seq] * w[j]
    return y.astype(x.dtype), xc[:, seq:]


def _rwkv7(u, shift_buf, s0, mu, w0, w_up, a0, a_up, g_up, k_k, k_a, r_k, gn_g, gn_b):
    f32 = jnp.float32
    bsz, seq = u.shape[0], u.shape[1]
    prev = jnp.concatenate([shift_buf[:, None].astype(u.dtype), u[:, :-1]], axis=1)
    um = (u + (prev - u) * mu).astype(f32)
    cuts = [D_TM, 2 * D_TM, 3 * D_TM, 3 * D_TM + DECAY_RANK, 3 * D_TM + DECAY_RANK + AAA_RANK]
    r, k, v, xw, xa, xg = jnp.split(um, cuts, axis=-1)
    w_log = -jax.nn.softplus(-(w0 + jnp.tanh(xw) @ w_up)) - 0.5
    decay = jnp.exp(-jnp.exp(w_log))
    a = jax.nn.sigmoid(a0 + xa @ a_up)
    g = jax.nn.sigmoid(xg) @ g_up
    kk = k * k_k
    k = k * (1.0 + (a - 1.0) * k_a)
    heads = lambda t: t.astype(f32).reshape(bsz, seq, TM_HEADS, TM_HEAD)
    rh, kh, vh, dh, ah, kkh = map(heads, (r, k, v, decay, a, kk))
    kkh = kkh * lax.rsqrt(jnp.maximum(jnp.sum(kkh * kkh, -1, keepdims=True), 1e-24))

    def step(s, inp):
        r_t, d_t, k_t, v_t, kk_t, a_t = inp
        s_kk = jnp.einsum('bhvk,bhk->bhv', s, kk_t)
        s = (s * d_t[:, :, None, :]
             - s_kk[..., None] * (kk_t * a_t)[:, :, None, :]
             + v_t[..., None] * k_t[:, :, None, :])
        return s, jnp.einsum('bhvk,bhk->bhv', s, r_t)

    xs = tuple(jnp.moveaxis(t, 1, 0) for t in (rh, dh, kh, vh, kkh, ah))
    s_new, ys = lax.scan(step, s0.astype(f32), xs)
    y = jnp.moveaxis(ys, 0, 1)
    mean = jnp.mean(y, -1, keepdims=True)
    var = jnp.mean(jnp.square(y - mean), -1, keepdims=True)
    y = ((y - mean) * lax.rsqrt(var + GN_EPS)).reshape(bsz, seq, D_TM) * gn_g + gn_b
    bonus = (jnp.sum(rh * kh * r_k, -1, keepdims=True) * vh).reshape(bsz, seq, D_TM)
    out = (y + bonus) * g
    return out.astype(u.dtype), u[:, -1], s_new


def _rglru(xb, gate, conv_buf, h0, pos, conv_w, conv_b, wa, ba, wx, bx, lam, out_g):
    f32 = jnp.float32
    bsz, seq = xb.shape[0], xb.shape[1]
    xc, new_buf = _causal_dwconv(xb, conv_buf, conv_w, conv_b)
    xcf = xc.astype(f32)
    blocks = xcf.reshape(bsz, seq, LRU_BLOCKS, LRU_BLOCK)
    r_g = jax.nn.sigmoid(jnp.einsum('blhi,hij->blhj', blocks, wa).reshape(bsz, seq, D_LRU) + ba)
    i_g = jax.nn.sigmoid(jnp.einsum('blhi,hij->blhj', blocks, wx).reshape(bsz, seq, D_LRU) + bx)
    log_a = -LRU_C * r_g * jax.nn.softplus(-lam.astype(f32))
    a = jnp.exp(log_a)
    mult = jnp.where((pos == 0)[None, :, None], 1.0, jnp.sqrt(-jnp.expm1(2.0 * log_a)))
    b = xcf * i_g * mult
    b = b.at[:, 0].add(a[:, 0] * h0.astype(f32))

    def comb(l, r):
        return (l[0] * r[0], r[0] * l[1] + r[1])

    _, h = lax.associative_scan(comb, (a, b), axis=1)
    y = _rms(h * jax.nn.gelu(gate.astype(f32)), out_g)
    return y.astype(xb.dtype), new_buf, h[:, -1]


def _layer(x, pos, tm_shift, tm_wkv, lru_conv, lru_h, ffn_conv, w):
    (norm1_g, w_in, tm_mu, tm_w0, tm_w_up, tm_a0, tm_a_up, tm_g_up, tm_k_k, tm_k_a, tm_r_k,
     tm_gn_g, tm_gn_b, lru_conv_w, lru_conv_b, lru_wa, lru_ba, lru_wx, lru_bx, lru_lambda,
     lru_out_g, w_out, norm2_g, ffn_w_up, ffn_w_gate, ffn_conv_w, ffn_conv_b, ffn_w_down) = w
    xn = _rms(x, norm1_g)
    u = jnp.einsum('bld,de->ble', xn, w_in)
    u_tm, u_lx, u_lg = jnp.split(u, [D_TM_PROJ, D_TM_PROJ + D_LRU], axis=-1)
    y_tm, new_shift, new_wkv = _rwkv7(u_tm, tm_shift, tm_wkv, tm_mu, tm_w0, tm_w_up, tm_a0, tm_a_up,
                                      tm_g_up, tm_k_k, tm_k_a, tm_r_k, tm_gn_g, tm_gn_b)
    y_lru, new_lconv, new_h = _rglru(u_lx, u_lg, lru_conv, lru_h, pos, lru_conv_w, lru_conv_b,
                                     lru_wa, lru_ba, lru_wx, lru_bx, lru_lambda, lru_out_g)
    x = x + jnp.einsum('ble,ed->bld', jnp.concatenate([y_tm, y_lru], axis=-1), w_out)
    xn = _rms(x, norm2_g)
    up = jnp.einsum('bld,df->blf', xn, ffn_w_up)
    upc, new_fconv = _causal_dwconv(up, ffn_conv, ffn_conv_w, ffn_conv_b)
    hid = jax.nn.gelu(upc) * jnp.einsum('bld,df->blf', xn, ffn_w_gate)
    x = x + jnp.einsum('blf,fd->bld', hid, ffn_w_down)
    return x, (new_shift, new_wkv, new_lconv, new_h, new_fconv)


def setup_inputs(seed: int = 0) -> dict:
    key = jax.random.key(seed)
    ks = iter(jax.random.split(key, 48))
    nrm = lambda shape, scale: scale * jax.random.normal(next(ks), shape, jnp.float32)
    uni = lambda shape, lo, hi: jax.random.uniform(next(ks), shape, jnp.float32, lo, hi)
    a_init = uni((DEPTH, D_LRU), 0.9, 0.999) ** (1.0 / LRU_C)
    lam = jnp.log(a_init) - jnp.log1p(-a_init)
    return {
        'x_prompt': nrm((BATCH, SEQ, D_MODEL), 1.0),
        'x_sample': nrm((DEC_BATCH, DEC_SEQ, D_MODEL), 1.0),
        'state_tm_shift': nrm((DEPTH, DEC_BATCH, D_TM_PROJ), 1.0),
        'state_tm_wkv': nrm((DEPTH, DEC_BATCH, TM_HEADS, TM_HEAD, TM_HEAD), 0.1),
        'state_lru_conv': nrm((DEPTH, DEC_BATCH, LRU_CONV_W - 1, D_LRU), 1.0),
        'state_lru_h': nrm((DEPTH, DEC_BATCH, D_LRU), 0.5),
        'state_ffn_conv': nrm((DEPTH, DEC_BATCH, FFN_CONV_W - 1, D_FF), 1.0),
        'meta_tokens': nrm((N_META, D_MODEL), 1.0),
        'norm1_g': 1.0 + nrm((DEPTH, D_MODEL), 0.01),
        'w_in': nrm((DEPTH, D_MODEL, D_IN_PROJ), D_MODEL ** -0.5),
        'tm_mu': uni((DEPTH, D_TM_PROJ), 0.0, 1.0),
        'tm_w0': uni((DEPTH, D_TM), -6.0, 1.0),
        'tm_w_up': nrm((DEPTH, DECAY_RANK, D_TM), 0.1),
        'tm_a0': nrm((DEPTH, D_TM), 0.1),
        'tm_a_up': nrm((DEPTH, AAA_RANK, D_TM), 0.1),
        'tm_g_up': nrm((DEPTH, GATE_RANK, D_TM), GATE_RANK ** -0.5),
        'tm_k_k': 0.85 + nrm((DEPTH, D_TM), 0.02),
        'tm_k_a': 1.0 + nrm((DEPTH, D_TM), 0.02),
        'tm_r_k': nrm((DEPTH, TM_HEADS, TM_HEAD), 0.1),
        'tm_gn_g': 1.0 + nrm((DEPTH, D_TM), 0.01),
        'tm_gn_b': nrm((DEPTH, D_TM), 0.01),
        'lru_conv_w': nrm((DEPTH, LRU_CONV_W, D_LRU), LRU_CONV_W ** -0.5),
        'lru_conv_b': nrm((DEPTH, D_LRU), 0.01),
        'lru_wa': nrm((DEPTH, LRU_BLOCKS, LRU_BLOCK, LRU_BLOCK), LRU_BLOCK ** -0.5),
        'lru_ba': nrm((DEPTH, D_LRU), 0.1),
        'lru_wx': nrm((DEPTH, LRU_BLOCKS, LRU_BLOCK, LRU_BLOCK), LRU_BLOCK ** -0.5),
        'lru_bx': nrm((DEPTH, D_LRU), 0.1),
        'lru_lambda': lam,
        'lru_out_g': 1.0 + nrm((DEPTH, D_LRU), 0.01),
        'w_out': nrm((DEPTH, D_MIX, D_MODEL), D_MIX ** -0.5),
        'norm2_g': 1.0 + nrm((DEPTH, D_MODEL), 0.01),
        'ffn_w_up': nrm((DEPTH, D_MODEL, D_FF), D_MODEL ** -0.5),
        'ffn_w_gate': nrm((DEPTH, D_MODEL, D_FF), D_MODEL ** -0.5),
        'ffn_conv_w': nrm((DEPTH, FFN_CONV_W, D_FF), FFN_CONV_W ** -0.5),
        'ffn_conv_b': nrm((DEPTH, D_FF), 0.01),
        'ffn_w_down': nrm((DEPTH, D_FF, D_MODEL), D_FF ** -0.5),
        'norm_f_g': 1.0 + nrm((D_MODEL,), 0.01),
    }


def reference(x_prompt, x_sample, state_tm_shift, state_tm_wkv, state_lru_conv, state_lru_h,
              state_ffn_conv, meta_tokens, norm1_g, w_in, tm_mu, tm_w0, tm_w_up, tm_a0, tm_a_up,
              tm_g_up, tm_k_k, tm_k_a, tm_r_k, tm_gn_g, tm_gn_b, lru_conv_w, lru_conv_b, lru_wa,
              lru_ba, lru_wx, lru_bx, lru_lambda, lru_out_g, w_out, norm2_g, ffn_w_up, ffn_w_gate,
              ffn_conv_w, ffn_conv_b, ffn_w_down, norm_f_g):
    f32 = jnp.float32
    bsz = x_prompt.shape[0]
    xp = jnp.concatenate(
        [jnp.broadcast_to(meta_tokens[None].astype(x_prompt.dtype), (bsz, N_META, D_MODEL)), x_prompt],
        axis=1)
    xs = x_sample
    pos_p = jnp.arange(xp.shape[1])
    pos_s = PAST_LEN + jnp.arange(xs.shape[1])
    z_shift = jnp.zeros((bsz, D_TM_PROJ), xp.dtype)
    z_wkv = jnp.zeros((bsz, TM_HEADS, TM_HEAD, TM_HEAD), f32)
    z_lconv = jnp.zeros((bsz, LRU_CONV_W - 1, D_LRU), xp.dtype)
    z_h = jnp.zeros((bsz, D_LRU), f32)
    z_fconv = jnp.zeros((bsz, FFN_CONV_W - 1, D_FF), xp.dtype)
    p_states = [[] for _ in range(5)]
    s_states = [[] for _ in range(5)]
    for l in range(DEPTH):
        w = (norm1_g[l], w_in[l], tm_mu[l], tm_w0[l], tm_w_up[l], tm_a0[l], tm_a_up[l], tm_g_up[l],
             tm_k_k[l], tm_k_a[l], tm_r_k[l], tm_gn_g[l], tm_gn_b[l], lru_conv_w[l], lru_conv_b[l],
             lru_wa[l], lru_ba[l], lru_wx[l], lru_bx[l], lru_lambda[l], lru_out_g[l], w_out[l],
             norm2_g[l], ffn_w_up[l], ffn_w_gate[l], ffn_conv_w[l], ffn_conv_b[l], ffn_w_down[l])
        xp, ps = _layer(xp, pos_p, z_shift, z_wkv, z_lconv, z_h, z_fconv, w)
        xs, ss = _layer(xs, pos_s, state_tm_shift[l], state_tm_wkv[l], state_lru_conv[l],
                        state_lru_h[l], state_ffn_conv[l], w)
        for i in range(5):
            p_states[i].append(ps[i])
            s_states[i].append(ss[i])
    y_prompt = _rms(xp, norm_f_g)[:, N_META:]
    y_sample = _rms(xs, norm_f_g)
    p_tm_shift, p_tm_wkv, p_lru_conv, p_lru_h, p_ffn_conv = [jnp.stack(s) for s in p_states]
    s_tm_shift, s_tm_wkv, s_lru_conv, s_lru_h, s_ffn_conv = [jnp.stack(s) for s in s_states]
    return (y_prompt, y_sample, p_tm_shift, p_tm_wkv, p_lru_conv, p_lru_h, p_ffn_conv,
            s_tm_shift, s_tm_wkv, s_lru_conv, s_lru_h, s_ffn_conv)
```

```python
import functools

import jax
import jax.numpy as jnp
from jax import lax
from jax.experimental import pallas as pl
from jax.experimental.pallas import tpu as pltpu

F32 = jnp.float32
BF16 = jnp.bfloat16

D_MODEL = 1024
N_META = 16
D_TM = 512
TM_HEAD = 64
TM_HEADS = 8
DECAY_RANK = 64
AAA_RANK = 64
GATE_RANK = 128
D_TM_PROJ = 3 * D_TM + DECAY_RANK + AAA_RANK + GATE_RANK
D_LRU = 512
LRU_BLOCKS = 8
LRU_BLOCK = 64
LRU_CONV_W = 4
LRU_C = 8.0
D_IN_PROJ = D_TM_PROJ + 2 * D_LRU
D_FF = 3 * D_MODEL
FFN_CONV_W = 3
EPS = 1e-6
GN_EPS = 64e-5

SUBLANES = 8
LANES = 128
LORA_OFF = 3 * D_TM
GATE_OFF = LORA_OFF + DECAY_RANK + AAA_RANK
FF_COL_TILE = 1024
VMEM_LIMIT_BYTES = 60 * 1024 * 1024


def _dot(a, b):
    return jnp.dot(a.astype(BF16), b.astype(BF16), preferred_element_type=F32)


def _dot_nt(a, b):
    return lax.dot_general(a.astype(BF16), b.astype(BF16), (((1,), (1,)), ((), ())),
                           preferred_element_type=F32)


def _dot_tn(a, b):
    return lax.dot_general(a.astype(BF16), b.astype(BF16), (((0,), (0,)), ((), ())),
                           preferred_element_type=F32)


def _rms(x, g):
    return x * lax.rsqrt(jnp.mean(x * x, axis=-1, keepdims=True) + EPS) * g


def _softplus(z):
    return jnp.maximum(z, 0.0) + jnp.log(1.0 + jnp.exp(-jnp.abs(z)))


def _sigmoid(z):
    return 1.0 / (1.0 + jnp.exp(-z))


def _time_index(rows, cols, chunk):
    return lax.broadcasted_iota(jnp.int32, (rows, cols), 0) & (chunk - 1)


def _shift_rows(x, tloc, j, fill):
    return jnp.where(tloc >= j, pltpu.roll(x, j, axis=0), fill)


def _mixer_kernel(x_ref, st_shift_ref, st_wkv_ref, st_conv_ref, st_h_ref,
                  g1_ref, w_in_ref, mu_ref, w0_ref, wdec_ref, a0_ref, waaa_ref, wgate_ref,
                  kk_ref, ka_ref, rk_ref, gng_ref, gnb_ref,
                  cw_ref, cb_ref, wa_ref, ba_ref, wx_ref, bx_ref, lam_ref, og_ref,
                  seg_ref, segt_ref,
                  y_ref, o_shift_ref, o_wkv_ref, o_conv_ref, o_h_ref,
                  u_s, prev_s, xs1_s, xs2_s, xs3_s, kap_s, rt_s, bt_s, kt_s, v_s, c_s, yt_s,
                  la_s, lb_s, h_s,
                  *, bb, chunk, valid, pos0):
    rows = bb * chunk
    ti = pl.program_id(1)

    @pl.when(ti == 0)
    def _():
        o_shift_ref[...] = st_shift_ref[...]
        o_wkv_ref[...] = st_wkv_ref[...]
        o_conv_ref[...] = st_conv_ref[...]
        o_h_ref[...] = st_h_ref[...]

    seg = seg_ref[...]
    segt = segt_ref[...]

    def head_sum(x):
        xh = x.astype(BF16)
        xl = (x - xh.astype(F32)).astype(BF16)
        s = (jnp.dot(xh, seg, preferred_element_type=F32) + jnp.dot(xl, seg, preferred_element_type=F32))
        sh = s.astype(BF16)
        sl = (s - sh.astype(F32)).astype(BF16)
        return (jnp.dot(sh, segt, preferred_element_type=F32) + jnp.dot(sl, segt, preferred_element_type=F32))

    x = x_ref[...].reshape(rows, D_MODEL)
    u_s[...] = _dot(_rms(x, g1_ref[...]), w_in_ref[...])

    prev_s[...] = pltpu.roll(u_s[:, :D_TM_PROJ], 1, axis=0)
    xb_all = u_s[:, D_TM_PROJ:D_TM_PROJ + D_LRU]
    xs1_s[...] = pltpu.roll(xb_all, 1, axis=0)
    xs2_s[...] = pltpu.roll(xb_all, 2, axis=0)
    xs3_s[...] = pltpu.roll(xb_all, 3, axis=0)
    row8 = lax.broadcasted_iota(jnp.int32, (SUBLANES, 1), 0)

    def patch_body(b, carry):
        r0 = pl.multiple_of(b * chunk, SUBLANES)
        head = pl.ds(r0, SUBLANES)
        tail = pl.ds(r0 + chunk - SUBLANES, SUBLANES)
        prev_s[head, :] = jnp.where(row8 < 1, pltpu.roll(o_shift_ref[b], 1, axis=0), prev_s[head, :])
        conv_tile = o_conv_ref[b]
        xs1_s[head, :] = jnp.where(row8 < 1, pltpu.roll(conv_tile, 1, axis=0), xs1_s[head, :])
        xs2_s[head, :] = jnp.where(row8 < 2, pltpu.roll(conv_tile, 2, axis=0), xs2_s[head, :])
        xs3_s[head, :] = jnp.where(row8 < 3, pltpu.roll(conv_tile, 3, axis=0), xs3_s[head, :])
        o_shift_ref[b] = u_s[tail, :D_TM_PROJ]
        o_conv_ref[b] = u_s[tail, D_TM_PROJ:D_TM_PROJ + D_LRU]
        return carry

    lax.fori_loop(0, bb, patch_body, 0)

    tloc = _time_index(rows, D_TM, chunk)

    u_tm = u_s[:, :D_TM_PROJ]
    um = u_tm + (prev_s[...] - u_tm) * mu_ref[...]
    r = um[:, 0:D_TM]
    k = um[:, D_TM:2 * D_TM]
    v = um[:, 2 * D_TM:3 * D_TM]
    x_lora = um[:, LORA_OFF:GATE_OFF]
    x_gate = um[:, GATE_OFF:D_TM_PROJ]
    w_log = -_softplus(-(w0_ref[...] + _dot(jnp.tanh(x_lora), wdec_ref[...]))) - 0.5
    log_decay = -jnp.exp(w_log)
    a = _sigmoid(a0_ref[...] + _dot(x_lora, waaa_ref[...]))
    gate_tm = _dot(_sigmoid(x_gate), wgate_ref[...])
    kk = k * kk_ref[...]
    k = k * (1.0 + (a - 1.0) * ka_ref[...])
    kk = kk * lax.rsqrt(jnp.maximum(head_sum(kk * kk), 1e-24))
    bonus = head_sum(r * k * rk_ref[...]) * v
    if valid < chunk:
        live = tloc < valid
        log_decay = jnp.where(live, log_decay, 0.0)
        kk = jnp.where(live, kk, 0.0)
        k = jnp.where(live, k, 0.0)
    c = log_decay
    s = 1
    while s < chunk:
        c = c + _shift_rows(c, tloc, s, 0.0)
        s *= 2
    c_s[...] = c
    kap_s[...] = kk * jnp.exp(c - log_decay)
    rt_s[...] = r * jnp.exp(c)
    inv_p = jnp.exp(-c)
    bt_s[...] = kk * a * inv_p
    kt_s[...] = k * inv_p
    v_s[...] = v

    xb = u_s[:, D_TM_PROJ:D_TM_PROJ + D_LRU]
    gate_lru = u_s[:, D_TM_PROJ + D_LRU:D_IN_PROJ]
    xc = cb_ref[...] + xs3_s[...] * cw_ref[0:1, :]
    xc = xc + xs2_s[...] * cw_ref[1:2, :]
    xc = xc + xs1_s[...] * cw_ref[2:3, :]
    xc = xc + xb * cw_ref[3:4, :]
    r_g = _sigmoid(_dot(xc, wa_ref[...]) + ba_ref[...])
    i_g = _sigmoid(_dot(xc, wx_ref[...]) + bx_ref[...])
    log_a = -LRU_C * r_g * _softplus(-lam_ref[...])
    pos = pos0 + ti * chunk + tloc
    mult = jnp.where(pos == 0, 1.0, jnp.sqrt(1.0 - jnp.exp(2.0 * log_a)))
    la = jnp.exp(log_a)
    lb = xc * i_g * mult
    s = 1
    while s < chunk:
        la_sh = _shift_rows(la, tloc, s, 1.0)
        lb_sh = _shift_rows(lb, tloc, s, 0.0)
        lb = la * lb_sh + lb
        la = la * la_sh
        s *= 2
    la_s[...] = la
    lb_s[...] = lb

    n_lv = max(1, (chunk - 1).bit_length())
    ri = lax.broadcasted_iota(jnp.int32, (chunk, chunk), 0)
    ci = lax.broadcasted_iota(jnp.int32, (chunk, chunk), 1)
    strict = ri > ci
    incl = ri >= ci
    eye = (ri == ci).astype(F32)

    def seq_body(b, carry):
        r0 = pl.multiple_of(b * chunk, SUBLANES)
        rs = pl.ds(r0, chunk)
        tail = pl.ds(r0 + chunk - SUBLANES, SUBLANES)
        h0 = o_h_ref[b][SUBLANES - 1:SUBLANES, :]
        h_s[rs, :] = la_s[rs, :] * h0 + lb_s[rs, :]
        o_h_ref[b] = h_s[tail, :]
        p_end = jnp.exp(c_s[tail, :][SUBLANES - 1:SUBLANES, :])
        for hd in range(TM_HEADS):
            cs = slice(hd * TM_HEAD, (hd + 1) * TM_HEAD)
            kap = kap_s[rs, cs]
            rt = rt_s[rs, cs]
            bt = bt_s[rs, cs]
            kt = kt_s[rs, cs]
            vv = v_s[rs, cs]
            s0 = o_wkv_ref[b, hd]
            lhs = jnp.concatenate([kap, rt], axis=0)
            g_b = _dot_nt(lhs, bt)
            g_k = _dot_nt(lhs, kt)
            m_ab = jnp.where(strict, g_b[:chunk], 0.0)
            m_ak = jnp.where(strict, g_k[:chunk], 0.0)
            m_rb = jnp.where(incl, g_b[chunk:], 0.0)
            m_rk = jnp.where(incl, g_k[chunk:], 0.0)
            t_inv = eye - m_ab
            m_pow = _dot(m_ab, m_ab)
            for lv in range(1, n_lv):
                if lv < n_lv - 1:
                    prod = _dot(jnp.concatenate([t_inv, m_pow], axis=0), m_pow)
                    t_inv = t_inv + prod[:chunk]
                    m_pow = prod[chunk:]
                else:
                    t_inv = t_inv + _dot(t_inv, m_pow)
            xv = _dot(jnp.concatenate([m_ak, m_rk], axis=0), vv)
            z = _dot_nt(lhs, s0)
            uu = -_dot(t_inv, z[:chunk] + xv[:chunk])
            yt_s[rs, cs] = z[chunk:] + _dot(m_rb, uu) + xv[chunk:]
            upd = _dot_tn(jnp.concatenate([uu, vv], axis=0), jnp.concatenate([bt, kt], axis=0))
            o_wkv_ref[b, hd] = (s0 + upd) * p_end[:, cs]
        return carry

    lax.fori_loop(0, bb, seq_body, 0)

    yv = yt_s[...]
    mean = head_sum(yv) * (1.0 / TM_HEAD)
    cen = yv - mean
    var = head_sum(cen * cen) * (1.0 / TM_HEAD)
    y_tm = ((cen * lax.rsqrt(var + GN_EPS)) * gng_ref[...] + gnb_ref[...] + bonus) * gate_tm
    y_lru = _rms(h_s[...] * jax.nn.gelu(gate_lru), og_ref[...])
    y_ref[:, :, 0:D_TM] = y_tm.astype(y_ref.dtype).reshape(bb, chunk, D_TM)
    y_ref[:, :, D_TM:D_MODEL] = y_lru.astype(y_ref.dtype).reshape(bb, chunk, D_LRU)


def _const_spec(shape):
    zeros = (0,) * len(shape)
    return pl.BlockSpec(shape, lambda bi, ti: zeros, pipeline_mode=pl.Buffered(1))


def _mixer(x, st_shift, st_wkv, st_conv, st_h, weights, *, bb, chunk, valid, pos0):
    batch, length, _ = x.shape
    assert batch % bb == 0 and length % chunk == 0 and chunk % SUBLANES == 0
    assert chunk & (chunk - 1) == 0 and valid <= chunk
    assert valid == chunk or length == chunk
    rows = bb * chunk
    grid = (batch // bb, length // chunk)
    tile3 = lambda w: pl.BlockSpec((bb, SUBLANES, w), lambda bi, ti: (bi, 0, 0))
    wkv_spec = pl.BlockSpec((bb, TM_HEADS, TM_HEAD, TM_HEAD), lambda bi, ti: (bi, 0, 0, 0))
    in_specs = [pl.BlockSpec((bb, chunk, D_MODEL), lambda bi, ti: (bi, ti, 0)),
                tile3(D_TM_PROJ), wkv_spec, tile3(D_LRU), tile3(D_LRU)]
    in_specs += [_const_spec(w.shape) for w in weights]
    out_specs = [pl.BlockSpec((bb, chunk, D_MODEL), lambda bi, ti: (bi, ti, 0)),
                 tile3(D_TM_PROJ), wkv_spec, tile3(D_LRU), tile3(D_LRU)]
    y_dtype = BF16 if chunk % (2 * SUBLANES) == 0 else F32
    out_shape = [jax.ShapeDtypeStruct((batch, length, D_MODEL), y_dtype),
                 jax.ShapeDtypeStruct((batch, SUBLANES, D_TM_PROJ), F32),
                 jax.ShapeDtypeStruct((batch, TM_HEADS, TM_HEAD, TM_HEAD), F32),
                 jax.ShapeDtypeStruct((batch, SUBLANES, D_LRU), F32),
                 jax.ShapeDtypeStruct((batch, SUBLANES, D_LRU), F32)]
    sq = lambda w: pltpu.VMEM((rows, w), F32)
    scratch = [sq(D_IN_PROJ), sq(D_TM_PROJ)] + [sq(D_TM)] * 13
    return pl.pallas_call(
        functools.partial(_mixer_kernel, bb=bb, chunk=chunk, valid=valid, pos0=pos0),
        out_shape=out_shape, grid=grid, in_specs=in_specs, out_specs=out_specs,
        scratch_shapes=scratch, name="mixer",
        compiler_params=pltpu.CompilerParams(dimension_semantics=("arbitrary", "arbitrary"),
                                             vmem_limit_bytes=VMEM_LIMIT_BYTES),
    )(x, st_shift, st_wkv, st_conv, st_h, *weights)


def _ffn_kernel(x_ref, ym_ref, st_conv_ref,
                w_out_ref, g2_ref, w_up_ref, w_gate_ref, cw_ref, cb_ref, w_down_ref, gf_ref,
                y_ref, o_conv_ref,
                up_s, u1_s, u2_s,
                *, bb, chunk):
    rows = bb * chunk
    ti = pl.program_id(1)

    @pl.when(ti == 0)
    def _():
        o_conv_ref[...] = st_conv_ref[...]

    x = x_ref[...].reshape(rows, D_MODEL)
    ym = ym_ref[...].reshape(rows, D_MODEL)
    x1 = x + _dot(ym, w_out_ref[...])
    xn = _rms(x1, g2_ref[...]).astype(BF16)
    row8 = lax.broadcasted_iota(jnp.int32, (SUBLANES, 1), 0)
    acc = x1
    for n in range(D_FF // FF_COL_TILE):
        cols = slice(n * FF_COL_TILE, (n + 1) * FF_COL_TILE)
        up = jnp.dot(xn, w_up_ref[:, cols], preferred_element_type=F32)
        gate = jnp.dot(xn, w_gate_ref[:, cols], preferred_element_type=F32)
        up_s[...] = up
        u1_s[...] = pltpu.roll(up, 1, axis=0)
        u2_s[...] = pltpu.roll(up, 2, axis=0)

        def patch_body(b, carry):
            r0 = pl.multiple_of(b * chunk, SUBLANES)
            head = pl.ds(r0, SUBLANES)
            tail = pl.ds(r0 + chunk - SUBLANES, SUBLANES)
            tile = o_conv_ref[b, :, cols]
            u1_s[head, :] = jnp.where(row8 < 1, pltpu.roll(tile, 1, axis=0), u1_s[head, :])
            u2_s[head, :] = jnp.where(row8 < 2, pltpu.roll(tile, 2, axis=0), u2_s[head, :])
            o_conv_ref[b, :, cols] = up_s[tail, :]
            return carry

        lax.fori_loop(0, bb, patch_body, 0)
        upc = cb_ref[:, cols] + u2_s[...] * cw_ref[0:1, cols]
        upc = upc + u1_s[...] * cw_ref[1:2, cols]
        upc = upc + up * cw_ref[2:3, cols]
        hid = (jax.nn.gelu(upc) * gate).astype(BF16)
        acc = acc + jnp.dot(hid, w_down_ref[cols, :], preferred_element_type=F32)
    y_ref[...] = _rms(acc, gf_ref[...]).reshape(bb, chunk, D_MODEL)


def _ffn(x, ym, st_conv, weights, *, bb, chunk):
    batch, length, _ = x.shape
    assert batch % bb == 0 and length % chunk == 0 and chunk % SUBLANES == 0
    rows = bb * chunk
    grid = (batch // bb, length // chunk)
    act = pl.BlockSpec((bb, chunk, D_MODEL), lambda bi, ti: (bi, ti, 0))
    tile = pl.BlockSpec((bb, SUBLANES, D_FF), lambda bi, ti: (bi, 0, 0))
    in_specs = [act, act, tile] + [_const_spec(w.shape) for w in weights]
    out_shape = [jax.ShapeDtypeStruct((batch, length, D_MODEL), F32),
                 jax.ShapeDtypeStruct((batch, SUBLANES, D_FF), F32)]
    scratch = [pltpu.VMEM((rows, FF_COL_TILE), F32)] * 3
    return pl.pallas_call(
        functools.partial(_ffn_kernel, bb=bb, chunk=chunk),
        out_shape=out_shape, grid=grid, in_specs=in_specs, out_specs=[act, tile],
        scratch_shapes=scratch, name="ffn",
        compiler_params=pltpu.CompilerParams(dimension_semantics=("arbitrary", "arbitrary"),
                                             vmem_limit_bytes=VMEM_LIMIT_BYTES),
    )(x, ym, st_conv, *weights)


def _row(v):
    return v.reshape(1, -1).astype(F32)


def _block_diag(w):
    nb, n, _ = w.shape
    eye = jnp.eye(nb, dtype=w.dtype)
    return (eye[:, None, :, None] * w[:, :, None, :]).reshape(nb * n, nb * n)


def _tail_tile(state, n_rows):
    return jnp.pad(state.astype(F32), ((0, 0), (SUBLANES - n_rows, 0), (0, 0)))


def _layer(x, states, mixer_w, ffn_w, *, mixer_bb, ffn_bb, mixer_chunk, ffn_chunk, valid, pos0):
    st_shift, st_wkv, st_conv, st_h, st_fconv = states
    ym, o_shift, o_wkv, o_conv, o_h = _mixer(x, st_shift, st_wkv, st_conv, st_h, mixer_w,
                                             bb=mixer_bb, chunk=mixer_chunk, valid=valid, pos0=pos0)
    y, o_fconv = _ffn(x, ym, st_fconv, ffn_w, bb=ffn_bb, chunk=ffn_chunk)
    return y, (o_shift, o_wkv, o_conv, o_h, o_fconv)


def kernel(x_prompt, x_sample, state_tm_shift, state_tm_wkv, state_lru_conv, state_lru_h, state_ffn_conv, meta_tokens, norm1_g, w_in, tm_mu, tm_w0, tm_w_up, tm_a0, tm_a_up, tm_g_up, tm_k_k, tm_k_a, tm_r_k, tm_gn_g, tm_gn_b, lru_conv_w, lru_conv_b, lru_wa, lru_ba, lru_wx, lru_bx, lru_lambda, lru_out_g, w_out, norm2_g, ffn_w_up, ffn_w_gate, ffn_conv_w, ffn_conv_b, ffn_w_down, norm_f_g):
    depth = w_in.shape[0]
    assert depth == 1
    l = 0
    zeros_lora = jnp.zeros((DECAY_RANK, D_TM), F32)
    head_id = jnp.arange(D_TM) // TM_HEAD
    seg = (head_id[:, None] == jnp.arange(LANES)[None, :]).astype(BF16)
    mixer_w = (
        _row(norm1_g[l]), w_in[l].astype(BF16), _row(tm_mu[l]), _row(tm_w0[l]),
        jnp.concatenate([tm_w_up[l], zeros_lora], axis=0).astype(BF16),
        _row(tm_a0[l]),
        jnp.concatenate([zeros_lora, tm_a_up[l]], axis=0).astype(BF16),
        tm_g_up[l].astype(BF16),
        _row(tm_k_k[l]), _row(tm_k_a[l]), _row(tm_r_k[l]), _row(tm_gn_g[l]), _row(tm_gn_b[l]),
        lru_conv_w[l].astype(F32), _row(lru_conv_b[l]),
        _block_diag(lru_wa[l]).astype(BF16), _row(lru_ba[l]),
        _block_diag(lru_wx[l]).astype(BF16), _row(lru_bx[l]),
        _row(lru_lambda[l]), _row(lru_out_g[l]),
        seg, seg.T,
    )
    ffn_w = (
        w_out[l].astype(BF16), _row(norm2_g[l]), ffn_w_up[l].astype(BF16), ffn_w_gate[l].astype(BF16),
        ffn_conv_w[l].astype(F32), _row(ffn_conv_b[l]), ffn_w_down[l].astype(BF16), _row(norm_f_g),
    )

    zero_states = (jnp.zeros((1, SUBLANES, D_TM_PROJ), F32),
                   jnp.zeros((1, TM_HEADS, TM_HEAD, TM_HEAD), F32),
                   jnp.zeros((1, SUBLANES, D_LRU), F32),
                   jnp.zeros((1, SUBLANES, D_LRU), F32),
                   jnp.zeros((1, SUBLANES, D_FF), F32))
    _, meta_states = _layer(meta_tokens[None].astype(F32), zero_states, mixer_w, ffn_w,
                            mixer_bb=1, ffn_bb=1, mixer_chunk=N_META, ffn_chunk=N_META,
                            valid=N_META, pos0=0)

    bsz, seq = x_prompt.shape[0], x_prompt.shape[1]
    p_init = tuple(jnp.broadcast_to(s, (bsz,) + s.shape[1:]) for s in meta_states)
    y_prompt, p_st = _layer(x_prompt, p_init, mixer_w, ffn_w,
                            mixer_bb=bsz, ffn_bb=1, mixer_chunk=64, ffn_chunk=512,
                            valid=64, pos0=N_META)

    dec_b, dec_seq = x_sample.shape[0], x_sample.shape[1]
    pad_seq = SUBLANES
    xs = jnp.pad(x_sample, ((0, 0), (0, pad_seq - dec_seq), (0, 0)))
    s_init = (_tail_tile(state_tm_shift[l][:, None, :], 1), state_tm_wkv[l].astype(F32),
              _tail_tile(state_lru_conv[l], LRU_CONV_W - 1), _tail_tile(state_lru_h[l][:, None, :], 1),
              _tail_tile(state_ffn_conv[l], FFN_CONV_W - 1))
    y_s, s_st = _layer(xs, s_init, mixer_w, ffn_w,
                       mixer_bb=16, ffn_bb=16, mixer_chunk=pad_seq, ffn_chunk=pad_seq,
                       valid=dec_seq, pos0=16384)
    y_sample = y_s[:, :dec_seq]

    def unpack(st, last):
        o_shift, o_wkv, o_conv, o_h, o_fconv = st
        return (o_shift[:, last][None], o_wkv[None],
                o_conv[:, last - (LRU_CONV_W - 2):last + 1][None], o_h[:, last][None],
                o_fconv[:, last - (FFN_CONV_W - 2):last + 1][None])

    p_out = unpack(p_st, SUBLANES - 1)
    s_out = unpack(s_st, dec_seq - 1)
    return (y_prompt, y_sample) + p_out + s_out
```

```python
import functools

import jax
import jax.numpy as jnp
from jax import lax
from jax.experimental import pallas as pl
from jax.experimental.pallas import tpu as pltpu

F32 = jnp.float32
BF16 = jnp.bfloat16

D_MODEL = 1024
N_META = 16
D_TM = 512
TM_HEAD = 64
TM_HEADS = 8
DECAY_RANK = 64
AAA_RANK = 64
GATE_RANK = 128
D_TM_PROJ = 3 * D_TM + DECAY_RANK + AAA_RANK + GATE_RANK
D_LRU = 512
LRU_BLOCKS = 8
LRU_BLOCK = 64
LRU_CONV_W = 4
LRU_C = 8.0
D_IN_PROJ = D_TM_PROJ + 2 * D_LRU
D_FF = 3 * D_MODEL
FFN_CONV_W = 3
EPS = 1e-6
GN_EPS = 64e-5

SUBLANES = 8
LANES = 128
LORA_OFF = 3 * D_TM
GATE_OFF = LORA_OFF + DECAY_RANK + AAA_RANK
FF_COL_TILE = 1024
VMEM_LIMIT_BYTES = 60 * 1024 * 1024


def _dot(a, b):
    return jnp.dot(a.astype(BF16), b.astype(BF16), preferred_element_type=F32)


def _dot_nt(a, b):
    return lax.dot_general(a.astype(BF16), b.astype(BF16), (((1,), (1,)), ((), ())),
                           preferred_element_type=F32)


def _dot_tn(a, b):
    return lax.dot_general(a.astype(BF16), b.astype(BF16), (((0,), (0,)), ((), ())),
                           preferred_element_type=F32)


def _bdot(a, b):
    return lax.dot_general(a.astype(BF16), b.astype(BF16), (((2,), (1,)), ((0,), (0,))),
                           preferred_element_type=F32)


def _bdot_nt(a, b):
    return lax.dot_general(a.astype(BF16), b.astype(BF16), (((2,), (2,)), ((0,), (0,))),
                           preferred_element_type=F32)


def _bdot_tn(a, b):
    return lax.dot_general(a.astype(BF16), b.astype(BF16), (((1,), (1,)), ((0,), (0,))),
                           preferred_element_type=F32)


def _heads(x):
    return jnp.stack([x[:, hd * TM_HEAD:(hd + 1) * TM_HEAD] for hd in range(TM_HEADS)])


def _rms(x, g):
    return x * lax.rsqrt(jnp.mean(x * x, axis=-1, keepdims=True) + EPS) * g


def _softplus(z):
    return jnp.maximum(z, 0.0) + jnp.log(1.0 + jnp.exp(-jnp.abs(z)))


def _sigmoid(z):
    return 1.0 / (1.0 + jnp.exp(-z))


def _time_index(rows, cols, chunk):
    return lax.broadcasted_iota(jnp.int32, (rows, cols), 0) & (chunk - 1)


def _shift_rows(x, tloc, j, fill):
    return jnp.where(tloc >= j, pltpu.roll(x, j, axis=0), fill)


def _mixer_kernel(x_ref, st_shift_ref, st_wkv_ref, st_conv_ref, st_h_ref,
                  g1_ref, w_in_ref, mu_ref, w0_ref, wdec_ref, a0_ref, waaa_ref, wgate_ref,
                  kk_ref, ka_ref, rk_ref, gng_ref, gnb_ref,
                  cw_ref, cb_ref, wa_ref, ba_ref, wx_ref, bx_ref, lam_ref, og_ref,
                  seg_ref, segt_ref,
                  y_ref, o_shift_ref, o_wkv_ref, o_conv_ref, o_h_ref,
                  u_s, prev_s, xs1_s, xs2_s, xs3_s, kap_s, rt_s, bt_s, kt_s, v_s, c_s, yt_s,
                  la_s, lb_s, h_s,
                  *, bb, chunk, valid, pos0):
    rows = bb * chunk
    ti = pl.program_id(1)

    @pl.when(ti == 0)
    def _():
        o_shift_ref[...] = st_shift_ref[...]
        o_wkv_ref[...] = st_wkv_ref[...]
        o_conv_ref[...] = st_conv_ref[...]
        o_h_ref[...] = st_h_ref[...]

    seg = seg_ref[...]
    segt = segt_ref[...]

    def head_sum(x):
        xh = x.astype(BF16)
        xl = (x - xh.astype(F32)).astype(BF16)
        s = (jnp.dot(xh, seg, preferred_element_type=F32) + jnp.dot(xl, seg, preferred_element_type=F32))
        sh = s.astype(BF16)
        sl = (s - sh.astype(F32)).astype(BF16)
        return (jnp.dot(sh, segt, preferred_element_type=F32) + jnp.dot(sl, segt, preferred_element_type=F32))

    x = x_ref[...].reshape(rows, D_MODEL)
    u_s[...] = _dot(_rms(x, g1_ref[...]), w_in_ref[...])

    prev_s[...] = pltpu.roll(u_s[:, :D_TM_PROJ], 1, axis=0)
    xb_all = u_s[:, D_TM_PROJ:D_TM_PROJ + D_LRU]
    xs1_s[...] = pltpu.roll(xb_all, 1, axis=0)
    xs2_s[...] = pltpu.roll(xb_all, 2, axis=0)
    xs3_s[...] = pltpu.roll(xb_all, 3, axis=0)
    row8 = lax.broadcasted_iota(jnp.int32, (SUBLANES, 1), 0)

    def patch_body(b, carry):
        r0 = pl.multiple_of(b * chunk, SUBLANES)
        head = pl.ds(r0, SUBLANES)
        tail = pl.ds(r0 + chunk - SUBLANES, SUBLANES)
        prev_s[head, :] = jnp.where(row8 < 1, pltpu.roll(o_shift_ref[b], 1, axis=0), prev_s[head, :])
        conv_tile = o_conv_ref[b]
        xs1_s[head, :] = jnp.where(row8 < 1, pltpu.roll(conv_tile, 1, axis=0), xs1_s[head, :])
        xs2_s[head, :] = jnp.where(row8 < 2, pltpu.roll(conv_tile, 2, axis=0), xs2_s[head, :])
        xs3_s[head, :] = jnp.where(row8 < 3, pltpu.roll(conv_tile, 3, axis=0), xs3_s[head, :])
        o_shift_ref[b] = u_s[tail, :D_TM_PROJ]
        o_conv_ref[b] = u_s[tail, D_TM_PROJ:D_TM_PROJ + D_LRU]
        return carry

    lax.fori_loop(0, bb, patch_body, 0)

    tloc = _time_index(rows, D_TM, chunk)

    u_tm = u_s[:, :D_TM_PROJ]
    um = u_tm + (prev_s[...] - u_tm) * mu_ref[...]
    r = um[:, 0:D_TM]
    k = um[:, D_TM:2 * D_TM]
    v = um[:, 2 * D_TM:3 * D_TM]
    x_lora = um[:, LORA_OFF:GATE_OFF]
    x_gate = um[:, GATE_OFF:D_TM_PROJ]
    w_log = -_softplus(-(w0_ref[...] + _dot(jnp.tanh(x_lora), wdec_ref[...]))) - 0.5
    log_decay = -jnp.exp(w_log)
    a = _sigmoid(a0_ref[...] + _dot(x_lora, waaa_ref[...]))
    gate_tm = _dot(_sigmoid(x_gate), wgate_ref[...])
    kk = k * kk_ref[...]
    k = k * (1.0 + (a - 1.0) * ka_ref[...])
    kk = kk * lax.rsqrt(jnp.maximum(head_sum(kk * kk), 1e-24))
    bonus = head_sum(r * k * rk_ref[...]) * v
    if valid < chunk:
        live = tloc < valid
        log_decay = jnp.where(live, log_decay, 0.0)
        kk = jnp.where(live, kk, 0.0)
        k = jnp.where(live, k, 0.0)
    c = log_decay
    s = 1
    while s < chunk:
        c = c + _shift_rows(c, tloc, s, 0.0)
        s *= 2
    c_s[...] = c
    kap_s[...] = kk * jnp.exp(c - log_decay)
    rt_s[...] = r * jnp.exp(c)
    inv_p = jnp.exp(-c)
    bt_s[...] = kk * a * inv_p
    kt_s[...] = k * inv_p
    v_s[...] = v

    xb = u_s[:, D_TM_PROJ:D_TM_PROJ + D_LRU]
    gate_lru = u_s[:, D_TM_PROJ + D_LRU:D_IN_PROJ]
    xc = cb_ref[...] + xs3_s[...] * cw_ref[0:1, :]
    xc = xc + xs2_s[...] * cw_ref[1:2, :]
    xc = xc + xs1_s[...] * cw_ref[2:3, :]
    xc = xc + xb * cw_ref[3:4, :]
    r_g = _sigmoid(_dot(xc, wa_ref[...]) + ba_ref[...])
    i_g = _sigmoid(_dot(xc, wx_ref[...]) + bx_ref[...])
    log_a = -LRU_C * r_g * _softplus(-lam_ref[...])
    pos = pos0 + ti * chunk + tloc
    mult = jnp.where(pos == 0, 1.0, jnp.sqrt(1.0 - jnp.exp(2.0 * log_a)))
    la = jnp.exp(log_a)
    lb = xc * i_g * mult
    s = 1
    while s < chunk:
        la_sh = _shift_rows(la, tloc, s, 1.0)
        lb_sh = _shift_rows(lb, tloc, s, 0.0)
        lb = la * lb_sh + lb
        la = la * la_sh
        s *= 2
    la_s[...] = la
    lb_s[...] = lb

    n_lv = max(1, (chunk - 1).bit_length())
    ri = lax.broadcasted_iota(jnp.int32, (chunk, chunk), 0)
    ci = lax.broadcasted_iota(jnp.int32, (chunk, chunk), 1)
    strict = ri > ci
    incl = ri >= ci
    eye = (ri == ci).astype(F32)

    def seq_body(b, carry):
        r0 = pl.multiple_of(b * chunk, SUBLANES)
        rs = pl.ds(r0, chunk)
        tail = pl.ds(r0 + chunk - SUBLANES, SUBLANES)
        h0 = o_h_ref[b][SUBLANES - 1:SUBLANES, :]
        h_s[rs, :] = la_s[rs, :] * h0 + lb_s[rs, :]
        o_h_ref[b] = h_s[tail, :]
        p_end = jnp.exp(c_s[tail, :][SUBLANES - 1:SUBLANES, :])
        kap = _heads(kap_s[rs, :])
        rt = _heads(rt_s[rs, :])
        bt = _heads(bt_s[rs, :])
        kt = _heads(kt_s[rs, :])
        vv = _heads(v_s[rs, :])
        s0 = o_wkv_ref[b]
        lhs = jnp.concatenate([kap, rt], axis=1)
        g_b = _bdot_nt(lhs, bt)
        g_k = _bdot_nt(lhs, kt)
        m_ab = jnp.where(strict, g_b[:, :chunk], 0.0)
        m_ak = jnp.where(strict, g_k[:, :chunk], 0.0)
        m_rb = jnp.where(incl, g_b[:, chunk:], 0.0)
        m_rk = jnp.where(incl, g_k[:, chunk:], 0.0)
        t_inv = eye - m_ab
        m_pow = _bdot(m_ab, m_ab)
        for lv in range(1, n_lv):
            if lv < n_lv - 1:
                prod = _bdot(jnp.concatenate([t_inv, m_pow], axis=1), m_pow)
                t_inv = t_inv + prod[:, :chunk]
                m_pow = prod[:, chunk:]
            else:
                t_inv = t_inv + _bdot(t_inv, m_pow)
        xv = _bdot(jnp.concatenate([m_ak, m_rk], axis=1), vv)
        z = _bdot_nt(lhs, s0)
        uu = -_bdot(t_inv, z[:, :chunk] + xv[:, :chunk])
        yh = z[:, chunk:] + _bdot(m_rb, uu) + xv[:, chunk:]
        upd = _bdot_tn(jnp.concatenate([uu, vv], axis=1), jnp.concatenate([bt, kt], axis=1))
        o_wkv_ref[b] = (s0 + upd) * _heads(p_end)
        for hd in range(TM_HEADS):
            yt_s[rs, hd * TM_HEAD:(hd + 1) * TM_HEAD] = yh[hd]
        return carry

    lax.fori_loop(0, bb, seq_body, 0)

    yv = yt_s[...]
    mean = head_sum(yv) * (1.0 / TM_HEAD)
    cen = yv - mean
    var = head_sum(cen * cen) * (1.0 / TM_HEAD)
    y_tm = ((cen * lax.rsqrt(var + GN_EPS)) * gng_ref[...] + gnb_ref[...] + bonus) * gate_tm
    y_lru = _rms(h_s[...] * jax.nn.gelu(gate_lru), og_ref[...])
    y_ref[:, :, 0:D_TM] = y_tm.astype(y_ref.dtype).reshape(bb, chunk, D_TM)
    y_ref[:, :, D_TM:D_MODEL] = y_lru.astype(y_ref.dtype).reshape(bb, chunk, D_LRU)


def _const_spec(shape):
    zeros = (0,) * len(shape)
    return pl.BlockSpec(shape, lambda bi, ti: zeros, pipeline_mode=pl.Buffered(1))


def _mixer(x, st_shift, st_wkv, st_conv, st_h, weights, *, bb, chunk, valid, pos0):
    batch, length, _ = x.shape
    assert batch % bb == 0 and length % chunk == 0 and chunk % SUBLANES == 0
    assert chunk & (chunk - 1) == 0 and valid <= chunk
    assert valid == chunk or length == chunk
    rows = bb * chunk
    grid = (batch // bb, length // chunk)
    tile3 = lambda w: pl.BlockSpec((bb, SUBLANES, w), lambda bi, ti: (bi, 0, 0))
    wkv_spec = pl.BlockSpec((bb, TM_HEADS, TM_HEAD, TM_HEAD), lambda bi, ti: (bi, 0, 0, 0))
    in_specs = [pl.BlockSpec((bb, chunk, D_MODEL), lambda bi, ti: (bi, ti, 0)),
                tile3(D_TM_PROJ), wkv_spec, tile3(D_LRU), tile3(D_LRU)]
    in_specs += [_const_spec(w.shape) for w in weights]
    out_specs = [pl.BlockSpec((bb, chunk, D_MODEL), lambda bi, ti: (bi, ti, 0)),
                 tile3(D_TM_PROJ), wkv_spec, tile3(D_LRU), tile3(D_LRU)]
    y_dtype = BF16 if chunk % (2 * SUBLANES) == 0 else F32
    out_shape = [jax.ShapeDtypeStruct((batch, length, D_MODEL), y_dtype),
                 jax.ShapeDtypeStruct((batch, SUBLANES, D_TM_PROJ), F32),
                 jax.ShapeDtypeStruct((batch, TM_HEADS, TM_HEAD, TM_HEAD), F32),
                 jax.ShapeDtypeStruct((batch, SUBLANES, D_LRU), F32),
                 jax.ShapeDtypeStruct((batch, SUBLANES, D_LRU), F32)]
    sq = lambda w: pltpu.VMEM((rows, w), F32)
    scratch = [sq(D_IN_PROJ), sq(D_TM_PROJ)] + [sq(D_TM)] * 13
    return pl.pallas_call(
        functools.partial(_mixer_kernel, bb=bb, chunk=chunk, valid=valid, pos0=pos0),
        out_shape=out_shape, grid=grid, in_specs=in_specs, out_specs=out_specs,
        scratch_shapes=scratch, name="mixer",
        compiler_params=pltpu.CompilerParams(dimension_semantics=("arbitrary", "arbitrary"),
                                             vmem_limit_bytes=VMEM_LIMIT_BYTES),
    )(x, st_shift, st_wkv, st_conv, st_h, *weights)


def _ffn_kernel(x_ref, ym_ref, st_conv_ref,
                w_out_ref, g2_ref, w_up_ref, w_gate_ref, cw_ref, cb_ref, w_down_ref, gf_ref,
                y_ref, o_conv_ref,
                up_s, u1_s, u2_s,
                *, bb, chunk):
    rows = bb * chunk
    ti = pl.program_id(1)

    @pl.when(ti == 0)
    def _():
        o_conv_ref[...] = st_conv_ref[...]

    x = x_ref[...].reshape(rows, D_MODEL)
    ym = ym_ref[...].reshape(rows, D_MODEL)
    x1 = x + _dot(ym, w_out_ref[...])
    xn = _rms(x1, g2_ref[...]).astype(BF16)
    row8 = lax.broadcasted_iota(jnp.int32, (SUBLANES, 1), 0)
    acc = x1
    for n in range(D_FF // FF_COL_TILE):
        cols = slice(n * FF_COL_TILE, (n + 1) * FF_COL_TILE)
        up = jnp.dot(xn, w_up_ref[:, cols], preferred_element_type=F32)
        gate = jnp.dot(xn, w_gate_ref[:, cols], preferred_element_type=F32)
        up_s[...] = up
        u1_s[...] = pltpu.roll(up, 1, axis=0)
        u2_s[...] = pltpu.roll(up, 2, axis=0)

        def patch_body(b, carry):
            r0 = pl.multiple_of(b * chunk, SUBLANES)
            head = pl.ds(r0, SUBLANES)
            tail = pl.ds(r0 + chunk - SUBLANES, SUBLANES)
            tile = o_conv_ref[b, :, cols]
            u1_s[head, :] = jnp.where(row8 < 1, pltpu.roll(tile, 1, axis=0), u1_s[head, :])
            u2_s[head, :] = jnp.where(row8 < 2, pltpu.roll(tile, 2, axis=0), u2_s[head, :])
            o_conv_ref[b, :, cols] = up_s[tail, :]
            return carry

        lax.fori_loop(0, bb, patch_body, 0)
        upc = cb_ref[:, cols] + u2_s[...] * cw_ref[0:1, cols]
        upc = upc + u1_s[...] * cw_ref[1:2, cols]
        upc = upc + up * cw_ref[2:3, cols]
        hid = (jax.nn.gelu(upc) * gate).astype(BF16)
        acc = acc + jnp.dot(hid, w_down_ref[cols, :], preferred_element_type=F32)
    y_ref[...] = _rms(acc, gf_ref[...]).reshape(bb, chunk, D_MODEL)


def _ffn(x, ym, st_conv, weights, *, bb, chunk):
    batch, length, _ = x.shape
    assert batch % bb == 0 and length % chunk == 0 and chunk % SUBLANES == 0
    rows = bb * chunk
    grid = (batch // bb, length // chunk)
    act = pl.BlockSpec((bb, chunk, D_MODEL), lambda bi, ti: (bi, ti, 0))
    tile = pl.BlockSpec((bb, SUBLANES, D_FF), lambda bi, ti: (bi, 0, 0))
    in_specs = [act, act, tile] + [_const_spec(w.shape) for w in weights]
    out_shape = [jax.ShapeDtypeStruct((batch, length, D_MODEL), F32),
                 jax.ShapeDtypeStruct((batch, SUBLANES, D_FF), F32)]
    scratch = [pltpu.VMEM((rows, FF_COL_TILE), F32)] * 3
    return pl.pallas_call(
        functools.partial(_ffn_kernel, bb=bb, chunk=chunk),
        out_shape=out_shape, grid=grid, in_specs=in_specs, out_specs=[act, tile],
        scratch_shapes=scratch, name="ffn",
        compiler_params=pltpu.CompilerParams(dimension_semantics=("arbitrary", "arbitrary"),
                                             vmem_limit_bytes=VMEM_LIMIT_BYTES),
    )(x, ym, st_conv, *weights)


def _row(v):
    return v.reshape(1, -1).astype(F32)


def _block_diag(w):
    nb, n, _ = w.shape
    eye = jnp.eye(nb, dtype=w.dtype)
    return (eye[:, None, :, None] * w[:, :, None, :]).reshape(nb * n, nb * n)


def _tail_tile(state, n_rows):
    return jnp.pad(state.astype(F32), ((0, 0), (SUBLANES - n_rows, 0), (0, 0)))


def _layer(x, states, mixer_w, ffn_w, *, mixer_bb, ffn_bb, mixer_chunk, ffn_chunk, valid, pos0):
    st_shift, st_wkv, st_conv, st_h, st_fconv = states
    ym, o_shift, o_wkv, o_conv, o_h = _mixer(x, st_shift, st_wkv, st_conv, st_h, mixer_w,
                                             bb=mixer_bb, chunk=mixer_chunk, valid=valid, pos0=pos0)
    y, o_fconv = _ffn(x, ym, st_fconv, ffn_w, bb=ffn_bb, chunk=ffn_chunk)
    return y, (o_shift, o_wkv, o_conv, o_h, o_fconv)


def kernel(x_prompt, x_sample, state_tm_shift, state_tm_wkv, state_lru_conv, state_lru_h, state_ffn_conv, meta_tokens, norm1_g, w_in, tm_mu, tm_w0, tm_w_up, tm_a0, tm_a_up, tm_g_up, tm_k_k, tm_k_a, tm_r_k, tm_gn_g, tm_gn_b, lru_conv_w, lru_conv_b, lru_wa, lru_ba, lru_wx, lru_bx, lru_lambda, lru_out_g, w_out, norm2_g, ffn_w_up, ffn_w_gate, ffn_conv_w, ffn_conv_b, ffn_w_down, norm_f_g):
    depth = w_in.shape[0]
    assert depth == 1
    l = 0
    zeros_lora = jnp.zeros((DECAY_RANK, D_TM), F32)
    head_id = jnp.arange(D_TM) // TM_HEAD
    seg = (head_id[:, None] == jnp.arange(LANES)[None, :]).astype(BF16)
    mixer_w = (
        _row(norm1_g[l]), w_in[l].astype(BF16), _row(tm_mu[l]), _row(tm_w0[l]),
        jnp.concatenate([tm_w_up[l], zeros_lora], axis=0).astype(BF16),
        _row(tm_a0[l]),
        jnp.concatenate([zeros_lora, tm_a_up[l]], axis=0).astype(BF16),
        tm_g_up[l].astype(BF16),
        _row(tm_k_k[l]), _row(tm_k_a[l]), _row(tm_r_k[l]), _row(tm_gn_g[l]), _row(tm_gn_b[l]),
        lru_conv_w[l].astype(F32), _row(lru_conv_b[l]),
        _block_diag(lru_wa[l]).astype(BF16), _row(lru_ba[l]),
        _block_diag(lru_wx[l]).astype(BF16), _row(lru_bx[l]),
        _row(lru_lambda[l]), _row(lru_out_g[l]),
        seg, seg.T,
    )
    ffn_w = (
        w_out[l].astype(BF16), _row(norm2_g[l]), ffn_w_up[l].astype(BF16), ffn_w_gate[l].astype(BF16),
        ffn_conv_w[l].astype(F32), _row(ffn_conv_b[l]), ffn_w_down[l].astype(BF16), _row(norm_f_g),
    )

    zero_states = (jnp.zeros((1, SUBLANES, D_TM_PROJ), F32),
                   jnp.zeros((1, TM_HEADS, TM_HEAD, TM_HEAD), F32),
                   jnp.zeros((1, SUBLANES, D_LRU), F32),
                   jnp.zeros((1, SUBLANES, D_LRU), F32),
                   jnp.zeros((1, SUBLANES, D_FF), F32))
    _, meta_states = _layer(meta_tokens[None].astype(F32), zero_states, mixer_w, ffn_w,
                            mixer_bb=1, ffn_bb=1, mixer_chunk=N_META, ffn_chunk=N_META,
                            valid=N_META, pos0=0)

    bsz, seq = x_prompt.shape[0], x_prompt.shape[1]
    p_init = tuple(jnp.broadcast_to(s, (bsz,) + s.shape[1:]) for s in meta_states)
    y_prompt, p_st = _layer(x_prompt, p_init, mixer_w, ffn_w,
                            mixer_bb=bsz, ffn_bb=1, mixer_chunk=64, ffn_chunk=512,
                            valid=64, pos0=N_META)

    dec_b, dec_seq = x_sample.shape[0], x_sample.shape[1]
    pad_seq = SUBLANES
    xs = jnp.pad(x_sample, ((0, 0), (0, pad_seq - dec_seq), (0, 0)))
    s_init = (_tail_tile(state_tm_shift[l][:, None, :], 1), state_tm_wkv[l].astype(F32),
              _tail_tile(state_lru_conv[l], LRU_CONV_W - 1), _tail_tile(state_lru_h[l][:, None, :], 1),
              _tail_tile(state_ffn_conv[l], FFN_CONV_W - 1))
    y_s, s_st = _layer(xs, s_init, mixer_w, ffn_w,
                       mixer_bb=16, ffn_bb=16, mixer_chunk=pad_seq, ffn_chunk=pad_seq,
                       valid=dec_seq, pos0=16384)
    y_sample = y_s[:, :dec_seq]

    def unpack(st, last):
        o_shift, o_wkv, o_conv, o_h, o_fconv = st
        return (o_shift[:, last][None], o_wkv[None],
                o_conv[:, last - (LRU_CONV_W - 2):last + 1][None], o_h[:, last][None],
                o_fconv[:, last - (FFN_CONV_W - 2):last + 1][None])

    p_out = unpack(p_st, SUBLANES - 1)
    s_out = unpack(s_st, dec_seq - 1)
    return (y_prompt, y_sample) + p_out + s_out
```

```python
import functools

import jax
import jax.numpy as jnp
from jax import lax
from jax.experimental import pallas as pl
from jax.experimental.pallas import tpu as pltpu

F32 = jnp.float32
BF16 = jnp.bfloat16

D_MODEL = 1024
N_META = 16
D_TM = 512
TM_HEAD = 64
TM_HEADS = 8
DECAY_RANK = 64
AAA_RANK = 64
GATE_RANK = 128
D_TM_PROJ = 3 * D_TM + DECAY_RANK + AAA_RANK + GATE_RANK
D_LRU = 512
LRU_BLOCKS = 8
LRU_BLOCK = 64
LRU_CONV_W = 4
LRU_C = 8.0
D_IN_PROJ = D_TM_PROJ + 2 * D_LRU
D_FF = 3 * D_MODEL
FFN_CONV_W = 3
EPS = 1e-6
GN_EPS = 64e-5

SUBLANES = 8
LANES = 128
LORA_OFF = 3 * D_TM
GATE_OFF = LORA_OFF + DECAY_RANK + AAA_RANK
FF_COL_TILE = 1024
VMEM_LIMIT_BYTES = 60 * 1024 * 1024


def _dot(a, b):
    return jnp.dot(a.astype(BF16), b.astype(BF16), preferred_element_type=F32)


def _dot_nt(a, b):
    return lax.dot_general(a.astype(BF16), b.astype(BF16), (((1,), (1,)), ((), ())),
                           preferred_element_type=F32)


def _dot_tn(a, b):
    return lax.dot_general(a.astype(BF16), b.astype(BF16), (((0,), (0,)), ((), ())),
                           preferred_element_type=F32)


def _bdot(a, b):
    return lax.dot_general(a.astype(BF16), b.astype(BF16), (((2,), (1,)), ((0,), (0,))),
                           preferred_element_type=F32)


def _bdot_nt(a, b):
    return lax.dot_general(a.astype(BF16), b.astype(BF16), (((2,), (2,)), ((0,), (0,))),
                           preferred_element_type=F32)


def _bdot_tn(a, b):
    return lax.dot_general(a.astype(BF16), b.astype(BF16), (((1,), (1,)), ((0,), (0,))),
                           preferred_element_type=F32)


def _heads(x):
    return jnp.stack([x[:, hd * TM_HEAD:(hd + 1) * TM_HEAD] for hd in range(TM_HEADS)])


def _rms(x, g):
    return x * lax.rsqrt(jnp.mean(x * x, axis=-1, keepdims=True) + EPS) * g


def _softplus(z):
    return jnp.maximum(z, 0.0) + jnp.log(1.0 + jnp.exp(-jnp.abs(z)))


def _sigmoid(z):
    return 1.0 / (1.0 + jnp.exp(-z))


def _time_index(rows, cols, chunk):
    return lax.broadcasted_iota(jnp.int32, (rows, cols), 0) & (chunk - 1)


def _shift_rows(x, tloc, j, fill):
    return jnp.where(tloc >= j, pltpu.roll(x, j, axis=0), fill)


def _mixer_kernel(x_ref, st_shift_ref, st_wkv_ref, st_conv_ref, st_h_ref,
                  g1_ref, w_in_ref, mu_ref, w0_ref, wdec_ref, a0_ref, waaa_ref, wgate_ref,
                  kk_ref, ka_ref, rk_ref, gng_ref, gnb_ref,
                  cw_ref, cb_ref, wa_ref, ba_ref, wx_ref, bx_ref, lam_ref, og_ref,
                  seg_ref, segt_ref,
                  y_ref, o_shift_ref, o_wkv_ref, o_conv_ref, o_h_ref,
                  u_s, prev_s, xs1_s, xs2_s, xs3_s, kap_s, rt_s, bt_s, kt_s, v_s, c_s, yt_s,
                  la_s, lb_s, h_s,
                  *, bb, chunk, valid, pos0, group):
    rows = bb * chunk
    ti = pl.program_id(1)

    @pl.when(ti == 0)
    def _():
        o_shift_ref[...] = st_shift_ref[...]
        o_wkv_ref[...] = st_wkv_ref[...]
        o_conv_ref[...] = st_conv_ref[...]
        o_h_ref[...] = st_h_ref[...]

    seg = seg_ref[...]
    segt = segt_ref[...]

    def head_sum(x):
        xh = x.astype(BF16)
        xl = (x - xh.astype(F32)).astype(BF16)
        s = (jnp.dot(xh, seg, preferred_element_type=F32) + jnp.dot(xl, seg, preferred_element_type=F32))
        sh = s.astype(BF16)
        sl = (s - sh.astype(F32)).astype(BF16)
        return (jnp.dot(sh, segt, preferred_element_type=F32) + jnp.dot(sl, segt, preferred_element_type=F32))

    x = x_ref[...].reshape(rows, D_MODEL)
    u_s[...] = _dot(_rms(x, g1_ref[...]), w_in_ref[...])

    prev_s[...] = pltpu.roll(u_s[:, :D_TM_PROJ], 1, axis=0)
    xb_all = u_s[:, D_TM_PROJ:D_TM_PROJ + D_LRU]
    xs1_s[...] = pltpu.roll(xb_all, 1, axis=0)
    xs2_s[...] = pltpu.roll(xb_all, 2, axis=0)
    xs3_s[...] = pltpu.roll(xb_all, 3, axis=0)
    row8 = lax.broadcasted_iota(jnp.int32, (SUBLANES, 1), 0)

    def patch_body(b, carry):
        r0 = pl.multiple_of(b * chunk, SUBLANES)
        head = pl.ds(r0, SUBLANES)
        tail = pl.ds(r0 + chunk - SUBLANES, SUBLANES)
        prev_s[head, :] = jnp.where(row8 < 1, pltpu.roll(o_shift_ref[b], 1, axis=0), prev_s[head, :])
        conv_tile = o_conv_ref[b]
        xs1_s[head, :] = jnp.where(row8 < 1, pltpu.roll(conv_tile, 1, axis=0), xs1_s[head, :])
        xs2_s[head, :] = jnp.where(row8 < 2, pltpu.roll(conv_tile, 2, axis=0), xs2_s[head, :])
        xs3_s[head, :] = jnp.where(row8 < 3, pltpu.roll(conv_tile, 3, axis=0), xs3_s[head, :])
        o_shift_ref[b] = u_s[tail, :D_TM_PROJ]
        o_conv_ref[b] = u_s[tail, D_TM_PROJ:D_TM_PROJ + D_LRU]
        return carry

    lax.fori_loop(0, bb, patch_body, 0)

    tloc = _time_index(rows, D_TM, chunk)

    u_tm = u_s[:, :D_TM_PROJ]
    um = u_tm + (prev_s[...] - u_tm) * mu_ref[...]
    r = um[:, 0:D_TM]
    k = um[:, D_TM:2 * D_TM]
    v = um[:, 2 * D_TM:3 * D_TM]
    x_lora = um[:, LORA_OFF:GATE_OFF]
    x_gate = um[:, GATE_OFF:D_TM_PROJ]
    w_log = -_softplus(-(w0_ref[...] + _dot(jnp.tanh(x_lora), wdec_ref[...]))) - 0.5
    log_decay = -jnp.exp(w_log)
    a = _sigmoid(a0_ref[...] + _dot(x_lora, waaa_ref[...]))
    gate_tm = _dot(_sigmoid(x_gate), wgate_ref[...])
    kk = k * kk_ref[...]
    k = k * (1.0 + (a - 1.0) * ka_ref[...])
    kk = kk * lax.rsqrt(jnp.maximum(head_sum(kk * kk), 1e-24))
    bonus = head_sum(r * k * rk_ref[...]) * v
    if valid < chunk:
        live = tloc < valid
        log_decay = jnp.where(live, log_decay, 0.0)
        kk = jnp.where(live, kk, 0.0)
        k = jnp.where(live, k, 0.0)
    c = log_decay
    s = 1
    while s < chunk:
        c = c + _shift_rows(c, tloc, s, 0.0)
        s *= 2
    c_s[...] = c
    kap_s[...] = kk * jnp.exp(c - log_decay)
    rt_s[...] = r * jnp.exp(c)
    inv_p = jnp.exp(-c)
    bt_s[...] = kk * a * inv_p
    kt_s[...] = k * inv_p
    v_s[...] = v

    xb = u_s[:, D_TM_PROJ:D_TM_PROJ + D_LRU]
    gate_lru = u_s[:, D_TM_PROJ + D_LRU:D_IN_PROJ]
    xc = cb_ref[...] + xs3_s[...] * cw_ref[0:1, :]
    xc = xc + xs2_s[...] * cw_ref[1:2, :]
    xc = xc + xs1_s[...] * cw_ref[2:3, :]
    xc = xc + xb * cw_ref[3:4, :]
    r_g = _sigmoid(_dot(xc, wa_ref[...]) + ba_ref[...])
    i_g = _sigmoid(_dot(xc, wx_ref[...]) + bx_ref[...])
    log_a = -LRU_C * r_g * _softplus(-lam_ref[...])
    pos = pos0 + ti * chunk + tloc
    mult = jnp.where(pos == 0, 1.0, jnp.sqrt(1.0 - jnp.exp(2.0 * log_a)))
    la = jnp.exp(log_a)
    lb = xc * i_g * mult
    s = 1
    while s < chunk:
        la_sh = _shift_rows(la, tloc, s, 1.0)
        lb_sh = _shift_rows(lb, tloc, s, 0.0)
        lb = la * lb_sh + lb
        la = la * la_sh
        s *= 2
    la_s[...] = la
    lb_s[...] = lb

    n_lv = max(1, (chunk - 1).bit_length())
    ri = lax.broadcasted_iota(jnp.int32, (chunk, chunk), 0)
    ci = lax.broadcasted_iota(jnp.int32, (chunk, chunk), 1)
    strict = ri > ci
    incl = ri >= ci
    eye = (ri == ci).astype(F32)

    def seq_body(i, carry):
        seqs = [i * group + j for j in range(group)]
        rss = [pl.ds(pl.multiple_of(b * chunk, SUBLANES), chunk) for b in seqs]
        tails = [pl.ds(pl.multiple_of(b * chunk, SUBLANES) + chunk - SUBLANES, SUBLANES) for b in seqs]
        for b, rs, tail in zip(seqs, rss, tails):
            h0 = o_h_ref[b][SUBLANES - 1:SUBLANES, :]
            h_s[rs, :] = la_s[rs, :] * h0 + lb_s[rs, :]
            o_h_ref[b] = h_s[tail, :]
        gather = lambda ref: jnp.concatenate([_heads(ref[rs, :]) for rs in rss], axis=0)
        p_end = jnp.concatenate(
            [_heads(jnp.exp(c_s[tail, :][SUBLANES - 1:SUBLANES, :])) for tail in tails], axis=0)
        kap = gather(kap_s)
        rt = gather(rt_s)
        bt = gather(bt_s)
        kt = gather(kt_s)
        vv = gather(v_s)
        s0 = jnp.concatenate([o_wkv_ref[b] for b in seqs], axis=0)
        lhs = jnp.concatenate([kap, rt], axis=1)
        g_b = _bdot_nt(lhs, bt)
        g_k = _bdot_nt(lhs, kt)
        m_ab = jnp.where(strict, g_b[:, :chunk], 0.0)
        m_ak = jnp.where(strict, g_k[:, :chunk], 0.0)
        m_rb = jnp.where(incl, g_b[:, chunk:], 0.0)
        m_rk = jnp.where(incl, g_k[:, chunk:], 0.0)
        t_inv = eye - m_ab
        m_pow = _bdot(m_ab, m_ab)
        for lv in range(1, n_lv):
            if lv < n_lv - 1:
                prod = _bdot(jnp.concatenate([t_inv, m_pow], axis=1), m_pow)
                t_inv = t_inv + prod[:, :chunk]
                m_pow = prod[:, chunk:]
            else:
                t_inv = t_inv + _bdot(t_inv, m_pow)
        xv = _bdot(jnp.concatenate([m_ak, m_rk], axis=1), vv)
        z = _bdot_nt(lhs, s0)
        uu = -_bdot(t_inv, z[:, :chunk] + xv[:, :chunk])
        yh = z[:, chunk:] + _bdot(m_rb, uu) + xv[:, chunk:]
        upd = _bdot_tn(jnp.concatenate([uu, vv], axis=1), jnp.concatenate([bt, kt], axis=1))
        s_new = (s0 + upd) * p_end
        for j, (b, rs) in enumerate(zip(seqs, rss)):
            o_wkv_ref[b] = s_new[j * TM_HEADS:(j + 1) * TM_HEADS]
            for hd in range(TM_HEADS):
                yt_s[rs, hd * TM_HEAD:(hd + 1) * TM_HEAD] = yh[j * TM_HEADS + hd]
        return carry

    lax.fori_loop(0, bb // group, seq_body, 0)

    yv = yt_s[...]
    mean = head_sum(yv) * (1.0 / TM_HEAD)
    cen = yv - mean
    var = head_sum(cen * cen) * (1.0 / TM_HEAD)
    y_tm = ((cen * lax.rsqrt(var + GN_EPS)) * gng_ref[...] + gnb_ref[...] + bonus) * gate_tm
    y_lru = _rms(h_s[...] * jax.nn.gelu(gate_lru), og_ref[...])
    y_ref[:, :, 0:D_TM] = y_tm.astype(y_ref.dtype).reshape(bb, chunk, D_TM)
    y_ref[:, :, D_TM:D_MODEL] = y_lru.astype(y_ref.dtype).reshape(bb, chunk, D_LRU)


def _const_spec(shape):
    zeros = (0,) * len(shape)
    return pl.BlockSpec(shape, lambda bi, ti: zeros, pipeline_mode=pl.Buffered(1))


def _mixer(x, st_shift, st_wkv, st_conv, st_h, weights, *, bb, chunk, valid, pos0, group):
    batch, length, _ = x.shape
    assert batch % bb == 0 and length % chunk == 0 and chunk % SUBLANES == 0 and bb % group == 0
    assert chunk & (chunk - 1) == 0 and valid <= chunk
    assert valid == chunk or length == chunk
    rows = bb * chunk
    grid = (batch // bb, length // chunk)
    tile3 = lambda w: pl.BlockSpec((bb, SUBLANES, w), lambda bi, ti: (bi, 0, 0))
    wkv_spec = pl.BlockSpec((bb, TM_HEADS, TM_HEAD, TM_HEAD), lambda bi, ti: (bi, 0, 0, 0))
    in_specs = [pl.BlockSpec((bb, chunk, D_MODEL), lambda bi, ti: (bi, ti, 0)),
                tile3(D_TM_PROJ), wkv_spec, tile3(D_LRU), tile3(D_LRU)]
    in_specs += [_const_spec(w.shape) for w in weights]
    out_specs = [pl.BlockSpec((bb, chunk, D_MODEL), lambda bi, ti: (bi, ti, 0)),
                 tile3(D_TM_PROJ), wkv_spec, tile3(D_LRU), tile3(D_LRU)]
    y_dtype = BF16 if chunk % (2 * SUBLANES) == 0 else F32
    out_shape = [jax.ShapeDtypeStruct((batch, length, D_MODEL), y_dtype),
                 jax.ShapeDtypeStruct((batch, SUBLANES, D_TM_PROJ), F32),
                 jax.ShapeDtypeStruct((batch, TM_HEADS, TM_HEAD, TM_HEAD), F32),
                 jax.ShapeDtypeStruct((batch, SUBLANES, D_LRU), F32),
                 jax.ShapeDtypeStruct((batch, SUBLANES, D_LRU), F32)]
    sq = lambda w: pltpu.VMEM((rows, w), F32)
    scratch = [sq(D_IN_PROJ), sq(D_TM_PROJ)] + [sq(D_TM)] * 13
    return pl.pallas_call(
        functools.partial(_mixer_kernel, bb=bb, chunk=chunk, valid=valid, pos0=pos0, group=group),
        out_shape=out_shape, grid=grid, in_specs=in_specs, out_specs=out_specs,
        scratch_shapes=scratch, name="mixer",
        compiler_params=pltpu.CompilerParams(dimension_semantics=("arbitrary", "arbitrary"),
                                             vmem_limit_bytes=VMEM_LIMIT_BYTES),
    )(x, st_shift, st_wkv, st_conv, st_h, *weights)


def _ffn_kernel(x_ref, ym_ref, st_conv_ref,
                w_out_ref, g2_ref, w_up_ref, w_gate_ref, cw_ref, cb_ref, w_down_ref, gf_ref,
                y_ref, o_conv_ref,
                up_s, u1_s, u2_s,
                *, bb, chunk):
    rows = bb * chunk
    ti = pl.program_id(1)

    @pl.when(ti == 0)
    def _():
        o_conv_ref[...] = st_conv_ref[...]

    x = x_ref[...].reshape(rows, D_MODEL)
    ym = ym_ref[...].reshape(rows, D_MODEL)
    x1 = x + _dot(ym, w_out_ref[...])
    xn = _rms(x1, g2_ref[...]).astype(BF16)
    row8 = lax.broadcasted_iota(jnp.int32, (SUBLANES, 1), 0)
    acc = x1
    for n in range(D_FF // FF_COL_TILE):
        cols = slice(n * FF_COL_TILE, (n + 1) * FF_COL_TILE)
        up = jnp.dot(xn, w_up_ref[:, cols], preferred_element_type=F32)
        gate = jnp.dot(xn, w_gate_ref[:, cols], preferred_element_type=F32)
        up_s[...] = up
        u1_s[...] = pltpu.roll(up, 1, axis=0)
        u2_s[...] = pltpu.roll(up, 2, axis=0)

        def patch_body(b, carry):
            r0 = pl.multiple_of(b * chunk, SUBLANES)
            head = pl.ds(r0, SUBLANES)
            tail = pl.ds(r0 + chunk - SUBLANES, SUBLANES)
            tile = o_conv_ref[b, :, cols]
            u1_s[head, :] = jnp.where(row8 < 1, pltpu.roll(tile, 1, axis=0), u1_s[head, :])
            u2_s[head, :] = jnp.where(row8 < 2, pltpu.roll(tile, 2, axis=0), u2_s[head, :])
            o_conv_ref[b, :, cols] = up_s[tail, :]
            return carry

        lax.fori_loop(0, bb, patch_body, 0)
        upc = cb_ref[:, cols] + u2_s[...] * cw_ref[0:1, cols]
        upc = upc + u1_s[...] * cw_ref[1:2, cols]
        upc = upc + up * cw_ref[2:3, cols]
        hid = (jax.nn.gelu(upc) * gate).astype(BF16)
        acc = acc + jnp.dot(hid, w_down_ref[cols, :], preferred_element_type=F32)
    y_ref[...] = _rms(acc, gf_ref[...]).reshape(bb, chunk, D_MODEL)


def _ffn(x, ym, st_conv, weights, *, bb, chunk):
    batch, length, _ = x.shape
    assert batch % bb == 0 and length % chunk == 0 and chunk % SUBLANES == 0
    rows = bb * chunk
    grid = (batch // bb, length // chunk)
    act = pl.BlockSpec((bb, chunk, D_MODEL), lambda bi, ti: (bi, ti, 0))
    tile = pl.BlockSpec((bb, SUBLANES, D_FF), lambda bi, ti: (bi, 0, 0))
    in_specs = [act, act, tile] + [_const_spec(w.shape) for w in weights]
    out_shape = [jax.ShapeDtypeStruct((batch, length, D_MODEL), F32),
                 jax.ShapeDtypeStruct((batch, SUBLANES, D_FF), F32)]
    scratch = [pltpu.VMEM((rows, FF_COL_TILE), F32)] * 3
    return pl.pallas_call(
        functools.partial(_ffn_kernel, bb=bb, chunk=chunk),
        out_shape=out_shape, grid=grid, in_specs=in_specs, out_specs=[act, tile],
        scratch_shapes=scratch, name="ffn",
        compiler_params=pltpu.CompilerParams(dimension_semantics=("arbitrary", "arbitrary"),
                                             vmem_limit_bytes=VMEM_LIMIT_BYTES),
    )(x, ym, st_conv, *weights)


def _row(v):
    return v.reshape(1, -1).astype(F32)


def _block_diag(w):
    nb, n, _ = w.shape
    eye = jnp.eye(nb, dtype=w.dtype)
    return (eye[:, None, :, None] * w[:, :, None, :]).reshape(nb * n, nb * n)


def _tail_tile(state, n_rows):
    return jnp.pad(state.astype(F32), ((0, 0), (SUBLANES - n_rows, 0), (0, 0)))


def _layer(x, states, mixer_w, ffn_w, *, mixer_bb, ffn_bb, mixer_chunk, ffn_chunk, valid, pos0, group):
    st_shift, st_wkv, st_conv, st_h, st_fconv = states
    ym, o_shift, o_wkv, o_conv, o_h = _mixer(x, st_shift, st_wkv, st_conv, st_h, mixer_w,
                                             bb=mixer_bb, chunk=mixer_chunk, valid=valid, pos0=pos0,
                                             group=group)
    y, o_fconv = _ffn(x, ym, st_fconv, ffn_w, bb=ffn_bb, chunk=ffn_chunk)
    return y, (o_shift, o_wkv, o_conv, o_h, o_fconv)


def kernel(x_prompt, x_sample, state_tm_shift, state_tm_wkv, state_lru_conv, state_lru_h, state_ffn_conv, meta_tokens, norm1_g, w_in, tm_mu, tm_w0, tm_w_up, tm_a0, tm_a_up, tm_g_up, tm_k_k, tm_k_a, tm_r_k, tm_gn_g, tm_gn_b, lru_conv_w, lru_conv_b, lru_wa, lru_ba, lru_wx, lru_bx, lru_lambda, lru_out_g, w_out, norm2_g, ffn_w_up, ffn_w_gate, ffn_conv_w, ffn_conv_b, ffn_w_down, norm_f_g):
    depth = w_in.shape[0]
    assert depth == 1
    l = 0
    zeros_lora = jnp.zeros((DECAY_RANK, D_TM), F32)
    head_id = jnp.arange(D_TM) // TM_HEAD
    seg = (head_id[:, None] == jnp.arange(LANES)[None, :]).astype(BF16)
    mixer_w = (
        _row(norm1_g[l]), w_in[l].astype(BF16), _row(tm_mu[l]), _row(tm_w0[l]),
        jnp.concatenate([tm_w_up[l], zeros_lora], axis=0).astype(BF16),
        _row(tm_a0[l]),
        jnp.concatenate([zeros_lora, tm_a_up[l]], axis=0).astype(BF16),
        tm_g_up[l].astype(BF16),
        _row(tm_k_k[l]), _row(tm_k_a[l]), _row(tm_r_k[l]), _row(tm_gn_g[l]), _row(tm_gn_b[l]),
        lru_conv_w[l].astype(F32), _row(lru_conv_b[l]),
        _block_diag(lru_wa[l]).astype(BF16), _row(lru_ba[l]),
        _block_diag(lru_wx[l]).astype(BF16), _row(lru_bx[l]),
        _row(lru_lambda[l]), _row(lru_out_g[l]),
        seg, seg.T,
    )
    ffn_w = (
        w_out[l].astype(BF16), _row(norm2_g[l]), ffn_w_up[l].astype(BF16), ffn_w_gate[l].astype(BF16),
        ffn_conv_w[l].astype(F32), _row(ffn_conv_b[l]), ffn_w_down[l].astype(BF16), _row(norm_f_g),
    )

    zero_states = (jnp.zeros((1, SUBLANES, D_TM_PROJ), F32),
                   jnp.zeros((1, TM_HEADS, TM_HEAD, TM_HEAD), F32),
                   jnp.zeros((1, SUBLANES, D_LRU), F32),
                   jnp.zeros((1, SUBLANES, D_LRU), F32),
                   jnp.zeros((1, SUBLANES, D_FF), F32))
    _, meta_states = _layer(meta_tokens[None].astype(F32), zero_states, mixer_w, ffn_w,
                            mixer_bb=1, ffn_bb=1, mixer_chunk=N_META, ffn_chunk=N_META,
                            valid=N_META, pos0=0, group=1)

    bsz, seq = x_prompt.shape[0], x_prompt.shape[1]
    p_init = tuple(jnp.broadcast_to(s, (bsz,) + s.shape[1:]) for s in meta_states)
    y_prompt, p_st = _layer(x_prompt, p_init, mixer_w, ffn_w,
                            mixer_bb=bsz, ffn_bb=1, mixer_chunk=64, ffn_chunk=512,
                            valid=64, pos0=N_META, group=2)

    dec_b, dec_seq = x_sample.shape[0], x_sample.shape[1]
    pad_seq = SUBLANES
    xs = jnp.pad(x_sample, ((0, 0), (0, pad_seq - dec_seq), (0, 0)))
    s_init = (_tail_tile(state_tm_shift[l][:, None, :], 1), state_tm_wkv[l].astype(F32),
              _tail_tile(state_lru_conv[l], LRU_CONV_W - 1), _tail_tile(state_lru_h[l][:, None, :], 1),
              _tail_tile(state_ffn_conv[l], FFN_CONV_W - 1))
    y_s, s_st = _layer(xs, s_init, mixer_w, ffn_w,
                       mixer_bb=16, ffn_bb=16, mixer_chunk=pad_seq, ffn_chunk=pad_seq,
                       valid=dec_seq, pos0=16384, group=4)
    y_sample = y_s[:, :dec_seq]

    def unpack(st, last):
        o_shift, o_wkv, o_conv, o_h, o_fconv = st
        return (o_shift[:, last][None], o_wkv[None],
                o_conv[:, last - (LRU_CONV_W - 2):last + 1][None], o_h[:, last][None],
                o_fconv[:, last - (FFN_CONV_W - 2):last + 1][None])

    p_out = unpack(p_st, SUBLANES - 1)
    s_out = unpack(s_st, dec_seq - 1)
    return (y_prompt, y_sample) + p_out + s_out
```

```python
import functools
import math

import jax
import jax.numpy as jnp
from jax import lax
from jax.experimental import pallas as pl
from jax.experimental.pallas import tpu as pltpu

F32 = jnp.float32
BF16 = jnp.bfloat16

D_MODEL = 1024
N_META = 16
PAST_LEN = 16384
D_TM = 512
TM_HEAD = 64
TM_HEADS = 8
DECAY_RANK = 64
AAA_RANK = 64
GATE_RANK = 128
D_TM_PROJ = 3 * D_TM + DECAY_RANK + AAA_RANK + GATE_RANK
D_LRU = 512
LRU_CONV_W = 4
LRU_C = 8.0
D_IN_PROJ = D_TM_PROJ + 2 * D_LRU
D_FF = 3 * D_MODEL
FFN_CONV_W = 3
EPS = 1e-6
GN_EPS = 64e-5

SUBLANES = 8
LANES = 128
PAIRS = TM_HEADS // 2
LORA_OFF = 3 * D_TM
GATE_OFF = LORA_OFF + DECAY_RANK + AAA_RANK
FF_COL_TILE = 1024
VMEM_LIMIT_BYTES = 60 * 1024 * 1024


def _dot(a, b):
    return jnp.dot(a.astype(BF16), b.astype(BF16), preferred_element_type=F32)


def _bdot(a, b):
    return lax.dot_general(a.astype(BF16), b.astype(BF16), (((2,), (1,)), ((0,), (0,))),
                           preferred_element_type=F32)


def _bdot_nt(a, b):
    return lax.dot_general(a.astype(BF16), b.astype(BF16), (((2,), (2,)), ((0,), (0,))),
                           preferred_element_type=F32)


def _bdot_tn(a, b):
    return lax.dot_general(a.astype(BF16), b.astype(BF16), (((1,), (1,)), ((0,), (0,))),
                           preferred_element_type=F32)


def _rms(x, g):
    return x * lax.rsqrt(jnp.mean(x * x, axis=-1, keepdims=True) + EPS) * g


def _sigmoid(z):
    return 0.5 * jnp.tanh(0.5 * z) + 0.5


def _softplus(z):
    return jnp.maximum(z, 0.0) + jnp.log(1.0 + jnp.exp(-jnp.abs(z)))


def _block_diag_rows(x, left):
    return jnp.concatenate([jnp.where(left, x, 0.0), jnp.where(left, 0.0, x)], axis=1)


def _mixer_kernel(x_ref, st_shift_ref, st_wkv_ref, st_conv_ref, st_h_ref,
                  g1_ref, w_in_ref, mu_ref, w0_ref, wdec_ref, a0_ref, waaa_ref, wgate_ref,
                  kk_ref, ka_ref, rk_ref, gng_ref, gnb_ref,
                  cw_ref, cb_ref, wa_ref, ba_ref, wx_ref, bx_ref, lam_ref, og_ref,
                  seg_ref, segt_ref,
                  y_ref, o_shift_ref, o_wkv_ref, o_conv_ref, o_h_ref,
                  tm_ext, xb_ext, kap_s, rt_s, bt_s, kt_s, v_s, yt_s, pe_s,
                  *, bb, chunk, valid, pos0, group):
    rows = bb * chunk
    conv_rows = (LRU_CONV_W - 1) * bb
    ti = pl.program_id(1)

    @pl.when(ti == 0)
    def _():
        tm_ext[0:bb, :] = st_shift_ref[...]
        xb_ext[0:conv_rows, :] = st_conv_ref[...].reshape(conv_rows, D_LRU)
        o_wkv_ref[...] = st_wkv_ref[...]
        o_h_ref[...] = st_h_ref[...]

    seg = seg_ref[...]
    segt = segt_ref[...]

    def head_sum(x):
        s = jnp.dot(x.astype(BF16), seg, preferred_element_type=F32)
        sh = s.astype(BF16)
        sl = (s - sh.astype(F32)).astype(BF16)
        return (jnp.dot(sh, segt, preferred_element_type=F32) + jnp.dot(sl, segt, preferred_element_type=F32))

    x = x_ref[...].reshape(rows, D_MODEL)
    u = _dot(_rms(x, g1_ref[...]), w_in_ref[...])
    u_tm = u[:, :D_TM_PROJ]
    xb = u[:, D_TM_PROJ:D_TM_PROJ + D_LRU]
    gate_lru = u[:, D_TM_PROJ + D_LRU:D_IN_PROJ]
    tm_ext[bb:bb + rows, :] = u_tm
    xb_ext[conv_rows:conv_rows + rows, :] = xb

    row = lax.broadcasted_iota(jnp.int32, (rows, D_TM), 0)

    um = u_tm + (tm_ext[0:rows, :] - u_tm) * mu_ref[...]
    r = um[:, 0:D_TM]
    k = um[:, D_TM:2 * D_TM]
    v = um[:, 2 * D_TM:3 * D_TM]
    x_lora = um[:, LORA_OFF:GATE_OFF]
    x_gate = um[:, GATE_OFF:D_TM_PROJ]
    log_decay = -math.exp(-0.5) * _sigmoid(w0_ref[...] + _dot(jnp.tanh(x_lora), wdec_ref[...]))
    a = _sigmoid(a0_ref[...] + _dot(x_lora, waaa_ref[...]))
    gate_tm = _dot(_sigmoid(x_gate), wgate_ref[...])
    kk = k * kk_ref[...]
    k = k * (1.0 + (a - 1.0) * ka_ref[...])
    kk = kk * lax.rsqrt(jnp.maximum(head_sum(kk * kk), 1e-24))
    bonus = head_sum(r * k * rk_ref[...]) * v
    if valid < chunk:
        live = row < valid * bb
        log_decay = jnp.where(live, log_decay, 0.0)
        kk = jnp.where(live, kk, 0.0)
        k = jnp.where(live, k, 0.0)
    acc = log_decay[0:bb]
    cum = [acc]
    for t in range(1, chunk):
        acc = acc + log_decay[t * bb:(t + 1) * bb]
        cum.append(acc)
    c = jnp.concatenate(cum, axis=0)
    inv_p = jnp.exp(-c)
    kap = kk * jnp.exp(c - log_decay)
    rt = r * jnp.exp(c)
    bt = kk * a * inv_p
    kt = k * inv_p
    p_end = jnp.exp(cum[chunk - 1])
    for p in range(PAIRS):
        ls = slice(p * LANES, (p + 1) * LANES)
        kap_s[p] = kap[:, ls]
        rt_s[p] = rt[:, ls]
        bt_s[p] = bt[:, ls]
        kt_s[p] = kt[:, ls]
        v_s[p] = v[:, ls]
        pe_s[p] = p_end[:, ls]

    xc = cb_ref[...] + xb_ext[0:rows, :] * cw_ref[0:1, :]
    xc = xc + xb_ext[bb:bb + rows, :] * cw_ref[1:2, :]
    xc = xc + xb_ext[2 * bb:2 * bb + rows, :] * cw_ref[2:3, :]
    xc = xc + xb * cw_ref[3:4, :]
    r_g = _sigmoid(_dot(xc, wa_ref[...]) + ba_ref[...])
    i_g = _sigmoid(_dot(xc, wx_ref[...]) + bx_ref[...])
    la = jnp.exp(-LRU_C * r_g * _softplus(-lam_ref[...]))
    mult = jnp.sqrt(1.0 - la * la)
    if pos0 == 0:
        mult = jnp.where(jnp.logical_and(row < bb, ti == 0), 1.0, mult)
    lb = xc * i_g * mult
    h = o_h_ref[...]
    hs = []
    for t in range(chunk):
        ts = slice(t * bb, (t + 1) * bb)
        h = la[ts] * h + lb[ts]
        hs.append(h)
    o_h_ref[...] = hs[valid - 1]
    y_lru = _rms(jnp.concatenate(hs, axis=0) * jax.nn.gelu(gate_lru), og_ref[...])

    n_lv = max(1, (chunk - 1).bit_length())
    cw2 = 2 * chunk
    left = lax.broadcasted_iota(jnp.int32, (1, 1, LANES), 2) < TM_HEAD
    left_c = lax.broadcasted_iota(jnp.int32, (1, 1, cw2), 2) < chunk
    ri = lax.broadcasted_iota(jnp.int32, (chunk, cw2), 0)
    ci = lax.broadcasted_iota(jnp.int32, (chunk, cw2), 1) & (chunk - 1)
    strict = ri > ci
    incl = ri >= ci
    eye = (ri == ci).astype(F32)

    def seq_body(i, carry):
        seqs = [i * group + j for j in range(group)]
        tsel = [pl.ds(b, chunk, stride=bb) for b in seqs]
        gather = lambda ref: jnp.stack([ref.at[p][ts, :] for ts in tsel for p in range(PAIRS)])
        kap_g = gather(kap_s)
        rt_g = gather(rt_s)
        bt_g = gather(bt_s)
        kt_g = gather(kt_s)
        vv = gather(v_s)
        pe = jnp.stack([pe_s.at[p][pl.ds(b, 1), :] for b in seqs for p in range(PAIRS)])
        s0 = jnp.concatenate([o_wkv_ref[b] for b in seqs], axis=0)
        lhs = jnp.concatenate([kap_g, rt_g], axis=1)
        g_b = _bdot_nt(lhs, _block_diag_rows(bt_g, left))
        g_k = _bdot_nt(lhs, _block_diag_rows(kt_g, left))
        m_ab = jnp.where(strict, g_b[:, :chunk], 0.0)
        m_ak = jnp.where(strict, g_k[:, :chunk], 0.0)
        m_rb = jnp.where(incl, g_b[:, chunk:], 0.0)
        m_rk = jnp.where(incl, g_k[:, chunk:], 0.0)
        t_inv = eye - m_ab
        m_pow = _bdot(m_ab, _block_diag_rows(m_ab, left_c))
        for lv in range(1, n_lv):
            if lv < n_lv - 1:
                prod = _bdot(jnp.concatenate([t_inv, m_pow], axis=1), _block_diag_rows(m_pow, left_c))
                t_inv = t_inv + prod[:, :chunk]
                m_pow = prod[:, chunk:]
            else:
                t_inv = t_inv + _bdot(t_inv, _block_diag_rows(m_pow, left_c))
        xv = _bdot(jnp.concatenate([m_ak, m_rk], axis=1), _block_diag_rows(vv, left))
        z = _bdot_nt(lhs, _block_diag_rows(s0, left))
        uu = -_bdot(t_inv, _block_diag_rows(z[:, :chunk] + xv[:, :chunk], left))
        yh = z[:, chunk:] + _bdot(m_rb, _block_diag_rows(uu, left)) + xv[:, chunk:]
        full = _bdot_tn(jnp.concatenate([uu, vv], axis=1), jnp.concatenate([bt_g, kt_g], axis=1))
        upd = jnp.where(left, full[:, :TM_HEAD], full[:, TM_HEAD:])
        s_new = (s0 + upd) * pe
        for j, (b, ts) in enumerate(zip(seqs, tsel)):
            o_wkv_ref[b] = s_new[j * PAIRS:(j + 1) * PAIRS]
            for p in range(PAIRS):
                yt_s.at[p][ts, :] = yh[j * PAIRS + p]
        return carry

    lax.fori_loop(0, bb // group, seq_body, 0)

    yv = jnp.concatenate([yt_s[p] for p in range(PAIRS)], axis=-1)
    mean = head_sum(yv) * (1.0 / TM_HEAD)
    cen = yv - mean
    var = head_sum(cen * cen) * (1.0 / TM_HEAD)
    y_tm = ((cen * lax.rsqrt(var + GN_EPS)) * gng_ref[...] + gnb_ref[...] + bonus) * gate_tm
    y_ref[:, :, 0:D_TM] = y_tm.reshape(chunk, bb, D_TM)
    y_ref[:, :, D_TM:D_MODEL] = y_lru.reshape(chunk, bb, D_LRU)

    o_shift_ref[...] = tm_ext[valid * bb:(valid + 1) * bb, :]
    o_conv_ref[...] = xb_ext[valid * bb:valid * bb + conv_rows, :].reshape(LRU_CONV_W - 1, bb, D_LRU)
    tm_ext[0:bb, :] = tm_ext[rows:rows + bb, :]
    xb_ext[0:conv_rows, :] = xb_ext[rows:rows + conv_rows, :]


def _const_spec(shape):
    zeros = (0,) * len(shape)
    return pl.BlockSpec(shape, lambda bi, ti: zeros, pipeline_mode=pl.Buffered(1))


def _mixer(x, st_shift, st_wkv, st_conv, st_h, weights, *, bb, chunk, valid, pos0, group):
    length, batch, _ = x.shape
    assert batch % bb == 0 and length % chunk == 0 and bb % SUBLANES == 0 and bb % group == 0
    assert chunk & (chunk - 1) == 0 and LRU_CONV_W - 1 <= valid <= chunk
    assert valid == chunk or length == chunk
    rows = bb * chunk
    grid = (batch // bb, length // chunk)
    act = pl.BlockSpec((chunk, bb, D_MODEL), lambda bi, ti: (ti, bi, 0))
    vec = lambda w: pl.BlockSpec((bb, w), lambda bi, ti: (bi, 0))
    wkv_spec = pl.BlockSpec((bb, PAIRS, TM_HEAD, LANES), lambda bi, ti: (bi, 0, 0, 0))
    conv_spec = pl.BlockSpec((LRU_CONV_W - 1, bb, D_LRU), lambda bi, ti: (0, bi, 0))
    in_specs = [act, vec(D_TM_PROJ), wkv_spec, conv_spec, vec(D_LRU)]
    in_specs += [_const_spec(w.shape) for w in weights]
    out_specs = [act, vec(D_TM_PROJ), wkv_spec, conv_spec, vec(D_LRU)]
    out_shape = [jax.ShapeDtypeStruct((length, batch, D_MODEL), F32),
                 jax.ShapeDtypeStruct((batch, D_TM_PROJ), F32),
                 jax.ShapeDtypeStruct((batch, PAIRS, TM_HEAD, LANES), F32),
                 jax.ShapeDtypeStruct((LRU_CONV_W - 1, batch, D_LRU), F32),
                 jax.ShapeDtypeStruct((batch, D_LRU), F32)]
    pair_rows = lambda n: pltpu.VMEM((PAIRS, n, LANES), F32)
    scratch = [pltpu.VMEM((rows + bb, D_TM_PROJ), F32),
               pltpu.VMEM((rows + (LRU_CONV_W - 1) * bb, D_LRU), F32)]
    scratch += [pair_rows(rows)] * 6 + [pair_rows(bb)]
    return pl.pallas_call(
        functools.partial(_mixer_kernel, bb=bb, chunk=chunk, valid=valid, pos0=pos0, group=group),
        out_shape=out_shape, grid=grid, in_specs=in_specs, out_specs=out_specs,
        scratch_shapes=scratch, name="mixer",
        compiler_params=pltpu.CompilerParams(dimension_semantics=("arbitrary", "arbitrary"),
                                             vmem_limit_bytes=VMEM_LIMIT_BYTES),
    )(x, st_shift, st_wkv, st_conv, st_h, *weights)


def _ffn_kernel(x_ref, ym_ref, st_conv_ref,
                w_out_ref, g2_ref, w_up_ref, w_gate_ref, cw_ref, cb_ref, w_down_ref, gf_ref,
                y_ref, o_conv_ref,
                up_ext,
                *, bb, chunk, valid):
    rows = bb * chunk
    conv_rows = (FFN_CONV_W - 1) * bb
    ti = pl.program_id(1)

    @pl.when(ti == 0)
    def _():
        up_ext[0:conv_rows, :] = st_conv_ref[...].reshape(conv_rows, D_FF)

    x = x_ref[...].reshape(rows, D_MODEL)
    ym = ym_ref[...].reshape(rows, D_MODEL)
    x1 = x + _dot(ym, w_out_ref[...])
    xn = _rms(x1, g2_ref[...]).astype(BF16)
    acc = x1
    for n in range(D_FF // FF_COL_TILE):
        cols = slice(n * FF_COL_TILE, (n + 1) * FF_COL_TILE)
        up = jnp.dot(xn, w_up_ref[:, cols], preferred_element_type=F32)
        gate = jnp.dot(xn, w_gate_ref[:, cols], preferred_element_type=F32)
        up_ext[conv_rows:conv_rows + rows, cols] = up
        upc = cb_ref[:, cols] + up_ext[0:rows, cols] * cw_ref[0:1, cols]
        upc = upc + up_ext[bb:bb + rows, cols] * cw_ref[1:2, cols]
        upc = upc + up * cw_ref[2:3, cols]
        hid = (jax.nn.gelu(upc) * gate).astype(BF16)
        acc = acc + jnp.dot(hid, w_down_ref[cols, :], preferred_element_type=F32)
    y_ref[...] = _rms(acc, gf_ref[...]).reshape(chunk, bb, D_MODEL)
    o_conv_ref[...] = up_ext[valid * bb:valid * bb + conv_rows, :].reshape(FFN_CONV_W - 1, bb, D_FF)
    up_ext[0:conv_rows, :] = up_ext[rows:rows + conv_rows, :]


def _ffn(x, ym, st_conv, weights, *, bb, chunk, valid):
    length, batch, _ = x.shape
    assert batch % bb == 0 and length % chunk == 0 and bb % SUBLANES == 0
    assert FFN_CONV_W - 1 <= valid <= chunk and (valid == chunk or length == chunk)
    rows = bb * chunk
    grid = (batch // bb, length // chunk)
    act = pl.BlockSpec((chunk, bb, D_MODEL), lambda bi, ti: (ti, bi, 0))
    conv_spec = pl.BlockSpec((FFN_CONV_W - 1, bb, D_FF), lambda bi, ti: (0, bi, 0))
    in_specs = [act, act, conv_spec] + [_const_spec(w.shape) for w in weights]
    out_shape = [jax.ShapeDtypeStruct((length, batch, D_MODEL), F32),
                 jax.ShapeDtypeStruct((FFN_CONV_W - 1, batch, D_FF), F32)]
    scratch = [pltpu.VMEM((rows + (FFN_CONV_W - 1) * bb, D_FF), F32)]
    return pl.pallas_call(
        functools.partial(_ffn_kernel, bb=bb, chunk=chunk, valid=valid),
        out_shape=out_shape, grid=grid, in_specs=in_specs, out_specs=[act, conv_spec],
        scratch_shapes=scratch, name="ffn",
        compiler_params=pltpu.CompilerParams(dimension_semantics=("arbitrary", "arbitrary"),
                                             vmem_limit_bytes=VMEM_LIMIT_BYTES),
    )(x, ym, st_conv, *weights)


def _row(v):
    return v.reshape(1, -1).astype(F32)


def _block_diag(w):
    nb, n, _ = w.shape
    eye = jnp.eye(nb, dtype=w.dtype)
    return (eye[:, None, :, None] * w[:, :, None, :]).reshape(nb * n, nb * n)


def _wkv_to_pairs(s):
    b = s.shape[0]
    s = s.astype(F32).reshape(b, PAIRS, 2, TM_HEAD, TM_HEAD)
    return jnp.transpose(s, (0, 1, 3, 2, 4)).reshape(b, PAIRS, TM_HEAD, LANES)


def _wkv_from_pairs(s):
    b = s.shape[0]
    s = s.reshape(b, PAIRS, TM_HEAD, 2, TM_HEAD)
    return jnp.transpose(s, (0, 1, 3, 2, 4)).reshape(b, TM_HEADS, TM_HEAD, TM_HEAD)


def _layer(x, states, mixer_w, ffn_w, *, mixer_bb, ffn_bb, mixer_chunk, ffn_chunk, valid, pos0, group):
    st_shift, st_wkv, st_conv, st_h, st_fconv = states
    ym, o_shift, o_wkv, o_conv, o_h = _mixer(x, st_shift, st_wkv, st_conv, st_h, mixer_w,
                                             bb=mixer_bb, chunk=mixer_chunk, valid=valid, pos0=pos0,
                                             group=group)
    y, o_fconv = _ffn(x, ym, st_fconv, ffn_w, bb=ffn_bb, chunk=ffn_chunk, valid=valid)
    return y, (o_shift, o_wkv, o_conv, o_h, o_fconv)


def kernel(x_prompt, x_sample, state_tm_shift, state_tm_wkv, state_lru_conv, state_lru_h, state_ffn_conv, meta_tokens, norm1_g, w_in, tm_mu, tm_w0, tm_w_up, tm_a0, tm_a_up, tm_g_up, tm_k_k, tm_k_a, tm_r_k, tm_gn_g, tm_gn_b, lru_conv_w, lru_conv_b, lru_wa, lru_ba, lru_wx, lru_bx, lru_lambda, lru_out_g, w_out, norm2_g, ffn_w_up, ffn_w_gate, ffn_conv_w, ffn_conv_b, ffn_w_down, norm_f_g):
    depth = w_in.shape[0]
    assert depth == 1
    l = 0
    zeros_lora = jnp.zeros((DECAY_RANK, D_TM), F32)
    head_id = jnp.arange(D_TM) // TM_HEAD
    seg = (head_id[:, None] == jnp.arange(LANES)[None, :]).astype(BF16)
    mixer_w = (
        _row(norm1_g[l]), w_in[l].astype(BF16), _row(tm_mu[l]), _row(tm_w0[l]),
        jnp.concatenate([tm_w_up[l], zeros_lora], axis=0).astype(BF16),
        _row(tm_a0[l]),
        jnp.concatenate([zeros_lora, tm_a_up[l]], axis=0).astype(BF16),
        tm_g_up[l].astype(BF16),
        _row(tm_k_k[l]), _row(tm_k_a[l]), _row(tm_r_k[l]), _row(tm_gn_g[l]), _row(tm_gn_b[l]),
        lru_conv_w[l].astype(F32), _row(lru_conv_b[l]),
        _block_diag(lru_wa[l]).astype(BF16), _row(lru_ba[l]),
        _block_diag(lru_wx[l]).astype(BF16), _row(lru_bx[l]),
        _row(lru_lambda[l]), _row(lru_out_g[l]),
        seg, seg.T,
    )
    ffn_w = (
        w_out[l].astype(BF16), _row(norm2_g[l]), ffn_w_up[l].astype(BF16), ffn_w_gate[l].astype(BF16),
        ffn_conv_w[l].astype(F32), _row(ffn_conv_b[l]), ffn_w_down[l].astype(BF16), _row(norm_f_g),
    )

    bsz, seq = x_prompt.shape[0], x_prompt.shape[1]
    x_meta = jnp.broadcast_to(meta_tokens[:, None, :].astype(F32), (N_META, bsz, D_MODEL))
    zero_states = (jnp.zeros((bsz, D_TM_PROJ), F32),
                   jnp.zeros((bsz, PAIRS, TM_HEAD, LANES), F32),
                   jnp.zeros((LRU_CONV_W - 1, bsz, D_LRU), F32),
                   jnp.zeros((bsz, D_LRU), F32),
                   jnp.zeros((FFN_CONV_W - 1, bsz, D_FF), F32))
    _, p_init = _layer(x_meta, zero_states, mixer_w, ffn_w,
                       mixer_bb=bsz, ffn_bb=bsz, mixer_chunk=N_META, ffn_chunk=N_META,
                       valid=N_META, pos0=0, group=4)

    y_p, p_st = _layer(jnp.transpose(x_prompt, (1, 0, 2)), p_init, mixer_w, ffn_w,
                       mixer_bb=bsz, ffn_bb=bsz, mixer_chunk=64, ffn_chunk=64,
                       valid=64, pos0=N_META, group=4)
    y_prompt = jnp.transpose(y_p, (1, 0, 2))

    dec_b, dec_seq = x_sample.shape[0], x_sample.shape[1]
    pad_seq = SUBLANES
    xs = jnp.transpose(jnp.pad(x_sample, ((0, 0), (0, pad_seq - dec_seq), (0, 0))), (1, 0, 2))
    s_init = (state_tm_shift[l].astype(F32), _wkv_to_pairs(state_tm_wkv[l]),
              jnp.transpose(state_lru_conv[l].astype(F32), (1, 0, 2)), state_lru_h[l].astype(F32),
              jnp.transpose(state_ffn_conv[l].astype(F32), (1, 0, 2)))
    y_s, s_st = _layer(xs, s_init, mixer_w, ffn_w,
                       mixer_bb=32, ffn_bb=64, mixer_chunk=pad_seq, ffn_chunk=pad_seq,
                       valid=dec_seq, pos0=PAST_LEN, group=4)
    y_sample = jnp.transpose(y_s[:dec_seq], (1, 0, 2))

    def unpack(st):
        o_shift, o_wkv, o_conv, o_h, o_fconv = st
        return (o_shift[None], _wkv_from_pairs(o_wkv)[None], jnp.transpose(o_conv, (1, 0, 2))[None],
                o_h[None], jnp.transpose(o_fconv, (1, 0, 2))[None])

    return (y_prompt, y_sample) + unpack(p_st) + unpack(s_st)
```

```python
import functools
import math

import jax
import jax.numpy as jnp
from jax import lax
from jax.experimental import pallas as pl
from jax.experimental.pallas import tpu as pltpu

F32 = jnp.float32
BF16 = jnp.bfloat16

D_MODEL = 1024
N_META = 16
PAST_LEN = 16384
D_TM = 512
TM_HEAD = 64
TM_HEADS = 8
DECAY_RANK = 64
AAA_RANK = 64
GATE_RANK = 128
D_TM_PROJ = 3 * D_TM + DECAY_RANK + AAA_RANK + GATE_RANK
D_LRU = 512
LRU_CONV_W = 4
LRU_C = 8.0
D_IN_PROJ = D_TM_PROJ + 2 * D_LRU
D_FF = 3 * D_MODEL
FFN_CONV_W = 3
EPS = 1e-6
GN_EPS = 64e-5

SUBLANES = 8
LANES = 128
PAIRS = TM_HEADS // 2
LORA_OFF = 3 * D_TM
GATE_OFF = LORA_OFF + DECAY_RANK + AAA_RANK
FF_COL_TILE = 1024
VMEM_LIMIT_BYTES = 60 * 1024 * 1024


def _dot(a, b):
    return jnp.dot(a.astype(BF16), b.astype(BF16), preferred_element_type=F32)


def _bdot(a, b):
    return lax.dot_general(a.astype(BF16), b.astype(BF16), (((2,), (1,)), ((0,), (0,))),
                           preferred_element_type=F32)


def _bdot_nt(a, b):
    return lax.dot_general(a.astype(BF16), b.astype(BF16), (((2,), (2,)), ((0,), (0,))),
                           preferred_element_type=F32)


def _bdot_tn(a, b):
    return lax.dot_general(a.astype(BF16), b.astype(BF16), (((1,), (1,)), ((0,), (0,))),
                           preferred_element_type=F32)


def _rms(x, g):
    return x * lax.rsqrt(jnp.mean(x * x, axis=-1, keepdims=True) + EPS) * g


def _sigmoid(z):
    return 0.5 * jnp.tanh(0.5 * z) + 0.5


def _softplus(z):
    return jnp.maximum(z, 0.0) + jnp.log(1.0 + jnp.exp(-jnp.abs(z)))


def _block_diag_rows(x, left):
    return jnp.concatenate([jnp.where(left, x, 0.0), jnp.where(left, 0.0, x)], axis=1)


def _mixer_kernel(x_ref, st_shift_ref, st_wkv_ref, st_conv_ref, st_h_ref,
                  g1_ref, w_in_ref, mu_ref, w0_ref, wdec_ref, a0_ref, waaa_ref, wgate_ref,
                  kk_ref, ka_ref, rk_ref, gng_ref, gnb_ref,
                  cw_ref, cb_ref, wa_ref, ba_ref, wx_ref, bx_ref, lam_ref, og_ref,
                  seg_ref, segt_ref,
                  y_ref, o_shift_ref, o_wkv_ref, o_conv_ref, o_h_ref,
                  tm_ext, xb_ext, kap_s, rt_s, bt_s, kt_s, v_s, yt_s, pe_s,
                  *, bb, chunk, pos0, group):
    rows = bb * chunk
    conv_rows = (LRU_CONV_W - 1) * bb
    wkv_chunk = max(chunk, SUBLANES)
    ti = pl.program_id(1)

    @pl.when(ti == 0)
    def _():
        tm_ext[0:bb, :] = st_shift_ref[...]
        xb_ext[0:conv_rows, :] = st_conv_ref[...].reshape(conv_rows, D_LRU)
        o_wkv_ref[...] = st_wkv_ref[...]
        o_h_ref[...] = st_h_ref[...]
        if wkv_chunk > chunk:
            for ref in (kap_s, rt_s, bt_s, kt_s, v_s):
                ref[:, rows:wkv_chunk * bb, :] = jnp.zeros((PAIRS, wkv_chunk * bb - rows, LANES), F32)

    seg = seg_ref[...]
    segt = segt_ref[...]

    def head_sum(x):
        s = jnp.dot(x.astype(BF16), seg, preferred_element_type=F32)
        sh = s.astype(BF16)
        sl = (s - sh.astype(F32)).astype(BF16)
        return (jnp.dot(sh, segt, preferred_element_type=F32) + jnp.dot(sl, segt, preferred_element_type=F32))

    x = x_ref[...].reshape(rows, D_MODEL)
    u = _dot(_rms(x, g1_ref[...]), w_in_ref[...])
    u_tm = u[:, :D_TM_PROJ]
    xb = u[:, D_TM_PROJ:D_TM_PROJ + D_LRU]
    gate_lru = u[:, D_TM_PROJ + D_LRU:D_IN_PROJ]
    tm_ext[bb:bb + rows, :] = u_tm
    xb_ext[conv_rows:conv_rows + rows, :] = xb

    row = lax.broadcasted_iota(jnp.int32, (rows, D_TM), 0)

    um = u_tm + (tm_ext[0:rows, :] - u_tm) * mu_ref[...]
    r = um[:, 0:D_TM]
    k = um[:, D_TM:2 * D_TM]
    v = um[:, 2 * D_TM:3 * D_TM]
    x_lora = um[:, LORA_OFF:GATE_OFF]
    x_gate = um[:, GATE_OFF:D_TM_PROJ]
    log_decay = -math.exp(-0.5) * _sigmoid(w0_ref[...] + _dot(jnp.tanh(x_lora), wdec_ref[...]))
    a = _sigmoid(a0_ref[...] + _dot(x_lora, waaa_ref[...]))
    gate_tm = _dot(_sigmoid(x_gate), wgate_ref[...])
    kk = k * kk_ref[...]
    k = k * (1.0 + (a - 1.0) * ka_ref[...])
    kk = kk * lax.rsqrt(jnp.maximum(head_sum(kk * kk), 1e-24))
    bonus = head_sum(r * k * rk_ref[...]) * v
    acc = log_decay[0:bb]
    cum = [acc]
    for t in range(1, chunk):
        acc = acc + log_decay[t * bb:(t + 1) * bb]
        cum.append(acc)
    c = jnp.concatenate(cum, axis=0)
    inv_p = jnp.exp(-c)
    kap = kk * jnp.exp(c - log_decay)
    rt = r * jnp.exp(c)
    bt = kk * a * inv_p
    kt = k * inv_p
    p_end = jnp.exp(cum[chunk - 1])
    for p in range(PAIRS):
        ls = slice(p * LANES, (p + 1) * LANES)
        kap_s[p, 0:rows, :] = kap[:, ls]
        rt_s[p, 0:rows, :] = rt[:, ls]
        bt_s[p, 0:rows, :] = bt[:, ls]
        kt_s[p, 0:rows, :] = kt[:, ls]
        v_s[p, 0:rows, :] = v[:, ls]
        pe_s[p] = p_end[:, ls]

    xc = cb_ref[...] + xb_ext[0:rows, :] * cw_ref[0:1, :]
    xc = xc + xb_ext[bb:bb + rows, :] * cw_ref[1:2, :]
    xc = xc + xb_ext[2 * bb:2 * bb + rows, :] * cw_ref[2:3, :]
    xc = xc + xb * cw_ref[3:4, :]
    r_g = _sigmoid(_dot(xc, wa_ref[...]) + ba_ref[...])
    i_g = _sigmoid(_dot(xc, wx_ref[...]) + bx_ref[...])
    la = jnp.exp(-LRU_C * r_g * _softplus(-lam_ref[...]))
    mult = jnp.sqrt(1.0 - la * la)
    if pos0 == 0:
        mult = jnp.where(jnp.logical_and(row < bb, ti == 0), 1.0, mult)
    lb = xc * i_g * mult
    h = o_h_ref[...]
    hs = []
    for t in range(chunk):
        ts = slice(t * bb, (t + 1) * bb)
        h = la[ts] * h + lb[ts]
        hs.append(h)
    o_h_ref[...] = h
    y_lru = _rms(jnp.concatenate(hs, axis=0) * jax.nn.gelu(gate_lru), og_ref[...])

    wc = wkv_chunk
    n_lv = max(1, (wc - 1).bit_length())
    cw2 = 2 * wc
    left = lax.broadcasted_iota(jnp.int32, (1, 1, LANES), 2) < TM_HEAD
    left_c = lax.broadcasted_iota(jnp.int32, (1, 1, cw2), 2) < wc
    ri = lax.broadcasted_iota(jnp.int32, (wc, cw2), 0)
    ci = lax.broadcasted_iota(jnp.int32, (wc, cw2), 1) & (wc - 1)
    strict = ri > ci
    incl = ri >= ci
    eye = (ri == ci).astype(F32)

    def seq_body(i, carry):
        seqs = [i * group + j for j in range(group)]
        tsel = [pl.ds(b, wc, stride=bb) for b in seqs]
        gather = lambda ref: jnp.stack([ref.at[p][ts, :] for ts in tsel for p in range(PAIRS)])
        kap_g = gather(kap_s)
        rt_g = gather(rt_s)
        bt_g = gather(bt_s)
        kt_g = gather(kt_s)
        vv = gather(v_s)
        pe = jnp.stack([pe_s.at[p][pl.ds(b, 1), :] for b in seqs for p in range(PAIRS)])
        s0 = jnp.concatenate([o_wkv_ref[b] for b in seqs], axis=0)
        lhs = jnp.concatenate([kap_g, rt_g], axis=1)
        g_b = _bdot_nt(lhs, _block_diag_rows(bt_g, left))
        g_k = _bdot_nt(lhs, _block_diag_rows(kt_g, left))
        m_ab = jnp.where(strict, g_b[:, :wc], 0.0)
        m_ak = jnp.where(strict, g_k[:, :wc], 0.0)
        m_rb = jnp.where(incl, g_b[:, wc:], 0.0)
        m_rk = jnp.where(incl, g_k[:, wc:], 0.0)
        t_inv = eye - m_ab
        m_pow = _bdot(m_ab, _block_diag_rows(m_ab, left_c))
        for lv in range(1, n_lv):
            if lv < n_lv - 1:
                prod = _bdot(jnp.concatenate([t_inv, m_pow], axis=1), _block_diag_rows(m_pow, left_c))
                t_inv = t_inv + prod[:, :wc]
                m_pow = prod[:, wc:]
            else:
                t_inv = t_inv + _bdot(t_inv, _block_diag_rows(m_pow, left_c))
        xv = _bdot(jnp.concatenate([m_ak, m_rk], axis=1), _block_diag_rows(vv, left))
        z = _bdot_nt(lhs, _block_diag_rows(s0, left))
        uu = -_bdot(t_inv, _block_diag_rows(z[:, :wc] + xv[:, :wc], left))
        yh = z[:, wc:] + _bdot(m_rb, _block_diag_rows(uu, left)) + xv[:, wc:]
        full = _bdot_tn(jnp.concatenate([uu, vv], axis=1), jnp.concatenate([bt_g, kt_g], axis=1))
        upd = jnp.where(left, full[:, :TM_HEAD], full[:, TM_HEAD:])
        s_new = (s0 + upd) * pe
        for j, (b, ts) in enumerate(zip(seqs, tsel)):
            o_wkv_ref[b] = s_new[j * PAIRS:(j + 1) * PAIRS]
            for p in range(PAIRS):
                yt_s.at[p][ts, :] = yh[j * PAIRS + p]
        return carry

    n_iter = bb // group
    if n_iter <= 2:
        for i in range(n_iter):
            seq_body(i, 0)
    else:
        lax.fori_loop(0, n_iter, seq_body, 0)

    yv = jnp.concatenate([yt_s[p, 0:rows, :] for p in range(PAIRS)], axis=-1)
    mean = head_sum(yv) * (1.0 / TM_HEAD)
    cen = yv - mean
    var = head_sum(cen * cen) * (1.0 / TM_HEAD)
    y_tm = ((cen * lax.rsqrt(var + GN_EPS)) * gng_ref[...] + gnb_ref[...] + bonus) * gate_tm
    y_ref[:, :, 0:D_TM] = y_tm.reshape(chunk, bb, D_TM)
    y_ref[:, :, D_TM:D_MODEL] = y_lru.reshape(chunk, bb, D_LRU)

    o_shift_ref[...] = tm_ext[rows:rows + bb, :]
    o_conv_ref[...] = xb_ext[rows:rows + conv_rows, :].reshape(LRU_CONV_W - 1, bb, D_LRU)
    tm_ext[0:bb, :] = tm_ext[rows:rows + bb, :]
    xb_ext[0:conv_rows, :] = xb_ext[rows:rows + conv_rows, :]


def _const_spec(shape):
    zeros = (0,) * len(shape)
    return pl.BlockSpec(shape, lambda bi, ti: zeros, pipeline_mode=pl.Buffered(1))


def _mixer(x, st_shift, st_wkv, st_conv, st_h, weights, *, bb, chunk, pos0, group):
    length, batch, _ = x.shape
    assert batch % bb == 0 and length % chunk == 0 and bb % SUBLANES == 0 and bb % group == 0
    assert chunk & (chunk - 1) == 0 and chunk >= LRU_CONV_W - 1
    rows = bb * chunk
    grid = (batch // bb, length // chunk)
    act = pl.BlockSpec((chunk, bb, D_MODEL), lambda bi, ti: (ti, bi, 0))
    vec = lambda w: pl.BlockSpec((bb, w), lambda bi, ti: (bi, 0))
    wkv_spec = pl.BlockSpec((bb, PAIRS, TM_HEAD, LANES), lambda bi, ti: (bi, 0, 0, 0))
    conv_spec = pl.BlockSpec((LRU_CONV_W - 1, bb, D_LRU), lambda bi, ti: (0, bi, 0))
    in_specs = [act, vec(D_TM_PROJ), wkv_spec, conv_spec, vec(D_LRU)]
    in_specs += [_const_spec(w.shape) for w in weights]
    out_specs = [act, vec(D_TM_PROJ), wkv_spec, conv_spec, vec(D_LRU)]
    out_shape = [jax.ShapeDtypeStruct((length, batch, D_MODEL), F32),
                 jax.ShapeDtypeStruct((batch, D_TM_PROJ), F32),
                 jax.ShapeDtypeStruct((batch, PAIRS, TM_HEAD, LANES), F32),
                 jax.ShapeDtypeStruct((LRU_CONV_W - 1, batch, D_LRU), F32),
                 jax.ShapeDtypeStruct((batch, D_LRU), F32)]
    pair_rows = lambda n: pltpu.VMEM((PAIRS, n, LANES), F32)
    scratch = [pltpu.VMEM((rows + bb, D_TM_PROJ), F32),
               pltpu.VMEM((rows + (LRU_CONV_W - 1) * bb, D_LRU), F32)]
    scratch += [pair_rows(max(chunk, SUBLANES) * bb)] * 6 + [pair_rows(bb)]
    return pl.pallas_call(
        functools.partial(_mixer_kernel, bb=bb, chunk=chunk, pos0=pos0, group=group),
        out_shape=out_shape, grid=grid, in_specs=in_specs, out_specs=out_specs,
        scratch_shapes=scratch, name="mixer",
        compiler_params=pltpu.CompilerParams(dimension_semantics=("arbitrary", "arbitrary"),
                                             vmem_limit_bytes=VMEM_LIMIT_BYTES),
    )(x, st_shift, st_wkv, st_conv, st_h, *weights)


def _ffn_kernel(x_ref, ym_ref, st_conv_ref,
                w_out_ref, g2_ref, w_up_ref, w_gate_ref, cw_ref, cb_ref, w_down_ref, gf_ref,
                y_ref, o_conv_ref,
                up_ext,
                *, bb, chunk):
    rows = bb * chunk
    conv_rows = (FFN_CONV_W - 1) * bb
    ti = pl.program_id(1)

    @pl.when(ti == 0)
    def _():
        up_ext[0:conv_rows, :] = st_conv_ref[...].reshape(conv_rows, D_FF)

    x = x_ref[...].reshape(rows, D_MODEL)
    ym = ym_ref[...].reshape(rows, D_MODEL)
    x1 = x + _dot(ym, w_out_ref[...])
    xn = _rms(x1, g2_ref[...]).astype(BF16)
    acc = x1
    for n in range(D_FF // FF_COL_TILE):
        cols = slice(n * FF_COL_TILE, (n + 1) * FF_COL_TILE)
        up = jnp.dot(xn, w_up_ref[:, cols], preferred_element_type=F32)
        gate = jnp.dot(xn, w_gate_ref[:, cols], preferred_element_type=F32)
        up_ext[conv_rows:conv_rows + rows, cols] = up
        upc = cb_ref[:, cols] + up_ext[0:rows, cols] * cw_ref[0:1, cols]
        upc = upc + up_ext[bb:bb + rows, cols] * cw_ref[1:2, cols]
        upc = upc + up * cw_ref[2:3, cols]
        hid = (jax.nn.gelu(upc) * gate).astype(BF16)
        acc = acc + jnp.dot(hid, w_down_ref[cols, :], preferred_element_type=F32)
    y_ref[...] = _rms(acc, gf_ref[...]).reshape(chunk, bb, D_MODEL)
    o_conv_ref[...] = up_ext[rows:rows + conv_rows, :].reshape(FFN_CONV_W - 1, bb, D_FF)
    up_ext[0:conv_rows, :] = up_ext[rows:rows + conv_rows, :]


def _ffn(x, ym, st_conv, weights, *, bb, chunk):
    length, batch, _ = x.shape
    assert batch % bb == 0 and length % chunk == 0 and bb % SUBLANES == 0
    assert chunk >= FFN_CONV_W - 1
    rows = bb * chunk
    grid = (batch // bb, length // chunk)
    act = pl.BlockSpec((chunk, bb, D_MODEL), lambda bi, ti: (ti, bi, 0))
    conv_spec = pl.BlockSpec((FFN_CONV_W - 1, bb, D_FF), lambda bi, ti: (0, bi, 0))
    in_specs = [act, act, conv_spec] + [_const_spec(w.shape) for w in weights]
    out_shape = [jax.ShapeDtypeStruct((length, batch, D_MODEL), F32),
                 jax.ShapeDtypeStruct((FFN_CONV_W - 1, batch, D_FF), F32)]
    scratch = [pltpu.VMEM((rows + (FFN_CONV_W - 1) * bb, D_FF), F32)]
    return pl.pallas_call(
        functools.partial(_ffn_kernel, bb=bb, chunk=chunk),
        out_shape=out_shape, grid=grid, in_specs=in_specs, out_specs=[act, conv_spec],
        scratch_shapes=scratch, name="ffn",
        compiler_params=pltpu.CompilerParams(dimension_semantics=("arbitrary", "arbitrary"),
                                             vmem_limit_bytes=VMEM_LIMIT_BYTES),
    )(x, ym, st_conv, *weights)


def _row(v):
    return v.reshape(1, -1).astype(F32)


def _block_diag(w):
    nb, n, _ = w.shape
    eye = jnp.eye(nb, dtype=w.dtype)
    return (eye[:, None, :, None] * w[:, :, None, :]).reshape(nb * n, nb * n)


def _wkv_to_pairs(s):
    b = s.shape[0]
    s = s.astype(F32).reshape(b, PAIRS, 2, TM_HEAD, TM_HEAD)
    return jnp.transpose(s, (0, 1, 3, 2, 4)).reshape(b, PAIRS, TM_HEAD, LANES)


def _wkv_from_pairs(s):
    b = s.shape[0]
    s = s.reshape(b, PAIRS, TM_HEAD, 2, TM_HEAD)
    return jnp.transpose(s, (0, 1, 3, 2, 4)).reshape(b, TM_HEADS, TM_HEAD, TM_HEAD)


def _layer(x, states, mixer_w, ffn_w, *, mixer_bb, ffn_bb, mixer_chunk, ffn_chunk, pos0, group):
    st_shift, st_wkv, st_conv, st_h, st_fconv = states
    ym, o_shift, o_wkv, o_conv, o_h = _mixer(x, st_shift, st_wkv, st_conv, st_h, mixer_w,
                                             bb=mixer_bb, chunk=mixer_chunk, pos0=pos0, group=group)
    y, o_fconv = _ffn(x, ym, st_fconv, ffn_w, bb=ffn_bb, chunk=ffn_chunk)
    return y, (o_shift, o_wkv, o_conv, o_h, o_fconv)


def kernel(x_prompt, x_sample, state_tm_shift, state_tm_wkv, state_lru_conv, state_lru_h, state_ffn_conv, meta_tokens, norm1_g, w_in, tm_mu, tm_w0, tm_w_up, tm_a0, tm_a_up, tm_g_up, tm_k_k, tm_k_a, tm_r_k, tm_gn_g, tm_gn_b, lru_conv_w, lru_conv_b, lru_wa, lru_ba, lru_wx, lru_bx, lru_lambda, lru_out_g, w_out, norm2_g, ffn_w_up, ffn_w_gate, ffn_conv_w, ffn_conv_b, ffn_w_down, norm_f_g):
    depth = w_in.shape[0]
    assert depth == 1
    l = 0
    zeros_lora = jnp.zeros((DECAY_RANK, D_TM), F32)
    head_id = jnp.arange(D_TM) // TM_HEAD
    seg = (head_id[:, None] == jnp.arange(LANES)[None, :]).astype(BF16)
    mixer_w = (
        _row(norm1_g[l]), w_in[l].astype(BF16), _row(tm_mu[l]), _row(tm_w0[l]),
        jnp.concatenate([tm_w_up[l], zeros_lora], axis=0).astype(BF16),
        _row(tm_a0[l]),
        jnp.concatenate([zeros_lora, tm_a_up[l]], axis=0).astype(BF16),
        tm_g_up[l].astype(BF16),
        _row(tm_k_k[l]), _row(tm_k_a[l]), _row(tm_r_k[l]), _row(tm_gn_g[l]), _row(tm_gn_b[l]),
        lru_conv_w[l].astype(F32), _row(lru_conv_b[l]),
        _block_diag(lru_wa[l]).astype(BF16), _row(lru_ba[l]),
        _block_diag(lru_wx[l]).astype(BF16), _row(lru_bx[l]),
        _row(lru_lambda[l]), _row(lru_out_g[l]),
        seg, seg.T,
    )
    ffn_w = (
        w_out[l].astype(BF16), _row(norm2_g[l]), ffn_w_up[l].astype(BF16), ffn_w_gate[l].astype(BF16),
        ffn_conv_w[l].astype(F32), _row(ffn_conv_b[l]), ffn_w_down[l].astype(BF16), _row(norm_f_g),
    )

    bsz, seq = x_prompt.shape[0], x_prompt.shape[1]
    x_meta = jnp.broadcast_to(meta_tokens[:, None, :].astype(F32), (N_META, bsz, D_MODEL))
    zero_states = (jnp.zeros((bsz, D_TM_PROJ), F32),
                   jnp.zeros((bsz, PAIRS, TM_HEAD, LANES), F32),
                   jnp.zeros((LRU_CONV_W - 1, bsz, D_LRU), F32),
                   jnp.zeros((bsz, D_LRU), F32),
                   jnp.zeros((FFN_CONV_W - 1, bsz, D_FF), F32))
    _, p_init = _layer(x_meta, zero_states, mixer_w, ffn_w,
                       mixer_bb=bsz, ffn_bb=bsz, mixer_chunk=N_META, ffn_chunk=N_META,
                       pos0=0, group=4)

    y_p, p_st = _layer(jnp.transpose(x_prompt, (1, 0, 2)), p_init, mixer_w, ffn_w,
                       mixer_bb=bsz, ffn_bb=bsz, mixer_chunk=64, ffn_chunk=64,
                       pos0=N_META, group=4)
    y_prompt = jnp.transpose(y_p, (1, 0, 2))

    dec_b, dec_seq = x_sample.shape[0], x_sample.shape[1]
    s_init = (state_tm_shift[l].astype(F32), _wkv_to_pairs(state_tm_wkv[l]),
              jnp.transpose(state_lru_conv[l].astype(F32), (1, 0, 2)), state_lru_h[l].astype(F32),
              jnp.transpose(state_ffn_conv[l].astype(F32), (1, 0, 2)))
    y_s, s_st = _layer(jnp.transpose(x_sample, (1, 0, 2)), s_init, mixer_w, ffn_w,
                       mixer_bb=32, ffn_bb=dec_b, mixer_chunk=dec_seq, ffn_chunk=dec_seq,
                       pos0=PAST_LEN, group=8)
    y_sample = jnp.transpose(y_s, (1, 0, 2))

    def unpack(st):
        o_shift, o_wkv, o_conv, o_h, o_fconv = st
        return (o_shift[None], _wkv_from_pairs(o_wkv)[None], jnp.transpose(o_conv, (1, 0, 2))[None],
                o_h[None], jnp.transpose(o_fconv, (1, 0, 2))[None])

    return (y_prompt, y_sample) + unpack(p_st) + unpack(s_st)
```

```python
import functools
import math

import jax
import jax.numpy as jnp
from jax import lax
from jax.experimental import pallas as pl
from jax.experimental.pallas import tpu as pltpu

F32 = jnp.float32
BF16 = jnp.bfloat16

D_MODEL = 1024
N_META = 16
PAST_LEN = 16384
D_TM = 512
TM_HEAD = 64
TM_HEADS = 8
DECAY_RANK = 64
AAA_RANK = 64
GATE_RANK = 128
D_TM_PROJ = 3 * D_TM + DECAY_RANK + AAA_RANK + GATE_RANK
D_LRU = 512
LRU_CONV_W = 4
LRU_C = 8.0
D_IN_PROJ = D_TM_PROJ + 2 * D_LRU
D_FF = 3 * D_MODEL
FFN_CONV_W = 3
EPS = 1e-6
GN_EPS = 64e-5

SUBLANES = 8
LANES = 128
PAIRS = TM_HEADS // 2
LORA_OFF = 3 * D_TM
GATE_OFF = LORA_OFF + DECAY_RANK + AAA_RANK
FF_COL_TILE = 1024
VMEM_LIMIT_BYTES = 60 * 1024 * 1024


def _dot(a, b):
    return jnp.dot(a.astype(BF16), b.astype(BF16), preferred_element_type=F32)


def _bdot(a, b):
    return lax.dot_general(a.astype(BF16), b.astype(BF16), (((2,), (1,)), ((0,), (0,))),
                           preferred_element_type=F32)


def _bdot_nt(a, b):
    return lax.dot_general(a.astype(BF16), b.astype(BF16), (((2,), (2,)), ((0,), (0,))),
                           preferred_element_type=F32)


def _bdot_tn(a, b):
    return lax.dot_general(a.astype(BF16), b.astype(BF16), (((1,), (1,)), ((0,), (0,))),
                           preferred_element_type=F32)


def _rms(x, g):
    return x * lax.rsqrt(jnp.mean(x * x, axis=-1, keepdims=True) + EPS) * g


def _sigmoid(z):
    return 0.5 * jnp.tanh(0.5 * z) + 0.5


def _softplus(z):
    return jnp.maximum(z, 0.0) + jnp.log(1.0 + jnp.exp(-jnp.abs(z)))


def _block_diag_rows(x, left):
    return jnp.concatenate([jnp.where(left, x, 0.0), jnp.where(left, 0.0, x)], axis=1)


class _TimeMajorStream:
    def __init__(self, hbm, buf, sem, *, bb, chunk, to_hbm):
        self.hbm, self.buf, self.sem = hbm, buf, sem
        self.bb, self.chunk, self.to_hbm = bb, chunk, to_hbm

    def _copy(self, bi, ti, slot, t):
        seqs = pl.ds(pl.multiple_of(bi * self.bb, SUBLANES), self.bb)
        hbm_rows = self.hbm.at[seqs, ti * self.chunk + t, :]
        tile = self.buf.at[slot, t]
        src, dst = (tile, hbm_rows) if self.to_hbm else (hbm_rows, tile)
        return pltpu.make_async_copy(src, dst, self.sem.at[slot])

    def start(self, bi, ti, slot):
        def body(t, carry):
            self._copy(bi, ti, slot, t).start()
            return carry
        lax.fori_loop(0, self.chunk, body, 0)

    def wait(self, bi, ti, slot):
        def body(t, carry):
            self._copy(bi, ti, slot, t).wait()
            return carry
        lax.fori_loop(0, self.chunk, body, 0)


def _fetch_chunk(stream, bi, ti, n_chunks):
    slot = ti % 2

    @pl.when(ti == 0)
    def _():
        stream.start(bi, ti, slot)

    stream.wait(bi, ti, slot)

    @pl.when(ti + 1 < n_chunks)
    def _():
        stream.start(bi, ti + 1, 1 - slot)

    return slot


def _mixer_kernel(x_hbm, st_shift_ref, st_wkv_ref, st_conv_ref, st_h_ref,
                  g1_ref, w_in_ref, mu_ref, w0_ref, wdec_ref, a0_ref, waaa_ref, wgate_ref,
                  kk_ref, ka_ref, rk_ref, gng_ref, gnb_ref,
                  cw_ref, cb_ref, wa_ref, ba_ref, wx_ref, bx_ref, lam_ref, og_ref,
                  seg_ref, segt_ref,
                  y_ref, o_shift_ref, o_wkv_ref, o_conv_ref, o_h_ref,
                  tm_ext, xb_ext, kap_s, rt_s, bt_s, kt_s, v_s, yt_s, pe_s, x_buf, x_sem,
                  *, bb, chunk, pos0, group):
    rows = bb * chunk
    conv_rows = (LRU_CONV_W - 1) * bb
    wkv_chunk = max(chunk, SUBLANES)
    bi = pl.program_id(0)
    ti = pl.program_id(1)
    x_stream = _TimeMajorStream(x_hbm, x_buf, x_sem, bb=bb, chunk=chunk, to_hbm=False)
    x_slot = _fetch_chunk(x_stream, bi, ti, pl.num_programs(1))

    @pl.when(ti == 0)
    def _():
        tm_ext[0:bb, :] = st_shift_ref[...]
        xb_ext[0:conv_rows, :] = st_conv_ref[...].reshape(conv_rows, D_LRU)
        o_wkv_ref[...] = st_wkv_ref[...]
        o_h_ref[...] = st_h_ref[...]
        if wkv_chunk > chunk:
            for ref in (kap_s, rt_s, bt_s, kt_s, v_s):
                ref[:, rows:wkv_chunk * bb, :] = jnp.zeros((PAIRS, wkv_chunk * bb - rows, LANES), F32)

    seg = seg_ref[...]
    segt = segt_ref[...]

    def head_sum(x):
        s = jnp.dot(x.astype(BF16), seg, preferred_element_type=F32)
        sh = s.astype(BF16)
        sl = (s - sh.astype(F32)).astype(BF16)
        return (jnp.dot(sh, segt, preferred_element_type=F32) + jnp.dot(sl, segt, preferred_element_type=F32))

    x = x_buf[x_slot].reshape(rows, D_MODEL)
    u = _dot(_rms(x, g1_ref[...]), w_in_ref[...])
    u_tm = u[:, :D_TM_PROJ]
    xb = u[:, D_TM_PROJ:D_TM_PROJ + D_LRU]
    gate_lru = u[:, D_TM_PROJ + D_LRU:D_IN_PROJ]
    tm_ext[bb:bb + rows, :] = u_tm
    xb_ext[conv_rows:conv_rows + rows, :] = xb

    row = lax.broadcasted_iota(jnp.int32, (rows, D_TM), 0)

    um = u_tm + (tm_ext[0:rows, :] - u_tm) * mu_ref[...]
    r = um[:, 0:D_TM]
    k = um[:, D_TM:2 * D_TM]
    v = um[:, 2 * D_TM:3 * D_TM]
    x_lora = um[:, LORA_OFF:GATE_OFF]
    x_gate = um[:, GATE_OFF:D_TM_PROJ]
    log_decay = -math.exp(-0.5) * _sigmoid(w0_ref[...] + _dot(jnp.tanh(x_lora), wdec_ref[...]))
    a = _sigmoid(a0_ref[...] + _dot(x_lora, waaa_ref[...]))
    gate_tm = _dot(_sigmoid(x_gate), wgate_ref[...])
    kk = k * kk_ref[...]
    k = k * (1.0 + (a - 1.0) * ka_ref[...])
    kk = kk * lax.rsqrt(jnp.maximum(head_sum(kk * kk), 1e-24))
    bonus = head_sum(r * k * rk_ref[...]) * v
    acc = log_decay[0:bb]
    cum = [acc]
    for t in range(1, chunk):
        acc = acc + log_decay[t * bb:(t + 1) * bb]
        cum.append(acc)
    c = jnp.concatenate(cum, axis=0)
    inv_p = jnp.exp(-c)
    kap = kk * jnp.exp(c - log_decay)
    rt = r * jnp.exp(c)
    bt = kk * a * inv_p
    kt = k * inv_p
    p_end = jnp.exp(cum[chunk - 1])
    for p in range(PAIRS):
        ls = slice(p * LANES, (p + 1) * LANES)
        kap_s[p, 0:rows, :] = kap[:, ls]
        rt_s[p, 0:rows, :] = rt[:, ls]
        bt_s[p, 0:rows, :] = bt[:, ls]
        kt_s[p, 0:rows, :] = kt[:, ls]
        v_s[p, 0:rows, :] = v[:, ls]
        pe_s[p] = p_end[:, ls]

    xc = cb_ref[...] + xb_ext[0:rows, :] * cw_ref[0:1, :]
    xc = xc + xb_ext[bb:bb + rows, :] * cw_ref[1:2, :]
    xc = xc + xb_ext[2 * bb:2 * bb + rows, :] * cw_ref[2:3, :]
    xc = xc + xb * cw_ref[3:4, :]
    r_g = _sigmoid(_dot(xc, wa_ref[...]) + ba_ref[...])
    i_g = _sigmoid(_dot(xc, wx_ref[...]) + bx_ref[...])
    la = jnp.exp(-LRU_C * r_g * _softplus(-lam_ref[...]))
    mult = jnp.sqrt(1.0 - la * la)
    if pos0 == 0:
        mult = jnp.where(jnp.logical_and(row < bb, ti == 0), 1.0, mult)
    lb = xc * i_g * mult
    h = o_h_ref[...]
    hs = []
    for t in range(chunk):
        ts = slice(t * bb, (t + 1) * bb)
        h = la[ts] * h + lb[ts]
        hs.append(h)
    o_h_ref[...] = h
    y_lru = _rms(jnp.concatenate(hs, axis=0) * jax.nn.gelu(gate_lru), og_ref[...])

    wc = wkv_chunk
    n_lv = max(1, (wc - 1).bit_length())
    cw2 = 2 * wc
    left = lax.broadcasted_iota(jnp.int32, (1, 1, LANES), 2) < TM_HEAD
    left_c = lax.broadcasted_iota(jnp.int32, (1, 1, cw2), 2) < wc
    ri = lax.broadcasted_iota(jnp.int32, (wc, cw2), 0)
    ci = lax.broadcasted_iota(jnp.int32, (wc, cw2), 1) & (wc - 1)
    strict = ri > ci
    incl = ri >= ci
    eye = (ri == ci).astype(F32)

    def seq_body(i, carry):
        seqs = [i * group + j for j in range(group)]
        tsel = [pl.ds(b, wc, stride=bb) for b in seqs]
        gather = lambda ref: jnp.stack([ref.at[p][ts, :] for ts in tsel for p in range(PAIRS)])
        kap_g = gather(kap_s)
        rt_g = gather(rt_s)
        bt_g = gather(bt_s)
        kt_g = gather(kt_s)
        vv = gather(v_s)
        pe = jnp.stack([pe_s.at[p][pl.ds(b, 1), :] for b in seqs for p in range(PAIRS)])
        s0 = jnp.concatenate([o_wkv_ref[b] for b in seqs], axis=0)
        lhs = jnp.concatenate([kap_g, rt_g], axis=1)
        g_b = _bdot_nt(lhs, _block_diag_rows(bt_g, left))
        g_k = _bdot_nt(lhs, _block_diag_rows(kt_g, left))
        m_ab = jnp.where(strict, g_b[:, :wc], 0.0)
        m_ak = jnp.where(strict, g_k[:, :wc], 0.0)
        m_rb = jnp.where(incl, g_b[:, wc:], 0.0)
        m_rk = jnp.where(incl, g_k[:, wc:], 0.0)
        t_inv = eye - m_ab
        m_pow = _bdot(m_ab, _block_diag_rows(m_ab, left_c))
        for lv in range(1, n_lv):
            if lv < n_lv - 1:
                prod = _bdot(jnp.concatenate([t_inv, m_pow], axis=1), _block_diag_rows(m_pow, left_c))
                t_inv = t_inv + prod[:, :wc]
                m_pow = prod[:, wc:]
            else:
                t_inv = t_inv + _bdot(t_inv, _block_diag_rows(m_pow, left_c))
        xv = _bdot(jnp.concatenate([m_ak, m_rk], axis=1), _block_diag_rows(vv, left))
        z = _bdot_nt(lhs, _block_diag_rows(s0, left))
        uu = -_bdot(t_inv, _block_diag_rows(z[:, :wc] + xv[:, :wc], left))
        yh = z[:, wc:] + _bdot(m_rb, _block_diag_rows(uu, left)) + xv[:, wc:]
        full = _bdot_tn(jnp.concatenate([uu, vv], axis=1), jnp.concatenate([bt_g, kt_g], axis=1))
        upd = jnp.where(left, full[:, :TM_HEAD], full[:, TM_HEAD:])
        s_new = (s0 + upd) * pe
        for j, (b, ts) in enumerate(zip(seqs, tsel)):
            o_wkv_ref[b] = s_new[j * PAIRS:(j + 1) * PAIRS]
            for p in range(PAIRS):
                yt_s.at[p][ts, :] = yh[j * PAIRS + p]
        return carry

    n_iter = bb // group
    if n_iter <= 2:
        for i in range(n_iter):
            seq_body(i, 0)
    else:
        lax.fori_loop(0, n_iter, seq_body, 0)

    yv = jnp.concatenate([yt_s[p, 0:rows, :] for p in range(PAIRS)], axis=-1)
    mean = head_sum(yv) * (1.0 / TM_HEAD)
    cen = yv - mean
    var = head_sum(cen * cen) * (1.0 / TM_HEAD)
    y_tm = ((cen * lax.rsqrt(var + GN_EPS)) * gng_ref[...] + gnb_ref[...] + bonus) * gate_tm
    y_ref[:, :, 0:D_TM] = y_tm.reshape(chunk, bb, D_TM)
    y_ref[:, :, D_TM:D_MODEL] = y_lru.reshape(chunk, bb, D_LRU)

    o_shift_ref[...] = tm_ext[rows:rows + bb, :]
    o_conv_ref[...] = xb_ext[rows:rows + conv_rows, :].reshape(LRU_CONV_W - 1, bb, D_LRU)
    tm_ext[0:bb, :] = tm_ext[rows:rows + bb, :]
    xb_ext[0:conv_rows, :] = xb_ext[rows:rows + conv_rows, :]


def _const_spec(shape):
    zeros = (0,) * len(shape)
    return pl.BlockSpec(shape, lambda bi, ti: zeros, pipeline_mode=pl.Buffered(1))


def _mixer(x, st_shift, st_wkv, st_conv, st_h, weights, *, bb, chunk, pos0, group):
    batch, length, _ = x.shape
    assert batch % bb == 0 and length % chunk == 0 and bb % SUBLANES == 0 and bb % group == 0
    assert chunk & (chunk - 1) == 0 and chunk >= LRU_CONV_W - 1
    rows = bb * chunk
    grid = (batch // bb, length // chunk)
    act = pl.BlockSpec((chunk, bb, D_MODEL), lambda bi, ti: (ti, bi, 0))
    vec = lambda w: pl.BlockSpec((bb, w), lambda bi, ti: (bi, 0))
    wkv_spec = pl.BlockSpec((bb, PAIRS, TM_HEAD, LANES), lambda bi, ti: (bi, 0, 0, 0))
    conv_spec = pl.BlockSpec((LRU_CONV_W - 1, bb, D_LRU), lambda bi, ti: (0, bi, 0))
    in_specs = [pl.BlockSpec(memory_space=pl.ANY), vec(D_TM_PROJ), wkv_spec, conv_spec, vec(D_LRU)]
    in_specs += [_const_spec(w.shape) for w in weights]
    out_specs = [act, vec(D_TM_PROJ), wkv_spec, conv_spec, vec(D_LRU)]
    out_shape = [jax.ShapeDtypeStruct((length, batch, D_MODEL), F32),
                 jax.ShapeDtypeStruct((batch, D_TM_PROJ), F32),
                 jax.ShapeDtypeStruct((batch, PAIRS, TM_HEAD, LANES), F32),
                 jax.ShapeDtypeStruct((LRU_CONV_W - 1, batch, D_LRU), F32),
                 jax.ShapeDtypeStruct((batch, D_LRU), F32)]
    pair_rows = lambda n: pltpu.VMEM((PAIRS, n, LANES), F32)
    scratch = [pltpu.VMEM((rows + bb, D_TM_PROJ), F32),
               pltpu.VMEM((rows + (LRU_CONV_W - 1) * bb, D_LRU), F32)]
    scratch += [pair_rows(max(chunk, SUBLANES) * bb)] * 6 + [pair_rows(bb)]
    scratch += [pltpu.VMEM((2, chunk, bb, D_MODEL), F32), pltpu.SemaphoreType.DMA((2,))]
    return pl.pallas_call(
        functools.partial(_mixer_kernel, bb=bb, chunk=chunk, pos0=pos0, group=group),
        out_shape=out_shape, grid=grid, in_specs=in_specs, out_specs=out_specs,
        scratch_shapes=scratch, name="mixer",
        compiler_params=pltpu.CompilerParams(dimension_semantics=("arbitrary", "arbitrary"),
                                             vmem_limit_bytes=VMEM_LIMIT_BYTES),
    )(x, st_shift, st_wkv, st_conv, st_h, *weights)


def _ffn_kernel(x_hbm, ym_ref, st_conv_ref,
                w_out_ref, g2_ref, w_up_ref, w_gate_ref, cw_ref, cb_ref, w_down_ref, gf_ref,
                y_hbm, o_conv_ref,
                up_ext, x_buf, x_sem, y_buf, y_sem,
                *, bb, chunk):
    rows = bb * chunk
    conv_rows = (FFN_CONV_W - 1) * bb
    bi = pl.program_id(0)
    ti = pl.program_id(1)
    n_chunks = pl.num_programs(1)
    x_stream = _TimeMajorStream(x_hbm, x_buf, x_sem, bb=bb, chunk=chunk, to_hbm=False)
    y_stream = _TimeMajorStream(y_hbm, y_buf, y_sem, bb=bb, chunk=chunk, to_hbm=True)
    slot = _fetch_chunk(x_stream, bi, ti, n_chunks)

    @pl.when(ti == 0)
    def _():
        up_ext[0:conv_rows, :] = st_conv_ref[...].reshape(conv_rows, D_FF)

    x = x_buf[slot].reshape(rows, D_MODEL)
    ym = ym_ref[...].reshape(rows, D_MODEL)
    x1 = x + _dot(ym, w_out_ref[...])
    xn = _rms(x1, g2_ref[...]).astype(BF16)
    acc = x1
    for n in range(D_FF // FF_COL_TILE):
        cols = slice(n * FF_COL_TILE, (n + 1) * FF_COL_TILE)
        up = jnp.dot(xn, w_up_ref[:, cols], preferred_element_type=F32)
        gate = jnp.dot(xn, w_gate_ref[:, cols], preferred_element_type=F32)
        up_ext[conv_rows:conv_rows + rows, cols] = up
        upc = cb_ref[:, cols] + up_ext[0:rows, cols] * cw_ref[0:1, cols]
        upc = upc + up_ext[bb:bb + rows, cols] * cw_ref[1:2, cols]
        upc = upc + up * cw_ref[2:3, cols]
        hid = (jax.nn.gelu(upc) * gate).astype(BF16)
        acc = acc + jnp.dot(hid, w_down_ref[cols, :], preferred_element_type=F32)
    o_conv_ref[...] = up_ext[rows:rows + conv_rows, :].reshape(FFN_CONV_W - 1, bb, D_FF)
    up_ext[0:conv_rows, :] = up_ext[rows:rows + conv_rows, :]

    y_buf[slot] = _rms(acc, gf_ref[...]).reshape(chunk, bb, D_MODEL)
    y_stream.start(bi, ti, slot)

    @pl.when(ti > 0)
    def _():
        y_stream.wait(bi, ti - 1, 1 - slot)

    @pl.when(ti == n_chunks - 1)
    def _():
        y_stream.wait(bi, ti, slot)


def _ffn(x, ym, st_conv, weights, *, bb, chunk):
    batch, length, _ = x.shape
    assert batch % bb == 0 and length % chunk == 0 and bb % SUBLANES == 0
    assert chunk >= FFN_CONV_W - 1
    rows = bb * chunk
    grid = (batch // bb, length // chunk)
    any_spec = pl.BlockSpec(memory_space=pl.ANY)
    act = pl.BlockSpec((chunk, bb, D_MODEL), lambda bi, ti: (ti, bi, 0))
    conv_spec = pl.BlockSpec((FFN_CONV_W - 1, bb, D_FF), lambda bi, ti: (0, bi, 0))
    in_specs = [any_spec, act, conv_spec] + [_const_spec(w.shape) for w in weights]
    out_shape = [jax.ShapeDtypeStruct((batch, length, D_MODEL), F32),
                 jax.ShapeDtypeStruct((FFN_CONV_W - 1, batch, D_FF), F32)]
    stream_buf = pltpu.VMEM((2, chunk, bb, D_MODEL), F32)
    scratch = [pltpu.VMEM((rows + (FFN_CONV_W - 1) * bb, D_FF), F32),
               stream_buf, pltpu.SemaphoreType.DMA((2,)), stream_buf, pltpu.SemaphoreType.DMA((2,))]
    return pl.pallas_call(
        functools.partial(_ffn_kernel, bb=bb, chunk=chunk),
        out_shape=out_shape, grid=grid, in_specs=in_specs, out_specs=[any_spec, conv_spec],
        scratch_shapes=scratch, name="ffn",
        compiler_params=pltpu.CompilerParams(dimension_semantics=("arbitrary", "arbitrary"),
                                             vmem_limit_bytes=VMEM_LIMIT_BYTES),
    )(x, ym, st_conv, *weights)


def _row(v):
    return v.reshape(1, -1).astype(F32)


def _block_diag(w):
    nb, n, _ = w.shape
    eye = jnp.eye(nb, dtype=w.dtype)
    return (eye[:, None, :, None] * w[:, :, None, :]).reshape(nb * n, nb * n)


def _wkv_to_pairs(s):
    b = s.shape[0]
    s = s.astype(F32).reshape(b, PAIRS, 2, TM_HEAD, TM_HEAD)
    return jnp.transpose(s, (0, 1, 3, 2, 4)).reshape(b, PAIRS, TM_HEAD, LANES)


def _wkv_from_pairs(s):
    b = s.shape[0]
    s = s.reshape(b, PAIRS, TM_HEAD, 2, TM_HEAD)
    return jnp.transpose(s, (0, 1, 3, 2, 4)).reshape(b, TM_HEADS, TM_HEAD, TM_HEAD)


def _layer(x, states, mixer_w, ffn_w, *, mixer_bb, ffn_bb, mixer_chunk, ffn_chunk, pos0, group):
    st_shift, st_wkv, st_conv, st_h, st_fconv = states
    ym, o_shift, o_wkv, o_conv, o_h = _mixer(x, st_shift, st_wkv, st_conv, st_h, mixer_w,
                                             bb=mixer_bb, chunk=mixer_chunk, pos0=pos0, group=group)
    y, o_fconv = _ffn(x, ym, st_fconv, ffn_w, bb=ffn_bb, chunk=ffn_chunk)
    return y, (o_shift, o_wkv, o_conv, o_h, o_fconv)


def kernel(x_prompt, x_sample, state_tm_shift, state_tm_wkv, state_lru_conv, state_lru_h, state_ffn_conv, meta_tokens, norm1_g, w_in, tm_mu, tm_w0, tm_w_up, tm_a0, tm_a_up, tm_g_up, tm_k_k, tm_k_a, tm_r_k, tm_gn_g, tm_gn_b, lru_conv_w, lru_conv_b, lru_wa, lru_ba, lru_wx, lru_bx, lru_lambda, lru_out_g, w_out, norm2_g, ffn_w_up, ffn_w_gate, ffn_conv_w, ffn_conv_b, ffn_w_down, norm_f_g):
    depth = w_in.shape[0]
    assert depth == 1
    l = 0
    zeros_lora = jnp.zeros((DECAY_RANK, D_TM), F32)
    head_id = jnp.arange(D_TM) // TM_HEAD
    seg = (head_id[:, None] == jnp.arange(LANES)[None, :]).astype(BF16)
    mixer_w = (
        _row(norm1_g[l]), w_in[l].astype(BF16), _row(tm_mu[l]), _row(tm_w0[l]),
        jnp.concatenate([tm_w_up[l], zeros_lora], axis=0).astype(BF16),
        _row(tm_a0[l]),
        jnp.concatenate([zeros_lora, tm_a_up[l]], axis=0).astype(BF16),
        tm_g_up[l].astype(BF16),
        _row(tm_k_k[l]), _row(tm_k_a[l]), _row(tm_r_k[l]), _row(tm_gn_g[l]), _row(tm_gn_b[l]),
        lru_conv_w[l].astype(F32), _row(lru_conv_b[l]),
        _block_diag(lru_wa[l]).astype(BF16), _row(lru_ba[l]),
        _block_diag(lru_wx[l]).astype(BF16), _row(lru_bx[l]),
        _row(lru_lambda[l]), _row(lru_out_g[l]),
        seg, seg.T,
    )
    ffn_w = (
        w_out[l].astype(BF16), _row(norm2_g[l]), ffn_w_up[l].astype(BF16), ffn_w_gate[l].astype(BF16),
        ffn_conv_w[l].astype(F32), _row(ffn_conv_b[l]), ffn_w_down[l].astype(BF16), _row(norm_f_g),
    )

    bsz, seq = x_prompt.shape[0], x_prompt.shape[1]
    x_meta = jnp.broadcast_to(meta_tokens[None].astype(F32), (bsz, N_META, D_MODEL))
    zero_states = (jnp.zeros((bsz, D_TM_PROJ), F32),
                   jnp.zeros((bsz, PAIRS, TM_HEAD, LANES), F32),
                   jnp.zeros((LRU_CONV_W - 1, bsz, D_LRU), F32),
                   jnp.zeros((bsz, D_LRU), F32),
                   jnp.zeros((FFN_CONV_W - 1, bsz, D_FF), F32))
    _, p_init = _layer(x_meta, zero_states, mixer_w, ffn_w,
                       mixer_bb=bsz, ffn_bb=bsz, mixer_chunk=N_META, ffn_chunk=N_META,
                       pos0=0, group=4)

    y_prompt, p_st = _layer(x_prompt, p_init, mixer_w, ffn_w,
                            mixer_bb=bsz, ffn_bb=bsz, mixer_chunk=64, ffn_chunk=64,
                            pos0=N_META, group=4)

    dec_b, dec_seq = x_sample.shape[0], x_sample.shape[1]
    s_init = (state_tm_shift[l].astype(F32), _wkv_to_pairs(state_tm_wkv[l]),
              jnp.transpose(state_lru_conv[l].astype(F32), (1, 0, 2)), state_lru_h[l].astype(F32),
              jnp.transpose(state_ffn_conv[l].astype(F32), (1, 0, 2)))
    y_sample, s_st = _layer(x_sample, s_init, mixer_w, ffn_w,
                            mixer_bb=32, ffn_bb=64, mixer_chunk=dec_seq, ffn_chunk=dec_seq,
                            pos0=PAST_LEN, group=8)

    def unpack(st):
        o_shift, o_wkv, o_conv, o_h, o_fconv = st
        return (o_shift[None], _wkv_from_pairs(o_wkv)[None], jnp.transpose(o_conv, (1, 0, 2))[None],
                o_h[None], jnp.transpose(o_fconv, (1, 0, 2))[None])

    return (y_prompt, y_sample) + unpack(p_st) + unpack(s_st)
```

```python
import functools
import math

import jax
import jax.numpy as jnp
from jax import lax
from jax.experimental import pallas as pl
from jax.experimental.pallas import tpu as pltpu

F32 = jnp.float32
BF16 = jnp.bfloat16

D_MODEL = 1024
N_META = 16
PAST_LEN = 16384
D_TM = 512
TM_HEAD = 64
TM_HEADS = 8
DECAY_RANK = 64
AAA_RANK = 64
GATE_RANK = 128
D_TM_PROJ = 3 * D_TM + DECAY_RANK + AAA_RANK + GATE_RANK
D_LRU = 512
LRU_CONV_W = 4
LRU_C = 8.0
D_IN_PROJ = D_TM_PROJ + 2 * D_LRU
D_FF = 3 * D_MODEL
FFN_CONV_W = 3
EPS = 1e-6
GN_EPS = 64e-5

SUBLANES = 8
LANES = 128
PAIRS = TM_HEADS // 2
LORA_OFF = 3 * D_TM
GATE_OFF = LORA_OFF + DECAY_RANK + AAA_RANK
FF_COL_TILE = 1024
VMEM_LIMIT_BYTES = 60 * 1024 * 1024


def _dot(a, b):
    return jnp.dot(a.astype(BF16), b.astype(BF16), preferred_element_type=F32)


def _bdot(a, b):
    return lax.dot_general(a.astype(BF16), b.astype(BF16), (((2,), (1,)), ((0,), (0,))),
                           preferred_element_type=F32)


def _bdot_nt(a, b):
    return lax.dot_general(a.astype(BF16), b.astype(BF16), (((2,), (2,)), ((0,), (0,))),
                           preferred_element_type=F32)


def _bdot_tn(a, b):
    return lax.dot_general(a.astype(BF16), b.astype(BF16), (((1,), (1,)), ((0,), (0,))),
                           preferred_element_type=F32)


def _rms(x, g):
    return x * lax.rsqrt(jnp.mean(x * x, axis=-1, keepdims=True) + EPS) * g


def _sigmoid(z):
    return 0.5 * jnp.tanh(0.5 * z) + 0.5


def _softplus(z):
    return jnp.maximum(z, 0.0) + jnp.log(1.0 + jnp.exp(-jnp.abs(z)))


def _block_diag_rows(x, left):
    return jnp.concatenate([jnp.where(left, x, 0.0), jnp.where(left, 0.0, x)], axis=1)


class _TimeMajorStream:
    def __init__(self, hbm, buf, sem, *, bb, chunk, to_hbm):
        self.hbm, self.buf, self.sem = hbm, buf, sem
        self.bb, self.chunk, self.to_hbm = bb, chunk, to_hbm

    def _copy(self, bi, ti, slot, t):
        seqs = pl.ds(pl.multiple_of(bi * self.bb, SUBLANES), self.bb)
        hbm_rows = self.hbm.at[seqs, ti * self.chunk + t, :]
        tile = self.buf.at[slot, t]
        src, dst = (tile, hbm_rows) if self.to_hbm else (hbm_rows, tile)
        return pltpu.make_async_copy(src, dst, self.sem.at[slot])

    def start(self, bi, ti, slot):
        for t in range(self.chunk):
            self._copy(bi, ti, slot, t).start()

    def wait(self, bi, ti, slot):
        for t in range(self.chunk):
            self._copy(bi, ti, slot, t).wait()


READ_SLOTS = 3


def _read_chunk_begin(stream, bi, ti, n_chunks):
    last = n_chunks - 1

    @pl.when(ti == 0)
    def _():
        stream.start(bi, 0, 0)
        stream.start(bi, jnp.minimum(1, last), 1)

    slot = lax.rem(ti, READ_SLOTS)
    stream.wait(bi, ti, slot)
    return slot


def _read_chunk_end(stream, bi, ti, n_chunks):
    last = n_chunks - 1
    stream.start(bi, jnp.minimum(ti + 2, last), lax.rem(ti + 2, READ_SLOTS))

    @pl.when(ti == last)
    def _():
        stream.wait(bi, last, lax.rem(ti + 1, READ_SLOTS))
        stream.wait(bi, last, lax.rem(ti + 2, READ_SLOTS))


def _mixer_kernel(x_hbm, st_shift_ref, st_wkv_ref, st_conv_ref, st_h_ref,
                  g1_ref, w_in_ref, mu_ref, w0_ref, wdec_ref, a0_ref, waaa_ref, wgate_ref,
                  kk_ref, ka_ref, rk_ref, gng_ref, gnb_ref,
                  cw_ref, cb_ref, wa_ref, ba_ref, wx_ref, bx_ref, lam_ref, og_ref,
                  seg_ref, segt_ref, w_out_ref,
                  x1_ref, o_shift_ref, o_wkv_ref, o_conv_ref, o_h_ref,
                  tm_ext, xb_ext, kap_s, rt_s, bt_s, kt_s, v_s, yt_s, pe_s, x_buf, x_sem,
                  *, bb, chunk, pos0, group):
    rows = bb * chunk
    conv_rows = (LRU_CONV_W - 1) * bb
    wkv_chunk = max(chunk, SUBLANES)
    bi = pl.program_id(0)
    ti = pl.program_id(1)
    n_chunks = pl.num_programs(1)
    x_stream = _TimeMajorStream(x_hbm, x_buf, x_sem, bb=bb, chunk=chunk, to_hbm=False)
    x_slot = _read_chunk_begin(x_stream, bi, ti, n_chunks)

    @pl.when(ti == 0)
    def _():
        tm_ext[0:bb, :] = st_shift_ref[...]
        xb_ext[0:conv_rows, :] = st_conv_ref[...].reshape(conv_rows, D_LRU)
        o_wkv_ref[...] = st_wkv_ref[...]
        o_h_ref[...] = st_h_ref[...]
        if wkv_chunk > chunk:
            for ref in (kap_s, rt_s, bt_s, kt_s, v_s):
                ref[:, rows:wkv_chunk * bb, :] = jnp.zeros((PAIRS, wkv_chunk * bb - rows, LANES), F32)

    seg = seg_ref[...]
    segt = segt_ref[...]

    def head_sum(x):
        s = jnp.dot(x.astype(BF16), seg, preferred_element_type=F32)
        sh = s.astype(BF16)
        sl = (s - sh.astype(F32)).astype(BF16)
        return (jnp.dot(sh, segt, preferred_element_type=F32) + jnp.dot(sl, segt, preferred_element_type=F32))

    x = x_buf[x_slot].reshape(rows, D_MODEL)
    u = _dot(_rms(x, g1_ref[...]), w_in_ref[...])
    u_tm = u[:, :D_TM_PROJ]
    xb = u[:, D_TM_PROJ:D_TM_PROJ + D_LRU]
    gate_lru = u[:, D_TM_PROJ + D_LRU:D_IN_PROJ]
    tm_ext[bb:bb + rows, :] = u_tm
    xb_ext[conv_rows:conv_rows + rows, :] = xb

    row = lax.broadcasted_iota(jnp.int32, (rows, D_TM), 0)

    um = u_tm + (tm_ext[0:rows, :] - u_tm) * mu_ref[...]
    r = um[:, 0:D_TM]
    k = um[:, D_TM:2 * D_TM]
    v = um[:, 2 * D_TM:3 * D_TM]
    x_lora = um[:, LORA_OFF:GATE_OFF]
    x_gate = um[:, GATE_OFF:D_TM_PROJ]
    log_decay = -math.exp(-0.5) * _sigmoid(w0_ref[...] + _dot(jnp.tanh(x_lora), wdec_ref[...]))
    a = _sigmoid(a0_ref[...] + _dot(x_lora, waaa_ref[...]))
    gate_tm = _dot(_sigmoid(x_gate), wgate_ref[...])
    kk = k * kk_ref[...]
    k = k * (1.0 + (a - 1.0) * ka_ref[...])
    kk = kk * lax.rsqrt(jnp.maximum(head_sum(kk * kk), 1e-24))
    bonus = head_sum(r * k * rk_ref[...]) * v
    acc = log_decay[0:bb]
    cum = [acc]
    for t in range(1, chunk):
        acc = acc + log_decay[t * bb:(t + 1) * bb]
        cum.append(acc)
    c = jnp.concatenate(cum, axis=0)
    inv_p = jnp.exp(-c)
    kap = kk * jnp.exp(c - log_decay)
    rt = r * jnp.exp(c)
    bt = kk * a * inv_p
    kt = k * inv_p
    p_end = jnp.exp(cum[chunk - 1])
    for p in range(PAIRS):
        ls = slice(p * LANES, (p + 1) * LANES)
        kap_s[p, 0:rows, :] = kap[:, ls]
        rt_s[p, 0:rows, :] = rt[:, ls]
        bt_s[p, 0:rows, :] = bt[:, ls]
        kt_s[p, 0:rows, :] = kt[:, ls]
        v_s[p, 0:rows, :] = v[:, ls]
        pe_s[p] = p_end[:, ls]

    xc = cb_ref[...] + xb_ext[0:rows, :] * cw_ref[0:1, :]
    xc = xc + xb_ext[bb:bb + rows, :] * cw_ref[1:2, :]
    xc = xc + xb_ext[2 * bb:2 * bb + rows, :] * cw_ref[2:3, :]
    xc = xc + xb * cw_ref[3:4, :]
    r_g = _sigmoid(_dot(xc, wa_ref[...]) + ba_ref[...])
    i_g = _sigmoid(_dot(xc, wx_ref[...]) + bx_ref[...])
    la = jnp.exp(-LRU_C * r_g * _softplus(-lam_ref[...]))
    mult = jnp.sqrt(1.0 - la * la)
    if pos0 == 0:
        mult = jnp.where(jnp.logical_and(row < bb, ti == 0), 1.0, mult)
    lb = xc * i_g * mult
    h = o_h_ref[...]
    hs = []
    for t in range(chunk):
        ts = slice(t * bb, (t + 1) * bb)
        h = la[ts] * h + lb[ts]
        hs.append(h)
    o_h_ref[...] = h
    y_lru = _rms(jnp.concatenate(hs, axis=0) * jax.nn.gelu(gate_lru), og_ref[...])

    wc = wkv_chunk
    n_lv = max(1, (wc - 1).bit_length())
    cw2 = 2 * wc
    left = lax.broadcasted_iota(jnp.int32, (1, 1, LANES), 2) < TM_HEAD
    left_c = lax.broadcasted_iota(jnp.int32, (1, 1, cw2), 2) < wc
    ri = lax.broadcasted_iota(jnp.int32, (wc, cw2), 0)
    ci = lax.broadcasted_iota(jnp.int32, (wc, cw2), 1) & (wc - 1)
    strict = ri > ci
    incl = ri >= ci
    eye = (ri == ci).astype(F32)

    def seq_body(i, carry):
        seqs = [i * group + j for j in range(group)]
        tsel = [pl.ds(b, wc, stride=bb) for b in seqs]
        gather = lambda ref: jnp.stack([ref.at[p][ts, :] for ts in tsel for p in range(PAIRS)])
        kap_g = gather(kap_s)
        rt_g = gather(rt_s)
        bt_g = gather(bt_s)
        kt_g = gather(kt_s)
        vv = gather(v_s)
        pe = jnp.stack([pe_s.at[p][pl.ds(b, 1), :] for b in seqs for p in range(PAIRS)])
        s0 = jnp.concatenate([o_wkv_ref[b] for b in seqs], axis=0)
        lhs = jnp.concatenate([kap_g, rt_g], axis=1)
        g_b = _bdot_nt(lhs, _block_diag_rows(bt_g, left))
        g_k = _bdot_nt(lhs, _block_diag_rows(kt_g, left))
        m_ab = jnp.where(strict, g_b[:, :wc], 0.0)
        m_ak = jnp.where(strict, g_k[:, :wc], 0.0)
        m_rb = jnp.where(incl, g_b[:, wc:], 0.0)
        m_rk = jnp.where(incl, g_k[:, wc:], 0.0)
        t_inv = eye - m_ab
        m_pow = _bdot(m_ab, _block_diag_rows(m_ab, left_c))
        for lv in range(1, n_lv):
            if lv < n_lv - 1:
                prod = _bdot(jnp.concatenate([t_inv, m_pow], axis=1), _block_diag_rows(m_pow, left_c))
                t_inv = t_inv + prod[:, :wc]
                m_pow = prod[:, wc:]
            else:
                t_inv = t_inv + _bdot(t_inv, _block_diag_rows(m_pow, left_c))
        xv = _bdot(jnp.concatenate([m_ak, m_rk], axis=1), _block_diag_rows(vv, left))
        z = _bdot_nt(lhs, _block_diag_rows(s0, left))
        uu = -_bdot(t_inv, _block_diag_rows(z[:, :wc] + xv[:, :wc], left))
        yh = z[:, wc:] + _bdot(m_rb, _block_diag_rows(uu, left)) + xv[:, wc:]
        full = _bdot_tn(jnp.concatenate([uu, vv], axis=1), jnp.concatenate([bt_g, kt_g], axis=1))
        upd = jnp.where(left, full[:, :TM_HEAD], full[:, TM_HEAD:])
        s_new = (s0 + upd) * pe
        for j, (b, ts) in enumerate(zip(seqs, tsel)):
            o_wkv_ref[b] = s_new[j * PAIRS:(j + 1) * PAIRS]
            for p in range(PAIRS):
                yt_s.at[p][ts, :] = yh[j * PAIRS + p]
        return carry

    n_iter = bb // group
    if n_iter <= 2:
        for i in range(n_iter):
            seq_body(i, 0)
    else:
        lax.fori_loop(0, n_iter, seq_body, 0)

    yv = jnp.concatenate([yt_s[p, 0:rows, :] for p in range(PAIRS)], axis=-1)
    mean = head_sum(yv) * (1.0 / TM_HEAD)
    cen = yv - mean
    var = head_sum(cen * cen) * (1.0 / TM_HEAD)
    y_tm = ((cen * lax.rsqrt(var + GN_EPS)) * gng_ref[...] + gnb_ref[...] + bonus) * gate_tm
    mixed = jnp.concatenate([y_tm.astype(BF16), y_lru.astype(BF16)], axis=-1)
    x1 = x_buf[x_slot].reshape(rows, D_MODEL) + jnp.dot(mixed, w_out_ref[...], preferred_element_type=F32)
    x1_ref[...] = x1.reshape(chunk, bb, D_MODEL)

    o_shift_ref[...] = tm_ext[rows:rows + bb, :]
    o_conv_ref[...] = xb_ext[rows:rows + conv_rows, :].reshape(LRU_CONV_W - 1, bb, D_LRU)
    tm_ext[0:bb, :] = tm_ext[rows:rows + bb, :]
    xb_ext[0:conv_rows, :] = xb_ext[rows:rows + conv_rows, :]
    _read_chunk_end(x_stream, bi, ti, n_chunks)


def _const_spec(shape):
    zeros = (0,) * len(shape)
    return pl.BlockSpec(shape, lambda bi, ti: zeros, pipeline_mode=pl.Buffered(1))


def _mixer(x, st_shift, st_wkv, st_conv, st_h, weights, *, bb, chunk, pos0, group):
    batch, length, _ = x.shape
    assert batch % bb == 0 and length % chunk == 0 and bb % SUBLANES == 0 and bb % group == 0
    assert chunk & (chunk - 1) == 0 and chunk >= LRU_CONV_W - 1
    rows = bb * chunk
    grid = (batch // bb, length // chunk)
    act = pl.BlockSpec((chunk, bb, D_MODEL), lambda bi, ti: (ti, bi, 0))
    vec = lambda w: pl.BlockSpec((bb, w), lambda bi, ti: (bi, 0))
    wkv_spec = pl.BlockSpec((bb, PAIRS, TM_HEAD, LANES), lambda bi, ti: (bi, 0, 0, 0))
    conv_spec = pl.BlockSpec((LRU_CONV_W - 1, bb, D_LRU), lambda bi, ti: (0, bi, 0))
    in_specs = [pl.BlockSpec(memory_space=pl.ANY), vec(D_TM_PROJ), wkv_spec, conv_spec, vec(D_LRU)]
    in_specs += [_const_spec(w.shape) for w in weights]
    out_specs = [act, vec(D_TM_PROJ), wkv_spec, conv_spec, vec(D_LRU)]
    out_shape = [jax.ShapeDtypeStruct((length, batch, D_MODEL), F32),
                 jax.ShapeDtypeStruct((batch, D_TM_PROJ), F32),
                 jax.ShapeDtypeStruct((batch, PAIRS, TM_HEAD, LANES), F32),
                 jax.ShapeDtypeStruct((LRU_CONV_W - 1, batch, D_LRU), F32),
                 jax.ShapeDtypeStruct((batch, D_LRU), F32)]
    pair_rows = lambda n: pltpu.VMEM((PAIRS, n, LANES), F32)
    scratch = [pltpu.VMEM((rows + bb, D_TM_PROJ), F32),
               pltpu.VMEM((rows + (LRU_CONV_W - 1) * bb, D_LRU), F32)]
    scratch += [pair_rows(max(chunk, SUBLANES) * bb)] * 6 + [pair_rows(bb)]
    scratch += [pltpu.VMEM((READ_SLOTS, chunk, bb, D_MODEL), F32), pltpu.SemaphoreType.DMA((READ_SLOTS,))]
    return pl.pallas_call(
        functools.partial(_mixer_kernel, bb=bb, chunk=chunk, pos0=pos0, group=group),
        out_shape=out_shape, grid=grid, in_specs=in_specs, out_specs=out_specs,
        scratch_shapes=scratch, name="mixer",
        compiler_params=pltpu.CompilerParams(dimension_semantics=("arbitrary", "arbitrary"),
                                             vmem_limit_bytes=VMEM_LIMIT_BYTES),
    )(x, st_shift, st_wkv, st_conv, st_h, *weights)


def _ffn_kernel(x1_ref, st_conv_ref,
                g2_ref, w_up_ref, w_gate_ref, cw_ref, cb_ref, w_down_ref, gf_ref,
                y_hbm, o_conv_ref,
                up_ext, y_buf, y_sem,
                *, bb, chunk):
    rows = bb * chunk
    conv_rows = (FFN_CONV_W - 1) * bb
    bi = pl.program_id(0)
    ti = pl.program_id(1)
    n_chunks = pl.num_programs(1)
    y_stream = _TimeMajorStream(y_hbm, y_buf, y_sem, bb=bb, chunk=chunk, to_hbm=True)
    slot = lax.rem(ti, 2)

    @pl.when(ti == 0)
    def _():
        up_ext[0:conv_rows, :] = st_conv_ref[...].reshape(conv_rows, D_FF)

    x1 = x1_ref[...].reshape(rows, D_MODEL)
    xn = _rms(x1, g2_ref[...]).astype(BF16)
    acc = x1
    for n in range(D_FF // FF_COL_TILE):
        cols = slice(n * FF_COL_TILE, (n + 1) * FF_COL_TILE)
        up = jnp.dot(xn, w_up_ref[:, cols], preferred_element_type=F32)
        gate = jnp.dot(xn, w_gate_ref[:, cols], preferred_element_type=F32)
        up_ext[conv_rows:conv_rows + rows, cols] = up
        upc = cb_ref[:, cols] + up_ext[0:rows, cols] * cw_ref[0:1, cols]
        upc = upc + up_ext[bb:bb + rows, cols] * cw_ref[1:2, cols]
        upc = upc + up * cw_ref[2:3, cols]
        hid = (jax.nn.gelu(upc) * gate).astype(BF16)
        acc = acc + jnp.dot(hid, w_down_ref[cols, :], preferred_element_type=F32)
    o_conv_ref[...] = up_ext[rows:rows + conv_rows, :].reshape(FFN_CONV_W - 1, bb, D_FF)
    up_ext[0:conv_rows, :] = up_ext[rows:rows + conv_rows, :]

    y_buf[slot] = _rms(acc, gf_ref[...]).reshape(chunk, bb, D_MODEL)
    y_stream.start(bi, ti, slot)

    @pl.when(ti > 0)
    def _():
        y_stream.wait(bi, ti - 1, 1 - slot)

    @pl.when(ti == n_chunks - 1)
    def _():
        y_stream.wait(bi, ti, slot)


def _ffn(x1, st_conv, weights, *, bb, chunk):
    length, batch, _ = x1.shape
    assert batch % bb == 0 and length % chunk == 0 and bb % SUBLANES == 0
    assert chunk >= FFN_CONV_W - 1
    rows = bb * chunk
    grid = (batch // bb, length // chunk)
    act = pl.BlockSpec((chunk, bb, D_MODEL), lambda bi, ti: (ti, bi, 0))
    conv_spec = pl.BlockSpec((FFN_CONV_W - 1, bb, D_FF), lambda bi, ti: (0, bi, 0))
    in_specs = [act, conv_spec] + [_const_spec(w.shape) for w in weights]
    out_shape = [jax.ShapeDtypeStruct((batch, length, D_MODEL), F32),
                 jax.ShapeDtypeStruct((FFN_CONV_W - 1, batch, D_FF), F32)]
    scratch = [pltpu.VMEM((rows + (FFN_CONV_W - 1) * bb, D_FF), F32),
               pltpu.VMEM((2, chunk, bb, D_MODEL), F32), pltpu.SemaphoreType.DMA((2,))]
    return pl.pallas_call(
        functools.partial(_ffn_kernel, bb=bb, chunk=chunk),
        out_shape=out_shape, grid=grid, in_specs=in_specs,
        out_specs=[pl.BlockSpec(memory_space=pl.ANY), conv_spec],
        scratch_shapes=scratch, name="ffn",
        compiler_params=pltpu.CompilerParams(dimension_semantics=("arbitrary", "arbitrary"),
                                             vmem_limit_bytes=VMEM_LIMIT_BYTES),
    )(x1, st_conv, *weights)


def _row(v):
    return v.reshape(1, -1).astype(F32)


def _block_diag(w):
    nb, n, _ = w.shape
    eye = jnp.eye(nb, dtype=w.dtype)
    return (eye[:, None, :, None] * w[:, :, None, :]).reshape(nb * n, nb * n)


def _wkv_to_pairs(s):
    b = s.shape[0]
    s = s.astype(F32).reshape(b, PAIRS, 2, TM_HEAD, TM_HEAD)
    return jnp.transpose(s, (0, 1, 3, 2, 4)).reshape(b, PAIRS, TM_HEAD, LANES)


def _wkv_from_pairs(s):
    b = s.shape[0]
    s = s.reshape(b, PAIRS, TM_HEAD, 2, TM_HEAD)
    return jnp.transpose(s, (0, 1, 3, 2, 4)).reshape(b, TM_HEADS, TM_HEAD, TM_HEAD)


def _layer(x, states, mixer_w, ffn_w, *, mixer_bb, ffn_bb, mixer_chunk, ffn_chunk, pos0, group):
    st_shift, st_wkv, st_conv, st_h, st_fconv = states
    x1, o_shift, o_wkv, o_conv, o_h = _mixer(x, st_shift, st_wkv, st_conv, st_h, mixer_w,
                                             bb=mixer_bb, chunk=mixer_chunk, pos0=pos0, group=group)
    y, o_fconv = _ffn(x1, st_fconv, ffn_w, bb=ffn_bb, chunk=ffn_chunk)
    return y, (o_shift, o_wkv, o_conv, o_h, o_fconv)


def kernel(x_prompt, x_sample, state_tm_shift, state_tm_wkv, state_lru_conv, state_lru_h, state_ffn_conv, meta_tokens, norm1_g, w_in, tm_mu, tm_w0, tm_w_up, tm_a0, tm_a_up, tm_g_up, tm_k_k, tm_k_a, tm_r_k, tm_gn_g, tm_gn_b, lru_conv_w, lru_conv_b, lru_wa, lru_ba, lru_wx, lru_bx, lru_lambda, lru_out_g, w_out, norm2_g, ffn_w_up, ffn_w_gate, ffn_conv_w, ffn_conv_b, ffn_w_down, norm_f_g):
    depth = w_in.shape[0]
    assert depth == 1
    l = 0
    zeros_lora = jnp.zeros((DECAY_RANK, D_TM), F32)
    head_id = jnp.arange(D_TM) // TM_HEAD
    seg = (head_id[:, None] == jnp.arange(LANES)[None, :]).astype(BF16)
    mixer_w = (
        _row(norm1_g[l]), w_in[l].astype(BF16), _row(tm_mu[l]), _row(tm_w0[l]),
        jnp.concatenate([tm_w_up[l], zeros_lora], axis=0).astype(BF16),
        _row(tm_a0[l]),
        jnp.concatenate([zeros_lora, tm_a_up[l]], axis=0).astype(BF16),
        tm_g_up[l].astype(BF16),
        _row(tm_k_k[l]), _row(tm_k_a[l]), _row(tm_r_k[l]), _row(tm_gn_g[l]), _row(tm_gn_b[l]),
        lru_conv_w[l].astype(F32), _row(lru_conv_b[l]),
        _block_diag(lru_wa[l]).astype(BF16), _row(lru_ba[l]),
        _block_diag(lru_wx[l]).astype(BF16), _row(lru_bx[l]),
        _row(lru_lambda[l]), _row(lru_out_g[l]),
        seg, seg.T, w_out[l].astype(BF16),
    )
    ffn_w = (
        _row(norm2_g[l]), ffn_w_up[l].astype(BF16), ffn_w_gate[l].astype(BF16),
        ffn_conv_w[l].astype(F32), _row(ffn_conv_b[l]), ffn_w_down[l].astype(BF16), _row(norm_f_g),
    )

    bsz, seq = x_prompt.shape[0], x_prompt.shape[1]
    x_meta = jnp.broadcast_to(meta_tokens[None].astype(F32), (bsz, N_META, D_MODEL))
    zero_states = (jnp.zeros((bsz, D_TM_PROJ), F32),
                   jnp.zeros((bsz, PAIRS, TM_HEAD, LANES), F32),
                   jnp.zeros((LRU_CONV_W - 1, bsz, D_LRU), F32),
                   jnp.zeros((bsz, D_LRU), F32),
                   jnp.zeros((FFN_CONV_W - 1, bsz, D_FF), F32))
    _, p_init = _layer(x_meta, zero_states, mixer_w, ffn_w,
                       mixer_bb=bsz, ffn_bb=bsz, mixer_chunk=N_META, ffn_chunk=N_META,
                       pos0=0, group=4)

    y_prompt, p_st = _layer(x_prompt, p_init, mixer_w, ffn_w,
                            mixer_bb=bsz, ffn_bb=bsz, mixer_chunk=64, ffn_chunk=64,
                            pos0=N_META, group=4)

    dec_b, dec_seq = x_sample.shape[0], x_sample.shape[1]
    s_init = (state_tm_shift[l].astype(F32), _wkv_to_pairs(state_tm_wkv[l]),
              jnp.transpose(state_lru_conv[l].astype(F32), (1, 0, 2)), state_lru_h[l].astype(F32),
              jnp.transpose(state_ffn_conv[l].astype(F32), (1, 0, 2)))
    y_sample, s_st = _layer(x_sample, s_init, mixer_w, ffn_w,
                            mixer_bb=32, ffn_bb=64, mixer_chunk=dec_seq, ffn_chunk=dec_seq,
                            pos0=PAST_LEN, group=8)

    def unpack(st):
        o_shift, o_wkv, o_conv, o_h, o_fconv = st
        return (o_shift[None], _wkv_from_pairs(o_wkv)[None], jnp.transpose(o_conv, (1, 0, 2))[None],
                o_h[None], jnp.transpose(o_fconv, (1, 0, 2))[None])

    return (y_prompt, y_sample) + unpack(p_st) + unpack(s_st)
```

```python
import functools
import math

import jax
import jax.numpy as jnp
from jax import lax
from jax.experimental import pallas as pl
from jax.experimental.pallas import tpu as pltpu

F32 = jnp.float32
BF16 = jnp.bfloat16

D_MODEL = 1024
N_META = 16
PAST_LEN = 16384
D_TM = 512
TM_HEAD = 64
TM_HEADS = 8
DECAY_RANK = 64
AAA_RANK = 64
GATE_RANK = 128
D_TM_PROJ = 3 * D_TM + DECAY_RANK + AAA_RANK + GATE_RANK
D_LRU = 512
LRU_CONV_W = 4
LRU_C = 8.0
D_IN_PROJ = D_TM_PROJ + 2 * D_LRU
D_FF = 3 * D_MODEL
FFN_CONV_W = 3
EPS = 1e-6
GN_EPS = 64e-5

SUBLANES = 8
LANES = 128
PAIRS = TM_HEADS // 2
LORA_OFF = 3 * D_TM
GATE_OFF = LORA_OFF + DECAY_RANK + AAA_RANK
FF_COL_TILE = 1024
VMEM_LIMIT_BYTES = 60 * 1024 * 1024


def _dot(a, b):
    return jnp.dot(a.astype(BF16), b.astype(BF16), preferred_element_type=F32)


def _bdot(a, b):
    return lax.dot_general(a.astype(BF16), b.astype(BF16), (((2,), (1,)), ((0,), (0,))),
                           preferred_element_type=F32)


def _bdot_nt(a, b):
    return lax.dot_general(a.astype(BF16), b.astype(BF16), (((2,), (2,)), ((0,), (0,))),
                           preferred_element_type=F32)


def _bdot_tn(a, b):
    return lax.dot_general(a.astype(BF16), b.astype(BF16), (((1,), (1,)), ((0,), (0,))),
                           preferred_element_type=F32)


def _rms(x, g):
    return x * lax.rsqrt(jnp.mean(x * x, axis=-1, keepdims=True) + EPS) * g


def _sigmoid(z):
    return 0.5 * jnp.tanh(0.5 * z) + 0.5


def _softplus(z):
    return jnp.maximum(z, 0.0) + jnp.log(1.0 + jnp.exp(-jnp.abs(z)))


def _block_diag_rows(x, left):
    return jnp.concatenate([jnp.where(left, x, 0.0), jnp.where(left, 0.0, x)], axis=1)


class _TimeMajorStream:
    def __init__(self, hbm, buf, sem, *, bb, chunk, to_hbm):
        self.hbm, self.buf, self.sem = hbm, buf, sem
        self.bb, self.chunk, self.to_hbm = bb, chunk, to_hbm

    def _copy(self, bi, ti, slot, t):
        seqs = pl.ds(pl.multiple_of(bi * self.bb, SUBLANES), self.bb)
        hbm_rows = self.hbm.at[seqs, ti * self.chunk + t, :]
        tile = self.buf.at[slot, t]
        src, dst = (tile, hbm_rows) if self.to_hbm else (hbm_rows, tile)
        return pltpu.make_async_copy(src, dst, self.sem.at[slot])

    def start(self, bi, ti, slot):
        for t in range(self.chunk):
            self._copy(bi, ti, slot, t).start()

    def wait(self, bi, ti, slot):
        for t in range(self.chunk):
            self._copy(bi, ti, slot, t).wait()


READ_SLOTS = 3


def _read_chunk_begin(stream, bi, ti, n_chunks):
    last = n_chunks - 1

    @pl.when(ti == 0)
    def _():
        stream.start(bi, 0, 0)
        stream.start(bi, jnp.minimum(1, last), 1)

    slot = lax.rem(ti, READ_SLOTS)
    stream.wait(bi, ti, slot)
    return slot


def _read_chunk_end(stream, bi, ti, n_chunks):
    last = n_chunks - 1
    stream.start(bi, jnp.minimum(ti + 2, last), lax.rem(ti + 2, READ_SLOTS))

    @pl.when(ti == last)
    def _():
        stream.wait(bi, last, lax.rem(ti + 1, READ_SLOTS))
        stream.wait(bi, last, lax.rem(ti + 2, READ_SLOTS))


def _mixer_kernel(x_hbm, st_shift_ref, st_wkv_ref, st_conv_ref, st_h_ref,
                  g1_ref, w_in_ref, mu_ref, w0_ref, wdec_ref, a0_ref, waaa_ref, wgate_ref,
                  kk_ref, ka_ref, rk_ref, gng_ref, gnb_ref,
                  cw_ref, cb_ref, wa_ref, ba_ref, wx_ref, bx_ref, lam_ref, og_ref,
                  seg_ref, segt_ref, w_out_ref,
                  x1_ref, o_shift_ref, o_wkv_ref, o_conv_ref, o_h_ref,
                  tm_ext, xb_ext, kap_s, rt_s, bt_s, kt_s, v_s, yt_s, pe_s, x_buf, x_sem,
                  *, bb, chunk, pos0, group):
    rows = bb * chunk
    conv_rows = (LRU_CONV_W - 1) * bb
    wkv_chunk = max(chunk, SUBLANES)
    bi = pl.program_id(0)
    ti = pl.program_id(1)
    n_chunks = pl.num_programs(1)
    x_stream = _TimeMajorStream(x_hbm, x_buf, x_sem, bb=bb, chunk=chunk, to_hbm=False)
    x_slot = _read_chunk_begin(x_stream, bi, ti, n_chunks)

    @pl.when(ti == 0)
    def _():
        tm_ext[0:bb, :] = st_shift_ref[...]
        xb_ext[0:conv_rows, :] = st_conv_ref[...].reshape(conv_rows, D_LRU)
        o_wkv_ref[...] = st_wkv_ref[...]
        o_h_ref[...] = st_h_ref[...]
        if wkv_chunk > chunk:
            for ref in (kap_s, rt_s, bt_s, kt_s, v_s):
                ref[:, rows:wkv_chunk * bb, :] = jnp.zeros((PAIRS, wkv_chunk * bb - rows, LANES), F32)

    seg = seg_ref[...]
    segt = segt_ref[...]

    def head_sum(x):
        s = jnp.dot(x.astype(BF16), seg, preferred_element_type=F32)
        return jnp.dot(s.astype(BF16), segt, preferred_element_type=F32)

    x = x_buf[x_slot].reshape(rows, D_MODEL)
    u = _dot(_rms(x, g1_ref[...]), w_in_ref[...])
    u_tm = u[:, :D_TM_PROJ]
    xb = u[:, D_TM_PROJ:D_TM_PROJ + D_LRU]
    gate_lru = u[:, D_TM_PROJ + D_LRU:D_IN_PROJ]
    tm_ext[bb:bb + rows, :] = u_tm
    xb_ext[conv_rows:conv_rows + rows, :] = xb

    row = lax.broadcasted_iota(jnp.int32, (rows, D_TM), 0)

    um = u_tm + (tm_ext[0:rows, :] - u_tm) * mu_ref[...]
    r = um[:, 0:D_TM]
    k = um[:, D_TM:2 * D_TM]
    v = um[:, 2 * D_TM:3 * D_TM]
    x_lora = um[:, LORA_OFF:GATE_OFF]
    x_gate = um[:, GATE_OFF:D_TM_PROJ]
    log_decay = -math.exp(-0.5) * _sigmoid(w0_ref[...] + _dot(jnp.tanh(x_lora), wdec_ref[...]))
    a = _sigmoid(a0_ref[...] + _dot(x_lora, waaa_ref[...]))
    gate_tm = _dot(_sigmoid(x_gate), wgate_ref[...])
    kk = k * kk_ref[...]
    k = k * (1.0 + (a - 1.0) * ka_ref[...])
    kk = kk * lax.rsqrt(jnp.maximum(head_sum(kk * kk), 1e-24))
    bonus = head_sum(r * k * rk_ref[...]) * v
    acc = log_decay[0:bb]
    cum = [acc]
    for t in range(1, chunk):
        acc = acc + log_decay[t * bb:(t + 1) * bb]
        cum.append(acc)
    c = jnp.concatenate(cum, axis=0)
    inv_p = jnp.exp(-c)
    kap = kk * jnp.exp(c - log_decay)
    rt = r * jnp.exp(c)
    bt = kk * a * inv_p
    kt = k * inv_p
    p_end = jnp.exp(cum[chunk - 1])
    for p in range(PAIRS):
        ls = slice(p * LANES, (p + 1) * LANES)
        kap_s[p, 0:rows, :] = kap[:, ls]
        rt_s[p, 0:rows, :] = rt[:, ls]
        bt_s[p, 0:rows, :] = bt[:, ls]
        kt_s[p, 0:rows, :] = kt[:, ls]
        v_s[p, 0:rows, :] = v[:, ls]
        pe_s[p] = p_end[:, ls]

    xc = cb_ref[...] + xb_ext[0:rows, :] * cw_ref[0:1, :]
    xc = xc + xb_ext[bb:bb + rows, :] * cw_ref[1:2, :]
    xc = xc + xb_ext[2 * bb:2 * bb + rows, :] * cw_ref[2:3, :]
    xc = xc + xb * cw_ref[3:4, :]
    xc_bf = xc.astype(BF16)

    def block_gate(w_ref):
        return jnp.concatenate(
            [jnp.dot(xc_bf[:, p * LANES:(p + 1) * LANES], w_ref[p], preferred_element_type=F32)
             for p in range(D_LRU // LANES)], axis=-1)

    r_g = _sigmoid(block_gate(wa_ref) + ba_ref[...])
    i_g = _sigmoid(block_gate(wx_ref) + bx_ref[...])
    la = jnp.exp(-LRU_C * r_g * _softplus(-lam_ref[...]))
    mult = jnp.sqrt(1.0 - la * la)
    if pos0 == 0:
        mult = jnp.where(jnp.logical_and(row < bb, ti == 0), 1.0, mult)
    lb = xc * i_g * mult
    h = o_h_ref[...]
    hs = []
    for t in range(chunk):
        ts = slice(t * bb, (t + 1) * bb)
        h = la[ts] * h + lb[ts]
        hs.append(h)
    o_h_ref[...] = h
    y_lru = _rms(jnp.concatenate(hs, axis=0) * jax.nn.gelu(gate_lru), og_ref[...])

    wc = wkv_chunk
    n_lv = max(1, (wc - 1).bit_length())
    cw2 = 2 * wc
    left = lax.broadcasted_iota(jnp.int32, (1, 1, LANES), 2) < TM_HEAD
    left_c = lax.broadcasted_iota(jnp.int32, (1, 1, cw2), 2) < wc
    ri = lax.broadcasted_iota(jnp.int32, (wc, cw2), 0)
    ci = lax.broadcasted_iota(jnp.int32, (wc, cw2), 1) & (wc - 1)
    strict = ri > ci
    incl = ri >= ci
    eye = (ri == ci).astype(F32)

    def seq_body(i, carry):
        seqs = [i * group + j for j in range(group)]
        tsel = [pl.ds(b, wc, stride=bb) for b in seqs]
        gather = lambda ref: jnp.stack([ref.at[p][ts, :] for ts in tsel for p in range(PAIRS)])
        kap_g = gather(kap_s)
        rt_g = gather(rt_s)
        bt_g = gather(bt_s)
        kt_g = gather(kt_s)
        vv = gather(v_s)
        pe = jnp.stack([pe_s.at[p][pl.ds(b, 1), :] for b in seqs for p in range(PAIRS)])
        s0 = jnp.concatenate([o_wkv_ref[b] for b in seqs], axis=0)
        lhs = jnp.concatenate([kap_g, rt_g], axis=1)
        g_b = _bdot_nt(lhs, _block_diag_rows(bt_g, left))
        g_k = _bdot_nt(lhs, _block_diag_rows(kt_g, left))
        m_ab = jnp.where(strict, g_b[:, :wc], 0.0)
        m_ak = jnp.where(strict, g_k[:, :wc], 0.0)
        m_rb = jnp.where(incl, g_b[:, wc:], 0.0)
        m_rk = jnp.where(incl, g_k[:, wc:], 0.0)
        t_inv = eye - m_ab
        m_pow = _bdot(m_ab, _block_diag_rows(m_ab, left_c))
        for lv in range(1, n_lv):
            if lv < n_lv - 1:
                prod = _bdot(jnp.concatenate([t_inv, m_pow], axis=1), _block_diag_rows(m_pow, left_c))
                t_inv = t_inv + prod[:, :wc]
                m_pow = prod[:, wc:]
            else:
                t_inv = t_inv + _bdot(t_inv, _block_diag_rows(m_pow, left_c))
        xv = _bdot(jnp.concatenate([m_ak, m_rk], axis=1), _block_diag_rows(vv, left))
        z = _bdot_nt(lhs, _block_diag_rows(s0, left))
        uu = -_bdot(t_inv, _block_diag_rows(z[:, :wc] + xv[:, :wc], left))
        yh = z[:, wc:] + _bdot(m_rb, _block_diag_rows(uu, left)) + xv[:, wc:]
        full = _bdot_tn(jnp.concatenate([uu, vv], axis=1), jnp.concatenate([bt_g, kt_g], axis=1))
        upd = jnp.where(left, full[:, :TM_HEAD], full[:, TM_HEAD:])
        s_new = (s0 + upd) * pe
        for j, (b, ts) in enumerate(zip(seqs, tsel)):
            o_wkv_ref[b] = s_new[j * PAIRS:(j + 1) * PAIRS]
            for p in range(PAIRS):
                yt_s.at[p][ts, :] = yh[j * PAIRS + p]
        return carry

    n_iter = bb // group
    if n_iter <= 2:
        for i in range(n_iter):
            seq_body(i, 0)
    else:
        lax.fori_loop(0, n_iter, seq_body, 0)

    yv = jnp.concatenate([yt_s[p, 0:rows, :] for p in range(PAIRS)], axis=-1)
    mean = head_sum(yv) * (1.0 / TM_HEAD)
    cen = yv - mean
    var = head_sum(cen * cen) * (1.0 / TM_HEAD)
    y_tm = ((cen * lax.rsqrt(var + GN_EPS)) * gng_ref[...] + gnb_ref[...] + bonus) * gate_tm
    mixed = jnp.concatenate([y_tm.astype(BF16), y_lru.astype(BF16)], axis=-1)
    x1 = x_buf[x_slot].reshape(rows, D_MODEL) + jnp.dot(mixed, w_out_ref[...], preferred_element_type=F32)
    x1_ref[...] = x1.reshape(chunk, bb, D_MODEL)

    o_shift_ref[...] = tm_ext[rows:rows + bb, :]
    o_conv_ref[...] = xb_ext[rows:rows + conv_rows, :].reshape(LRU_CONV_W - 1, bb, D_LRU)
    tm_ext[0:bb, :] = tm_ext[rows:rows + bb, :]
    xb_ext[0:conv_rows, :] = xb_ext[rows:rows + conv_rows, :]
    _read_chunk_end(x_stream, bi, ti, n_chunks)


def _const_spec(shape):
    zeros = (0,) * len(shape)
    return pl.BlockSpec(shape, lambda bi, ti: zeros, pipeline_mode=pl.Buffered(1))


def _mixer(x, st_shift, st_wkv, st_conv, st_h, weights, *, bb, chunk, pos0, group):
    batch, length, _ = x.shape
    assert batch % bb == 0 and length % chunk == 0 and bb % SUBLANES == 0 and bb % group == 0
    assert chunk & (chunk - 1) == 0 and chunk >= LRU_CONV_W - 1
    rows = bb * chunk
    grid = (batch // bb, length // chunk)
    act = pl.BlockSpec((chunk, bb, D_MODEL), lambda bi, ti: (ti, bi, 0))
    vec = lambda w: pl.BlockSpec((bb, w), lambda bi, ti: (bi, 0))
    wkv_spec = pl.BlockSpec((bb, PAIRS, TM_HEAD, LANES), lambda bi, ti: (bi, 0, 0, 0))
    conv_spec = pl.BlockSpec((LRU_CONV_W - 1, bb, D_LRU), lambda bi, ti: (0, bi, 0))
    in_specs = [pl.BlockSpec(memory_space=pl.ANY), vec(D_TM_PROJ), wkv_spec, conv_spec, vec(D_LRU)]
    in_specs += [_const_spec(w.shape) for w in weights]
    out_specs = [act, vec(D_TM_PROJ), wkv_spec, conv_spec, vec(D_LRU)]
    out_shape = [jax.ShapeDtypeStruct((length, batch, D_MODEL), F32),
                 jax.ShapeDtypeStruct((batch, D_TM_PROJ), F32),
                 jax.ShapeDtypeStruct((batch, PAIRS, TM_HEAD, LANES), F32),
                 jax.ShapeDtypeStruct((LRU_CONV_W - 1, batch, D_LRU), F32),
                 jax.ShapeDtypeStruct((batch, D_LRU), F32)]
    pair_rows = lambda n: pltpu.VMEM((PAIRS, n, LANES), F32)
    scratch = [pltpu.VMEM((rows + bb, D_TM_PROJ), F32),
               pltpu.VMEM((rows + (LRU_CONV_W - 1) * bb, D_LRU), F32)]
    scratch += [pair_rows(max(chunk, SUBLANES) * bb)] * 6 + [pair_rows(bb)]
    scratch += [pltpu.VMEM((READ_SLOTS, chunk, bb, D_MODEL), F32), pltpu.SemaphoreType.DMA((READ_SLOTS,))]
    return pl.pallas_call(
        functools.partial(_mixer_kernel, bb=bb, chunk=chunk, pos0=pos0, group=group),
        out_shape=out_shape, grid=grid, in_specs=in_specs, out_specs=out_specs,
        scratch_shapes=scratch, name="mixer",
        compiler_params=pltpu.CompilerParams(dimension_semantics=("arbitrary", "arbitrary"),
                                             vmem_limit_bytes=VMEM_LIMIT_BYTES),
    )(x, st_shift, st_wkv, st_conv, st_h, *weights)


def _ffn_kernel(x1_ref, st_conv_ref,
                g2_ref, w_up_ref, w_gate_ref, cw_ref, cb_ref, w_down_ref, gf_ref,
                y_hbm, o_conv_ref,
                up_ext, y_buf, y_sem,
                *, bb, chunk):
    rows = bb * chunk
    conv_rows = (FFN_CONV_W - 1) * bb
    bi = pl.program_id(0)
    ti = pl.program_id(1)
    n_chunks = pl.num_programs(1)
    y_stream = _TimeMajorStream(y_hbm, y_buf, y_sem, bb=bb, chunk=chunk, to_hbm=True)
    slot = lax.rem(ti, 2)

    @pl.when(ti == 0)
    def _():
        up_ext[0:conv_rows, :] = st_conv_ref[...].reshape(conv_rows, D_FF)

    x1 = x1_ref[...].reshape(rows, D_MODEL)
    xn = _rms(x1, g2_ref[...]).astype(BF16)
    acc = x1
    for n in range(D_FF // FF_COL_TILE):
        cols = slice(n * FF_COL_TILE, (n + 1) * FF_COL_TILE)
        up = jnp.dot(xn, w_up_ref[:, cols], preferred_element_type=F32)
        gate = jnp.dot(xn, w_gate_ref[:, cols], preferred_element_type=F32)
        up_ext[conv_rows:conv_rows + rows, cols] = up
        upc = cb_ref[:, cols] + up_ext[0:rows, cols] * cw_ref[0:1, cols]
        upc = upc + up_ext[bb:bb + rows, cols] * cw_ref[1:2, cols]
        upc = upc + up * cw_ref[2:3, cols]
        hid = (jax.nn.gelu(upc) * gate).astype(BF16)
        acc = acc + jnp.dot(hid, w_down_ref[cols, :], preferred_element_type=F32)
    o_conv_ref[...] = up_ext[rows:rows + conv_rows, :].reshape(FFN_CONV_W - 1, bb, D_FF)
    up_ext[0:conv_rows, :] = up_ext[rows:rows + conv_rows, :]

    y_buf[slot] = _rms(acc, gf_ref[...]).reshape(chunk, bb, D_MODEL)
    y_stream.start(bi, ti, slot)

    @pl.when(ti > 0)
    def _():
        y_stream.wait(bi, ti - 1, 1 - slot)

    @pl.when(ti == n_chunks - 1)
    def _():
        y_stream.wait(bi, ti, slot)


def _ffn(x1, st_conv, weights, *, bb, chunk):
    length, batch, _ = x1.shape
    assert batch % bb == 0 and length % chunk == 0 and bb % SUBLANES == 0
    assert chunk >= FFN_CONV_W - 1
    rows = bb * chunk
    grid = (batch // bb, length // chunk)
    act = pl.BlockSpec((chunk, bb, D_MODEL), lambda bi, ti: (ti, bi, 0))
    conv_spec = pl.BlockSpec((FFN_CONV_W - 1, bb, D_FF), lambda bi, ti: (0, bi, 0))
    in_specs = [act, conv_spec] + [_const_spec(w.shape) for w in weights]
    out_shape = [jax.ShapeDtypeStruct((batch, length, D_MODEL), F32),
                 jax.ShapeDtypeStruct((FFN_CONV_W - 1, batch, D_FF), F32)]
    scratch = [pltpu.VMEM((rows + (FFN_CONV_W - 1) * bb, D_FF), F32),
               pltpu.VMEM((2, chunk, bb, D_MODEL), F32), pltpu.SemaphoreType.DMA((2,))]
    return pl.pallas_call(
        functools.partial(_ffn_kernel, bb=bb, chunk=chunk),
        out_shape=out_shape, grid=grid, in_specs=in_specs,
        out_specs=[pl.BlockSpec(memory_space=pl.ANY), conv_spec],
        scratch_shapes=scratch, name="ffn",
        compiler_params=pltpu.CompilerParams(dimension_semantics=("arbitrary", "arbitrary"),
                                             vmem_limit_bytes=VMEM_LIMIT_BYTES),
    )(x1, st_conv, *weights)


def _ffn_state_kernel(x1_ref, g2_ref, w_up_ref, o_conv_ref, *, bb):
    tail = FFN_CONV_W - 1
    x1 = x1_ref[...].reshape(tail * bb, D_MODEL)
    up = _dot(_rms(x1, g2_ref[...]), w_up_ref[...])
    o_conv_ref[...] = up.reshape(tail, bb, D_FF)


def _ffn_state(x1, g2, w_up):
    length, batch, _ = x1.shape
    tail = FFN_CONV_W - 1
    assert length % tail == 0 and batch % SUBLANES == 0
    return pl.pallas_call(
        functools.partial(_ffn_state_kernel, bb=batch),
        out_shape=jax.ShapeDtypeStruct((tail, batch, D_FF), F32), grid=(1,),
        in_specs=[pl.BlockSpec((tail, batch, D_MODEL), lambda i: (length // tail - 1, 0, 0)),
                  pl.BlockSpec(g2.shape, lambda i: (0, 0)), pl.BlockSpec(w_up.shape, lambda i: (0, 0))],
        out_specs=pl.BlockSpec((tail, batch, D_FF), lambda i: (0, 0, 0)), name="ffn_state",
        compiler_params=pltpu.CompilerParams(dimension_semantics=("arbitrary",),
                                             vmem_limit_bytes=VMEM_LIMIT_BYTES),
    )(x1, g2, w_up)


def _row(v):
    return v.reshape(1, -1).astype(F32)


def _pair_block_diag(w):
    nb, n, _ = w.shape
    w = w.reshape(nb // 2, 2, n, n)
    eye = jnp.eye(2, dtype=w.dtype)
    return (eye[None, :, None, :, None] * w[:, :, :, None, :]).reshape(nb // 2, 2 * n, 2 * n)


def _wkv_to_pairs(s):
    b = s.shape[0]
    s = s.astype(F32).reshape(b, PAIRS, 2, TM_HEAD, TM_HEAD)
    return jnp.transpose(s, (0, 1, 3, 2, 4)).reshape(b, PAIRS, TM_HEAD, LANES)


def _wkv_from_pairs(s):
    b = s.shape[0]
    s = s.reshape(b, PAIRS, TM_HEAD, 2, TM_HEAD)
    return jnp.transpose(s, (0, 1, 3, 2, 4)).reshape(b, TM_HEADS, TM_HEAD, TM_HEAD)


def _layer(x, states, mixer_w, ffn_w, *, mixer_bb, ffn_bb, mixer_chunk, ffn_chunk, pos0, group):
    st_shift, st_wkv, st_conv, st_h, st_fconv = states
    x1, o_shift, o_wkv, o_conv, o_h = _mixer(x, st_shift, st_wkv, st_conv, st_h, mixer_w,
                                             bb=mixer_bb, chunk=mixer_chunk, pos0=pos0, group=group)
    y, o_fconv = _ffn(x1, st_fconv, ffn_w, bb=ffn_bb, chunk=ffn_chunk)
    return y, (o_shift, o_wkv, o_conv, o_h, o_fconv)


def kernel(x_prompt, x_sample, state_tm_shift, state_tm_wkv, state_lru_conv, state_lru_h, state_ffn_conv, meta_tokens, norm1_g, w_in, tm_mu, tm_w0, tm_w_up, tm_a0, tm_a_up, tm_g_up, tm_k_k, tm_k_a, tm_r_k, tm_gn_g, tm_gn_b, lru_conv_w, lru_conv_b, lru_wa, lru_ba, lru_wx, lru_bx, lru_lambda, lru_out_g, w_out, norm2_g, ffn_w_up, ffn_w_gate, ffn_conv_w, ffn_conv_b, ffn_w_down, norm_f_g):
    depth = w_in.shape[0]
    assert depth == 1
    l = 0
    zeros_lora = jnp.zeros((DECAY_RANK, D_TM), F32)
    head_id = jnp.arange(D_TM) // TM_HEAD
    seg = (head_id[:, None] == jnp.arange(LANES)[None, :]).astype(BF16)
    mixer_w = (
        _row(norm1_g[l]), w_in[l].astype(BF16), _row(tm_mu[l]), _row(tm_w0[l]),
        jnp.concatenate([tm_w_up[l], zeros_lora], axis=0).astype(BF16),
        _row(tm_a0[l]),
        jnp.concatenate([zeros_lora, tm_a_up[l]], axis=0).astype(BF16),
        tm_g_up[l].astype(BF16),
        _row(tm_k_k[l]), _row(tm_k_a[l]), _row(tm_r_k[l]), _row(tm_gn_g[l]), _row(tm_gn_b[l]),
        lru_conv_w[l].astype(F32), _row(lru_conv_b[l]),
        _pair_block_diag(lru_wa[l]).astype(BF16), _row(lru_ba[l]),
        _pair_block_diag(lru_wx[l]).astype(BF16), _row(lru_bx[l]),
        _row(lru_lambda[l]), _row(lru_out_g[l]),
        seg, seg.T, w_out[l].astype(BF16),
    )
    ffn_w = (
        _row(norm2_g[l]), ffn_w_up[l].astype(BF16), ffn_w_gate[l].astype(BF16),
        ffn_conv_w[l].astype(F32), _row(ffn_conv_b[l]), ffn_w_down[l].astype(BF16), _row(norm_f_g),
    )

    bsz, seq = x_prompt.shape[0], x_prompt.shape[1]
    x_meta = jnp.broadcast_to(meta_tokens[None].astype(F32), (bsz, N_META, D_MODEL))
    x1_meta, *meta_st = _mixer(x_meta, jnp.zeros((bsz, D_TM_PROJ), F32),
                               jnp.zeros((bsz, PAIRS, TM_HEAD, LANES), F32),
                               jnp.zeros((LRU_CONV_W - 1, bsz, D_LRU), F32),
                               jnp.zeros((bsz, D_LRU), F32), mixer_w,
                               bb=bsz, chunk=N_META, pos0=0, group=4)
    p_init = tuple(meta_st) + (_ffn_state(x1_meta, ffn_w[0], ffn_w[1]),)

    y_prompt, p_st = _layer(x_prompt, p_init, mixer_w, ffn_w,
                            mixer_bb=bsz, ffn_bb=bsz, mixer_chunk=64, ffn_chunk=64,
                            pos0=N_META, group=4)

    dec_b, dec_seq = x_sample.shape[0], x_sample.shape[1]
    s_init = (state_tm_shift[l].astype(F32), _wkv_to_pairs(state_tm_wkv[l]),
              jnp.transpose(state_lru_conv[l].astype(F32), (1, 0, 2)), state_lru_h[l].astype(F32),
              jnp.transpose(state_ffn_conv[l].astype(F32), (1, 0, 2)))
    y_sample, s_st = _layer(x_sample, s_init, mixer_w, ffn_w,
                            mixer_bb=32, ffn_bb=64, mixer_chunk=dec_seq, ffn_chunk=dec_seq,
                            pos0=PAST_LEN, group=8)

    def unpack(st):
        o_shift, o_wkv, o_conv, o_h, o_fconv = st
        return (o_shift[None], _wkv_from_pairs(o_wkv)[None], jnp.transpose(o_conv, (1, 0, 2))[None],
                o_h[None], jnp.transpose(o_fconv, (1, 0, 2))[None])

    return (y_prompt, y_sample) + unpack(p_st) + unpack(s_st)
```

```python
import functools
import math

import jax
import jax.numpy as jnp
from jax import lax
from jax.experimental import pallas as pl
from jax.experimental.pallas import tpu as pltpu

F32 = jnp.float32
BF16 = jnp.bfloat16

D_MODEL = 1024
N_META = 16
PAST_LEN = 16384
D_TM = 512
TM_HEAD = 64
TM_HEADS = 8
DECAY_RANK = 64
AAA_RANK = 64
GATE_RANK = 128
D_TM_PROJ = 3 * D_TM + DECAY_RANK + AAA_RANK + GATE_RANK
D_LRU = 512
LRU_CONV_W = 4
LRU_C = 8.0
D_IN_PROJ = D_TM_PROJ + 2 * D_LRU
D_FF = 3 * D_MODEL
FFN_CONV_W = 3
EPS = 1e-6
GN_EPS = 64e-5

SUBLANES = 8
LANES = 128
PAIRS = TM_HEADS // 2
LORA_OFF = 3 * D_TM
GATE_OFF = LORA_OFF + DECAY_RANK + AAA_RANK
FF_COL_TILE = 1024
VMEM_LIMIT_BYTES = 60 * 1024 * 1024


def _dot(a, b):
    return jnp.dot(a.astype(BF16), b.astype(BF16), preferred_element_type=F32)


def _bdot(a, b):
    return lax.dot_general(a.astype(BF16), b.astype(BF16), (((2,), (1,)), ((0,), (0,))),
                           preferred_element_type=F32)


def _bdot_nt(a, b):
    return lax.dot_general(a.astype(BF16), b.astype(BF16), (((2,), (2,)), ((0,), (0,))),
                           preferred_element_type=F32)


def _bdot_tn(a, b):
    return lax.dot_general(a.astype(BF16), b.astype(BF16), (((1,), (1,)), ((0,), (0,))),
                           preferred_element_type=F32)


def _rms(x, g):
    return x * lax.rsqrt(jnp.mean(x * x, axis=-1, keepdims=True) + EPS) * g


def _sigmoid(z):
    return 0.5 * jnp.tanh(0.5 * z) + 0.5


def _softplus(z):
    return jnp.maximum(z, 0.0) + jnp.log(1.0 + jnp.exp(-jnp.abs(z)))


def _block_diag_rows(x, left):
    return jnp.concatenate([jnp.where(left, x, 0.0), jnp.where(left, 0.0, x)], axis=1)


class _TimeMajorStream:
    def __init__(self, hbm, buf, sem, *, bb, chunk, to_hbm):
        self.hbm, self.buf, self.sem = hbm, buf, sem
        self.bb, self.chunk, self.to_hbm = bb, chunk, to_hbm

    def _copy(self, bi, ti, slot, t):
        seqs = pl.ds(pl.multiple_of(bi * self.bb, SUBLANES), self.bb)
        hbm_rows = self.hbm.at[seqs, ti * self.chunk + t, :]
        tile = self.buf.at[slot, t]
        src, dst = (tile, hbm_rows) if self.to_hbm else (hbm_rows, tile)
        return pltpu.make_async_copy(src, dst, self.sem.at[slot])

    def start(self, bi, ti, slot):
        for t in range(self.chunk):
            self._copy(bi, ti, slot, t).start()

    def wait(self, bi, ti, slot):
        for t in range(self.chunk):
            self._copy(bi, ti, slot, t).wait()


READ_SLOTS = 3


def _read_chunk_begin(stream, bi, ti, n_chunks):
    last = n_chunks - 1

    @pl.when(ti == 0)
    def _():
        stream.start(bi, 0, 0)
        stream.start(bi, jnp.minimum(1, last), 1)

    slot = lax.rem(ti, READ_SLOTS)
    stream.wait(bi, ti, slot)
    return slot


def _read_chunk_end(stream, bi, ti, n_chunks):
    last = n_chunks - 1
    stream.start(bi, jnp.minimum(ti + 2, last), lax.rem(ti + 2, READ_SLOTS))

    @pl.when(ti == last)
    def _():
        stream.wait(bi, last, lax.rem(ti + 1, READ_SLOTS))
        stream.wait(bi, last, lax.rem(ti + 2, READ_SLOTS))


def _mixer_kernel(x_hbm, st_shift_ref, st_wkv_ref, st_conv_ref, st_h_ref,
                  g1_ref, w_in_ref, mu_ref, w0_ref, wdec_ref, a0_ref, waaa_ref, wgate_ref,
                  kk_ref, ka_ref, rk_ref, gng_ref, gnb_ref,
                  cw_ref, cb_ref, wa_ref, ba_ref, wx_ref, bx_ref, lam_ref, og_ref,
                  seg_ref, segt_ref, w_out_ref,
                  x1_ref, o_shift_ref, o_wkv_ref, o_conv_ref, o_h_ref,
                  tm_ext, xb_ext, kap_s, rt_s, bt_s, kt_s, v_s, yt_s, pe_s, wkv_s, x_buf, x_sem,
                  *, bb, chunk, pos0, group):
    rows = bb * chunk
    conv_rows = (LRU_CONV_W - 1) * bb
    wkv_chunk = max(chunk, SUBLANES)
    bi = pl.program_id(0)
    ti = pl.program_id(1)
    n_chunks = pl.num_programs(1)
    x_stream = _TimeMajorStream(x_hbm, x_buf, x_sem, bb=bb, chunk=chunk, to_hbm=False)
    x_slot = _read_chunk_begin(x_stream, bi, ti, n_chunks)

    @pl.when(ti == 0)
    def _():
        tm_ext[0:bb, :] = st_shift_ref[...]
        xb_ext[0:conv_rows, :] = st_conv_ref[...].reshape(conv_rows, D_LRU)
        o_h_ref[...] = st_h_ref[...]

        def pair_up(b, carry):
            for p in range(PAIRS):
                wkv_s[b, p] = jnp.concatenate([st_wkv_ref[b, 2 * p], st_wkv_ref[b, 2 * p + 1]], axis=-1)
            return carry

        lax.fori_loop(0, bb, pair_up, 0)
        if wkv_chunk > chunk:
            for ref in (kap_s, rt_s, bt_s, kt_s, v_s):
                ref[:, rows:wkv_chunk * bb, :] = jnp.zeros((PAIRS, wkv_chunk * bb - rows, LANES), F32)

    seg = seg_ref[...]
    segt = segt_ref[...]

    def head_sum(x):
        s = jnp.dot(x.astype(BF16), seg, preferred_element_type=F32)
        return jnp.dot(s.astype(BF16), segt, preferred_element_type=F32)

    x = x_buf[x_slot].reshape(rows, D_MODEL)
    u = _dot(_rms(x, g1_ref[...]), w_in_ref[...])
    u_tm = u[:, :D_TM_PROJ]
    xb = u[:, D_TM_PROJ:D_TM_PROJ + D_LRU]
    gate_lru = u[:, D_TM_PROJ + D_LRU:D_IN_PROJ]
    tm_ext[bb:bb + rows, :] = u_tm
    xb_ext[conv_rows:conv_rows + rows, :] = xb

    row = lax.broadcasted_iota(jnp.int32, (rows, D_TM), 0)

    um = u_tm + (tm_ext[0:rows, :] - u_tm) * mu_ref[...]
    r = um[:, 0:D_TM]
    k = um[:, D_TM:2 * D_TM]
    v = um[:, 2 * D_TM:3 * D_TM]
    x_lora = um[:, LORA_OFF:GATE_OFF]
    x_gate = um[:, GATE_OFF:D_TM_PROJ]
    log_decay = -math.exp(-0.5) * _sigmoid(w0_ref[...] + _dot(jnp.tanh(x_lora), wdec_ref[...]))
    a = _sigmoid(a0_ref[...] + _dot(x_lora, waaa_ref[...]))
    gate_tm = _dot(_sigmoid(x_gate), wgate_ref[...])
    kk = k * kk_ref[...]
    k = k * (1.0 + (a - 1.0) * ka_ref[...])
    kk = kk * lax.rsqrt(jnp.maximum(head_sum(kk * kk), 1e-24))
    bonus = head_sum(r * k * rk_ref[...]) * v
    acc = log_decay[0:bb]
    cum = [acc]
    for t in range(1, chunk):
        acc = acc + log_decay[t * bb:(t + 1) * bb]
        cum.append(acc)
    c = jnp.concatenate(cum, axis=0)
    inv_p = jnp.exp(-c)
    kap = kk * jnp.exp(c - log_decay)
    rt = r * jnp.exp(c)
    bt = kk * a * inv_p
    kt = k * inv_p
    p_end = jnp.exp(cum[chunk - 1])
    for p in range(PAIRS):
        ls = slice(p * LANES, (p + 1) * LANES)
        kap_s[p, 0:rows, :] = kap[:, ls]
        rt_s[p, 0:rows, :] = rt[:, ls]
        bt_s[p, 0:rows, :] = bt[:, ls]
        kt_s[p, 0:rows, :] = kt[:, ls]
        v_s[p, 0:rows, :] = v[:, ls]
        pe_s[p] = p_end[:, ls]

    xc = cb_ref[...] + xb_ext[0:rows, :] * cw_ref[0:1, :]
    xc = xc + xb_ext[bb:bb + rows, :] * cw_ref[1:2, :]
    xc = xc + xb_ext[2 * bb:2 * bb + rows, :] * cw_ref[2:3, :]
    xc = xc + xb * cw_ref[3:4, :]
    xc_bf = xc.astype(BF16)

    def block_gate(w_ref):
        return jnp.concatenate(
            [jnp.dot(xc_bf[:, p * LANES:(p + 1) * LANES], w_ref[p], preferred_element_type=F32)
             for p in range(D_LRU // LANES)], axis=-1)

    r_g = _sigmoid(block_gate(wa_ref) + ba_ref[...])
    i_g = _sigmoid(block_gate(wx_ref) + bx_ref[...])
    la = jnp.exp(-LRU_C * r_g * _softplus(-lam_ref[...]))
    mult = jnp.sqrt(1.0 - la * la)
    if pos0 == 0:
        mult = jnp.where(jnp.logical_and(row < bb, ti == 0), 1.0, mult)
    lb = xc * i_g * mult
    h = o_h_ref[...]
    hs = []
    for t in range(chunk):
        ts = slice(t * bb, (t + 1) * bb)
        h = la[ts] * h + lb[ts]
        hs.append(h)
    o_h_ref[...] = h
    y_lru = _rms(jnp.concatenate(hs, axis=0) * jax.nn.gelu(gate_lru), og_ref[...])

    wc = wkv_chunk
    n_lv = max(1, (wc - 1).bit_length())
    cw2 = 2 * wc
    left = lax.broadcasted_iota(jnp.int32, (1, 1, LANES), 2) < TM_HEAD
    left_c = lax.broadcasted_iota(jnp.int32, (1, 1, cw2), 2) < wc
    ri = lax.broadcasted_iota(jnp.int32, (wc, cw2), 0)
    ci = lax.broadcasted_iota(jnp.int32, (wc, cw2), 1) & (wc - 1)
    strict = ri > ci
    incl = ri >= ci
    eye = (ri == ci).astype(F32)

    def seq_body(i, carry):
        seqs = [i * group + j for j in range(group)]
        tsel = [pl.ds(b, wc, stride=bb) for b in seqs]
        gather = lambda ref: jnp.stack([ref.at[p][ts, :] for ts in tsel for p in range(PAIRS)])
        kap_g = gather(kap_s)
        rt_g = gather(rt_s)
        bt_g = gather(bt_s)
        kt_g = gather(kt_s)
        vv = gather(v_s)
        pe = jnp.stack([pe_s.at[p][pl.ds(b, 1), :] for b in seqs for p in range(PAIRS)])
        s0 = jnp.concatenate([wkv_s[b] for b in seqs], axis=0)
        lhs = jnp.concatenate([kap_g, rt_g], axis=1)
        g_b = _bdot_nt(lhs, _block_diag_rows(bt_g, left))
        g_k = _bdot_nt(lhs, _block_diag_rows(kt_g, left))
        m_ab = jnp.where(strict, g_b[:, :wc], 0.0)
        m_ak = jnp.where(strict, g_k[:, :wc], 0.0)
        m_rb = jnp.where(incl, g_b[:, wc:], 0.0)
        m_rk = jnp.where(incl, g_k[:, wc:], 0.0)
        t_inv = eye - m_ab
        m_pow = _bdot(m_ab, _block_diag_rows(m_ab, left_c))
        for lv in range(1, n_lv):
            if lv < n_lv - 1:
                prod = _bdot(jnp.concatenate([t_inv, m_pow], axis=1), _block_diag_rows(m_pow, left_c))
                t_inv = t_inv + prod[:, :wc]
                m_pow = prod[:, wc:]
            else:
                t_inv = t_inv + _bdot(t_inv, _block_diag_rows(m_pow, left_c))
        xv = _bdot(jnp.concatenate([m_ak, m_rk], axis=1), _block_diag_rows(vv, left))
        z = _bdot_nt(lhs, _block_diag_rows(s0, left))
        uu = -_bdot(t_inv, _block_diag_rows(z[:, :wc] + xv[:, :wc], left))
        yh = z[:, wc:] + _bdot(m_rb, _block_diag_rows(uu, left)) + xv[:, wc:]
        full = _bdot_tn(jnp.concatenate([uu, vv], axis=1), jnp.concatenate([bt_g, kt_g], axis=1))
        upd = jnp.where(left, full[:, :TM_HEAD], full[:, TM_HEAD:])
        s_new = (s0 + upd) * pe
        for j, (b, ts) in enumerate(zip(seqs, tsel)):
            wkv_s[b] = s_new[j * PAIRS:(j + 1) * PAIRS]
            for p in range(PAIRS):
                yt_s.at[p][ts, :] = yh[j * PAIRS + p]
        return carry

    n_iter = bb // group
    if n_iter <= 2:
        for i in range(n_iter):
            seq_body(i, 0)
    else:
        lax.fori_loop(0, n_iter, seq_body, 0)

    yv = jnp.concatenate([yt_s[p, 0:rows, :] for p in range(PAIRS)], axis=-1)
    mean = head_sum(yv) * (1.0 / TM_HEAD)
    cen = yv - mean
    var = head_sum(cen * cen) * (1.0 / TM_HEAD)
    y_tm = ((cen * lax.rsqrt(var + GN_EPS)) * gng_ref[...] + gnb_ref[...] + bonus) * gate_tm
    mixed = jnp.concatenate([y_tm.astype(BF16), y_lru.astype(BF16)], axis=-1)
    x1 = x_buf[x_slot].reshape(rows, D_MODEL) + jnp.dot(mixed, w_out_ref[...], preferred_element_type=F32)
    x1_ref[...] = x1.reshape(chunk, bb, D_MODEL)

    o_shift_ref[...] = tm_ext[rows:rows + bb, :]
    o_conv_ref[...] = xb_ext[rows:rows + conv_rows, :].reshape(LRU_CONV_W - 1, bb, D_LRU)
    tm_ext[0:bb, :] = tm_ext[rows:rows + bb, :]
    xb_ext[0:conv_rows, :] = xb_ext[rows:rows + conv_rows, :]

    @pl.when(ti == n_chunks - 1)
    def _():
        def split_pairs(b, carry):
            for p in range(PAIRS):
                pair = wkv_s[b, p]
                o_wkv_ref[b, 2 * p] = pair[:, :TM_HEAD]
                o_wkv_ref[b, 2 * p + 1] = pair[:, TM_HEAD:]
            return carry

        lax.fori_loop(0, bb, split_pairs, 0)

    _read_chunk_end(x_stream, bi, ti, n_chunks)


def _const_spec(shape):
    zeros = (0,) * len(shape)
    return pl.BlockSpec(shape, lambda bi, ti: zeros, pipeline_mode=pl.Buffered(1))


def _mixer(x, st_shift, st_wkv, st_conv, st_h, weights, *, bb, chunk, pos0, group):
    batch, length, _ = x.shape
    assert batch % bb == 0 and length % chunk == 0 and bb % SUBLANES == 0 and bb % group == 0
    assert chunk & (chunk - 1) == 0 and chunk >= LRU_CONV_W - 1
    rows = bb * chunk
    grid = (batch // bb, length // chunk)
    act = pl.BlockSpec((chunk, bb, D_MODEL), lambda bi, ti: (ti, bi, 0))
    vec = lambda w: pl.BlockSpec((bb, w), lambda bi, ti: (bi, 0))
    wkv_spec = pl.BlockSpec((bb, TM_HEADS, TM_HEAD, TM_HEAD), lambda bi, ti: (bi, 0, 0, 0))
    conv_spec = pl.BlockSpec((LRU_CONV_W - 1, bb, D_LRU), lambda bi, ti: (0, bi, 0))
    in_specs = [pl.BlockSpec(memory_space=pl.ANY), vec(D_TM_PROJ), wkv_spec, conv_spec, vec(D_LRU)]
    in_specs += [_const_spec(w.shape) for w in weights]
    out_specs = [act, vec(D_TM_PROJ), wkv_spec, conv_spec, vec(D_LRU)]
    out_shape = [jax.ShapeDtypeStruct((length, batch, D_MODEL), F32),
                 jax.ShapeDtypeStruct((batch, D_TM_PROJ), F32),
                 jax.ShapeDtypeStruct((batch, TM_HEADS, TM_HEAD, TM_HEAD), F32),
                 jax.ShapeDtypeStruct((LRU_CONV_W - 1, batch, D_LRU), F32),
                 jax.ShapeDtypeStruct((batch, D_LRU), F32)]
    pair_rows = lambda n: pltpu.VMEM((PAIRS, n, LANES), F32)
    scratch = [pltpu.VMEM((rows + bb, D_TM_PROJ), F32),
               pltpu.VMEM((rows + (LRU_CONV_W - 1) * bb, D_LRU), F32)]
    scratch += [pair_rows(max(chunk, SUBLANES) * bb)] * 6 + [pair_rows(bb)]
    scratch += [pltpu.VMEM((bb, PAIRS, TM_HEAD, LANES), F32)]
    scratch += [pltpu.VMEM((READ_SLOTS, chunk, bb, D_MODEL), F32), pltpu.SemaphoreType.DMA((READ_SLOTS,))]
    return pl.pallas_call(
        functools.partial(_mixer_kernel, bb=bb, chunk=chunk, pos0=pos0, group=group),
        out_shape=out_shape, grid=grid, in_specs=in_specs, out_specs=out_specs,
        scratch_shapes=scratch, name="mixer",
        compiler_params=pltpu.CompilerParams(dimension_semantics=("arbitrary", "arbitrary"),
                                             vmem_limit_bytes=VMEM_LIMIT_BYTES),
    )(x, st_shift, st_wkv, st_conv, st_h, *weights)


def _ffn_kernel(x1_ref, st_conv_ref,
                g2_ref, w_up_ref, w_gate_ref, cw_ref, cb_ref, w_down_ref, gf_ref,
                y_hbm, o_conv_ref,
                up_ext, y_buf, y_sem,
                *, bb, chunk):
    rows = bb * chunk
    conv_rows = (FFN_CONV_W - 1) * bb
    bi = pl.program_id(0)
    ti = pl.program_id(1)
    n_chunks = pl.num_programs(1)
    y_stream = _TimeMajorStream(y_hbm, y_buf, y_sem, bb=bb, chunk=chunk, to_hbm=True)
    slot = lax.rem(ti, 2)

    @pl.when(ti == 0)
    def _():
        up_ext[0:conv_rows, :] = st_conv_ref[...].reshape(conv_rows, D_FF)

    x1 = x1_ref[...].reshape(rows, D_MODEL)
    xn = _rms(x1, g2_ref[...]).astype(BF16)
    acc = x1
    for n in range(D_FF // FF_COL_TILE):
        cols = slice(n * FF_COL_TILE, (n + 1) * FF_COL_TILE)
        up = jnp.dot(xn, w_up_ref[:, cols], preferred_element_type=F32)
        gate = jnp.dot(xn, w_gate_ref[:, cols], preferred_element_type=F32)
        up_ext[conv_rows:conv_rows + rows, cols] = up
        upc = cb_ref[:, cols] + up_ext[0:rows, cols] * cw_ref[0:1, cols]
        upc = upc + up_ext[bb:bb + rows, cols] * cw_ref[1:2, cols]
        upc = upc + up * cw_ref[2:3, cols]
        hid = (jax.nn.gelu(upc) * gate).astype(BF16)
        acc = acc + jnp.dot(hid, w_down_ref[cols, :], preferred_element_type=F32)
    o_conv_ref[...] = up_ext[rows:rows + conv_rows, :].reshape(FFN_CONV_W - 1, bb, D_FF)
    up_ext[0:conv_rows, :] = up_ext[rows:rows + conv_rows, :]

    y_buf[slot] = _rms(acc, gf_ref[...]).reshape(chunk, bb, D_MODEL)
    y_stream.start(bi, ti, slot)

    @pl.when(ti > 0)
    def _():
        y_stream.wait(bi, ti - 1, 1 - slot)

    @pl.when(ti == n_chunks - 1)
    def _():
        y_stream.wait(bi, ti, slot)


def _ffn(x1, st_conv, weights, *, bb, chunk):
    length, batch, _ = x1.shape
    assert batch % bb == 0 and length % chunk == 0 and bb % SUBLANES == 0
    assert chunk >= FFN_CONV_W - 1
    rows = bb * chunk
    grid = (batch // bb, length // chunk)
    act = pl.BlockSpec((chunk, bb, D_MODEL), lambda bi, ti: (ti, bi, 0))
    conv_spec = pl.BlockSpec((FFN_CONV_W - 1, bb, D_FF), lambda bi, ti: (0, bi, 0))
    in_specs = [act, conv_spec] + [_const_spec(w.shape) for w in weights]
    out_shape = [jax.ShapeDtypeStruct((batch, length, D_MODEL), F32),
                 jax.ShapeDtypeStruct((FFN_CONV_W - 1, batch, D_FF), F32)]
    scratch = [pltpu.VMEM((rows + (FFN_CONV_W - 1) * bb, D_FF), F32),
               pltpu.VMEM((2, chunk, bb, D_MODEL), F32), pltpu.SemaphoreType.DMA((2,))]
    return pl.pallas_call(
        functools.partial(_ffn_kernel, bb=bb, chunk=chunk),
        out_shape=out_shape, grid=grid, in_specs=in_specs,
        out_specs=[pl.BlockSpec(memory_space=pl.ANY), conv_spec],
        scratch_shapes=scratch, name="ffn",
        compiler_params=pltpu.CompilerParams(dimension_semantics=("arbitrary", "arbitrary"),
                                             vmem_limit_bytes=VMEM_LIMIT_BYTES),
    )(x1, st_conv, *weights)


def _ffn_state_kernel(x1_ref, g2_ref, w_up_ref, o_conv_ref, *, bb):
    tail = FFN_CONV_W - 1
    x1 = x1_ref[...].reshape(tail * bb, D_MODEL)
    up = _dot(_rms(x1, g2_ref[...]), w_up_ref[...])
    o_conv_ref[...] = up.reshape(tail, bb, D_FF)


def _ffn_state(x1, g2, w_up):
    length, batch, _ = x1.shape
    tail = FFN_CONV_W - 1
    assert length % tail == 0 and batch % SUBLANES == 0
    return pl.pallas_call(
        functools.partial(_ffn_state_kernel, bb=batch),
        out_shape=jax.ShapeDtypeStruct((tail, batch, D_FF), F32), grid=(1,),
        in_specs=[pl.BlockSpec((tail, batch, D_MODEL), lambda i: (length // tail - 1, 0, 0)),
                  pl.BlockSpec(g2.shape, lambda i: (0, 0)), pl.BlockSpec(w_up.shape, lambda i: (0, 0))],
        out_specs=pl.BlockSpec((tail, batch, D_FF), lambda i: (0, 0, 0)), name="ffn_state",
        compiler_params=pltpu.CompilerParams(dimension_semantics=("arbitrary",),
                                             vmem_limit_bytes=VMEM_LIMIT_BYTES),
    )(x1, g2, w_up)


def _row(v):
    return v.reshape(1, -1).astype(F32)


def _pair_block_diag(w):
    nb, n, _ = w.shape
    w = w.reshape(nb // 2, 2, n, n)
    eye = jnp.eye(2, dtype=w.dtype)
    return (eye[None, :, None, :, None] * w[:, :, :, None, :]).reshape(nb // 2, 2 * n, 2 * n)


def _layer(x, states, mixer_w, ffn_w, *, mixer_bb, ffn_bb, mixer_chunk, ffn_chunk, pos0, group):
    st_shift, st_wkv, st_conv, st_h, st_fconv = states
    x1, o_shift, o_wkv, o_conv, o_h = _mixer(x, st_shift, st_wkv, st_conv, st_h, mixer_w,
                                             bb=mixer_bb, chunk=mixer_chunk, pos0=pos0, group=group)
    y, o_fconv = _ffn(x1, st_fconv, ffn_w, bb=ffn_bb, chunk=ffn_chunk)
    return y, (o_shift, o_wkv, o_conv, o_h, o_fconv)


def kernel(x_prompt, x_sample, state_tm_shift, state_tm_wkv, state_lru_conv, state_lru_h, state_ffn_conv, meta_tokens, norm1_g, w_in, tm_mu, tm_w0, tm_w_up, tm_a0, tm_a_up, tm_g_up, tm_k_k, tm_k_a, tm_r_k, tm_gn_g, tm_gn_b, lru_conv_w, lru_conv_b, lru_wa, lru_ba, lru_wx, lru_bx, lru_lambda, lru_out_g, w_out, norm2_g, ffn_w_up, ffn_w_gate, ffn_conv_w, ffn_conv_b, ffn_w_down, norm_f_g):
    depth = w_in.shape[0]
    assert depth == 1
    l = 0
    zeros_lora = jnp.zeros((DECAY_RANK, D_TM), F32)
    head_id = jnp.arange(D_TM) // TM_HEAD
    seg = (head_id[:, None] == jnp.arange(LANES)[None, :]).astype(BF16)
    mixer_w = (
        _row(norm1_g[l]), w_in[l].astype(BF16), _row(tm_mu[l]), _row(tm_w0[l]),
        jnp.concatenate([tm_w_up[l], zeros_lora], axis=0).astype(BF16),
        _row(tm_a0[l]),
        jnp.concatenate([zeros_lora, tm_a_up[l]], axis=0).astype(BF16),
        tm_g_up[l].astype(BF16),
        _row(tm_k_k[l]), _row(tm_k_a[l]), _row(tm_r_k[l]), _row(tm_gn_g[l]), _row(tm_gn_b[l]),
        lru_conv_w[l].astype(F32), _row(lru_conv_b[l]),
        _pair_block_diag(lru_wa[l]).astype(BF16), _row(lru_ba[l]),
        _pair_block_diag(lru_wx[l]).astype(BF16), _row(lru_bx[l]),
        _row(lru_lambda[l]), _row(lru_out_g[l]),
        seg, seg.T, w_out[l].astype(BF16),
    )
    ffn_w = (
        _row(norm2_g[l]), ffn_w_up[l].astype(BF16), ffn_w_gate[l].astype(BF16),
        ffn_conv_w[l].astype(F32), _row(ffn_conv_b[l]), ffn_w_down[l].astype(BF16), _row(norm_f_g),
    )

    bsz, seq = x_prompt.shape[0], x_prompt.shape[1]
    x_meta = jnp.broadcast_to(meta_tokens[None].astype(F32), (bsz, N_META, D_MODEL))
    x1_meta, *meta_st = _mixer(x_meta, jnp.zeros((bsz, D_TM_PROJ), F32),
                               jnp.zeros((bsz, TM_HEADS, TM_HEAD, TM_HEAD), F32),
                               jnp.zeros((LRU_CONV_W - 1, bsz, D_LRU), F32),
                               jnp.zeros((bsz, D_LRU), F32), mixer_w,
                               bb=bsz, chunk=N_META, pos0=0, group=4)
    p_init = tuple(meta_st) + (_ffn_state(x1_meta, ffn_w[0], ffn_w[1]),)

    y_prompt, p_st = _layer(x_prompt, p_init, mixer_w, ffn_w,
                            mixer_bb=bsz, ffn_bb=bsz, mixer_chunk=64, ffn_chunk=64,
                            pos0=N_META, group=4)

    dec_b, dec_seq = x_sample.shape[0], x_sample.shape[1]
    s_init = (state_tm_shift[l].astype(F32), state_tm_wkv[l].astype(F32),
              jnp.transpose(state_lru_conv[l].astype(F32), (1, 0, 2)), state_lru_h[l].astype(F32),
              jnp.transpose(state_ffn_conv[l].astype(F32), (1, 0, 2)))
    y_sample, s_st = _layer(x_sample, s_init, mixer_w, ffn_w,
                            mixer_bb=32, ffn_bb=64, mixer_chunk=dec_seq, ffn_chunk=dec_seq,
                            pos0=PAST_LEN, group=8)

    def unpack(st):
        o_shift, o_wkv, o_conv, o_h, o_fconv = st
        return (o_shift[None], o_wkv[None], jnp.transpose(o_conv, (1, 0, 2))[None],
                o_h[None], jnp.transpose(o_fconv, (1, 0, 2))[None])

    return (y_prompt, y_sample) + unpack(p_st) + unpack(s_st)
```

```python
import functools
import math

import jax
import jax.numpy as jnp
from jax import lax
from jax.experimental import pallas as pl
from jax.experimental.pallas import tpu as pltpu

F32 = jnp.float32
BF16 = jnp.bfloat16

D_MODEL = 1024
N_META = 16
PAST_LEN = 16384
D_TM = 512
TM_HEAD = 64
TM_HEADS = 8
DECAY_RANK = 64
AAA_RANK = 64
GATE_RANK = 128
D_TM_PROJ = 3 * D_TM + DECAY_RANK + AAA_RANK + GATE_RANK
D_LRU = 512
LRU_CONV_W = 4
LRU_C = 8.0
D_IN_PROJ = D_TM_PROJ + 2 * D_LRU
D_FF = 3 * D_MODEL
FFN_CONV_W = 3
EPS = 1e-6
GN_EPS = 64e-5

SUBLANES = 8
LANES = 128
PAIRS = TM_HEADS // 2
LORA_OFF = 3 * D_TM
GATE_OFF = LORA_OFF + DECAY_RANK + AAA_RANK
FF_COL_TILE = 1024
VMEM_LIMIT_BYTES = 60 * 1024 * 1024


def _dot(a, b):
    return jnp.dot(a.astype(BF16), b.astype(BF16), preferred_element_type=F32)


def _bdot(a, b):
    return lax.dot_general(a.astype(BF16), b.astype(BF16), (((2,), (1,)), ((0,), (0,))),
                           preferred_element_type=F32)


def _bdot_nt(a, b):
    return lax.dot_general(a.astype(BF16), b.astype(BF16), (((2,), (2,)), ((0,), (0,))),
                           preferred_element_type=F32)


def _bdot_tn(a, b):
    return lax.dot_general(a.astype(BF16), b.astype(BF16), (((1,), (1,)), ((0,), (0,))),
                           preferred_element_type=F32)


def _rms(x, g):
    return x * lax.rsqrt(jnp.mean(x * x, axis=-1, keepdims=True) + EPS) * g


def _sigmoid(z):
    return 0.5 * jnp.tanh(0.5 * z) + 0.5


def _softplus(z):
    return jnp.maximum(z, 0.0) + jnp.log(1.0 + jnp.exp(-jnp.abs(z)))


def _block_diag_rows(x, left):
    return jnp.concatenate([jnp.where(left, x, 0.0), jnp.where(left, 0.0, x)], axis=1)


class _TimeMajorStream:
    def __init__(self, hbm, buf, sem, *, bb, chunk, to_hbm):
        self.hbm, self.buf, self.sem = hbm, buf, sem
        self.bb, self.chunk, self.to_hbm = bb, chunk, to_hbm

    def _copy(self, bi, ti, slot, t):
        seqs = pl.ds(pl.multiple_of(bi * self.bb, SUBLANES), self.bb)
        hbm_rows = self.hbm.at[seqs, ti * self.chunk + t, :]
        tile = self.buf.at[slot, t]
        src, dst = (tile, hbm_rows) if self.to_hbm else (hbm_rows, tile)
        return pltpu.make_async_copy(src, dst, self.sem.at[slot])

    def start(self, bi, ti, slot):
        for t in range(self.chunk):
            self._copy(bi, ti, slot, t).start()

    def wait(self, bi, ti, slot):
        for t in range(self.chunk):
            self._copy(bi, ti, slot, t).wait()


READ_SLOTS = 3


def _read_chunk_begin(stream, bi, ti, n_chunks):
    last = n_chunks - 1

    @pl.when(ti == 0)
    def _():
        stream.start(bi, 0, 0)
        stream.start(bi, jnp.minimum(1, last), 1)

    slot = lax.rem(ti, READ_SLOTS)
    stream.wait(bi, ti, slot)
    return slot


def _read_chunk_end(stream, bi, ti, n_chunks):
    last = n_chunks - 1
    stream.start(bi, jnp.minimum(ti + 2, last), lax.rem(ti + 2, READ_SLOTS))

    @pl.when(ti == last)
    def _():
        stream.wait(bi, last, lax.rem(ti + 1, READ_SLOTS))
        stream.wait(bi, last, lax.rem(ti + 2, READ_SLOTS))


def _mixer_kernel(x_hbm, st_shift_ref, st_wkv_ref, st_conv_ref, st_h_ref,
                  g1_ref, w_in_ref, mu_ref, w0_ref, wdec_ref, a0_ref, waaa_ref, wgate_ref,
                  kk_ref, ka_ref, rk_ref, gng_ref, gnb_ref,
                  cw_ref, cb_ref, wa_ref, ba_ref, wx_ref, bx_ref, lam_ref, og_ref,
                  seg_ref, segt_ref, w_out_ref,
                  x1_ref, o_shift_ref, o_wkv_ref, o_conv_ref, o_h_ref,
                  tm_ext, xb_ext, kap_s, rt_s, bt_s, kt_s, v_s, yt_s, pe_s, x_buf, x_sem,
                  *, bb, chunk, pos0, group):
    rows = bb * chunk
    conv_rows = (LRU_CONV_W - 1) * bb
    wkv_chunk = max(chunk, SUBLANES)
    bi = pl.program_id(0)
    ti = pl.program_id(1)
    n_chunks = pl.num_programs(1)
    x_stream = _TimeMajorStream(x_hbm, x_buf, x_sem, bb=bb, chunk=chunk, to_hbm=False)
    x_slot = _read_chunk_begin(x_stream, bi, ti, n_chunks)

    @pl.when(ti == 0)
    def _():
        tm_ext[0:bb, :] = st_shift_ref[...]
        xb_ext[0:conv_rows, :] = st_conv_ref[...].reshape(conv_rows, D_LRU)
        o_wkv_ref[...] = st_wkv_ref[...]
        o_h_ref[...] = st_h_ref[...]
        if wkv_chunk > chunk:
            for ref in (kap_s, rt_s, bt_s, kt_s, v_s):
                ref[:, rows:wkv_chunk * bb, :] = jnp.zeros((PAIRS, wkv_chunk * bb - rows, LANES), F32)

    seg = seg_ref[...]
    segt = segt_ref[...]

    def head_sum(x):
        s = jnp.dot(x.astype(BF16), seg, preferred_element_type=F32)
        return jnp.dot(s.astype(BF16), segt, preferred_element_type=F32)

    x = x_buf[x_slot].reshape(rows, D_MODEL)
    u = _dot(_rms(x, g1_ref[...]), w_in_ref[...])
    u_tm = u[:, :D_TM_PROJ]
    xb = u[:, D_TM_PROJ:D_TM_PROJ + D_LRU]
    gate_lru = u[:, D_TM_PROJ + D_LRU:D_IN_PROJ]
    tm_ext[bb:bb + rows, :] = u_tm
    xb_ext[conv_rows:conv_rows + rows, :] = xb

    row = lax.broadcasted_iota(jnp.int32, (rows, D_TM), 0)

    um = u_tm + (tm_ext[0:rows, :] - u_tm) * mu_ref[...]
    r = um[:, 0:D_TM]
    k = um[:, D_TM:2 * D_TM]
    v = um[:, 2 * D_TM:3 * D_TM]
    x_lora = um[:, LORA_OFF:GATE_OFF]
    x_gate = um[:, GATE_OFF:D_TM_PROJ]
    log_decay = -math.exp(-0.5) * _sigmoid(w0_ref[...] + _dot(jnp.tanh(x_lora), wdec_ref[...]))
    a = _sigmoid(a0_ref[...] + _dot(x_lora, waaa_ref[...]))
    gate_tm = _dot(_sigmoid(x_gate), wgate_ref[...])
    kk = k * kk_ref[...]
    k = k * (1.0 + (a - 1.0) * ka_ref[...])
    kk = kk * lax.rsqrt(jnp.maximum(head_sum(kk * kk), 1e-24))
    bonus = head_sum(r * k * rk_ref[...]) * v
    acc = log_decay[0:bb]
    cum = [acc]
    for t in range(1, chunk):
        acc = acc + log_decay[t * bb:(t + 1) * bb]
        cum.append(acc)
    c = jnp.concatenate(cum, axis=0)
    inv_p = jnp.exp(-c)
    kap = kk * jnp.exp(c - log_decay)
    rt = r * jnp.exp(c)
    bt = kk * a * inv_p
    kt = k * inv_p
    p_end = jnp.exp(cum[chunk - 1])
    for p in range(PAIRS):
        ls = slice(p * LANES, (p + 1) * LANES)
        kap_s[p, 0:rows, :] = kap[:, ls]
        rt_s[p, 0:rows, :] = rt[:, ls]
        bt_s[p, 0:rows, :] = bt[:, ls]
        kt_s[p, 0:rows, :] = kt[:, ls]
        v_s[p, 0:rows, :] = v[:, ls]
        pe_s[p] = p_end[:, ls]

    xc = cb_ref[...] + xb_ext[0:rows, :] * cw_ref[0:1, :]
    xc = xc + xb_ext[bb:bb + rows, :] * cw_ref[1:2, :]
    xc = xc + xb_ext[2 * bb:2 * bb + rows, :] * cw_ref[2:3, :]
    xc = xc + xb * cw_ref[3:4, :]
    xc_bf = xc.astype(BF16)

    def block_gate(w_ref):
        return jnp.concatenate(
            [jnp.dot(xc_bf[:, p * LANES:(p + 1) * LANES], w_ref[p], preferred_element_type=F32)
             for p in range(D_LRU // LANES)], axis=-1)

    r_g = _sigmoid(block_gate(wa_ref) + ba_ref[...])
    i_g = _sigmoid(block_gate(wx_ref) + bx_ref[...])
    la = jnp.exp(-LRU_C * r_g * _softplus(-lam_ref[...]))
    mult = jnp.sqrt(1.0 - la * la)
    if pos0 == 0:
        mult = jnp.where(jnp.logical_and(row < bb, ti == 0), 1.0, mult)
    lb = xc * i_g * mult
    h = o_h_ref[...]
    hs = []
    for t in range(chunk):
        ts = slice(t * bb, (t + 1) * bb)
        h = la[ts] * h + lb[ts]
        hs.append(h)
    o_h_ref[...] = h
    y_lru = _rms(jnp.concatenate(hs, axis=0) * jax.nn.gelu(gate_lru), og_ref[...])

    wc = wkv_chunk
    n_lv = max(1, (chunk - 1).bit_length())
    cw2 = 2 * wc
    left = lax.broadcasted_iota(jnp.int32, (1, 1, LANES), 2) < TM_HEAD
    left_c = lax.broadcasted_iota(jnp.int32, (1, 1, cw2), 2) < wc
    ri = lax.broadcasted_iota(jnp.int32, (wc, cw2), 0)
    ci = lax.broadcasted_iota(jnp.int32, (wc, cw2), 1) & (wc - 1)
    strict = ri > ci
    incl = ri >= ci
    eye = (ri == ci).astype(F32)

    def seq_body(i, carry):
        seqs = [i * group + j for j in range(group)]
        tsel = [pl.ds(b, wc, stride=bb) for b in seqs]
        gather = lambda ref: jnp.stack([ref.at[p][ts, :] for ts in tsel for p in range(PAIRS)])
        kap_g = gather(kap_s)
        rt_g = gather(rt_s)
        bt_g = gather(bt_s)
        kt_g = gather(kt_s)
        vv = gather(v_s)
        pe = jnp.stack([pe_s.at[p][pl.ds(b, 1), :] for b in seqs for p in range(PAIRS)])
        s0 = jnp.concatenate([o_wkv_ref[b] for b in seqs], axis=0)
        lhs = jnp.concatenate([kap_g, rt_g], axis=1)
        g_b = _bdot_nt(lhs, _block_diag_rows(bt_g, left))
        g_k = _bdot_nt(lhs, _block_diag_rows(kt_g, left))
        m_ab = jnp.where(strict, g_b[:, :wc], 0.0)
        m_ak = jnp.where(strict, g_k[:, :wc], 0.0)
        m_rb = jnp.where(incl, g_b[:, wc:], 0.0)
        m_rk = jnp.where(incl, g_k[:, wc:], 0.0)
        t_inv = eye - m_ab
        m_pow = _bdot(m_ab, _block_diag_rows(m_ab, left_c))
        for lv in range(1, n_lv):
            if lv < n_lv - 1:
                prod = _bdot(jnp.concatenate([t_inv, m_pow], axis=1), _block_diag_rows(m_pow, left_c))
                t_inv = t_inv + prod[:, :wc]
                m_pow = prod[:, wc:]
            else:
                t_inv = t_inv + _bdot(t_inv, _block_diag_rows(m_pow, left_c))
        xv = _bdot(jnp.concatenate([m_ak, m_rk], axis=1), _block_diag_rows(vv, left))
        z = _bdot_nt(lhs, _block_diag_rows(s0, left))
        uu = -_bdot(t_inv, _block_diag_rows(z[:, :wc] + xv[:, :wc], left))
        yh = z[:, wc:] + _bdot(m_rb, _block_diag_rows(uu, left)) + xv[:, wc:]
        full = _bdot_tn(jnp.concatenate([uu, vv], axis=1), jnp.concatenate([bt_g, kt_g], axis=1))
        upd = jnp.where(left, full[:, :TM_HEAD], full[:, TM_HEAD:])
        s_new = (s0 + upd) * pe
        for j, (b, ts) in enumerate(zip(seqs, tsel)):
            o_wkv_ref[b] = s_new[j * PAIRS:(j + 1) * PAIRS]
            for p in range(PAIRS):
                yt_s.at[p][ts, :] = yh[j * PAIRS + p]
        return carry

    n_iter = bb // group
    if n_iter <= 2:
        for i in range(n_iter):
            seq_body(i, 0)
    else:
        lax.fori_loop(0, n_iter, seq_body, 0)

    yv = jnp.concatenate([yt_s[p, 0:rows, :] for p in range(PAIRS)], axis=-1)
    mean = head_sum(yv) * (1.0 / TM_HEAD)
    cen = yv - mean
    var = head_sum(cen * cen) * (1.0 / TM_HEAD)
    y_tm = ((cen * lax.rsqrt(var + GN_EPS)) * gng_ref[...] + gnb_ref[...] + bonus) * gate_tm
    mixed = jnp.concatenate([y_tm.astype(BF16), y_lru.astype(BF16)], axis=-1)
    x1 = x_buf[x_slot].reshape(rows, D_MODEL) + jnp.dot(mixed, w_out_ref[...], preferred_element_type=F32)
    x1_ref[...] = x1.reshape(chunk, bb, D_MODEL)

    o_shift_ref[...] = tm_ext[rows:rows + bb, :]
    o_conv_ref[...] = xb_ext[rows:rows + conv_rows, :].reshape(LRU_CONV_W - 1, bb, D_LRU)
    tm_ext[0:bb, :] = tm_ext[rows:rows + bb, :]
    xb_ext[0:conv_rows, :] = xb_ext[rows:rows + conv_rows, :]
    _read_chunk_end(x_stream, bi, ti, n_chunks)


def _const_spec(shape):
    zeros = (0,) * len(shape)
    return pl.BlockSpec(shape, lambda bi, ti: zeros, pipeline_mode=pl.Buffered(1))


def _mixer(x, st_shift, st_wkv, st_conv, st_h, weights, *, bb, chunk, pos0, group):
    batch, length, _ = x.shape
    assert batch % bb == 0 and length % chunk == 0 and bb % SUBLANES == 0 and bb % group == 0
    assert chunk & (chunk - 1) == 0 and chunk >= LRU_CONV_W - 1
    rows = bb * chunk
    grid = (batch // bb, length // chunk)
    act = pl.BlockSpec((chunk, bb, D_MODEL), lambda bi, ti: (ti, bi, 0))
    vec = lambda w: pl.BlockSpec((bb, w), lambda bi, ti: (bi, 0))
    wkv_spec = pl.BlockSpec((bb, PAIRS, TM_HEAD, LANES), lambda bi, ti: (bi, 0, 0, 0))
    conv_spec = pl.BlockSpec((LRU_CONV_W - 1, bb, D_LRU), lambda bi, ti: (0, bi, 0))
    in_specs = [pl.BlockSpec(memory_space=pl.ANY), vec(D_TM_PROJ), wkv_spec, conv_spec, vec(D_LRU)]
    in_specs += [_const_spec(w.shape) for w in weights]
    out_specs = [act, vec(D_TM_PROJ), wkv_spec, conv_spec, vec(D_LRU)]
    out_shape = [jax.ShapeDtypeStruct((length, batch, D_MODEL), F32),
                 jax.ShapeDtypeStruct((batch, D_TM_PROJ), F32),
                 jax.ShapeDtypeStruct((batch, PAIRS, TM_HEAD, LANES), F32),
                 jax.ShapeDtypeStruct((LRU_CONV_W - 1, batch, D_LRU), F32),
                 jax.ShapeDtypeStruct((batch, D_LRU), F32)]
    pair_rows = lambda n: pltpu.VMEM((PAIRS, n, LANES), F32)
    scratch = [pltpu.VMEM((rows + bb, D_TM_PROJ), F32),
               pltpu.VMEM((rows + (LRU_CONV_W - 1) * bb, D_LRU), F32)]
    scratch += [pair_rows(max(chunk, SUBLANES) * bb)] * 6 + [pair_rows(bb)]
    scratch += [pltpu.VMEM((READ_SLOTS, chunk, bb, D_MODEL), F32), pltpu.SemaphoreType.DMA((READ_SLOTS,))]
    return pl.pallas_call(
        functools.partial(_mixer_kernel, bb=bb, chunk=chunk, pos0=pos0, group=group),
        out_shape=out_shape, grid=grid, in_specs=in_specs, out_specs=out_specs,
        scratch_shapes=scratch, name="mixer",
        compiler_params=pltpu.CompilerParams(dimension_semantics=("arbitrary", "arbitrary"),
                                             vmem_limit_bytes=VMEM_LIMIT_BYTES),
    )(x, st_shift, st_wkv, st_conv, st_h, *weights)


def _ffn_kernel(x1_ref, st_conv_ref,
                g2_ref, w_up_ref, w_gate_ref, cw_ref, cb_ref, w_down_ref, gf_ref,
                y_hbm, o_conv_ref,
                up_ext, y_buf, y_sem,
                *, bb, chunk):
    rows = bb * chunk
    conv_rows = (FFN_CONV_W - 1) * bb
    bi = pl.program_id(0)
    ti = pl.program_id(1)
    n_chunks = pl.num_programs(1)
    y_stream = _TimeMajorStream(y_hbm, y_buf, y_sem, bb=bb, chunk=chunk, to_hbm=True)
    slot = lax.rem(ti, 2)

    @pl.when(ti == 0)
    def _():
        up_ext[0:conv_rows, :] = st_conv_ref[...].reshape(conv_rows, D_FF)

    x1 = x1_ref[...].reshape(rows, D_MODEL)
    xn = _rms(x1, g2_ref[...]).astype(BF16)
    acc = x1
    for n in range(D_FF // FF_COL_TILE):
        cols = slice(n * FF_COL_TILE, (n + 1) * FF_COL_TILE)
        up = jnp.dot(xn, w_up_ref[:, cols], preferred_element_type=F32)
        gate = jnp.dot(xn, w_gate_ref[:, cols], preferred_element_type=F32)
        up_ext[conv_rows:conv_rows + rows, cols] = up
        upc = cb_ref[:, cols] + up_ext[0:rows, cols] * cw_ref[0:1, cols]
        upc = upc + up_ext[bb:bb + rows, cols] * cw_ref[1:2, cols]
        upc = upc + up * cw_ref[2:3, cols]
        hid = (jax.nn.gelu(upc) * gate).astype(BF16)
        acc = acc + jnp.dot(hid, w_down_ref[cols, :], preferred_element_type=F32)
    o_conv_ref[...] = up_ext[rows:rows + conv_rows, :].reshape(FFN_CONV_W - 1, bb, D_FF)
    up_ext[0:conv_rows, :] = up_ext[rows:rows + conv_rows, :]

    y_buf[slot] = _rms(acc, gf_ref[...]).reshape(chunk, bb, D_MODEL)
    y_stream.start(bi, ti, slot)

    @pl.when(ti > 0)
    def _():
        y_stream.wait(bi, ti - 1, 1 - slot)

    @pl.when(ti == n_chunks - 1)
    def _():
        y_stream.wait(bi, ti, slot)


def _ffn(x1, st_conv, weights, *, bb, chunk):
    length, batch, _ = x1.shape
    assert batch % bb == 0 and length % chunk == 0 and bb % SUBLANES == 0
    assert chunk >= FFN_CONV_W - 1
    rows = bb * chunk
    grid = (batch // bb, length // chunk)
    act = pl.BlockSpec((chunk, bb, D_MODEL), lambda bi, ti: (ti, bi, 0))
    conv_spec = pl.BlockSpec((FFN_CONV_W - 1, bb, D_FF), lambda bi, ti: (0, bi, 0))
    in_specs = [act, conv_spec] + [_const_spec(w.shape) for w in weights]
    out_shape = [jax.ShapeDtypeStruct((batch, length, D_MODEL), F32),
                 jax.ShapeDtypeStruct((FFN_CONV_W - 1, batch, D_FF), F32)]
    scratch = [pltpu.VMEM((rows + (FFN_CONV_W - 1) * bb, D_FF), F32),
               pltpu.VMEM((2, chunk, bb, D_MODEL), F32), pltpu.SemaphoreType.DMA((2,))]
    return pl.pallas_call(
        functools.partial(_ffn_kernel, bb=bb, chunk=chunk),
        out_shape=out_shape, grid=grid, in_specs=in_specs,
        out_specs=[pl.BlockSpec(memory_space=pl.ANY), conv_spec],
        scratch_shapes=scratch, name="ffn",
        compiler_params=pltpu.CompilerParams(dimension_semantics=("arbitrary", "arbitrary"),
                                             vmem_limit_bytes=VMEM_LIMIT_BYTES),
    )(x1, st_conv, *weights)


def _ffn_state_kernel(x1_ref, g2_ref, w_up_ref, o_conv_ref, *, bb):
    tail = FFN_CONV_W - 1
    x1 = x1_ref[...].reshape(tail * bb, D_MODEL)
    up = _dot(_rms(x1, g2_ref[...]), w_up_ref[...])
    o_conv_ref[...] = up.reshape(tail, bb, D_FF)


def _ffn_state(x1, g2, w_up):
    length, batch, _ = x1.shape
    tail = FFN_CONV_W - 1
    assert length % tail == 0 and batch % SUBLANES == 0
    return pl.pallas_call(
        functools.partial(_ffn_state_kernel, bb=batch),
        out_shape=jax.ShapeDtypeStruct((tail, batch, D_FF), F32), grid=(1,),
        in_specs=[pl.BlockSpec((tail, batch, D_MODEL), lambda i: (length // tail - 1, 0, 0)),
                  pl.BlockSpec(g2.shape, lambda i: (0, 0)), pl.BlockSpec(w_up.shape, lambda i: (0, 0))],
        out_specs=pl.BlockSpec((tail, batch, D_FF), lambda i: (0, 0, 0)), name="ffn_state",
        compiler_params=pltpu.CompilerParams(dimension_semantics=("arbitrary",),
                                             vmem_limit_bytes=VMEM_LIMIT_BYTES),
    )(x1, g2, w_up)


def _row(v):
    return v.reshape(1, -1).astype(F32)


def _pair_block_diag(w):
    nb, n, _ = w.shape
    w = w.reshape(nb // 2, 2, n, n)
    eye = jnp.eye(2, dtype=w.dtype)
    return (eye[None, :, None, :, None] * w[:, :, :, None, :]).reshape(nb // 2, 2 * n, 2 * n)


def _wkv_to_pairs(s):
    b = s.shape[0]
    s = s.astype(F32).reshape(b, PAIRS, 2, TM_HEAD, TM_HEAD)
    return jnp.transpose(s, (0, 1, 3, 2, 4)).reshape(b, PAIRS, TM_HEAD, LANES)


def _wkv_from_pairs(s):
    b = s.shape[0]
    s = s.reshape(b, PAIRS, TM_HEAD, 2, TM_HEAD)
    return jnp.transpose(s, (0, 1, 3, 2, 4)).reshape(b, TM_HEADS, TM_HEAD, TM_HEAD)


def _layer(x, states, mixer_w, ffn_w, *, mixer_bb, ffn_bb, mixer_chunk, ffn_chunk, pos0, group):
    st_shift, st_wkv, st_conv, st_h, st_fconv = states
    x1, o_shift, o_wkv, o_conv, o_h = _mixer(x, st_shift, st_wkv, st_conv, st_h, mixer_w,
                                             bb=mixer_bb, chunk=mixer_chunk, pos0=pos0, group=group)
    y, o_fconv = _ffn(x1, st_fconv, ffn_w, bb=ffn_bb, chunk=ffn_chunk)
    return y, (o_shift, o_wkv, o_conv, o_h, o_fconv)


def kernel(x_prompt, x_sample, state_tm_shift, state_tm_wkv, state_lru_conv, state_lru_h, state_ffn_conv, meta_tokens, norm1_g, w_in, tm_mu, tm_w0, tm_w_up, tm_a0, tm_a_up, tm_g_up, tm_k_k, tm_k_a, tm_r_k, tm_gn_g, tm_gn_b, lru_conv_w, lru_conv_b, lru_wa, lru_ba, lru_wx, lru_bx, lru_lambda, lru_out_g, w_out, norm2_g, ffn_w_up, ffn_w_gate, ffn_conv_w, ffn_conv_b, ffn_w_down, norm_f_g):
    depth = w_in.shape[0]
    assert depth == 1
    l = 0
    zeros_lora = jnp.zeros((DECAY_RANK, D_TM), F32)
    head_id = jnp.arange(D_TM) // TM_HEAD
    seg = (head_id[:, None] == jnp.arange(LANES)[None, :]).astype(BF16)
    mixer_w = (
        _row(norm1_g[l]), w_in[l].astype(BF16), _row(tm_mu[l]), _row(tm_w0[l]),
        jnp.concatenate([tm_w_up[l], zeros_lora], axis=0).astype(BF16),
        _row(tm_a0[l]),
        jnp.concatenate([zeros_lora, tm_a_up[l]], axis=0).astype(BF16),
        tm_g_up[l].astype(BF16),
        _row(tm_k_k[l]), _row(tm_k_a[l]), _row(tm_r_k[l]), _row(tm_gn_g[l]), _row(tm_gn_b[l]),
        lru_conv_w[l].astype(F32), _row(lru_conv_b[l]),
        _pair_block_diag(lru_wa[l]).astype(BF16), _row(lru_ba[l]),
        _pair_block_diag(lru_wx[l]).astype(BF16), _row(lru_bx[l]),
        _row(lru_lambda[l]), _row(lru_out_g[l]),
        seg, seg.T, w_out[l].astype(BF16),
    )
    ffn_w = (
        _row(norm2_g[l]), ffn_w_up[l].astype(BF16), ffn_w_gate[l].astype(BF16),
        ffn_conv_w[l].astype(F32), _row(ffn_conv_b[l]), ffn_w_down[l].astype(BF16), _row(norm_f_g),
    )

    bsz, seq = x_prompt.shape[0], x_prompt.shape[1]
    x_meta = jnp.broadcast_to(meta_tokens[None].astype(F32), (bsz, N_META, D_MODEL))
    x1_meta, *meta_st = _mixer(x_meta, jnp.zeros((bsz, D_TM_PROJ), F32),
                               jnp.zeros((bsz, PAIRS, TM_HEAD, LANES), F32),
                               jnp.zeros((LRU_CONV_W - 1, bsz, D_LRU), F32),
                               jnp.zeros((bsz, D_LRU), F32), mixer_w,
                               bb=bsz, chunk=N_META, pos0=0, group=4)
    p_init = tuple(meta_st) + (_ffn_state(x1_meta, ffn_w[0], ffn_w[1]),)

    y_prompt, p_st = _layer(x_prompt, p_init, mixer_w, ffn_w,
                            mixer_bb=bsz, ffn_bb=bsz, mixer_chunk=64, ffn_chunk=64,
                            pos0=N_META, group=4)

    dec_b, dec_seq = x_sample.shape[0], x_sample.shape[1]
    s_init = (state_tm_shift[l].astype(F32), _wkv_to_pairs(state_tm_wkv[l]),
              jnp.transpose(state_lru_conv[l].astype(F32), (1, 0, 2)), state_lru_h[l].astype(F32),
              jnp.transpose(state_ffn_conv[l].astype(F32), (1, 0, 2)))
    y_sample, s_st = _layer(x_sample, s_init, mixer_w, ffn_w,
                            mixer_bb=32, ffn_bb=64, mixer_chunk=dec_seq, ffn_chunk=dec_seq,
                            pos0=PAST_LEN, group=16)

    def unpack(st):
        o_shift, o_wkv, o_conv, o_h, o_fconv = st
        return (o_shift[None], _wkv_from_pairs(o_wkv)[None], jnp.transpose(o_conv, (1, 0, 2))[None],
                o_h[None], jnp.transpose(o_fconv, (1, 0, 2))[None])

    return (y_prompt, y_sample) + unpack(p_st) + unpack(s_st)
```

```python
import functools
import math

import jax
import jax.numpy as jnp
from jax import lax
from jax.experimental import pallas as pl
from jax.experimental.pallas import tpu as pltpu

F32 = jnp.float32
BF16 = jnp.bfloat16

D_MODEL = 1024
N_META = 16
PAST_LEN = 16384
D_TM = 512
TM_HEAD = 64
TM_HEADS = 8
DECAY_RANK = 64
AAA_RANK = 64
GATE_RANK = 128
D_TM_PROJ = 3 * D_TM + DECAY_RANK + AAA_RANK + GATE_RANK
D_LRU = 512
LRU_CONV_W = 4
LRU_C = 8.0
D_IN_PROJ = D_TM_PROJ + 2 * D_LRU
D_FF = 3 * D_MODEL
FFN_CONV_W = 3
EPS = 1e-6
GN_EPS = 64e-5

SUBLANES = 8
LANES = 128
PAIRS = TM_HEADS // 2
LORA_OFF = 3 * D_TM
GATE_OFF = LORA_OFF + DECAY_RANK + AAA_RANK
FF_COL_TILE = 1024
VMEM_LIMIT_BYTES = 60 * 1024 * 1024


def _dot(a, b):
    return jnp.dot(a.astype(BF16), b.astype(BF16), preferred_element_type=F32)


def _bdot(a, b):
    return lax.dot_general(a.astype(BF16), b.astype(BF16), (((2,), (1,)), ((0,), (0,))),
                           preferred_element_type=F32)


def _bdot_nt(a, b):
    return lax.dot_general(a.astype(BF16), b.astype(BF16), (((2,), (2,)), ((0,), (0,))),
                           preferred_element_type=F32)


def _bdot_tn(a, b):
    return lax.dot_general(a.astype(BF16), b.astype(BF16), (((1,), (1,)), ((0,), (0,))),
                           preferred_element_type=F32)


def _rms(x, g):
    return x * lax.rsqrt(jnp.mean(x * x, axis=-1, keepdims=True) + EPS) * g


def _sigmoid(z):
    return 0.5 * jnp.tanh(0.5 * z) + 0.5


def _softplus(z):
    return jnp.maximum(z, 0.0) + jnp.log(1.0 + jnp.exp(-jnp.abs(z)))


def _block_diag_rows(x, left):
    return jnp.concatenate([jnp.where(left, x, 0.0), jnp.where(left, 0.0, x)], axis=1)


def _shift_time(carry, x, steps, bb):
    n = steps * bb
    return jnp.concatenate([carry[carry.shape[0] - n:], x[:x.shape[0] - n]], axis=0)


class _TimeMajorStream:
    def __init__(self, hbm, buf, sem, *, bb, chunk, to_hbm):
        self.hbm, self.buf, self.sem = hbm, buf, sem
        self.bb, self.chunk, self.to_hbm = bb, chunk, to_hbm

    def _copy(self, bi, ti, slot, t):
        seqs = pl.ds(pl.multiple_of(bi * self.bb, SUBLANES), self.bb)
        hbm_rows = self.hbm.at[seqs, ti * self.chunk + t, :]
        tile = self.buf.at[slot, t]
        src, dst = (tile, hbm_rows) if self.to_hbm else (hbm_rows, tile)
        return pltpu.make_async_copy(src, dst, self.sem.at[slot])

    def start(self, bi, ti, slot):
        for t in range(self.chunk):
            self._copy(bi, ti, slot, t).start()

    def wait(self, bi, ti, slot):
        for t in range(self.chunk):
            self._copy(bi, ti, slot, t).wait()


READ_SLOTS = 3


def _read_chunk_begin(stream, bi, ti, n_chunks):
    last = n_chunks - 1

    @pl.when(ti == 0)
    def _():
        stream.start(bi, 0, 0)
        stream.start(bi, jnp.minimum(1, last), 1)

    slot = lax.rem(ti, READ_SLOTS)
    stream.wait(bi, ti, slot)
    return slot


def _read_chunk_end(stream, bi, ti, n_chunks):
    last = n_chunks - 1
    stream.start(bi, jnp.minimum(ti + 2, last), lax.rem(ti + 2, READ_SLOTS))

    @pl.when(ti == last)
    def _():
        stream.wait(bi, last, lax.rem(ti + 1, READ_SLOTS))
        stream.wait(bi, last, lax.rem(ti + 2, READ_SLOTS))


def _mixer_kernel(x_hbm, st_shift_ref, st_wkv_ref, st_conv_ref, st_h_ref,
                  g1_ref, w_in_ref, mu_ref, w0_ref, wdec_ref, a0_ref, waaa_ref, wgate_ref,
                  kk_ref, ka_ref, rk_ref, gng_ref, gnb_ref,
                  cw_ref, cb_ref, wa_ref, ba_ref, wx_ref, bx_ref, lam_ref, og_ref,
                  seg_ref, segt_ref, w_out_ref,
                  x1_ref, o_shift_ref, o_wkv_ref, o_conv_ref, o_h_ref,
                  tm_carry, xb_carry, kap_s, rt_s, bt_s, kt_s, v_s, yt_s, pe_s, x_buf, x_sem,
                  *, bb, chunk, pos0, group):
    rows = bb * chunk
    conv_rows = (LRU_CONV_W - 1) * bb
    wkv_chunk = max(chunk, SUBLANES)
    bi = pl.program_id(0)
    ti = pl.program_id(1)
    n_chunks = pl.num_programs(1)
    x_stream = _TimeMajorStream(x_hbm, x_buf, x_sem, bb=bb, chunk=chunk, to_hbm=False)
    x_slot = _read_chunk_begin(x_stream, bi, ti, n_chunks)

    @pl.when(ti == 0)
    def _():
        tm_carry[...] = st_shift_ref[...]
        xb_carry[...] = st_conv_ref[...].reshape(conv_rows, D_LRU)
        o_wkv_ref[...] = st_wkv_ref[...]
        o_h_ref[...] = st_h_ref[...]
        if wkv_chunk > chunk:
            for ref in (kap_s, rt_s, bt_s, kt_s, v_s):
                ref[:, rows:wkv_chunk * bb, :] = jnp.zeros((PAIRS, wkv_chunk * bb - rows, LANES), F32)

    seg = seg_ref[...]
    segt = segt_ref[...]

    def head_sum(x):
        s = jnp.dot(x.astype(BF16), seg, preferred_element_type=F32)
        return jnp.dot(s.astype(BF16), segt, preferred_element_type=F32)

    x = x_buf[x_slot].reshape(rows, D_MODEL)
    u = _dot(_rms(x, g1_ref[...]), w_in_ref[...])
    u_tm = u[:, :D_TM_PROJ]
    xb = u[:, D_TM_PROJ:D_TM_PROJ + D_LRU]
    gate_lru = u[:, D_TM_PROJ + D_LRU:D_IN_PROJ]

    row = lax.broadcasted_iota(jnp.int32, (rows, D_TM), 0)

    um = u_tm + (_shift_time(tm_carry[...], u_tm, 1, bb) - u_tm) * mu_ref[...]
    r = um[:, 0:D_TM]
    k = um[:, D_TM:2 * D_TM]
    v = um[:, 2 * D_TM:3 * D_TM]
    x_lora = um[:, LORA_OFF:GATE_OFF]
    x_gate = um[:, GATE_OFF:D_TM_PROJ]
    log_decay = -math.exp(-0.5) * _sigmoid(w0_ref[...] + _dot(jnp.tanh(x_lora), wdec_ref[...]))
    a = _sigmoid(a0_ref[...] + _dot(x_lora, waaa_ref[...]))
    gate_tm = _dot(_sigmoid(x_gate), wgate_ref[...])
    kk = k * kk_ref[...]
    k = k * (1.0 + (a - 1.0) * ka_ref[...])
    kk = kk * lax.rsqrt(jnp.maximum(head_sum(kk * kk), 1e-24))
    bonus = head_sum(r * k * rk_ref[...]) * v
    acc = log_decay[0:bb]
    cum = [acc]
    for t in range(1, chunk):
        acc = acc + log_decay[t * bb:(t + 1) * bb]
        cum.append(acc)
    c = jnp.concatenate(cum, axis=0)
    inv_p = jnp.exp(-c)
    kap = kk * jnp.exp(c - log_decay)
    rt = r * jnp.exp(c)
    bt = kk * a * inv_p
    kt = k * inv_p
    p_end = jnp.exp(cum[chunk - 1])
    for p in range(PAIRS):
        ls = slice(p * LANES, (p + 1) * LANES)
        kap_s[p, 0:rows, :] = kap[:, ls]
        rt_s[p, 0:rows, :] = rt[:, ls]
        bt_s[p, 0:rows, :] = bt[:, ls]
        kt_s[p, 0:rows, :] = kt[:, ls]
        v_s[p, 0:rows, :] = v[:, ls]
        pe_s[p] = p_end[:, ls]

    xb_prev = xb_carry[...]
    xc = cb_ref[...] + _shift_time(xb_prev, xb, 3, bb) * cw_ref[0:1, :]
    xc = xc + _shift_time(xb_prev, xb, 2, bb) * cw_ref[1:2, :]
    xc = xc + _shift_time(xb_prev, xb, 1, bb) * cw_ref[2:3, :]
    xc = xc + xb * cw_ref[3:4, :]
    xc_bf = xc.astype(BF16)

    def block_gate(w_ref):
        return jnp.concatenate(
            [jnp.dot(xc_bf[:, p * LANES:(p + 1) * LANES], w_ref[p], preferred_element_type=F32)
             for p in range(D_LRU // LANES)], axis=-1)

    r_g = _sigmoid(block_gate(wa_ref) + ba_ref[...])
    i_g = _sigmoid(block_gate(wx_ref) + bx_ref[...])
    la = jnp.exp(-LRU_C * r_g * _softplus(-lam_ref[...]))
    mult = jnp.sqrt(1.0 - la * la)
    if pos0 == 0:
        mult = jnp.where(jnp.logical_and(row < bb, ti == 0), 1.0, mult)
    lb = xc * i_g * mult
    h = o_h_ref[...]
    hs = []
    for t in range(chunk):
        ts = slice(t * bb, (t + 1) * bb)
        h = la[ts] * h + lb[ts]
        hs.append(h)
    o_h_ref[...] = h
    y_lru = _rms(jnp.concatenate(hs, axis=0) * jax.nn.gelu(gate_lru), og_ref[...])

    wc = wkv_chunk
    n_lv = max(1, (chunk - 1).bit_length())
    cw2 = 2 * wc
    left = lax.broadcasted_iota(jnp.int32, (1, 1, LANES), 2) < TM_HEAD
    left_c = lax.broadcasted_iota(jnp.int32, (1, 1, cw2), 2) < wc
    ri = lax.broadcasted_iota(jnp.int32, (wc, cw2), 0)
    ci = lax.broadcasted_iota(jnp.int32, (wc, cw2), 1) & (wc - 1)
    strict = ri > ci
    incl = ri >= ci
    eye = (ri == ci).astype(F32)

    def seq_body(i, carry):
        seqs = [i * group + j for j in range(group)]
        tsel = [pl.ds(b, wc, stride=bb) for b in seqs]
        gather = lambda ref: jnp.stack([ref.at[p][ts, :] for ts in tsel for p in range(PAIRS)])
        kap_g = gather(kap_s)
        rt_g = gather(rt_s)
        bt_g = gather(bt_s)
        kt_g = gather(kt_s)
        vv = gather(v_s)
        pe = jnp.stack([pe_s.at[p][pl.ds(b, 1), :] for b in seqs for p in range(PAIRS)])
        s0 = jnp.concatenate([o_wkv_ref[b] for b in seqs], axis=0)
        lhs = jnp.concatenate([kap_g, rt_g], axis=1)
        g_b = _bdot_nt(lhs, _block_diag_rows(bt_g, left))
        g_k = _bdot_nt(lhs, _block_diag_rows(kt_g, left))
        m_ab = jnp.where(strict, g_b[:, :wc], 0.0)
        m_ak = jnp.where(strict, g_k[:, :wc], 0.0)
        m_rb = jnp.where(incl, g_b[:, wc:], 0.0)
        m_rk = jnp.where(incl, g_k[:, wc:], 0.0)
        t_inv = eye - m_ab
        m_pow = _bdot(m_ab, _block_diag_rows(m_ab, left_c))
        for lv in range(1, n_lv):
            if lv < n_lv - 1:
                prod = _bdot(jnp.concatenate([t_inv, m_pow], axis=1), _block_diag_rows(m_pow, left_c))
                t_inv = t_inv + prod[:, :wc]
                m_pow = prod[:, wc:]
            else:
                t_inv = t_inv + _bdot(t_inv, _block_diag_rows(m_pow, left_c))
        xv = _bdot(jnp.concatenate([m_ak, m_rk], axis=1), _block_diag_rows(vv, left))
        z = _bdot_nt(lhs, _block_diag_rows(s0, left))
        uu = -_bdot(t_inv, _block_diag_rows(z[:, :wc] + xv[:, :wc], left))
        yh = z[:, wc:] + _bdot(m_rb, _block_diag_rows(uu, left)) + xv[:, wc:]
        full = _bdot_tn(jnp.concatenate([uu, vv], axis=1), jnp.concatenate([bt_g, kt_g], axis=1))
        upd = jnp.where(left, full[:, :TM_HEAD], full[:, TM_HEAD:])
        s_new = (s0 + upd) * pe
        for j, (b, ts) in enumerate(zip(seqs, tsel)):
            o_wkv_ref[b] = s_new[j * PAIRS:(j + 1) * PAIRS]
            for p in range(PAIRS):
                yt_s.at[p][ts, :] = yh[j * PAIRS + p]
        return carry

    n_iter = bb // group
    if n_iter <= 2:
        for i in range(n_iter):
            seq_body(i, 0)
    else:
        lax.fori_loop(0, n_iter, seq_body, 0)

    yv = jnp.concatenate([yt_s[p, 0:rows, :] for p in range(PAIRS)], axis=-1)
    mean = head_sum(yv) * (1.0 / TM_HEAD)
    cen = yv - mean
    var = head_sum(cen * cen) * (1.0 / TM_HEAD)
    y_tm = ((cen * lax.rsqrt(var + GN_EPS)) * gng_ref[...] + gnb_ref[...] + bonus) * gate_tm
    mixed = jnp.concatenate([y_tm.astype(BF16), y_lru.astype(BF16)], axis=-1)
    x1 = x_buf[x_slot].reshape(rows, D_MODEL) + jnp.dot(mixed, w_out_ref[...], preferred_element_type=F32)
    x1_ref[...] = x1.reshape(chunk, bb, D_MODEL)

    o_shift_ref[...] = u_tm[rows - bb:]
    o_conv_ref[...] = xb[rows - conv_rows:].reshape(LRU_CONV_W - 1, bb, D_LRU)
    tm_carry[...] = u_tm[rows - bb:]
    xb_carry[...] = xb[rows - conv_rows:]
    _read_chunk_end(x_stream, bi, ti, n_chunks)


def _const_spec(shape):
    zeros = (0,) * len(shape)
    return pl.BlockSpec(shape, lambda bi, ti: zeros, pipeline_mode=pl.Buffered(1))


def _mixer(x, st_shift, st_wkv, st_conv, st_h, weights, *, bb, chunk, pos0, group):
    batch, length, _ = x.shape
    assert batch % bb == 0 and length % chunk == 0 and bb % SUBLANES == 0 and bb % group == 0
    assert chunk & (chunk - 1) == 0 and chunk >= LRU_CONV_W - 1
    rows = bb * chunk
    grid = (batch // bb, length // chunk)
    act = pl.BlockSpec((chunk, bb, D_MODEL), lambda bi, ti: (ti, bi, 0))
    vec = lambda w: pl.BlockSpec((bb, w), lambda bi, ti: (bi, 0))
    wkv_spec = pl.BlockSpec((bb, PAIRS, TM_HEAD, LANES), lambda bi, ti: (bi, 0, 0, 0))
    conv_spec = pl.BlockSpec((LRU_CONV_W - 1, bb, D_LRU), lambda bi, ti: (0, bi, 0))
    in_specs = [pl.BlockSpec(memory_space=pl.ANY), vec(D_TM_PROJ), wkv_spec, conv_spec, vec(D_LRU)]
    in_specs += [_const_spec(w.shape) for w in weights]
    out_specs = [act, vec(D_TM_PROJ), wkv_spec, conv_spec, vec(D_LRU)]
    out_shape = [jax.ShapeDtypeStruct((length, batch, D_MODEL), F32),
                 jax.ShapeDtypeStruct((batch, D_TM_PROJ), F32),
                 jax.ShapeDtypeStruct((batch, PAIRS, TM_HEAD, LANES), F32),
                 jax.ShapeDtypeStruct((LRU_CONV_W - 1, batch, D_LRU), F32),
                 jax.ShapeDtypeStruct((batch, D_LRU), F32)]
    pair_rows = lambda n: pltpu.VMEM((PAIRS, n, LANES), F32)
    scratch = [pltpu.VMEM((bb, D_TM_PROJ), F32),
               pltpu.VMEM(((LRU_CONV_W - 1) * bb, D_LRU), F32)]
    scratch += [pair_rows(max(chunk, SUBLANES) * bb)] * 6 + [pair_rows(bb)]
    scratch += [pltpu.VMEM((READ_SLOTS, chunk, bb, D_MODEL), F32), pltpu.SemaphoreType.DMA((READ_SLOTS,))]
    return pl.pallas_call(
        functools.partial(_mixer_kernel, bb=bb, chunk=chunk, pos0=pos0, group=group),
        out_shape=out_shape, grid=grid, in_specs=in_specs, out_specs=out_specs,
        scratch_shapes=scratch, name="mixer",
        compiler_params=pltpu.CompilerParams(dimension_semantics=("arbitrary", "arbitrary"),
                                             vmem_limit_bytes=VMEM_LIMIT_BYTES),
    )(x, st_shift, st_wkv, st_conv, st_h, *weights)


def _ffn_kernel(x1_ref, st_conv_ref,
                g2_ref, w_up_ref, w_gate_ref, cw_ref, cb_ref, w_down_ref, gf_ref,
                y_hbm, o_conv_ref,
                up_carry, y_buf, y_sem,
                *, bb, chunk):
    rows = bb * chunk
    conv_rows = (FFN_CONV_W - 1) * bb
    bi = pl.program_id(0)
    ti = pl.program_id(1)
    n_chunks = pl.num_programs(1)
    y_stream = _TimeMajorStream(y_hbm, y_buf, y_sem, bb=bb, chunk=chunk, to_hbm=True)
    slot = lax.rem(ti, 2)

    @pl.when(ti == 0)
    def _():
        up_carry[...] = st_conv_ref[...].reshape(conv_rows, D_FF)

    x1 = x1_ref[...].reshape(rows, D_MODEL)
    xn = _rms(x1, g2_ref[...]).astype(BF16)
    acc = x1
    for n in range(D_FF // FF_COL_TILE):
        cols = slice(n * FF_COL_TILE, (n + 1) * FF_COL_TILE)
        up = jnp.dot(xn, w_up_ref[:, cols], preferred_element_type=F32)
        gate = jnp.dot(xn, w_gate_ref[:, cols], preferred_element_type=F32)
        up_prev = up_carry[:, cols]
        upc = cb_ref[:, cols] + _shift_time(up_prev, up, 2, bb) * cw_ref[0:1, cols]
        upc = upc + _shift_time(up_prev, up, 1, bb) * cw_ref[1:2, cols]
        upc = upc + up * cw_ref[2:3, cols]
        up_carry[:, cols] = up[rows - conv_rows:]
        o_conv_ref[:, :, cols] = up[rows - conv_rows:].reshape(FFN_CONV_W - 1, bb, FF_COL_TILE)
        hid = (jax.nn.gelu(upc) * gate).astype(BF16)
        acc = acc + jnp.dot(hid, w_down_ref[cols, :], preferred_element_type=F32)

    y_buf[slot] = _rms(acc, gf_ref[...]).reshape(chunk, bb, D_MODEL)
    y_stream.start(bi, ti, slot)

    @pl.when(ti > 0)
    def _():
        y_stream.wait(bi, ti - 1, 1 - slot)

    @pl.when(ti == n_chunks - 1)
    def _():
        y_stream.wait(bi, ti, slot)


def _ffn(x1, st_conv, weights, *, bb, chunk):
    length, batch, _ = x1.shape
    assert batch % bb == 0 and length % chunk == 0 and bb % SUBLANES == 0
    assert chunk >= FFN_CONV_W - 1
    rows = bb * chunk
    grid = (batch // bb, length // chunk)
    act = pl.BlockSpec((chunk, bb, D_MODEL), lambda bi, ti: (ti, bi, 0))
    conv_spec = pl.BlockSpec((FFN_CONV_W - 1, bb, D_FF), lambda bi, ti: (0, bi, 0))
    in_specs = [act, conv_spec] + [_const_spec(w.shape) for w in weights]
    out_shape = [jax.ShapeDtypeStruct((batch, length, D_MODEL), F32),
                 jax.ShapeDtypeStruct((FFN_CONV_W - 1, batch, D_FF), F32)]
    scratch = [pltpu.VMEM(((FFN_CONV_W - 1) * bb, D_FF), F32),
               pltpu.VMEM((2, chunk, bb, D_MODEL), F32), pltpu.SemaphoreType.DMA((2,))]
    return pl.pallas_call(
        functools.partial(_ffn_kernel, bb=bb, chunk=chunk),
        out_shape=out_shape, grid=grid, in_specs=in_specs,
        out_specs=[pl.BlockSpec(memory_space=pl.ANY), conv_spec],
        scratch_shapes=scratch, name="ffn",
        compiler_params=pltpu.CompilerParams(dimension_semantics=("arbitrary", "arbitrary"),
                                             vmem_limit_bytes=VMEM_LIMIT_BYTES),
    )(x1, st_conv, *weights)


def _ffn_state_kernel(x1_ref, g2_ref, w_up_ref, o_conv_ref, *, bb):
    tail = FFN_CONV_W - 1
    x1 = x1_ref[...].reshape(tail * bb, D_MODEL)
    up = _dot(_rms(x1, g2_ref[...]), w_up_ref[...])
    o_conv_ref[...] = up.reshape(tail, bb, D_FF)


def _ffn_state(x1, g2, w_up):
    length, batch, _ = x1.shape
    tail = FFN_CONV_W - 1
    assert length % tail == 0 and batch % SUBLANES == 0
    return pl.pallas_call(
        functools.partial(_ffn_state_kernel, bb=batch),
        out_shape=jax.ShapeDtypeStruct((tail, batch, D_FF), F32), grid=(1,),
        in_specs=[pl.BlockSpec((tail, batch, D_MODEL), lambda i: (length // tail - 1, 0, 0)),
                  pl.BlockSpec(g2.shape, lambda i: (0, 0)), pl.BlockSpec(w_up.shape, lambda i: (0, 0))],
        out_specs=pl.BlockSpec((tail, batch, D_FF), lambda i: (0, 0, 0)), name="ffn_state",
        compiler_params=pltpu.CompilerParams(dimension_semantics=("arbitrary",),
                                             vmem_limit_bytes=VMEM_LIMIT_BYTES),
    )(x1, g2, w_up)


def _row(v):
    return v.reshape(1, -1).astype(F32)


def _pair_block_diag(w):
    nb, n, _ = w.shape
    w = w.reshape(nb // 2, 2, n, n)
    eye = jnp.eye(2, dtype=w.dtype)
    return (eye[None, :, None, :, None] * w[:, :, :, None, :]).reshape(nb // 2, 2 * n, 2 * n)


def _wkv_to_pairs(s):
    b = s.shape[0]
    s = s.astype(F32).reshape(b, PAIRS, 2, TM_HEAD, TM_HEAD)
    return jnp.transpose(s, (0, 1, 3, 2, 4)).reshape(b, PAIRS, TM_HEAD, LANES)


def _wkv_from_pairs(s):
    b = s.shape[0]
    s = s.reshape(b, PAIRS, TM_HEAD, 2, TM_HEAD)
    return jnp.transpose(s, (0, 1, 3, 2, 4)).reshape(b, TM_HEADS, TM_HEAD, TM_HEAD)


def _layer(x, states, mixer_w, ffn_w, *, mixer_bb, ffn_bb, mixer_chunk, ffn_chunk, pos0, group):
    st_shift, st_wkv, st_conv, st_h, st_fconv = states
    x1, o_shift, o_wkv, o_conv, o_h = _mixer(x, st_shift, st_wkv, st_conv, st_h, mixer_w,
                                             bb=mixer_bb, chunk=mixer_chunk, pos0=pos0, group=group)
    y, o_fconv = _ffn(x1, st_fconv, ffn_w, bb=ffn_bb, chunk=ffn_chunk)
    return y, (o_shift, o_wkv, o_conv, o_h, o_fconv)


def kernel(x_prompt, x_sample, state_tm_shift, state_tm_wkv, state_lru_conv, state_lru_h, state_ffn_conv, meta_tokens, norm1_g, w_in, tm_mu, tm_w0, tm_w_up, tm_a0, tm_a_up, tm_g_up, tm_k_k, tm_k_a, tm_r_k, tm_gn_g, tm_gn_b, lru_conv_w, lru_conv_b, lru_wa, lru_ba, lru_wx, lru_bx, lru_lambda, lru_out_g, w_out, norm2_g, ffn_w_up, ffn_w_gate, ffn_conv_w, ffn_conv_b, ffn_w_down, norm_f_g):
    depth = w_in.shape[0]
    assert depth == 1
    l = 0
    zeros_lora = jnp.zeros((DECAY_RANK, D_TM), F32)
    head_id = jnp.arange(D_TM) // TM_HEAD
    seg = (head_id[:, None] == jnp.arange(LANES)[None, :]).astype(BF16)
    mixer_w = (
        _row(norm1_g[l]), w_in[l].astype(BF16), _row(tm_mu[l]), _row(tm_w0[l]),
        jnp.concatenate([tm_w_up[l], zeros_lora], axis=0).astype(BF16),
        _row(tm_a0[l]),
        jnp.concatenate([zeros_lora, tm_a_up[l]], axis=0).astype(BF16),
        tm_g_up[l].astype(BF16),
        _row(tm_k_k[l]), _row(tm_k_a[l]), _row(tm_r_k[l]), _row(tm_gn_g[l]), _row(tm_gn_b[l]),
        lru_conv_w[l].astype(F32), _row(lru_conv_b[l]),
        _pair_block_diag(lru_wa[l]).astype(BF16), _row(lru_ba[l]),
        _pair_block_diag(lru_wx[l]).astype(BF16), _row(lru_bx[l]),
        _row(lru_lambda[l]), _row(lru_out_g[l]),
        seg, seg.T, w_out[l].astype(BF16),
    )
    ffn_w = (
        _row(norm2_g[l]), ffn_w_up[l].astype(BF16), ffn_w_gate[l].astype(BF16),
        ffn_conv_w[l].astype(F32), _row(ffn_conv_b[l]), ffn_w_down[l].astype(BF16), _row(norm_f_g),
    )

    bsz, seq = x_prompt.shape[0], x_prompt.shape[1]
    x_meta = jnp.broadcast_to(meta_tokens[None].astype(F32), (bsz, N_META, D_MODEL))
    x1_meta, *meta_st = _mixer(x_meta, jnp.zeros((bsz, D_TM_PROJ), F32),
                               jnp.zeros((bsz, PAIRS, TM_HEAD, LANES), F32),
                               jnp.zeros((LRU_CONV_W - 1, bsz, D_LRU), F32),
                               jnp.zeros((bsz, D_LRU), F32), mixer_w,
                               bb=bsz, chunk=N_META, pos0=0, group=4)
    p_init = tuple(meta_st) + (_ffn_state(x1_meta, ffn_w[0], ffn_w[1]),)

    y_prompt, p_st = _layer(x_prompt, p_init, mixer_w, ffn_w,
                            mixer_bb=bsz, ffn_bb=bsz, mixer_chunk=64, ffn_chunk=64,
                            pos0=N_META, group=4)

    dec_b, dec_seq = x_sample.shape[0], x_sample.shape[1]
    s_init = (state_tm_shift[l].astype(F32), _wkv_to_pairs(state_tm_wkv[l]),
              jnp.transpose(state_lru_conv[l].astype(F32), (1, 0, 2)), state_lru_h[l].astype(F32),
              jnp.transpose(state_ffn_conv[l].astype(F32), (1, 0, 2)))
    y_sample, s_st = _layer(x_sample, s_init, mixer_w, ffn_w,
                            mixer_bb=32, ffn_bb=64, mixer_chunk=dec_seq, ffn_chunk=dec_seq,
                            pos0=PAST_LEN, group=16)

    def unpack(st):
        o_shift, o_wkv, o_conv, o_h, o_fconv = st
        return (o_shift[None], _wkv_from_pairs(o_wkv)[None], jnp.transpose(o_conv, (1, 0, 2))[None],
                o_h[None], jnp.transpose(o_fconv, (1, 0, 2))[None])

    return (y_prompt, y_sample) + unpack(p_st) + unpack(s_st)
```

```python
import functools
import math

import jax
import jax.numpy as jnp
from jax import lax
from jax.experimental import pallas as pl
from jax.experimental.pallas import tpu as pltpu

F32 = jnp.float32
BF16 = jnp.bfloat16

D_MODEL = 1024
N_META = 16
PAST_LEN = 16384
D_TM = 512
TM_HEAD = 64
TM_HEADS = 8
DECAY_RANK = 64
AAA_RANK = 64
GATE_RANK = 128
D_TM_PROJ = 3 * D_TM + DECAY_RANK + AAA_RANK + GATE_RANK
D_LRU = 512
LRU_CONV_W = 4
LRU_C = 8.0
D_IN_PROJ = D_TM_PROJ + 2 * D_LRU
D_FF = 3 * D_MODEL
FFN_CONV_W = 3
EPS = 1e-6
GN_EPS = 64e-5
LOG2_E = math.log2(math.e)

SUBLANES = 8
LANES = 128
PAIRS = TM_HEADS // 2
LORA_OFF = 3 * D_TM
GATE_OFF = LORA_OFF + DECAY_RANK + AAA_RANK
FF_COL_TILE = 1024
VMEM_LIMIT_BYTES = 60 * 1024 * 1024


def _dot(a, b):
    return jnp.dot(a.astype(BF16), b.astype(BF16), preferred_element_type=F32)


def _bdot(a, b):
    return lax.dot_general(a.astype(BF16), b.astype(BF16), (((2,), (1,)), ((0,), (0,))),
                           preferred_element_type=F32)


def _bdot_nt(a, b):
    return lax.dot_general(a.astype(BF16), b.astype(BF16), (((2,), (2,)), ((0,), (0,))),
                           preferred_element_type=F32)


def _bdot_tn(a, b):
    return lax.dot_general(a.astype(BF16), b.astype(BF16), (((1,), (1,)), ((0,), (0,))),
                           preferred_element_type=F32)


def _rms(x, g):
    return x * lax.rsqrt(jnp.mean(x * x, axis=-1, keepdims=True) + EPS) * g


def _sigmoid(z):
    return 0.5 * jnp.tanh(0.5 * z) + 0.5


def _softplus(z):
    return jnp.maximum(z, 0.0) + jnp.log(1.0 + jnp.exp(-jnp.abs(z)))


def _block_diag_rows(x, left):
    return jnp.concatenate([jnp.where(left, x, 0.0), jnp.where(left, 0.0, x)], axis=1)


def _shift_time(carry, x, steps, bb):
    n = steps * bb
    return jnp.concatenate([carry[carry.shape[0] - n:], x[:x.shape[0] - n]], axis=0)


class _TimeMajorStream:
    def __init__(self, hbm, buf, sem, *, bb, chunk, to_hbm):
        self.hbm, self.buf, self.sem = hbm, buf, sem
        self.bb, self.chunk, self.to_hbm = bb, chunk, to_hbm

    def _copy(self, bi, ti, slot, t):
        seqs = pl.ds(pl.multiple_of(bi * self.bb, SUBLANES), self.bb)
        hbm_rows = self.hbm.at[seqs, ti * self.chunk + t, :]
        tile = self.buf.at[slot, t]
        src, dst = (tile, hbm_rows) if self.to_hbm else (hbm_rows, tile)
        return pltpu.make_async_copy(src, dst, self.sem.at[slot])

    def start(self, bi, ti, slot):
        for t in range(self.chunk):
            self._copy(bi, ti, slot, t).start()

    def wait(self, bi, ti, slot):
        for t in range(self.chunk):
            self._copy(bi, ti, slot, t).wait()


READ_SLOTS = 3


def _read_chunk_begin(stream, bi, ti, n_chunks):
    last = n_chunks - 1

    @pl.when(ti == 0)
    def _():
        stream.start(bi, 0, 0)
        stream.start(bi, jnp.minimum(1, last), 1)

    slot = lax.rem(ti, READ_SLOTS)
    stream.wait(bi, ti, slot)
    return slot


def _read_chunk_end(stream, bi, ti, n_chunks):
    last = n_chunks - 1
    stream.start(bi, jnp.minimum(ti + 2, last), lax.rem(ti + 2, READ_SLOTS))

    @pl.when(ti == last)
    def _():
        stream.wait(bi, last, lax.rem(ti + 1, READ_SLOTS))
        stream.wait(bi, last, lax.rem(ti + 2, READ_SLOTS))


def _mixer_kernel(x_hbm, st_shift_ref, st_wkv_ref, st_conv_ref, st_h_ref,
                  g1_ref, w_in_ref, mu_ref, w0_ref, wdec_ref, a0_ref, waaa_ref, wgate_ref,
                  kk_ref, ka_ref, rk_ref, gng_ref, gnb_ref,
                  cw_ref, cb_ref, wa_ref, ba_ref, wx_ref, bx_ref, lam_ref, og_ref,
                  seg_ref, segt_ref, w_out_ref,
                  x1_ref, o_shift_ref, o_wkv_ref, o_conv_ref, o_h_ref,
                  tm_carry, xb_carry, kap_s, rt_s, bt_s, kt_s, v_s, yt_s, pe_s, x_buf, x_sem,
                  *, bb, chunk, pos0, group):
    rows = bb * chunk
    conv_rows = (LRU_CONV_W - 1) * bb
    wkv_chunk = max(chunk, SUBLANES)
    bi = pl.program_id(0)
    ti = pl.program_id(1)
    n_chunks = pl.num_programs(1)
    x_stream = _TimeMajorStream(x_hbm, x_buf, x_sem, bb=bb, chunk=chunk, to_hbm=False)
    x_slot = _read_chunk_begin(x_stream, bi, ti, n_chunks)

    @pl.when(ti == 0)
    def _():
        tm_carry[...] = st_shift_ref[...]
        xb_carry[...] = st_conv_ref[...].reshape(conv_rows, D_LRU)
        o_wkv_ref[...] = st_wkv_ref[...]
        o_h_ref[...] = st_h_ref[...]
        if wkv_chunk > chunk:
            for ref in (kap_s, rt_s, bt_s, kt_s, v_s):
                ref[:, rows:wkv_chunk * bb, :] = jnp.zeros((PAIRS, wkv_chunk * bb - rows, LANES), F32)

    seg = seg_ref[...]
    segt = segt_ref[...]

    def head_sum(x):
        s = jnp.dot(x.astype(BF16), seg, preferred_element_type=F32)
        return jnp.dot(s.astype(BF16), segt, preferred_element_type=F32)

    x = x_buf[x_slot].reshape(rows, D_MODEL)
    u = _dot(_rms(x, g1_ref[...]), w_in_ref[...])
    u_tm = u[:, :D_TM_PROJ]
    xb = u[:, D_TM_PROJ:D_TM_PROJ + D_LRU]
    gate_lru = u[:, D_TM_PROJ + D_LRU:D_IN_PROJ]

    row = lax.broadcasted_iota(jnp.int32, (rows, D_TM), 0)

    um = u_tm + (_shift_time(tm_carry[...], u_tm, 1, bb) - u_tm) * mu_ref[...]
    r = um[:, 0:D_TM]
    k = um[:, D_TM:2 * D_TM]
    v = um[:, 2 * D_TM:3 * D_TM]
    x_lora = um[:, LORA_OFF:GATE_OFF]
    x_gate = um[:, GATE_OFF:D_TM_PROJ]
    log2_decay = (-math.exp(-0.5) * LOG2_E) * _sigmoid(w0_ref[...] + _dot(jnp.tanh(x_lora), wdec_ref[...]))
    a = _sigmoid(a0_ref[...] + _dot(x_lora, waaa_ref[...]))
    gate_tm = _dot(_sigmoid(x_gate), wgate_ref[...])
    kk = k * kk_ref[...]
    k = k * (a * ka_ref[...] + (1.0 - ka_ref[...]))
    kk = kk * lax.rsqrt(jnp.maximum(head_sum(kk * kk), 1e-24))
    bonus = head_sum(r * k * rk_ref[...]) * v
    acc = log2_decay[0:bb]
    cum = [acc]
    for t in range(1, chunk):
        acc = acc + log2_decay[t * bb:(t + 1) * bb]
        cum.append(acc)
    c = jnp.concatenate(cum, axis=0)
    inv_p = jnp.exp2(-c)
    kap = kk * jnp.exp2(c - log2_decay)
    rt = r * jnp.exp2(c)
    bt = kk * a * inv_p
    kt = k * inv_p
    p_end = jnp.exp2(cum[chunk - 1])
    for p in range(PAIRS):
        ls = slice(p * LANES, (p + 1) * LANES)
        kap_s[p, 0:rows, :] = kap[:, ls]
        rt_s[p, 0:rows, :] = rt[:, ls]
        bt_s[p, 0:rows, :] = bt[:, ls]
        kt_s[p, 0:rows, :] = kt[:, ls]
        v_s[p, 0:rows, :] = v[:, ls]
        pe_s[p] = p_end[:, ls]

    xb_prev = xb_carry[...]
    xc = cb_ref[...] + _shift_time(xb_prev, xb, 3, bb) * cw_ref[0:1, :]
    xc = xc + _shift_time(xb_prev, xb, 2, bb) * cw_ref[1:2, :]
    xc = xc + _shift_time(xb_prev, xb, 1, bb) * cw_ref[2:3, :]
    xc = xc + xb * cw_ref[3:4, :]
    xc_bf = xc.astype(BF16)

    def block_gate(w_ref):
        return jnp.concatenate(
            [jnp.dot(xc_bf[:, p * LANES:(p + 1) * LANES], w_ref[p], preferred_element_type=F32)
             for p in range(D_LRU // LANES)], axis=-1)

    r_g = _sigmoid(block_gate(wa_ref) + ba_ref[...])
    i_g = _sigmoid(block_gate(wx_ref) + bx_ref[...])
    la = jnp.exp2(r_g * ((-LRU_C * LOG2_E) * _softplus(-lam_ref[...])))
    gap = 1.0 - la * la
    mult = jnp.where(gap > 0.0, gap * lax.rsqrt(gap), 0.0)
    if pos0 == 0:
        mult = jnp.where(jnp.logical_and(row < bb, ti == 0), 1.0, mult)
    lb = xc * i_g * mult
    h = o_h_ref[...]
    hs = []
    for t in range(chunk):
        ts = slice(t * bb, (t + 1) * bb)
        h = la[ts] * h + lb[ts]
        hs.append(h)
    o_h_ref[...] = h
    y_lru = _rms(jnp.concatenate(hs, axis=0) * jax.nn.gelu(gate_lru), og_ref[...])

    wc = wkv_chunk
    n_lv = max(1, (chunk - 1).bit_length())
    cw2 = 2 * wc
    left = lax.broadcasted_iota(jnp.int32, (1, 1, LANES), 2) < TM_HEAD
    left_c = lax.broadcasted_iota(jnp.int32, (1, 1, cw2), 2) < wc
    ri = lax.broadcasted_iota(jnp.int32, (wc, cw2), 0)
    ci = lax.broadcasted_iota(jnp.int32, (wc, cw2), 1) & (wc - 1)
    strict = ri > ci
    incl = ri >= ci
    eye = (ri == ci).astype(F32)

    def seq_body(i, carry):
        seqs = [i * group + j for j in range(group)]
        tsel = [pl.ds(b, wc, stride=bb) for b in seqs]
        gather = lambda ref: jnp.stack([ref.at[p][ts, :] for ts in tsel for p in range(PAIRS)])
        kap_g = gather(kap_s)
        rt_g = gather(rt_s)
        bt_g = gather(bt_s)
        kt_g = gather(kt_s)
        vv = gather(v_s)
        pe = jnp.stack([pe_s.at[p][pl.ds(b, 1), :] for b in seqs for p in range(PAIRS)])
        s0 = jnp.concatenate([o_wkv_ref[b] for b in seqs], axis=0)
        lhs = jnp.concatenate([kap_g, rt_g], axis=1)
        g_b = _bdot_nt(lhs, _block_diag_rows(bt_g, left))
        g_k = _bdot_nt(lhs, _block_diag_rows(kt_g, left))
        m_ab = jnp.where(strict, g_b[:, :wc], 0.0)
        m_ak = jnp.where(strict, g_k[:, :wc], 0.0)
        m_rb = jnp.where(incl, g_b[:, wc:], 0.0)
        m_rk = jnp.where(incl, g_k[:, wc:], 0.0)
        t_inv = eye - m_ab
        m_pow = _bdot(m_ab, _block_diag_rows(m_ab, left_c))
        for lv in range(1, n_lv):
            if lv < n_lv - 1:
                prod = _bdot(jnp.concatenate([t_inv, m_pow], axis=1), _block_diag_rows(m_pow, left_c))
                t_inv = t_inv + prod[:, :wc]
                m_pow = prod[:, wc:]
            else:
                t_inv = t_inv + _bdot(t_inv, _block_diag_rows(m_pow, left_c))
        xv = _bdot(jnp.concatenate([m_ak, m_rk], axis=1), _block_diag_rows(vv, left))
        z = _bdot_nt(lhs, _block_diag_rows(s0, left))
        uu = -_bdot(t_inv, _block_diag_rows(z[:, :wc] + xv[:, :wc], left))
        yh = z[:, wc:] + _bdot(m_rb, _block_diag_rows(uu, left)) + xv[:, wc:]
        full = _bdot_tn(jnp.concatenate([uu, vv], axis=1), jnp.concatenate([bt_g, kt_g], axis=1))
        upd = jnp.where(left, full[:, :TM_HEAD], full[:, TM_HEAD:])
        s_new = (s0 + upd) * pe
        for j, (b, ts) in enumerate(zip(seqs, tsel)):
            o_wkv_ref[b] = s_new[j * PAIRS:(j + 1) * PAIRS]
            for p in range(PAIRS):
                yt_s.at[p][ts, :] = yh[j * PAIRS + p]
        return carry

    n_iter = bb // group
    if n_iter <= 2:
        for i in range(n_iter):
            seq_body(i, 0)
    else:
        lax.fori_loop(0, n_iter, seq_body, 0)

    yv = jnp.concatenate([yt_s[p, 0:rows, :] for p in range(PAIRS)], axis=-1)
    mean = head_sum(yv) * (1.0 / TM_HEAD)
    cen = yv - mean
    var = head_sum(cen * cen) * (1.0 / TM_HEAD)
    y_tm = ((cen * lax.rsqrt(var + GN_EPS)) * gng_ref[...] + gnb_ref[...] + bonus) * gate_tm
    mixed = jnp.concatenate([y_tm.astype(BF16), y_lru.astype(BF16)], axis=-1)
    x1 = x_buf[x_slot].reshape(rows, D_MODEL) + jnp.dot(mixed, w_out_ref[...], preferred_element_type=F32)
    x1_ref[...] = x1.reshape(chunk, bb, D_MODEL)

    o_shift_ref[...] = u_tm[rows - bb:]
    o_conv_ref[...] = xb[rows - conv_rows:].reshape(LRU_CONV_W - 1, bb, D_LRU)
    tm_carry[...] = u_tm[rows - bb:]
    xb_carry[...] = xb[rows - conv_rows:]
    _read_chunk_end(x_stream, bi, ti, n_chunks)


def _const_spec(shape):
    zeros = (0,) * len(shape)
    return pl.BlockSpec(shape, lambda bi, ti: zeros, pipeline_mode=pl.Buffered(1))


def _mixer(x, st_shift, st_wkv, st_conv, st_h, weights, *, bb, chunk, pos0, group):
    batch, length, _ = x.shape
    assert batch % bb == 0 and length % chunk == 0 and bb % SUBLANES == 0 and bb % group == 0
    assert chunk & (chunk - 1) == 0 and chunk >= LRU_CONV_W - 1
    rows = bb * chunk
    grid = (batch // bb, length // chunk)
    act = pl.BlockSpec((chunk, bb, D_MODEL), lambda bi, ti: (ti, bi, 0))
    vec = lambda w: pl.BlockSpec((bb, w), lambda bi, ti: (bi, 0))
    wkv_spec = pl.BlockSpec((bb, PAIRS, TM_HEAD, LANES), lambda bi, ti: (bi, 0, 0, 0))
    conv_spec = pl.BlockSpec((LRU_CONV_W - 1, bb, D_LRU), lambda bi, ti: (0, bi, 0))
    in_specs = [pl.BlockSpec(memory_space=pl.ANY), vec(D_TM_PROJ), wkv_spec, conv_spec, vec(D_LRU)]
    in_specs += [_const_spec(w.shape) for w in weights]
    out_specs = [act, vec(D_TM_PROJ), wkv_spec, conv_spec, vec(D_LRU)]
    out_shape = [jax.ShapeDtypeStruct((length, batch, D_MODEL), F32),
                 jax.ShapeDtypeStruct((batch, D_TM_PROJ), F32),
                 jax.ShapeDtypeStruct((batch, PAIRS, TM_HEAD, LANES), F32),
                 jax.ShapeDtypeStruct((LRU_CONV_W - 1, batch, D_LRU), F32),
                 jax.ShapeDtypeStruct((batch, D_LRU), F32)]
    pair_rows = lambda n: pltpu.VMEM((PAIRS, n, LANES), F32)
    scratch = [pltpu.VMEM((bb, D_TM_PROJ), F32),
               pltpu.VMEM(((LRU_CONV_W - 1) * bb, D_LRU), F32)]
    scratch += [pair_rows(max(chunk, SUBLANES) * bb)] * 6 + [pair_rows(bb)]
    scratch += [pltpu.VMEM((READ_SLOTS, chunk, bb, D_MODEL), F32), pltpu.SemaphoreType.DMA((READ_SLOTS,))]
    return pl.pallas_call(
        functools.partial(_mixer_kernel, bb=bb, chunk=chunk, pos0=pos0, group=group),
        out_shape=out_shape, grid=grid, in_specs=in_specs, out_specs=out_specs,
        scratch_shapes=scratch, name="mixer",
        compiler_params=pltpu.CompilerParams(dimension_semantics=("arbitrary", "arbitrary"),
                                             vmem_limit_bytes=VMEM_LIMIT_BYTES),
    )(x, st_shift, st_wkv, st_conv, st_h, *weights)


def _ffn_kernel(x1_ref, st_conv_ref,
                g2_ref, w_up_ref, w_gate_ref, cw_ref, cb_ref, w_down_ref, gf_ref,
                y_hbm, o_conv_ref,
                up_carry, y_buf, y_sem,
                *, bb, chunk):
    rows = bb * chunk
    conv_rows = (FFN_CONV_W - 1) * bb
    bi = pl.program_id(0)
    ti = pl.program_id(1)
    n_chunks = pl.num_programs(1)
    y_stream = _TimeMajorStream(y_hbm, y_buf, y_sem, bb=bb, chunk=chunk, to_hbm=True)
    slot = lax.rem(ti, 2)

    @pl.when(ti == 0)
    def _():
        up_carry[...] = st_conv_ref[...].reshape(conv_rows, D_FF)

    x1 = x1_ref[...].reshape(rows, D_MODEL)
    xn = _rms(x1, g2_ref[...]).astype(BF16)
    acc = x1
    for n in range(D_FF // FF_COL_TILE):
        cols = slice(n * FF_COL_TILE, (n + 1) * FF_COL_TILE)
        up = jnp.dot(xn, w_up_ref[:, cols], preferred_element_type=F32)
        gate = jnp.dot(xn, w_gate_ref[:, cols], preferred_element_type=F32)
        up_prev = up_carry[:, cols]
        upc = cb_ref[:, cols] + _shift_time(up_prev, up, 2, bb) * cw_ref[0:1, cols]
        upc = upc + _shift_time(up_prev, up, 1, bb) * cw_ref[1:2, cols]
        upc = upc + up * cw_ref[2:3, cols]
        up_carry[:, cols] = up[rows - conv_rows:]
        o_conv_ref[:, :, cols] = up[rows - conv_rows:].reshape(FFN_CONV_W - 1, bb, FF_COL_TILE)
        hid = (jax.nn.gelu(upc) * gate).astype(BF16)
        acc = acc + jnp.dot(hid, w_down_ref[cols, :], preferred_element_type=F32)

    y_buf[slot] = _rms(acc, gf_ref[...]).reshape(chunk, bb, D_MODEL)
    y_stream.start(bi, ti, slot)

    @pl.when(ti > 0)
    def _():
        y_stream.wait(bi, ti - 1, 1 - slot)

    @pl.when(ti == n_chunks - 1)
    def _():
        y_stream.wait(bi, ti, slot)


def _ffn(x1, st_conv, weights, *, bb, chunk):
    length, batch, _ = x1.shape
    assert batch % bb == 0 and length % chunk == 0 and bb % SUBLANES == 0
    assert chunk >= FFN_CONV_W - 1
    rows = bb * chunk
    grid = (batch // bb, length // chunk)
    act = pl.BlockSpec((chunk, bb, D_MODEL), lambda bi, ti: (ti, bi, 0))
    conv_spec = pl.BlockSpec((FFN_CONV_W - 1, bb, D_FF), lambda bi, ti: (0, bi, 0))
    in_specs = [act, conv_spec] + [_const_spec(w.shape) for w in weights]
    out_shape = [jax.ShapeDtypeStruct((batch, length, D_MODEL), F32),
                 jax.ShapeDtypeStruct((FFN_CONV_W - 1, batch, D_FF), F32)]
    scratch = [pltpu.VMEM(((FFN_CONV_W - 1) * bb, D_FF), F32),
               pltpu.VMEM((2, chunk, bb, D_MODEL), F32), pltpu.SemaphoreType.DMA((2,))]
    return pl.pallas_call(
        functools.partial(_ffn_kernel, bb=bb, chunk=chunk),
        out_shape=out_shape, grid=grid, in_specs=in_specs,
        out_specs=[pl.BlockSpec(memory_space=pl.ANY), conv_spec],
        scratch_shapes=scratch, name="ffn",
        compiler_params=pltpu.CompilerParams(dimension_semantics=("arbitrary", "arbitrary"),
                                             vmem_limit_bytes=VMEM_LIMIT_BYTES),
    )(x1, st_conv, *weights)


def _ffn_state_kernel(x1_ref, g2_ref, w_up_ref, o_conv_ref, *, bb):
    tail = FFN_CONV_W - 1
    x1 = x1_ref[...].reshape(tail * bb, D_MODEL)
    up = _dot(_rms(x1, g2_ref[...]), w_up_ref[...])
    o_conv_ref[...] = up.reshape(tail, bb, D_FF)


def _ffn_state(x1, g2, w_up):
    length, batch, _ = x1.shape
    tail = FFN_CONV_W - 1
    assert length % tail == 0 and batch % SUBLANES == 0
    return pl.pallas_call(
        functools.partial(_ffn_state_kernel, bb=batch),
        out_shape=jax.ShapeDtypeStruct((tail, batch, D_FF), F32), grid=(1,),
        in_specs=[pl.BlockSpec((tail, batch, D_MODEL), lambda i: (length // tail - 1, 0, 0)),
                  pl.BlockSpec(g2.shape, lambda i: (0, 0)), pl.BlockSpec(w_up.shape, lambda i: (0, 0))],
        out_specs=pl.BlockSpec((tail, batch, D_FF), lambda i: (0, 0, 0)), name="ffn_state",
        compiler_params=pltpu.CompilerParams(dimension_semantics=("arbitrary",),
                                             vmem_limit_bytes=VMEM_LIMIT_BYTES),
    )(x1, g2, w_up)


def _row(v):
    return v.reshape(1, -1).astype(F32)


def _pair_block_diag(w):
    nb, n, _ = w.shape
    w = w.reshape(nb // 2, 2, n, n)
    eye = jnp.eye(2, dtype=w.dtype)
    return (eye[None, :, None, :, None] * w[:, :, :, None, :]).reshape(nb // 2, 2 * n, 2 * n)


def _wkv_to_pairs(s):
    b = s.shape[0]
    s = s.astype(F32).reshape(b, PAIRS, 2, TM_HEAD, TM_HEAD)
    return jnp.transpose(s, (0, 1, 3, 2, 4)).reshape(b, PAIRS, TM_HEAD, LANES)


def _wkv_from_pairs(s):
    b = s.shape[0]
    s = s.reshape(b, PAIRS, TM_HEAD, 2, TM_HEAD)
    return jnp.transpose(s, (0, 1, 3, 2, 4)).reshape(b, TM_HEADS, TM_HEAD, TM_HEAD)


def _layer(x, states, mixer_w, ffn_w, *, mixer_bb, ffn_bb, mixer_chunk, ffn_chunk, pos0, group):
    st_shift, st_wkv, st_conv, st_h, st_fconv = states
    x1, o_shift, o_wkv, o_conv, o_h = _mixer(x, st_shift, st_wkv, st_conv, st_h, mixer_w,
                                             bb=mixer_bb, chunk=mixer_chunk, pos0=pos0, group=group)
    y, o_fconv = _ffn(x1, st_fconv, ffn_w, bb=ffn_bb, chunk=ffn_chunk)
    return y, (o_shift, o_wkv, o_conv, o_h, o_fconv)


def kernel(x_prompt, x_sample, state_tm_shift, state_tm_wkv, state_lru_conv, state_lru_h, state_ffn_conv, meta_tokens, norm1_g, w_in, tm_mu, tm_w0, tm_w_up, tm_a0, tm_a_up, tm_g_up, tm_k_k, tm_k_a, tm_r_k, tm_gn_g, tm_gn_b, lru_conv_w, lru_conv_b, lru_wa, lru_ba, lru_wx, lru_bx, lru_lambda, lru_out_g, w_out, norm2_g, ffn_w_up, ffn_w_gate, ffn_conv_w, ffn_conv_b, ffn_w_down, norm_f_g):
    depth = w_in.shape[0]
    assert depth == 1
    l = 0
    zeros_lora = jnp.zeros((DECAY_RANK, D_TM), F32)
    head_id = jnp.arange(D_TM) // TM_HEAD
    seg = (head_id[:, None] == jnp.arange(LANES)[None, :]).astype(BF16)
    mixer_w = (
        _row(norm1_g[l]), w_in[l].astype(BF16), _row(tm_mu[l]), _row(tm_w0[l]),
        jnp.concatenate([tm_w_up[l], zeros_lora], axis=0).astype(BF16),
        _row(tm_a0[l]),
        jnp.concatenate([zeros_lora, tm_a_up[l]], axis=0).astype(BF16),
        tm_g_up[l].astype(BF16),
        _row(tm_k_k[l]), _row(tm_k_a[l]), _row(tm_r_k[l]), _row(tm_gn_g[l]), _row(tm_gn_b[l]),
        lru_conv_w[l].astype(F32), _row(lru_conv_b[l]),
        _pair_block_diag(lru_wa[l]).astype(BF16), _row(lru_ba[l]),
        _pair_block_diag(lru_wx[l]).astype(BF16), _row(lru_bx[l]),
        _row(lru_lambda[l]), _row(lru_out_g[l]),
        seg, seg.T, w_out[l].astype(BF16),
    )
    ffn_w = (
        _row(norm2_g[l]), ffn_w_up[l].astype(BF16), ffn_w_gate[l].astype(BF16),
        ffn_conv_w[l].astype(F32), _row(ffn_conv_b[l]), ffn_w_down[l].astype(BF16), _row(norm_f_g),
    )

    bsz, seq = x_prompt.shape[0], x_prompt.shape[1]
    x_meta = jnp.broadcast_to(meta_tokens[None].astype(F32), (bsz, N_META, D_MODEL))
    x1_meta, *meta_st = _mixer(x_meta, jnp.zeros((bsz, D_TM_PROJ), F32),
                               jnp.zeros((bsz, PAIRS, TM_HEAD, LANES), F32),
                               jnp.zeros((LRU_CONV_W - 1, bsz, D_LRU), F32),
                               jnp.zeros((bsz, D_LRU), F32), mixer_w,
                               bb=bsz, chunk=N_META, pos0=0, group=4)
    p_init = tuple(meta_st) + (_ffn_state(x1_meta, ffn_w[0], ffn_w[1]),)

    y_prompt, p_st = _layer(x_prompt, p_init, mixer_w, ffn_w,
                            mixer_bb=bsz, ffn_bb=bsz, mixer_chunk=64, ffn_chunk=64,
                            pos0=N_META, group=4)

    dec_b, dec_seq = x_sample.shape[0], x_sample.shape[1]
    s_init = (state_tm_shift[l].astype(F32), _wkv_to_pairs(state_tm_wkv[l]),
              jnp.transpose(state_lru_conv[l].astype(F32), (1, 0, 2)), state_lru_h[l].astype(F32),
              jnp.transpose(state_ffn_conv[l].astype(F32), (1, 0, 2)))
    y_sample, s_st = _layer(x_sample, s_init, mixer_w, ffn_w,
                            mixer_bb=32, ffn_bb=64, mixer_chunk=dec_seq, ffn_chunk=dec_seq,
                            pos0=PAST_LEN, group=16)

    def unpack(st):
        o_shift, o_wkv, o_conv, o_h, o_fconv = st
        return (o_shift[None], _wkv_from_pairs(o_wkv)[None], jnp.transpose(o_conv, (1, 0, 2))[None],
                o_h[None], jnp.transpose(o_fconv, (1, 0, 2))[None])

    return (y_prompt, y_sample) + unpack(p_st) + unpack(s_st)
```

```python
import functools
import math

import jax
import jax.numpy as jnp
from jax import lax
from jax.experimental import pallas as pl
from jax.experimental.pallas import tpu as pltpu

F32 = jnp.float32
BF16 = jnp.bfloat16

D_MODEL = 1024
N_META = 16
PAST_LEN = 16384
D_TM = 512
TM_HEAD = 64
TM_HEADS = 8
DECAY_RANK = 64
AAA_RANK = 64
GATE_RANK = 128
D_TM_PROJ = 3 * D_TM + DECAY_RANK + AAA_RANK + GATE_RANK
D_LRU = 512
LRU_CONV_W = 4
LRU_C = 8.0
D_IN_PROJ = D_TM_PROJ + 2 * D_LRU
D_FF = 3 * D_MODEL
FFN_CONV_W = 3
EPS = 1e-6
GN_EPS = 64e-5
LOG2_E = math.log2(math.e)

SUBLANES = 8
LANES = 128
PAIRS = TM_HEADS // 2
LORA_OFF = 3 * D_TM
GATE_OFF = LORA_OFF + DECAY_RANK + AAA_RANK
FF_COL_TILE = 1024
VMEM_LIMIT_BYTES = 60 * 1024 * 1024


def _dot(a, b):
    return jnp.dot(a.astype(BF16), b.astype(BF16), preferred_element_type=F32)


def _bdot(a, b):
    return lax.dot_general(a.astype(BF16), b.astype(BF16), (((2,), (1,)), ((0,), (0,))),
                           preferred_element_type=F32)


def _bdot_nt(a, b):
    return lax.dot_general(a.astype(BF16), b.astype(BF16), (((2,), (2,)), ((0,), (0,))),
                           preferred_element_type=F32)


def _bdot_tn(a, b):
    return lax.dot_general(a.astype(BF16), b.astype(BF16), (((1,), (1,)), ((0,), (0,))),
                           preferred_element_type=F32)


def _rms(x, g):
    return x * lax.rsqrt(jnp.mean(x * x, axis=-1, keepdims=True) + EPS) * g


def _sigmoid(z):
    return 0.5 * jnp.tanh(0.5 * z) + 0.5


def _softplus(z):
    return jnp.maximum(z, 0.0) + jnp.log(1.0 + jnp.exp(-jnp.abs(z)))


def _block_diag_rows(x, left):
    return jnp.concatenate([jnp.where(left, x, 0.0), jnp.where(left, 0.0, x)], axis=1)


def _shift_time(carry, x, steps, bb):
    n = steps * bb
    return jnp.concatenate([carry[carry.shape[0] - n:], x[:x.shape[0] - n]], axis=0)


class _TimeMajorStream:
    def __init__(self, hbm, buf, sem, *, bb, chunk, n_blocks, n_chunks, to_hbm):
        self.hbm, self.buf, self.sem = hbm, buf, sem
        self.bb, self.chunk, self.to_hbm = bb, chunk, to_hbm
        self.n_blocks, self.n_chunks = n_blocks, n_chunks

    def _copy(self, g, slot, t):
        if self.n_chunks == 1:
            bi, ti = g, 0
        elif self.n_blocks == 1:
            bi, ti = 0, g
        else:
            bi, ti = lax.div(g, self.n_chunks), lax.rem(g, self.n_chunks)
        seqs = pl.ds(pl.multiple_of(bi * self.bb, SUBLANES), self.bb)
        hbm_rows = self.hbm.at[seqs, ti * self.chunk + t, :]
        tile = self.buf.at[slot, t]
        src, dst = (tile, hbm_rows) if self.to_hbm else (hbm_rows, tile)
        return pltpu.make_async_copy(src, dst, self.sem.at[slot])

    def start(self, g, slot):
        for t in range(self.chunk):
            self._copy(g, slot, t).start()

    def wait(self, g, slot):
        for t in range(self.chunk):
            self._copy(g, slot, t).wait()


READ_SLOTS = 3


def _read_chunk_begin(stream, g, n_steps):
    last = n_steps - 1

    @pl.when(g == 0)
    def _():
        stream.start(0, 0)
        stream.start(jnp.minimum(1, last), 1)

    slot = lax.rem(g, READ_SLOTS)
    stream.wait(g, slot)
    return slot


def _read_chunk_end(stream, g, n_steps):
    last = n_steps - 1
    stream.start(jnp.minimum(g + 2, last), lax.rem(g + 2, READ_SLOTS))

    @pl.when(g == last)
    def _():
        stream.wait(last, lax.rem(g + 1, READ_SLOTS))
        stream.wait(last, lax.rem(g + 2, READ_SLOTS))


def _mixer_kernel(x_hbm, st_shift_ref, st_wkv_ref, st_conv_ref, st_h_ref,
                  g1_ref, w_in_ref, mu_ref, w0_ref, wdec_ref, a0_ref, waaa_ref, wgate_ref,
                  kk_ref, ka_ref, rk_ref, gng_ref, gnb_ref,
                  cw_ref, cb_ref, wa_ref, ba_ref, wx_ref, bx_ref, lam_ref, og_ref,
                  seg_ref, segt_ref, w_out_ref,
                  x1_ref, o_shift_ref, o_wkv_ref, o_conv_ref, o_h_ref,
                  tm_carry, xb_carry, kap_s, rt_s, bt_s, kt_s, v_s, yt_s, pe_s, x_buf, x_sem,
                  *, bb, chunk, pos0, group, n_blocks, n_chunks):
    rows = bb * chunk
    conv_rows = (LRU_CONV_W - 1) * bb
    wkv_chunk = max(chunk, SUBLANES)
    ti = pl.program_id(1)
    step = pl.program_id(0) * n_chunks + ti
    n_steps = n_blocks * n_chunks
    x_stream = _TimeMajorStream(x_hbm, x_buf, x_sem, bb=bb, chunk=chunk, n_blocks=n_blocks,
                                n_chunks=n_chunks, to_hbm=False)
    x_slot = _read_chunk_begin(x_stream, step, n_steps)

    @pl.when(ti == 0)
    def _():
        tm_carry[...] = st_shift_ref[...]
        xb_carry[...] = st_conv_ref[...].reshape(conv_rows, D_LRU)
        o_wkv_ref[...] = st_wkv_ref[...]
        o_h_ref[...] = st_h_ref[...]
        if wkv_chunk > chunk:
            for ref in (kap_s, rt_s, bt_s, kt_s, v_s):
                ref[:, rows:wkv_chunk * bb, :] = jnp.zeros((PAIRS, wkv_chunk * bb - rows, LANES), F32)

    seg = seg_ref[...]
    segt = segt_ref[...]

    def head_sum(x):
        s = jnp.dot(x.astype(BF16), seg, preferred_element_type=F32)
        return jnp.dot(s.astype(BF16), segt, preferred_element_type=F32)

    x = x_buf[x_slot].reshape(rows, D_MODEL)
    u = _dot(_rms(x, g1_ref[...]), w_in_ref[...])
    u_tm = u[:, :D_TM_PROJ]
    xb = u[:, D_TM_PROJ:D_TM_PROJ + D_LRU]
    gate_lru = u[:, D_TM_PROJ + D_LRU:D_IN_PROJ]

    row = lax.broadcasted_iota(jnp.int32, (rows, D_TM), 0)

    um = u_tm + (_shift_time(tm_carry[...], u_tm, 1, bb) - u_tm) * mu_ref[...]
    r = um[:, 0:D_TM]
    k = um[:, D_TM:2 * D_TM]
    v = um[:, 2 * D_TM:3 * D_TM]
    x_lora = um[:, LORA_OFF:GATE_OFF]
    x_gate = um[:, GATE_OFF:D_TM_PROJ]
    log2_decay = (-math.exp(-0.5) * LOG2_E) * _sigmoid(w0_ref[...] + _dot(jnp.tanh(x_lora), wdec_ref[...]))
    a = _sigmoid(a0_ref[...] + _dot(x_lora, waaa_ref[...]))
    gate_tm = _dot(_sigmoid(x_gate), wgate_ref[...])
    kk = k * kk_ref[...]
    k = k * (a * ka_ref[...] + (1.0 - ka_ref[...]))
    kk = kk * lax.rsqrt(jnp.maximum(head_sum(kk * kk), 1e-24))
    bonus = head_sum(r * k * rk_ref[...]) * v
    acc = log2_decay[0:bb]
    cum = [acc]
    for t in range(1, chunk):
        acc = acc + log2_decay[t * bb:(t + 1) * bb]
        cum.append(acc)
    c = jnp.concatenate(cum, axis=0)
    inv_p = jnp.exp2(-c)
    kap = kk * jnp.exp2(c - log2_decay)
    rt = r * jnp.exp2(c)
    bt = kk * a * inv_p
    kt = k * inv_p
    p_end = jnp.exp2(cum[chunk - 1])
    for p in range(PAIRS):
        ls = slice(p * LANES, (p + 1) * LANES)
        kap_s[p, 0:rows, :] = kap[:, ls]
        rt_s[p, 0:rows, :] = rt[:, ls]
        bt_s[p, 0:rows, :] = bt[:, ls]
        kt_s[p, 0:rows, :] = kt[:, ls]
        v_s[p, 0:rows, :] = v[:, ls]
        pe_s[p] = p_end[:, ls]

    xb_prev = xb_carry[...]
    xc = cb_ref[...] + _shift_time(xb_prev, xb, 3, bb) * cw_ref[0:1, :]
    xc = xc + _shift_time(xb_prev, xb, 2, bb) * cw_ref[1:2, :]
    xc = xc + _shift_time(xb_prev, xb, 1, bb) * cw_ref[2:3, :]
    xc = xc + xb * cw_ref[3:4, :]
    xc_bf = xc.astype(BF16)

    def block_gate(w_ref):
        return jnp.concatenate(
            [jnp.dot(xc_bf[:, p * LANES:(p + 1) * LANES], w_ref[p], preferred_element_type=F32)
             for p in range(D_LRU // LANES)], axis=-1)

    r_g = _sigmoid(block_gate(wa_ref) + ba_ref[...])
    i_g = _sigmoid(block_gate(wx_ref) + bx_ref[...])
    la = jnp.exp2(r_g * ((-LRU_C * LOG2_E) * _softplus(-lam_ref[...])))
    gap = 1.0 - la * la
    mult = jnp.where(gap > 0.0, gap * lax.rsqrt(gap), 0.0)
    if pos0 == 0:
        mult = jnp.where(jnp.logical_and(row < bb, ti == 0), 1.0, mult)
    lb = xc * i_g * mult
    h = o_h_ref[...]
    hs = []
    for t in range(chunk):
        ts = slice(t * bb, (t + 1) * bb)
        h = la[ts] * h + lb[ts]
        hs.append(h)
    o_h_ref[...] = h
    y_lru = _rms(jnp.concatenate(hs, axis=0) * jax.nn.gelu(gate_lru), og_ref[...])

    wc = wkv_chunk
    n_lv = max(1, (chunk - 1).bit_length())
    cw2 = 2 * wc
    left = lax.broadcasted_iota(jnp.int32, (1, 1, LANES), 2) < TM_HEAD
    left_c = lax.broadcasted_iota(jnp.int32, (1, 1, cw2), 2) < wc
    ri = lax.broadcasted_iota(jnp.int32, (wc, cw2), 0)
    ci = lax.broadcasted_iota(jnp.int32, (wc, cw2), 1) & (wc - 1)
    strict = ri > ci
    incl = ri >= ci
    eye = (ri == ci).astype(F32)

    def seq_body(i, carry):
        seqs = [i * group + j for j in range(group)]
        tsel = [pl.ds(b, wc, stride=bb) for b in seqs]
        gather = lambda ref: jnp.stack([ref.at[p][ts, :] for ts in tsel for p in range(PAIRS)])
        kap_g = gather(kap_s)
        rt_g = gather(rt_s)
        bt_g = gather(bt_s)
        kt_g = gather(kt_s)
        vv = gather(v_s)
        pe = jnp.stack([pe_s.at[p][pl.ds(b, 1), :] for b in seqs for p in range(PAIRS)])
        s0 = jnp.concatenate([o_wkv_ref[b] for b in seqs], axis=0)
        lhs = jnp.concatenate([kap_g, rt_g], axis=1)
        g_b = _bdot_nt(lhs, _block_diag_rows(bt_g, left))
        g_k = _bdot_nt(lhs, _block_diag_rows(kt_g, left))
        m_ab = jnp.where(strict, g_b[:, :wc], 0.0)
        m_ak = jnp.where(strict, g_k[:, :wc], 0.0)
        m_rb = jnp.where(incl, g_b[:, wc:], 0.0)
        m_rk = jnp.where(incl, g_k[:, wc:], 0.0)
        t_inv = eye - m_ab
        m_pow = _bdot(m_ab, _block_diag_rows(m_ab, left_c))
        for lv in range(1, n_lv):
            if lv < n_lv - 1:
                prod = _bdot(jnp.concatenate([t_inv, m_pow], axis=1), _block_diag_rows(m_pow, left_c))
                t_inv = t_inv + prod[:, :wc]
                m_pow = prod[:, wc:]
            else:
                t_inv = t_inv + _bdot(t_inv, _block_diag_rows(m_pow, left_c))
        xv = _bdot(jnp.concatenate([m_ak, m_rk], axis=1), _block_diag_rows(vv, left))
        z = _bdot_nt(lhs, _block_diag_rows(s0, left))
        uu = -_bdot(t_inv, _block_diag_rows(z[:, :wc] + xv[:, :wc], left))
        yh = z[:, wc:] + _bdot(m_rb, _block_diag_rows(uu, left)) + xv[:, wc:]
        full = _bdot_tn(jnp.concatenate([uu, vv], axis=1), jnp.concatenate([bt_g, kt_g], axis=1))
        upd = jnp.where(left, full[:, :TM_HEAD], full[:, TM_HEAD:])
        s_new = (s0 + upd) * pe
        for j, (b, ts) in enumerate(zip(seqs, tsel)):
            o_wkv_ref[b] = s_new[j * PAIRS:(j + 1) * PAIRS]
            for p in range(PAIRS):
                yt_s.at[p][ts, :] = yh[j * PAIRS + p]
        return carry

    n_iter = bb // group
    if n_iter <= 2:
        for i in range(n_iter):
            seq_body(i, 0)
    else:
        lax.fori_loop(0, n_iter, seq_body, 0)

    yv = jnp.concatenate([yt_s[p, 0:rows, :] for p in range(PAIRS)], axis=-1)
    mean = head_sum(yv) * (1.0 / TM_HEAD)
    cen = yv - mean
    var = head_sum(cen * cen) * (1.0 / TM_HEAD)
    y_tm = ((cen * lax.rsqrt(var + GN_EPS)) * gng_ref[...] + gnb_ref[...] + bonus) * gate_tm
    mixed = jnp.concatenate([y_tm.astype(BF16), y_lru.astype(BF16)], axis=-1)
    x1 = x_buf[x_slot].reshape(rows, D_MODEL) + jnp.dot(mixed, w_out_ref[...], preferred_element_type=F32)
    x1_ref[...] = x1.reshape(chunk, bb, D_MODEL)

    o_shift_ref[...] = u_tm[rows - bb:]
    o_conv_ref[...] = xb[rows - conv_rows:].reshape(LRU_CONV_W - 1, bb, D_LRU)
    tm_carry[...] = u_tm[rows - bb:]
    xb_carry[...] = xb[rows - conv_rows:]
    _read_chunk_end(x_stream, step, n_steps)


def _const_spec(shape):
    zeros = (0,) * len(shape)
    return pl.BlockSpec(shape, lambda bi, ti: zeros, pipeline_mode=pl.Buffered(1))


def _mixer(x, st_shift, st_wkv, st_conv, st_h, weights, *, bb, chunk, pos0, group):
    batch, length, _ = x.shape
    assert batch % bb == 0 and length % chunk == 0 and bb % SUBLANES == 0 and bb % group == 0
    assert chunk & (chunk - 1) == 0 and chunk >= LRU_CONV_W - 1
    rows = bb * chunk
    grid = (batch // bb, length // chunk)
    act = pl.BlockSpec((chunk, bb, D_MODEL), lambda bi, ti: (ti, bi, 0))
    vec = lambda w: pl.BlockSpec((bb, w), lambda bi, ti: (bi, 0))
    wkv_spec = pl.BlockSpec((bb, PAIRS, TM_HEAD, LANES), lambda bi, ti: (bi, 0, 0, 0))
    conv_spec = pl.BlockSpec((LRU_CONV_W - 1, bb, D_LRU), lambda bi, ti: (0, bi, 0))
    in_specs = [pl.BlockSpec(memory_space=pl.ANY), vec(D_TM_PROJ), wkv_spec, conv_spec, vec(D_LRU)]
    in_specs += [_const_spec(w.shape) for w in weights]
    out_specs = [act, vec(D_TM_PROJ), wkv_spec, conv_spec, vec(D_LRU)]
    out_shape = [jax.ShapeDtypeStruct((length, batch, D_MODEL), F32),
                 jax.ShapeDtypeStruct((batch, D_TM_PROJ), F32),
                 jax.ShapeDtypeStruct((batch, PAIRS, TM_HEAD, LANES), F32),
                 jax.ShapeDtypeStruct((LRU_CONV_W - 1, batch, D_LRU), F32),
                 jax.ShapeDtypeStruct((batch, D_LRU), F32)]
    pair_rows = lambda n: pltpu.VMEM((PAIRS, n, LANES), F32)
    scratch = [pltpu.VMEM((bb, D_TM_PROJ), F32),
               pltpu.VMEM(((LRU_CONV_W - 1) * bb, D_LRU), F32)]
    scratch += [pair_rows(max(chunk, SUBLANES) * bb)] * 6 + [pair_rows(bb)]
    scratch += [pltpu.VMEM((READ_SLOTS, chunk, bb, D_MODEL), F32), pltpu.SemaphoreType.DMA((READ_SLOTS,))]
    return pl.pallas_call(
        functools.partial(_mixer_kernel, bb=bb, chunk=chunk, pos0=pos0, group=group,
                          n_blocks=grid[0], n_chunks=grid[1]),
        out_shape=out_shape, grid=grid, in_specs=in_specs, out_specs=out_specs,
        scratch_shapes=scratch, name="mixer",
        compiler_params=pltpu.CompilerParams(dimension_semantics=("arbitrary", "arbitrary"),
                                             vmem_limit_bytes=VMEM_LIMIT_BYTES),
    )(x, st_shift, st_wkv, st_conv, st_h, *weights)


def _ffn_kernel(x1_ref, st_conv_ref,
                g2_ref, w_up_ref, w_gate_ref, cw_ref, cb_ref, w_down_ref, gf_ref,
                y_hbm, o_conv_ref,
                up_carry, y_buf, y_sem,
                *, bb, chunk, n_blocks, n_chunks):
    rows = bb * chunk
    conv_rows = (FFN_CONV_W - 1) * bb
    ti = pl.program_id(1)
    step = pl.program_id(0) * n_chunks + ti
    n_steps = n_blocks * n_chunks
    y_stream = _TimeMajorStream(y_hbm, y_buf, y_sem, bb=bb, chunk=chunk, n_blocks=n_blocks,
                                n_chunks=n_chunks, to_hbm=True)
    slot = lax.rem(step, 2)

    @pl.when(ti == 0)
    def _():
        up_carry[...] = st_conv_ref[...].reshape(conv_rows, D_FF)

    x1 = x1_ref[...].reshape(rows, D_MODEL)
    xn = _rms(x1, g2_ref[...]).astype(BF16)
    acc = x1
    for n in range(D_FF // FF_COL_TILE):
        cols = slice(n * FF_COL_TILE, (n + 1) * FF_COL_TILE)
        up = jnp.dot(xn, w_up_ref[:, cols], preferred_element_type=F32)
        gate = jnp.dot(xn, w_gate_ref[:, cols], preferred_element_type=F32)
        up_prev = up_carry[:, cols]
        upc = cb_ref[:, cols] + _shift_time(up_prev, up, 2, bb) * cw_ref[0:1, cols]
        upc = upc + _shift_time(up_prev, up, 1, bb) * cw_ref[1:2, cols]
        upc = upc + up * cw_ref[2:3, cols]
        up_carry[:, cols] = up[rows - conv_rows:]
        o_conv_ref[:, :, cols] = up[rows - conv_rows:].reshape(FFN_CONV_W - 1, bb, FF_COL_TILE)
        hid = (jax.nn.gelu(upc) * gate).astype(BF16)
        acc = acc + jnp.dot(hid, w_down_ref[cols, :], preferred_element_type=F32)

    y_buf[slot] = _rms(acc, gf_ref[...]).reshape(chunk, bb, D_MODEL)
    y_stream.start(step, slot)

    @pl.when(step > 0)
    def _():
        y_stream.wait(step - 1, 1 - slot)

    @pl.when(step == n_steps - 1)
    def _():
        y_stream.wait(step, slot)


def _ffn(x1, st_conv, weights, *, bb, chunk):
    length, batch, _ = x1.shape
    assert batch % bb == 0 and length % chunk == 0 and bb % SUBLANES == 0
    assert chunk >= FFN_CONV_W - 1
    rows = bb * chunk
    grid = (batch // bb, length // chunk)
    act = pl.BlockSpec((chunk, bb, D_MODEL), lambda bi, ti: (ti, bi, 0))
    conv_spec = pl.BlockSpec((FFN_CONV_W - 1, bb, D_FF), lambda bi, ti: (0, bi, 0))
    in_specs = [act, conv_spec] + [_const_spec(w.shape) for w in weights]
    out_shape = [jax.ShapeDtypeStruct((batch, length, D_MODEL), F32),
                 jax.ShapeDtypeStruct((FFN_CONV_W - 1, batch, D_FF), F32)]
    scratch = [pltpu.VMEM(((FFN_CONV_W - 1) * bb, D_FF), F32),
               pltpu.VMEM((2, chunk, bb, D_MODEL), F32), pltpu.SemaphoreType.DMA((2,))]
    return pl.pallas_call(
        functools.partial(_ffn_kernel, bb=bb, chunk=chunk, n_blocks=grid[0], n_chunks=grid[1]),
        out_shape=out_shape, grid=grid, in_specs=in_specs,
        out_specs=[pl.BlockSpec(memory_space=pl.ANY), conv_spec],
        scratch_shapes=scratch, name="ffn",
        compiler_params=pltpu.CompilerParams(dimension_semantics=("arbitrary", "arbitrary"),
                                             vmem_limit_bytes=VMEM_LIMIT_BYTES),
    )(x1, st_conv, *weights)


def _ffn_state_kernel(x1_ref, g2_ref, w_up_ref, o_conv_ref, *, bb):
    tail = FFN_CONV_W - 1
    x1 = x1_ref[...].reshape(tail * bb, D_MODEL)
    up = _dot(_rms(x1, g2_ref[...]), w_up_ref[...])
    o_conv_ref[...] = up.reshape(tail, bb, D_FF)


def _ffn_state(x1, g2, w_up):
    length, batch, _ = x1.shape
    tail = FFN_CONV_W - 1
    assert length % tail == 0 and batch % SUBLANES == 0
    return pl.pallas_call(
        functools.partial(_ffn_state_kernel, bb=batch),
        out_shape=jax.ShapeDtypeStruct((tail, batch, D_FF), F32), grid=(1,),
        in_specs=[pl.BlockSpec((tail, batch, D_MODEL), lambda i: (length // tail - 1, 0, 0)),
                  pl.BlockSpec(g2.shape, lambda i: (0, 0)), pl.BlockSpec(w_up.shape, lambda i: (0, 0))],
        out_specs=pl.BlockSpec((tail, batch, D_FF), lambda i: (0, 0, 0)), name="ffn_state",
        compiler_params=pltpu.CompilerParams(dimension_semantics=("arbitrary",),
                                             vmem_limit_bytes=VMEM_LIMIT_BYTES),
    )(x1, g2, w_up)


def _row(v):
    return v.reshape(1, -1).astype(F32)


def _pair_block_diag(w):
    nb, n, _ = w.shape
    w = w.reshape(nb // 2, 2, n, n)
    eye = jnp.eye(2, dtype=w.dtype)
    return (eye[None, :, None, :, None] * w[:, :, :, None, :]).reshape(nb // 2, 2 * n, 2 * n)


def _wkv_to_pairs(s):
    b = s.shape[0]
    s = s.astype(F32).reshape(b, PAIRS, 2, TM_HEAD, TM_HEAD)
    return jnp.transpose(s, (0, 1, 3, 2, 4)).reshape(b, PAIRS, TM_HEAD, LANES)


def _wkv_from_pairs(s):
    b = s.shape[0]
    s = s.reshape(b, PAIRS, TM_HEAD, 2, TM_HEAD)
    return jnp.transpose(s, (0, 1, 3, 2, 4)).reshape(b, TM_HEADS, TM_HEAD, TM_HEAD)


def _layer(x, states, mixer_w, ffn_w, *, mixer_bb, ffn_bb, mixer_chunk, ffn_chunk, pos0, group):
    st_shift, st_wkv, st_conv, st_h, st_fconv = states
    x1, o_shift, o_wkv, o_conv, o_h = _mixer(x, st_shift, st_wkv, st_conv, st_h, mixer_w,
                                             bb=mixer_bb, chunk=mixer_chunk, pos0=pos0, group=group)
    y, o_fconv = _ffn(x1, st_fconv, ffn_w, bb=ffn_bb, chunk=ffn_chunk)
    return y, (o_shift, o_wkv, o_conv, o_h, o_fconv)


def kernel(x_prompt, x_sample, state_tm_shift, state_tm_wkv, state_lru_conv, state_lru_h, state_ffn_conv, meta_tokens, norm1_g, w_in, tm_mu, tm_w0, tm_w_up, tm_a0, tm_a_up, tm_g_up, tm_k_k, tm_k_a, tm_r_k, tm_gn_g, tm_gn_b, lru_conv_w, lru_conv_b, lru_wa, lru_ba, lru_wx, lru_bx, lru_lambda, lru_out_g, w_out, norm2_g, ffn_w_up, ffn_w_gate, ffn_conv_w, ffn_conv_b, ffn_w_down, norm_f_g):
    depth = w_in.shape[0]
    assert depth == 1
    l = 0
    zeros_lora = jnp.zeros((DECAY_RANK, D_TM), F32)
    head_id = jnp.arange(D_TM) // TM_HEAD
    seg = (head_id[:, None] == jnp.arange(LANES)[None, :]).astype(BF16)
    mixer_w = (
        _row(norm1_g[l]), w_in[l].astype(BF16), _row(tm_mu[l]), _row(tm_w0[l]),
        jnp.concatenate([tm_w_up[l], zeros_lora], axis=0).astype(BF16),
        _row(tm_a0[l]),
        jnp.concatenate([zeros_lora, tm_a_up[l]], axis=0).astype(BF16),
        tm_g_up[l].astype(BF16),
        _row(tm_k_k[l]), _row(tm_k_a[l]), _row(tm_r_k[l]), _row(tm_gn_g[l]), _row(tm_gn_b[l]),
        lru_conv_w[l].astype(F32), _row(lru_conv_b[l]),
        _pair_block_diag(lru_wa[l]).astype(BF16), _row(lru_ba[l]),
        _pair_block_diag(lru_wx[l]).astype(BF16), _row(lru_bx[l]),
        _row(lru_lambda[l]), _row(lru_out_g[l]),
        seg, seg.T, w_out[l].astype(BF16),
    )
    ffn_w = (
        _row(norm2_g[l]), ffn_w_up[l].astype(BF16), ffn_w_gate[l].astype(BF16),
        ffn_conv_w[l].astype(F32), _row(ffn_conv_b[l]), ffn_w_down[l].astype(BF16), _row(norm_f_g),
    )

    bsz, seq = x_prompt.shape[0], x_prompt.shape[1]
    x_meta = jnp.broadcast_to(meta_tokens[None].astype(F32), (bsz, N_META, D_MODEL))
    x1_meta, *meta_st = _mixer(x_meta, jnp.zeros((bsz, D_TM_PROJ), F32),
                               jnp.zeros((bsz, PAIRS, TM_HEAD, LANES), F32),
                               jnp.zeros((LRU_CONV_W - 1, bsz, D_LRU), F32),
                               jnp.zeros((bsz, D_LRU), F32), mixer_w,
                               bb=bsz, chunk=N_META, pos0=0, group=4)
    p_init = tuple(meta_st) + (_ffn_state(x1_meta, ffn_w[0], ffn_w[1]),)

    y_prompt, p_st = _layer(x_prompt, p_init, mixer_w, ffn_w,
                            mixer_bb=bsz, ffn_bb=bsz, mixer_chunk=64, ffn_chunk=64,
                            pos0=N_META, group=4)

    dec_b, dec_seq = x_sample.shape[0], x_sample.shape[1]
    s_init = (state_tm_shift[l].astype(F32), _wkv_to_pairs(state_tm_wkv[l]),
              jnp.transpose(state_lru_conv[l].astype(F32), (1, 0, 2)), state_lru_h[l].astype(F32),
              jnp.transpose(state_ffn_conv[l].astype(F32), (1, 0, 2)))
    y_sample, s_st = _layer(x_sample, s_init, mixer_w, ffn_w,
                            mixer_bb=32, ffn_bb=64, mixer_chunk=dec_seq, ffn_chunk=dec_seq,
                            pos0=PAST_LEN, group=16)

    def unpack(st):
        o_shift, o_wkv, o_conv, o_h, o_fconv = st
        return (o_shift[None], _wkv_from_pairs(o_wkv)[None], jnp.transpose(o_conv, (1, 0, 2))[None],
                o_h[None], jnp.transpose(o_fconv, (1, 0, 2))[None])

    return (y_prompt, y_sample) + unpack(p_st) + unpack(s_st)
```

```python
import functools
import math

import jax
import jax.numpy as jnp
from jax import lax
from jax.experimental import pallas as pl
from jax.experimental.pallas import tpu as pltpu

F32 = jnp.float32
BF16 = jnp.bfloat16

D_MODEL = 1024
N_META = 16
PAST_LEN = 16384
D_TM = 512
TM_HEAD = 64
TM_HEADS = 8
DECAY_RANK = 64
AAA_RANK = 64
GATE_RANK = 128
D_TM_PROJ = 3 * D_TM + DECAY_RANK + AAA_RANK + GATE_RANK
D_LRU = 512
LRU_CONV_W = 4
LRU_C = 8.0
D_IN_PROJ = D_TM_PROJ + 2 * D_LRU
D_FF = 3 * D_MODEL
FFN_CONV_W = 3
EPS = 1e-6
GN_EPS = 64e-5
LOG2_E = math.log2(math.e)

SUBLANES = 8
LANES = 128
PAIRS = TM_HEADS // 2
LORA_OFF = 3 * D_TM
GATE_OFF = LORA_OFF + DECAY_RANK + AAA_RANK
FF_COL_TILE = 1024
VMEM_LIMIT_BYTES = 60 * 1024 * 1024


def _dot(a, b):
    return jnp.dot(a.astype(BF16), b.astype(BF16), preferred_element_type=F32)


def _bdot(a, b):
    return lax.dot_general(a.astype(BF16), b.astype(BF16), (((2,), (1,)), ((0,), (0,))),
                           preferred_element_type=F32)


def _bdot_nt(a, b):
    return lax.dot_general(a.astype(BF16), b.astype(BF16), (((2,), (2,)), ((0,), (0,))),
                           preferred_element_type=F32)


def _bdot_tn(a, b):
    return lax.dot_general(a.astype(BF16), b.astype(BF16), (((1,), (1,)), ((0,), (0,))),
                           preferred_element_type=F32)


def _rms(x, g):
    return x * lax.rsqrt(jnp.mean(x * x, axis=-1, keepdims=True) + EPS) * g


def _sigmoid(z):
    return 0.5 * jnp.tanh(0.5 * z) + 0.5


def _softplus(z):
    return jnp.maximum(z, 0.0) + jnp.log(1.0 + jnp.exp(-jnp.abs(z)))


def _block_diag_rows(x, left):
    return jnp.concatenate([jnp.where(left, x, 0.0), jnp.where(left, 0.0, x)], axis=1)


def _shift_time(carry, x, steps, bb):
    n = steps * bb
    return jnp.concatenate([carry[carry.shape[0] - n:], x[:x.shape[0] - n]], axis=0)


class _TimeMajorStream:
    def __init__(self, hbm, buf, sem, *, bb, chunk, n_blocks, n_chunks, to_hbm):
        self.hbm, self.buf, self.sem = hbm, buf, sem
        self.bb, self.chunk, self.to_hbm = bb, chunk, to_hbm
        self.n_blocks, self.n_chunks = n_blocks, n_chunks

    def _copy(self, g, slot, t):
        if self.n_chunks == 1:
            bi, ti = g, 0
        elif self.n_blocks == 1:
            bi, ti = 0, g
        else:
            bi, ti = lax.div(g, self.n_chunks), lax.rem(g, self.n_chunks)
        seqs = pl.ds(pl.multiple_of(bi * self.bb, SUBLANES), self.bb)
        hbm_rows = self.hbm.at[seqs, ti * self.chunk + t, :]
        tile = self.buf.at[slot, t]
        src, dst = (tile, hbm_rows) if self.to_hbm else (hbm_rows, tile)
        return pltpu.make_async_copy(src, dst, self.sem.at[slot])

    def start(self, g, slot):
        for t in range(self.chunk):
            self._copy(g, slot, t).start()

    def wait(self, g, slot):
        for t in range(self.chunk):
            self._copy(g, slot, t).wait()


READ_SLOTS = 3


def _read_chunk_begin(stream, g, n_steps):
    last = n_steps - 1

    @pl.when(g == 0)
    def _():
        stream.start(0, 0)
        stream.start(jnp.minimum(1, last), 1)

    slot = lax.rem(g, READ_SLOTS)
    stream.wait(g, slot)
    return slot


def _read_chunk_end(stream, g, n_steps):
    last = n_steps - 1
    stream.start(jnp.minimum(g + 2, last), lax.rem(g + 2, READ_SLOTS))

    @pl.when(g == last)
    def _():
        stream.wait(last, lax.rem(g + 1, READ_SLOTS))
        stream.wait(last, lax.rem(g + 2, READ_SLOTS))


def _mixer_kernel(x_hbm, st_shift_ref, st_wkv_ref, st_conv_ref, st_h_ref,
                  g1_ref, w_in_ref, mu_ref, w0_ref, wdec_ref, a0_ref, waaa_ref, wgate_ref,
                  kk_ref, ka_ref, rk_ref, gng_ref, gnb_ref,
                  cw_ref, cb_ref, wgates_ref, ba_ref, bx_ref, lam_ref, og_ref,
                  seg_ref, segt_ref, w_out_ref,
                  x1_ref, o_shift_ref, o_wkv_ref, o_conv_ref, o_h_ref,
                  tm_carry, xb_carry, kap_s, rt_s, bt_s, kt_s, v_s, yt_s, pe_s, x_buf, x_sem,
                  *, bb, chunk, pos0, group, n_blocks, n_chunks):
    rows = bb * chunk
    conv_rows = (LRU_CONV_W - 1) * bb
    wkv_chunk = max(chunk, SUBLANES)
    ti = pl.program_id(1)
    step = pl.program_id(0) * n_chunks + ti
    n_steps = n_blocks * n_chunks
    x_stream = _TimeMajorStream(x_hbm, x_buf, x_sem, bb=bb, chunk=chunk, n_blocks=n_blocks,
                                n_chunks=n_chunks, to_hbm=False)
    x_slot = _read_chunk_begin(x_stream, step, n_steps)

    @pl.when(ti == 0)
    def _():
        tm_carry[...] = st_shift_ref[...]
        xb_carry[...] = st_conv_ref[...].reshape(conv_rows, D_LRU)
        o_wkv_ref[...] = st_wkv_ref[...]
        o_h_ref[...] = st_h_ref[...]
        if wkv_chunk > chunk:
            for ref in (kap_s, rt_s, bt_s, kt_s, v_s):
                ref[:, rows:wkv_chunk * bb, :] = jnp.zeros((PAIRS, wkv_chunk * bb - rows, LANES), F32)

    seg = seg_ref[...]
    segt = segt_ref[...]

    def head_sum(x):
        s = jnp.dot(x.astype(BF16), seg, preferred_element_type=F32)
        return jnp.dot(s.astype(BF16), segt, preferred_element_type=F32)

    x = x_buf[x_slot].reshape(rows, D_MODEL)
    u = _dot(_rms(x, g1_ref[...]), w_in_ref[...])
    u_tm = u[:, :D_TM_PROJ]
    xb = u[:, D_TM_PROJ:D_TM_PROJ + D_LRU]
    gate_lru = u[:, D_TM_PROJ + D_LRU:D_IN_PROJ]

    row = lax.broadcasted_iota(jnp.int32, (rows, D_TM), 0)

    um = u_tm + (_shift_time(tm_carry[...], u_tm, 1, bb) - u_tm) * mu_ref[...]
    r = um[:, 0:D_TM]
    k = um[:, D_TM:2 * D_TM]
    v = um[:, 2 * D_TM:3 * D_TM]
    x_lora = um[:, LORA_OFF:GATE_OFF]
    x_gate = um[:, GATE_OFF:D_TM_PROJ]
    log2_decay = (-math.exp(-0.5) * LOG2_E) * _sigmoid(w0_ref[...] + _dot(jnp.tanh(x_lora), wdec_ref[...]))
    a = _sigmoid(a0_ref[...] + _dot(x_lora, waaa_ref[...]))
    gate_tm = _dot(_sigmoid(x_gate), wgate_ref[...])
    kk = k * kk_ref[...]
    k = k * (a * ka_ref[...] + (1.0 - ka_ref[...]))
    kk = kk * lax.rsqrt(jnp.maximum(head_sum(kk * kk), 1e-24))
    bonus = head_sum(r * k * rk_ref[...]) * v
    acc = log2_decay[0:bb]
    cum = [acc]
    for t in range(1, chunk):
        acc = acc + log2_decay[t * bb:(t + 1) * bb]
        cum.append(acc)
    c = jnp.concatenate(cum, axis=0)
    inv_p = jnp.exp2(-c)
    kap = kk * jnp.exp2(c - log2_decay)
    rt = r * jnp.exp2(c)
    bt = kk * a * inv_p
    kt = k * inv_p
    p_end = jnp.exp2(cum[chunk - 1])
    for p in range(PAIRS):
        ls = slice(p * LANES, (p + 1) * LANES)
        kap_s[p, 0:rows, :] = kap[:, ls]
        rt_s[p, 0:rows, :] = rt[:, ls]
        bt_s[p, 0:rows, :] = bt[:, ls]
        kt_s[p, 0:rows, :] = kt[:, ls]
        v_s[p, 0:rows, :] = v[:, ls]
        pe_s[p] = p_end[:, ls]

    xb_prev = xb_carry[...]
    xc = cb_ref[...] + _shift_time(xb_prev, xb, 3, bb) * cw_ref[0:1, :]
    xc = xc + _shift_time(xb_prev, xb, 2, bb) * cw_ref[1:2, :]
    xc = xc + _shift_time(xb_prev, xb, 1, bb) * cw_ref[2:3, :]
    xc = xc + xb * cw_ref[3:4, :]
    xc_bf = xc.astype(BF16)

    gates = [jnp.dot(xc_bf[:, p * LANES:(p + 1) * LANES], wgates_ref[p], preferred_element_type=F32)
             for p in range(D_LRU // LANES)]
    r_g = _sigmoid(jnp.concatenate([g[:, :LANES] for g in gates], axis=-1) + ba_ref[...])
    i_g = _sigmoid(jnp.concatenate([g[:, LANES:] for g in gates], axis=-1) + bx_ref[...])
    la = jnp.exp2(r_g * ((-LRU_C * LOG2_E) * _softplus(-lam_ref[...])))
    gap = 1.0 - la * la
    mult = jnp.where(gap > 0.0, gap * lax.rsqrt(gap), 0.0)
    if pos0 == 0:
        mult = jnp.where(jnp.logical_and(row < bb, ti == 0), 1.0, mult)
    lb = xc * i_g * mult
    h = o_h_ref[...]
    hs = []
    for t in range(chunk):
        ts = slice(t * bb, (t + 1) * bb)
        h = la[ts] * h + lb[ts]
        hs.append(h)
    o_h_ref[...] = h
    y_lru = _rms(jnp.concatenate(hs, axis=0) * jax.nn.gelu(gate_lru), og_ref[...])

    wc = wkv_chunk
    n_lv = max(1, (chunk - 1).bit_length())
    cw2 = 2 * wc
    left = lax.broadcasted_iota(jnp.int32, (1, 1, LANES), 2) < TM_HEAD
    left_c = lax.broadcasted_iota(jnp.int32, (1, 1, cw2), 2) < wc
    ri = lax.broadcasted_iota(jnp.int32, (wc, cw2), 0)
    ci = lax.broadcasted_iota(jnp.int32, (wc, cw2), 1) & (wc - 1)
    strict = ri > ci
    incl = ri >= ci
    eye = (ri == ci).astype(F32)

    def seq_body(i, carry):
        seqs = [i * group + j for j in range(group)]
        tsel = [pl.ds(b, wc, stride=bb) for b in seqs]
        gather = lambda ref: jnp.stack([ref.at[p][ts, :] for ts in tsel for p in range(PAIRS)])
        kap_g = gather(kap_s)
        rt_g = gather(rt_s)
        bt_g = gather(bt_s)
        kt_g = gather(kt_s)
        vv = gather(v_s)
        pe = jnp.stack([pe_s.at[p][pl.ds(b, 1), :] for b in seqs for p in range(PAIRS)])
        s0 = jnp.concatenate([o_wkv_ref[b] for b in seqs], axis=0)
        lhs = jnp.concatenate([kap_g, rt_g], axis=1)
        fused = _bdot_nt(lhs, jnp.concatenate([_block_diag_rows(bt_g, left), _block_diag_rows(kt_g, left),
                                               _block_diag_rows(s0, left)], axis=1))
        g_b = fused[:, :, 0:cw2]
        g_k = fused[:, :, cw2:2 * cw2]
        z = fused[:, :, 2 * cw2:]
        m_ab = jnp.where(strict, g_b[:, :wc], 0.0)
        m_ak = jnp.where(strict, g_k[:, :wc], 0.0)
        m_rb = jnp.where(incl, g_b[:, wc:], 0.0)
        m_rk = jnp.where(incl, g_k[:, wc:], 0.0)
        t_inv = eye - m_ab
        m_pow = _bdot(m_ab, _block_diag_rows(m_ab, left_c))
        for lv in range(1, n_lv):
            if lv < n_lv - 1:
                prod = _bdot(jnp.concatenate([t_inv, m_pow], axis=1), _block_diag_rows(m_pow, left_c))
                t_inv = t_inv + prod[:, :wc]
                m_pow = prod[:, wc:]
            else:
                t_inv = t_inv + _bdot(t_inv, _block_diag_rows(m_pow, left_c))
        xv = _bdot(jnp.concatenate([m_ak, m_rk], axis=1), _block_diag_rows(vv, left))
        uu = -_bdot(t_inv, _block_diag_rows(z[:, :wc] + xv[:, :wc], left))
        yh = z[:, wc:] + _bdot(m_rb, _block_diag_rows(uu, left)) + xv[:, wc:]
        full = _bdot_tn(jnp.concatenate([uu, vv], axis=1), jnp.concatenate([bt_g, kt_g], axis=1))
        upd = jnp.where(left, full[:, :TM_HEAD], full[:, TM_HEAD:])
        s_new = (s0 + upd) * pe
        for j, (b, ts) in enumerate(zip(seqs, tsel)):
            o_wkv_ref[b] = s_new[j * PAIRS:(j + 1) * PAIRS]
            for p in range(PAIRS):
                yt_s.at[p][ts, :] = yh[j * PAIRS + p]
        return carry

    n_iter = bb // group
    if n_iter <= 2:
        for i in range(n_iter):
            seq_body(i, 0)
    else:
        lax.fori_loop(0, n_iter, seq_body, 0)

    yv = jnp.concatenate([yt_s[p, 0:rows, :] for p in range(PAIRS)], axis=-1)
    mean = head_sum(yv) * (1.0 / TM_HEAD)
    cen = yv - mean
    var = head_sum(cen * cen) * (1.0 / TM_HEAD)
    y_tm = ((cen * lax.rsqrt(var + GN_EPS)) * gng_ref[...] + gnb_ref[...] + bonus) * gate_tm
    mixed = jnp.concatenate([y_tm.astype(BF16), y_lru.astype(BF16)], axis=-1)
    x1 = x_buf[x_slot].reshape(rows, D_MODEL) + jnp.dot(mixed, w_out_ref[...], preferred_element_type=F32)
    x1_ref[...] = x1.reshape(chunk, bb, D_MODEL)

    o_shift_ref[...] = u_tm[rows - bb:]
    o_conv_ref[...] = xb[rows - conv_rows:].reshape(LRU_CONV_W - 1, bb, D_LRU)
    tm_carry[...] = u_tm[rows - bb:]
    xb_carry[...] = xb[rows - conv_rows:]
    _read_chunk_end(x_stream, step, n_steps)


def _const_spec(shape):
    zeros = (0,) * len(shape)
    return pl.BlockSpec(shape, lambda bi, ti: zeros, pipeline_mode=pl.Buffered(1))


def _mixer(x, st_shift, st_wkv, st_conv, st_h, weights, *, bb, chunk, pos0, group):
    batch, length, _ = x.shape
    assert batch % bb == 0 and length % chunk == 0 and bb % SUBLANES == 0 and bb % group == 0
    assert chunk & (chunk - 1) == 0 and chunk >= LRU_CONV_W - 1
    rows = bb * chunk
    grid = (batch // bb, length // chunk)
    act = pl.BlockSpec((chunk, bb, D_MODEL), lambda bi, ti: (ti, bi, 0))
    vec = lambda w: pl.BlockSpec((bb, w), lambda bi, ti: (bi, 0))
    wkv_spec = pl.BlockSpec((bb, PAIRS, TM_HEAD, LANES), lambda bi, ti: (bi, 0, 0, 0))
    conv_spec = pl.BlockSpec((LRU_CONV_W - 1, bb, D_LRU), lambda bi, ti: (0, bi, 0))
    in_specs = [pl.BlockSpec(memory_space=pl.ANY), vec(D_TM_PROJ), wkv_spec, conv_spec, vec(D_LRU)]
    in_specs += [_const_spec(w.shape) for w in weights]
    out_specs = [act, vec(D_TM_PROJ), wkv_spec, conv_spec, vec(D_LRU)]
    out_shape = [jax.ShapeDtypeStruct((length, batch, D_MODEL), F32),
                 jax.ShapeDtypeStruct((batch, D_TM_PROJ), F32),
                 jax.ShapeDtypeStruct((batch, PAIRS, TM_HEAD, LANES), F32),
                 jax.ShapeDtypeStruct((LRU_CONV_W - 1, batch, D_LRU), F32),
                 jax.ShapeDtypeStruct((batch, D_LRU), F32)]
    pair_rows = lambda n: pltpu.VMEM((PAIRS, n, LANES), F32)
    scratch = [pltpu.VMEM((bb, D_TM_PROJ), F32),
               pltpu.VMEM(((LRU_CONV_W - 1) * bb, D_LRU), F32)]
    scratch += [pair_rows(max(chunk, SUBLANES) * bb)] * 6 + [pair_rows(bb)]
    scratch += [pltpu.VMEM((READ_SLOTS, chunk, bb, D_MODEL), F32), pltpu.SemaphoreType.DMA((READ_SLOTS,))]
    return pl.pallas_call(
        functools.partial(_mixer_kernel, bb=bb, chunk=chunk, pos0=pos0, group=group,
                          n_blocks=grid[0], n_chunks=grid[1]),
        out_shape=out_shape, grid=grid, in_specs=in_specs, out_specs=out_specs,
        scratch_shapes=scratch, name="mixer",
        compiler_params=pltpu.CompilerParams(dimension_semantics=("arbitrary", "arbitrary"),
                                             vmem_limit_bytes=VMEM_LIMIT_BYTES),
    )(x, st_shift, st_wkv, st_conv, st_h, *weights)


def _ffn_kernel(x1_ref, st_conv_ref,
                g2_ref, w_up_ref, w_gate_ref, cw_ref, cb_ref, w_down_ref, gf_ref,
                y_hbm, o_conv_ref,
                up_carry, y_buf, y_sem,
                *, bb, chunk, n_blocks, n_chunks):
    rows = bb * chunk
    conv_rows = (FFN_CONV_W - 1) * bb
    ti = pl.program_id(1)
    step = pl.program_id(0) * n_chunks + ti
    n_steps = n_blocks * n_chunks
    y_stream = _TimeMajorStream(y_hbm, y_buf, y_sem, bb=bb, chunk=chunk, n_blocks=n_blocks,
                                n_chunks=n_chunks, to_hbm=True)
    slot = lax.rem(step, 2)

    @pl.when(ti == 0)
    def _():
        up_carry[...] = st_conv_ref[...].reshape(conv_rows, D_FF)

    x1 = x1_ref[...].reshape(rows, D_MODEL)
    xn = _rms(x1, g2_ref[...]).astype(BF16)
    acc = x1
    for n in range(D_FF // FF_COL_TILE):
        cols = slice(n * FF_COL_TILE, (n + 1) * FF_COL_TILE)
        up = jnp.dot(xn, w_up_ref[:, cols], preferred_element_type=F32)
        gate = jnp.dot(xn, w_gate_ref[:, cols], preferred_element_type=F32)
        up_prev = up_carry[:, cols]
        upc = cb_ref[:, cols] + _shift_time(up_prev, up, 2, bb) * cw_ref[0:1, cols]
        upc = upc + _shift_time(up_prev, up, 1, bb) * cw_ref[1:2, cols]
        upc = upc + up * cw_ref[2:3, cols]
        up_carry[:, cols] = up[rows - conv_rows:]
        o_conv_ref[:, :, cols] = up[rows - conv_rows:].reshape(FFN_CONV_W - 1, bb, FF_COL_TILE)
        hid = (jax.nn.gelu(upc) * gate).astype(BF16)
        acc = acc + jnp.dot(hid, w_down_ref[cols, :], preferred_element_type=F32)

    y_buf[slot] = _rms(acc, gf_ref[...]).reshape(chunk, bb, D_MODEL)
    y_stream.start(step, slot)

    @pl.when(step > 0)
    def _():
        y_stream.wait(step - 1, 1 - slot)

    @pl.when(step == n_steps - 1)
    def _():
        y_stream.wait(step, slot)


def _ffn(x1, st_conv, weights, *, bb, chunk):
    length, batch, _ = x1.shape
    assert batch % bb == 0 and length % chunk == 0 and bb % SUBLANES == 0
    assert chunk >= FFN_CONV_W - 1
    rows = bb * chunk
    grid = (batch // bb, length // chunk)
    act = pl.BlockSpec((chunk, bb, D_MODEL), lambda bi, ti: (ti, bi, 0))
    conv_spec = pl.BlockSpec((FFN_CONV_W - 1, bb, D_FF), lambda bi, ti: (0, bi, 0))
    in_specs = [act, conv_spec] + [_const_spec(w.shape) for w in weights]
    out_shape = [jax.ShapeDtypeStruct((batch, length, D_MODEL), F32),
                 jax.ShapeDtypeStruct((FFN_CONV_W - 1, batch, D_FF), F32)]
    scratch = [pltpu.VMEM(((FFN_CONV_W - 1) * bb, D_FF), F32),
               pltpu.VMEM((2, chunk, bb, D_MODEL), F32), pltpu.SemaphoreType.DMA((2,))]
    return pl.pallas_call(
        functools.partial(_ffn_kernel, bb=bb, chunk=chunk, n_blocks=grid[0], n_chunks=grid[1]),
        out_shape=out_shape, grid=grid, in_specs=in_specs,
        out_specs=[pl.BlockSpec(memory_space=pl.ANY), conv_spec],
        scratch_shapes=scratch, name="ffn",
        compiler_params=pltpu.CompilerParams(dimension_semantics=("arbitrary", "arbitrary"),
                                             vmem_limit_bytes=VMEM_LIMIT_BYTES),
    )(x1, st_conv, *weights)


def _ffn_state_kernel(x1_ref, g2_ref, w_up_ref, o_conv_ref, *, bb):
    tail = FFN_CONV_W - 1
    x1 = x1_ref[...].reshape(tail * bb, D_MODEL)
    up = _dot(_rms(x1, g2_ref[...]), w_up_ref[...])
    o_conv_ref[...] = up.reshape(tail, bb, D_FF)


def _ffn_state(x1, g2, w_up):
    length, batch, _ = x1.shape
    tail = FFN_CONV_W - 1
    assert length % tail == 0 and batch % SUBLANES == 0
    return pl.pallas_call(
        functools.partial(_ffn_state_kernel, bb=batch),
        out_shape=jax.ShapeDtypeStruct((tail, batch, D_FF), F32), grid=(1,),
        in_specs=[pl.BlockSpec((tail, batch, D_MODEL), lambda i: (length // tail - 1, 0, 0)),
                  pl.BlockSpec(g2.shape, lambda i: (0, 0)), pl.BlockSpec(w_up.shape, lambda i: (0, 0))],
        out_specs=pl.BlockSpec((tail, batch, D_FF), lambda i: (0, 0, 0)), name="ffn_state",
        compiler_params=pltpu.CompilerParams(dimension_semantics=("arbitrary",),
                                             vmem_limit_bytes=VMEM_LIMIT_BYTES),
    )(x1, g2, w_up)


def _row(v):
    return v.reshape(1, -1).astype(F32)


def _pair_block_diag(w):
    nb, n, _ = w.shape
    w = w.reshape(nb // 2, 2, n, n)
    eye = jnp.eye(2, dtype=w.dtype)
    return (eye[None, :, None, :, None] * w[:, :, :, None, :]).reshape(nb // 2, 2 * n, 2 * n)


def _wkv_to_pairs(s):
    b = s.shape[0]
    s = s.astype(F32).reshape(b, PAIRS, 2, TM_HEAD, TM_HEAD)
    return jnp.transpose(s, (0, 1, 3, 2, 4)).reshape(b, PAIRS, TM_HEAD, LANES)


def _wkv_from_pairs(s):
    b = s.shape[0]
    s = s.reshape(b, PAIRS, TM_HEAD, 2, TM_HEAD)
    return jnp.transpose(s, (0, 1, 3, 2, 4)).reshape(b, TM_HEADS, TM_HEAD, TM_HEAD)


def _layer(x, states, mixer_w, ffn_w, *, mixer_bb, ffn_bb, mixer_chunk, ffn_chunk, pos0, group):
    st_shift, st_wkv, st_conv, st_h, st_fconv = states
    x1, o_shift, o_wkv, o_conv, o_h = _mixer(x, st_shift, st_wkv, st_conv, st_h, mixer_w,
                                             bb=mixer_bb, chunk=mixer_chunk, pos0=pos0, group=group)
    y, o_fconv = _ffn(x1, st_fconv, ffn_w, bb=ffn_bb, chunk=ffn_chunk)
    return y, (o_shift, o_wkv, o_conv, o_h, o_fconv)


def kernel(x_prompt, x_sample, state_tm_shift, state_tm_wkv, state_lru_conv, state_lru_h, state_ffn_conv, meta_tokens, norm1_g, w_in, tm_mu, tm_w0, tm_w_up, tm_a0, tm_a_up, tm_g_up, tm_k_k, tm_k_a, tm_r_k, tm_gn_g, tm_gn_b, lru_conv_w, lru_conv_b, lru_wa, lru_ba, lru_wx, lru_bx, lru_lambda, lru_out_g, w_out, norm2_g, ffn_w_up, ffn_w_gate, ffn_conv_w, ffn_conv_b, ffn_w_down, norm_f_g):
    depth = w_in.shape[0]
    assert depth == 1
    l = 0
    zeros_lora = jnp.zeros((DECAY_RANK, D_TM), F32)
    head_id = jnp.arange(D_TM) // TM_HEAD
    seg = (head_id[:, None] == jnp.arange(LANES)[None, :]).astype(BF16)
    mixer_w = (
        _row(norm1_g[l]), w_in[l].astype(BF16), _row(tm_mu[l]), _row(tm_w0[l]),
        jnp.concatenate([tm_w_up[l], zeros_lora], axis=0).astype(BF16),
        _row(tm_a0[l]),
        jnp.concatenate([zeros_lora, tm_a_up[l]], axis=0).astype(BF16),
        tm_g_up[l].astype(BF16),
        _row(tm_k_k[l]), _row(tm_k_a[l]), _row(tm_r_k[l]), _row(tm_gn_g[l]), _row(tm_gn_b[l]),
        lru_conv_w[l].astype(F32), _row(lru_conv_b[l]),
        jnp.concatenate([_pair_block_diag(lru_wa[l]), _pair_block_diag(lru_wx[l])], axis=-1).astype(BF16),
        _row(lru_ba[l]), _row(lru_bx[l]),
        _row(lru_lambda[l]), _row(lru_out_g[l]),
        seg, seg.T, w_out[l].astype(BF16),
    )
    ffn_w = (
        _row(norm2_g[l]), ffn_w_up[l].astype(BF16), ffn_w_gate[l].astype(BF16),
        ffn_conv_w[l].astype(F32), _row(ffn_conv_b[l]), ffn_w_down[l].astype(BF16), _row(norm_f_g),
    )

    bsz, seq = x_prompt.shape[0], x_prompt.shape[1]
    x_meta = jnp.broadcast_to(meta_tokens[None].astype(F32), (bsz, N_META, D_MODEL))
    x1_meta, *meta_st = _mixer(x_meta, jnp.zeros((bsz, D_TM_PROJ), F32),
                               jnp.zeros((bsz, PAIRS, TM_HEAD, LANES), F32),
                               jnp.zeros((LRU_CONV_W - 1, bsz, D_LRU), F32),
                               jnp.zeros((bsz, D_LRU), F32), mixer_w,
                               bb=bsz, chunk=N_META, pos0=0, group=4)
    p_init = tuple(meta_st) + (_ffn_state(x1_meta, ffn_w[0], ffn_w[1]),)

    y_prompt, p_st = _layer(x_prompt, p_init, mixer_w, ffn_w,
                            mixer_bb=bsz, ffn_bb=bsz, mixer_chunk=64, ffn_chunk=64,
                            pos0=N_META, group=4)

    dec_b, dec_seq = x_sample.shape[0], x_sample.shape[1]
    s_init = (state_tm_shift[l].astype(F32), _wkv_to_pairs(state_tm_wkv[l]),
              jnp.transpose(state_lru_conv[l].astype(F32), (1, 0, 2)), state_lru_h[l].astype(F32),
              jnp.transpose(state_ffn_conv[l].astype(F32), (1, 0, 2)))
    y_sample, s_st = _layer(x_sample, s_init, mixer_w, ffn_w,
                            mixer_bb=32, ffn_bb=64, mixer_chunk=dec_seq, ffn_chunk=dec_seq,
                            pos0=PAST_LEN, group=16)

    def unpack(st):
        o_shift, o_wkv, o_conv, o_h, o_fconv = st
        return (o_shift[None], _wkv_from_pairs(o_wkv)[None], jnp.transpose(o_conv, (1, 0, 2))[None],
                o_h[None], jnp.transpose(o_fconv, (1, 0, 2))[None])

    return (y_prompt, y_sample) + unpack(p_st) + unpack(s_st)
```

```python
import functools
import math

import jax
import jax.numpy as jnp
from jax import lax
from jax.experimental import pallas as pl
from jax.experimental.pallas import tpu as pltpu

F32 = jnp.float32
BF16 = jnp.bfloat16

D_MODEL = 1024
N_META = 16
PAST_LEN = 16384
D_TM = 512
TM_HEAD = 64
TM_HEADS = 8
DECAY_RANK = 64
AAA_RANK = 64
GATE_RANK = 128
D_TM_PROJ = 3 * D_TM + DECAY_RANK + AAA_RANK + GATE_RANK
D_LRU = 512
LRU_CONV_W = 4
LRU_C = 8.0
D_IN_PROJ = D_TM_PROJ + 2 * D_LRU
D_FF = 3 * D_MODEL
FFN_CONV_W = 3
EPS = 1e-6
GN_EPS = 64e-5
LOG2_E = math.log2(math.e)

SUBLANES = 8
LANES = 128
PAIRS = TM_HEADS // 2
LORA_OFF = 3 * D_TM
GATE_OFF = LORA_OFF + DECAY_RANK + AAA_RANK
FF_COL_TILE = 1024
VMEM_LIMIT_BYTES = 60 * 1024 * 1024


def _dot(a, b):
    return jnp.dot(a.astype(BF16), b.astype(BF16), preferred_element_type=F32)


def _bdot(a, b):
    return lax.dot_general(a.astype(BF16), b.astype(BF16), (((2,), (1,)), ((0,), (0,))),
                           preferred_element_type=F32)


def _bdot_nt(a, b):
    return lax.dot_general(a.astype(BF16), b.astype(BF16), (((2,), (2,)), ((0,), (0,))),
                           preferred_element_type=F32)


def _bdot_tn(a, b):
    return lax.dot_general(a.astype(BF16), b.astype(BF16), (((1,), (1,)), ((0,), (0,))),
                           preferred_element_type=F32)


def _rms(x, g):
    return x * lax.rsqrt(jnp.mean(x * x, axis=-1, keepdims=True) + EPS) * g


def _sigmoid(z):
    return 0.5 * jnp.tanh(0.5 * z) + 0.5


def _softplus(z):
    return jnp.maximum(z, 0.0) + jnp.log(1.0 + jnp.exp(-jnp.abs(z)))


def _block_diag_rows(x, left):
    return jnp.concatenate([jnp.where(left, x, 0.0), jnp.where(left, 0.0, x)], axis=1)


def _shift_time(carry, x, steps, bb):
    n = steps * bb
    return jnp.concatenate([carry[carry.shape[0] - n:], x[:x.shape[0] - n]], axis=0)


class _TimeMajorStream:
    def __init__(self, hbm, buf, sem, *, bb, chunk, n_blocks, n_chunks, to_hbm):
        self.hbm, self.buf, self.sem = hbm, buf, sem
        self.bb, self.chunk, self.to_hbm = bb, chunk, to_hbm
        self.n_blocks, self.n_chunks = n_blocks, n_chunks

    def _copy(self, g, slot, t):
        if self.n_chunks == 1:
            bi, ti = g, 0
        elif self.n_blocks == 1:
            bi, ti = 0, g
        else:
            bi, ti = lax.div(g, self.n_chunks), lax.rem(g, self.n_chunks)
        seqs = pl.ds(pl.multiple_of(bi * self.bb, SUBLANES), self.bb)
        hbm_rows = self.hbm.at[seqs, ti * self.chunk + t, :]
        tile = self.buf.at[slot, t]
        src, dst = (tile, hbm_rows) if self.to_hbm else (hbm_rows, tile)
        return pltpu.make_async_copy(src, dst, self.sem.at[slot])

    def start(self, g, slot):
        for t in range(self.chunk):
            self._copy(g, slot, t).start()

    def wait(self, g, slot):
        for t in range(self.chunk):
            self._copy(g, slot, t).wait()


READ_SLOTS = 3


def _read_chunk_begin(stream, g, n_steps):
    last = n_steps - 1

    @pl.when(g == 0)
    def _():
        stream.start(0, 0)
        stream.start(jnp.minimum(1, last), 1)

    slot = lax.rem(g, READ_SLOTS)
    stream.wait(g, slot)
    return slot


def _read_chunk_end(stream, g, n_steps):
    last = n_steps - 1
    stream.start(jnp.minimum(g + 2, last), lax.rem(g + 2, READ_SLOTS))

    @pl.when(g == last)
    def _():
        stream.wait(last, lax.rem(g + 1, READ_SLOTS))
        stream.wait(last, lax.rem(g + 2, READ_SLOTS))


def _mixer_kernel(x_hbm, st_shift_ref, st_wkv_ref, st_conv_ref, st_h_ref,
                  g1_ref, w_in_ref, mu_ref, w0_ref, wdec_ref, a0_ref, waaa_ref, wgate_ref,
                  kk_ref, ka_ref, rk_ref, gng_ref, gnb_ref,
                  cw_ref, cb_ref, wgates_ref, ba_ref, bx_ref, lam_ref, og_ref,
                  seg_ref, segt_ref, w_out_ref,
                  x1_ref, o_shift_ref, o_wkv_ref, o_conv_ref, o_h_ref,
                  tm_carry, xb_carry, kap_s, rt_s, bt_s, kt_s, v_s, yt_s, pe_s, x_buf, x_sem,
                  *, bb, chunk, pos0, n_blocks, n_chunks):
    rows = bb * chunk
    conv_rows = (LRU_CONV_W - 1) * bb
    wkv_chunk = max(chunk, SUBLANES)
    ti = pl.program_id(1)
    step = pl.program_id(0) * n_chunks + ti
    n_steps = n_blocks * n_chunks
    x_stream = _TimeMajorStream(x_hbm, x_buf, x_sem, bb=bb, chunk=chunk, n_blocks=n_blocks,
                                n_chunks=n_chunks, to_hbm=False)
    x_slot = _read_chunk_begin(x_stream, step, n_steps)

    @pl.when(ti == 0)
    def _():
        tm_carry[...] = st_shift_ref[...]
        xb_carry[...] = st_conv_ref[...].reshape(conv_rows, D_LRU)
        o_wkv_ref[...] = st_wkv_ref[...]
        o_h_ref[...] = st_h_ref[...]
        if wkv_chunk > chunk:
            for ref in (kap_s, rt_s, bt_s, kt_s, v_s):
                ref[:, rows:wkv_chunk * bb, :] = jnp.zeros((PAIRS, wkv_chunk * bb - rows, LANES), F32)

    seg = seg_ref[...]
    segt = segt_ref[...]

    def head_sum(x):
        s = jnp.dot(x.astype(BF16), seg, preferred_element_type=F32)
        return jnp.dot(s.astype(BF16), segt, preferred_element_type=F32)

    x = x_buf[x_slot].reshape(rows, D_MODEL)
    u = _dot(_rms(x, g1_ref[...]), w_in_ref[...])
    u_tm = u[:, :D_TM_PROJ]
    xb = u[:, D_TM_PROJ:D_TM_PROJ + D_LRU]
    gate_lru = u[:, D_TM_PROJ + D_LRU:D_IN_PROJ]

    row = lax.broadcasted_iota(jnp.int32, (rows, D_TM), 0)

    um = u_tm + (_shift_time(tm_carry[...], u_tm, 1, bb) - u_tm) * mu_ref[...]
    r = um[:, 0:D_TM]
    k = um[:, D_TM:2 * D_TM]
    v = um[:, 2 * D_TM:3 * D_TM]
    x_lora = um[:, LORA_OFF:GATE_OFF]
    x_gate = um[:, GATE_OFF:D_TM_PROJ]
    log2_decay = (-math.exp(-0.5) * LOG2_E) * _sigmoid(w0_ref[...] + _dot(jnp.tanh(x_lora), wdec_ref[...]))
    a = _sigmoid(a0_ref[...] + _dot(x_lora, waaa_ref[...]))
    gate_tm = _dot(_sigmoid(x_gate), wgate_ref[...])
    kk = k * kk_ref[...]
    k = k * (a * ka_ref[...] + (1.0 - ka_ref[...]))
    kk = kk * lax.rsqrt(jnp.maximum(head_sum(kk * kk), 1e-24))
    bonus = head_sum(r * k * rk_ref[...]) * v
    acc = log2_decay[0:bb]
    cum = [acc]
    for t in range(1, chunk):
        acc = acc + log2_decay[t * bb:(t + 1) * bb]
        cum.append(acc)
    c = jnp.concatenate(cum, axis=0)
    inv_p = jnp.exp2(-c)
    kap = kk * jnp.exp2(c - log2_decay)
    rt = r * jnp.exp2(c)
    bt = kk * a * inv_p
    kt = k * inv_p
    p_end = jnp.exp2(cum[chunk - 1])
    for p in range(PAIRS):
        ls = slice(p * LANES, (p + 1) * LANES)
        kap_s[p, 0:rows, :] = kap[:, ls]
        rt_s[p, 0:rows, :] = rt[:, ls]
        bt_s[p, 0:rows, :] = bt[:, ls]
        kt_s[p, 0:rows, :] = kt[:, ls]
        v_s[p, 0:rows, :] = v[:, ls]
        pe_s[p] = p_end[:, ls]

    xb_prev = xb_carry[...]
    xc = cb_ref[...] + _shift_time(xb_prev, xb, 3, bb) * cw_ref[0:1, :]
    xc = xc + _shift_time(xb_prev, xb, 2, bb) * cw_ref[1:2, :]
    xc = xc + _shift_time(xb_prev, xb, 1, bb) * cw_ref[2:3, :]
    xc = xc + xb * cw_ref[3:4, :]
    xc_bf = xc.astype(BF16)

    gates = [jnp.dot(xc_bf[:, p * LANES:(p + 1) * LANES], wgates_ref[p], preferred_element_type=F32)
             for p in range(D_LRU // LANES)]
    r_g = _sigmoid(jnp.concatenate([g[:, :LANES] for g in gates], axis=-1) + ba_ref[...])
    i_g = _sigmoid(jnp.concatenate([g[:, LANES:] for g in gates], axis=-1) + bx_ref[...])
    la = jnp.exp2(r_g * ((-LRU_C * LOG2_E) * _softplus(-lam_ref[...])))
    gap = 1.0 - la * la
    mult = jnp.where(gap > 0.0, gap * lax.rsqrt(gap), 0.0)
    if pos0 == 0:
        mult = jnp.where(jnp.logical_and(row < bb, ti == 0), 1.0, mult)
    lb = xc * i_g * mult
    h = o_h_ref[...]
    hs = []
    for t in range(chunk):
        ts = slice(t * bb, (t + 1) * bb)
        h = la[ts] * h + lb[ts]
        hs.append(h)
    o_h_ref[...] = h
    y_lru = _rms(jnp.concatenate(hs, axis=0) * jax.nn.gelu(gate_lru), og_ref[...])

    wc = wkv_chunk
    n_lv = max(1, (chunk - 1).bit_length())
    cw2 = 2 * wc
    left = lax.broadcasted_iota(jnp.int32, (1, 1, LANES), 2) < TM_HEAD
    left_c = lax.broadcasted_iota(jnp.int32, (1, 1, cw2), 2) < wc
    ri = lax.broadcasted_iota(jnp.int32, (wc, cw2), 0)
    ci = lax.broadcasted_iota(jnp.int32, (wc, cw2), 1) & (wc - 1)
    strict = ri > ci
    incl = ri >= ci
    eye = (ri == ci).astype(F32)

    tsel = [pl.ds(b, wc, stride=bb) for b in range(bb)]
    gather = lambda ref: jnp.stack([ref.at[p][ts, :] for ts in tsel for p in range(PAIRS)])
    kap_g = gather(kap_s)
    rt_g = gather(rt_s)
    bt_g = gather(bt_s)
    kt_g = gather(kt_s)
    vv = gather(v_s)
    pe = jnp.stack([pe_s.at[p][pl.ds(b, 1), :] for b in range(bb) for p in range(PAIRS)])
    s0 = o_wkv_ref[...].reshape(bb * PAIRS, TM_HEAD, LANES)
    lhs = jnp.concatenate([kap_g, rt_g], axis=1)
    fused = _bdot_nt(lhs, jnp.concatenate([_block_diag_rows(bt_g, left), _block_diag_rows(kt_g, left),
                                           _block_diag_rows(s0, left)], axis=1))
    g_b = fused[:, :, 0:cw2]
    g_k = fused[:, :, cw2:2 * cw2]
    z = fused[:, :, 2 * cw2:]
    m_ab = jnp.where(strict, g_b[:, :wc], 0.0)
    m_ak = jnp.where(strict, g_k[:, :wc], 0.0)
    m_rb = jnp.where(incl, g_b[:, wc:], 0.0)
    m_rk = jnp.where(incl, g_k[:, wc:], 0.0)
    t_inv = eye - m_ab
    m_pow = _bdot(m_ab, _block_diag_rows(m_ab, left_c))
    for lv in range(1, n_lv):
        if lv < n_lv - 1:
            prod = _bdot(jnp.concatenate([t_inv, m_pow], axis=1), _block_diag_rows(m_pow, left_c))
            t_inv = t_inv + prod[:, :wc]
            m_pow = prod[:, wc:]
        else:
            t_inv = t_inv + _bdot(t_inv, _block_diag_rows(m_pow, left_c))
    xv = _bdot(jnp.concatenate([m_ak, m_rk], axis=1), _block_diag_rows(vv, left))
    uu = -_bdot(t_inv, _block_diag_rows(z[:, :wc] + xv[:, :wc], left))
    yh = z[:, wc:] + _bdot(m_rb, _block_diag_rows(uu, left)) + xv[:, wc:]
    full = _bdot_tn(jnp.concatenate([uu, vv], axis=1), jnp.concatenate([bt_g, kt_g], axis=1))
    upd = jnp.where(left, full[:, :TM_HEAD], full[:, TM_HEAD:])
    o_wkv_ref[...] = ((s0 + upd) * pe).reshape(bb, PAIRS, TM_HEAD, LANES)
    for b, ts in enumerate(tsel):
        for p in range(PAIRS):
            yt_s.at[p][ts, :] = yh[b * PAIRS + p]

    yv = jnp.concatenate([yt_s[p, 0:rows, :] for p in range(PAIRS)], axis=-1)
    mean = head_sum(yv) * (1.0 / TM_HEAD)
    cen = yv - mean
    var = head_sum(cen * cen) * (1.0 / TM_HEAD)
    y_tm = ((cen * lax.rsqrt(var + GN_EPS)) * gng_ref[...] + gnb_ref[...] + bonus) * gate_tm
    mixed = jnp.concatenate([y_tm.astype(BF16), y_lru.astype(BF16)], axis=-1)
    x1 = x_buf[x_slot].reshape(rows, D_MODEL) + jnp.dot(mixed, w_out_ref[...], preferred_element_type=F32)
    x1_ref[...] = x1.reshape(chunk, bb, D_MODEL)

    o_shift_ref[...] = u_tm[rows - bb:]
    o_conv_ref[...] = xb[rows - conv_rows:].reshape(LRU_CONV_W - 1, bb, D_LRU)
    tm_carry[...] = u_tm[rows - bb:]
    xb_carry[...] = xb[rows - conv_rows:]
    _read_chunk_end(x_stream, step, n_steps)


def _const_spec(shape):
    zeros = (0,) * len(shape)
    return pl.BlockSpec(shape, lambda bi, ti: zeros, pipeline_mode=pl.Buffered(1))


def _mixer(x, st_shift, st_wkv, st_conv, st_h, weights, *, bb, chunk, pos0):
    batch, length, _ = x.shape
    assert batch % bb == 0 and length % chunk == 0 and bb % SUBLANES == 0
    assert chunk & (chunk - 1) == 0 and chunk >= LRU_CONV_W - 1
    rows = bb * chunk
    grid = (batch // bb, length // chunk)
    act = pl.BlockSpec((chunk, bb, D_MODEL), lambda bi, ti: (ti, bi, 0))
    vec = lambda w: pl.BlockSpec((bb, w), lambda bi, ti: (bi, 0))
    wkv_spec = pl.BlockSpec((bb, PAIRS, TM_HEAD, LANES), lambda bi, ti: (bi, 0, 0, 0))
    conv_spec = pl.BlockSpec((LRU_CONV_W - 1, bb, D_LRU), lambda bi, ti: (0, bi, 0))
    in_specs = [pl.BlockSpec(memory_space=pl.ANY), vec(D_TM_PROJ), wkv_spec, conv_spec, vec(D_LRU)]
    in_specs += [_const_spec(w.shape) for w in weights]
    out_specs = [act, vec(D_TM_PROJ), wkv_spec, conv_spec, vec(D_LRU)]
    out_shape = [jax.ShapeDtypeStruct((length, batch, D_MODEL), F32),
                 jax.ShapeDtypeStruct((batch, D_TM_PROJ), F32),
                 jax.ShapeDtypeStruct((batch, PAIRS, TM_HEAD, LANES), F32),
                 jax.ShapeDtypeStruct((LRU_CONV_W - 1, batch, D_LRU), F32),
                 jax.ShapeDtypeStruct((batch, D_LRU), F32)]
    pair_rows = lambda n: pltpu.VMEM((PAIRS, n, LANES), F32)
    scratch = [pltpu.VMEM((bb, D_TM_PROJ), F32),
               pltpu.VMEM(((LRU_CONV_W - 1) * bb, D_LRU), F32)]
    scratch += [pair_rows(max(chunk, SUBLANES) * bb)] * 6 + [pair_rows(bb)]
    scratch += [pltpu.VMEM((READ_SLOTS, chunk, bb, D_MODEL), F32), pltpu.SemaphoreType.DMA((READ_SLOTS,))]
    return pl.pallas_call(
        functools.partial(_mixer_kernel, bb=bb, chunk=chunk, pos0=pos0,
                          n_blocks=grid[0], n_chunks=grid[1]),
        out_shape=out_shape, grid=grid, in_specs=in_specs, out_specs=out_specs,
        scratch_shapes=scratch, name="mixer",
        compiler_params=pltpu.CompilerParams(dimension_semantics=("arbitrary", "arbitrary"),
                                             vmem_limit_bytes=VMEM_LIMIT_BYTES),
    )(x, st_shift, st_wkv, st_conv, st_h, *weights)


def _ffn_kernel(x1_ref, st_conv_ref,
                g2_ref, w_up_ref, w_gate_ref, cw_ref, cb_ref, w_down_ref, gf_ref,
                y_hbm, o_conv_ref,
                up_carry, y_buf, y_sem,
                *, bb, chunk, n_blocks, n_chunks):
    rows = bb * chunk
    conv_rows = (FFN_CONV_W - 1) * bb
    ti = pl.program_id(1)
    step = pl.program_id(0) * n_chunks + ti
    n_steps = n_blocks * n_chunks
    y_stream = _TimeMajorStream(y_hbm, y_buf, y_sem, bb=bb, chunk=chunk, n_blocks=n_blocks,
                                n_chunks=n_chunks, to_hbm=True)
    slot = lax.rem(step, 2)

    @pl.when(ti == 0)
    def _():
        up_carry[...] = st_conv_ref[...].reshape(conv_rows, D_FF)

    x1 = x1_ref[...].reshape(rows, D_MODEL)
    xn = _rms(x1, g2_ref[...]).astype(BF16)
    acc = x1
    for n in range(D_FF // FF_COL_TILE):
        cols = slice(n * FF_COL_TILE, (n + 1) * FF_COL_TILE)
        up = jnp.dot(xn, w_up_ref[:, cols], preferred_element_type=F32)
        gate = jnp.dot(xn, w_gate_ref[:, cols], preferred_element_type=F32)
        up_prev = up_carry[:, cols]
        upc = cb_ref[:, cols] + _shift_time(up_prev, up, 2, bb) * cw_ref[0:1, cols]
        upc = upc + _shift_time(up_prev, up, 1, bb) * cw_ref[1:2, cols]
        upc = upc + up * cw_ref[2:3, cols]
        up_carry[:, cols] = up[rows - conv_rows:]
        o_conv_ref[:, :, cols] = up[rows - conv_rows:].reshape(FFN_CONV_W - 1, bb, FF_COL_TILE)
        hid = (jax.nn.gelu(upc) * gate).astype(BF16)
        acc = acc + jnp.dot(hid, w_down_ref[cols, :], preferred_element_type=F32)

    y_buf[slot] = _rms(acc, gf_ref[...]).reshape(chunk, bb, D_MODEL)
    y_stream.start(step, slot)

    @pl.when(step > 0)
    def _():
        y_stream.wait(step - 1, 1 - slot)

    @pl.when(step == n_steps - 1)
    def _():
        y_stream.wait(step, slot)


def _ffn(x1, st_conv, weights, *, bb, chunk):
    length, batch, _ = x1.shape
    assert batch % bb == 0 and length % chunk == 0 and bb % SUBLANES == 0
    assert chunk >= FFN_CONV_W - 1
    rows = bb * chunk
    grid = (batch // bb, length // chunk)
    act = pl.BlockSpec((chunk, bb, D_MODEL), lambda bi, ti: (ti, bi, 0))
    conv_spec = pl.BlockSpec((FFN_CONV_W - 1, bb, D_FF), lambda bi, ti: (0, bi, 0))
    in_specs = [act, conv_spec] + [_const_spec(w.shape) for w in weights]
    out_shape = [jax.ShapeDtypeStruct((batch, length, D_MODEL), F32),
                 jax.ShapeDtypeStruct((FFN_CONV_W - 1, batch, D_FF), F32)]
    scratch = [pltpu.VMEM(((FFN_CONV_W - 1) * bb, D_FF), F32),
               pltpu.VMEM((2, chunk, bb, D_MODEL), F32), pltpu.SemaphoreType.DMA((2,))]
    return pl.pallas_call(
        functools.partial(_ffn_kernel, bb=bb, chunk=chunk, n_blocks=grid[0], n_chunks=grid[1]),
        out_shape=out_shape, grid=grid, in_specs=in_specs,
        out_specs=[pl.BlockSpec(memory_space=pl.ANY), conv_spec],
        scratch_shapes=scratch, name="ffn",
        compiler_params=pltpu.CompilerParams(dimension_semantics=("arbitrary", "arbitrary"),
                                             vmem_limit_bytes=VMEM_LIMIT_BYTES),
    )(x1, st_conv, *weights)


def _ffn_state_kernel(x1_ref, g2_ref, w_up_ref, o_conv_ref, *, bb):
    tail = FFN_CONV_W - 1
    x1 = x1_ref[...].reshape(tail * bb, D_MODEL)
    up = _dot(_rms(x1, g2_ref[...]), w_up_ref[...])
    o_conv_ref[...] = up.reshape(tail, bb, D_FF)


def _ffn_state(x1, g2, w_up):
    length, batch, _ = x1.shape
    tail = FFN_CONV_W - 1
    assert length % tail == 0 and batch % SUBLANES == 0
    return pl.pallas_call(
        functools.partial(_ffn_state_kernel, bb=batch),
        out_shape=jax.ShapeDtypeStruct((tail, batch, D_FF), F32), grid=(1,),
        in_specs=[pl.BlockSpec((tail, batch, D_MODEL), lambda i: (length // tail - 1, 0, 0)),
                  pl.BlockSpec(g2.shape, lambda i: (0, 0)), pl.BlockSpec(w_up.shape, lambda i: (0, 0))],
        out_specs=pl.BlockSpec((tail, batch, D_FF), lambda i: (0, 0, 0)), name="ffn_state",
        compiler_params=pltpu.CompilerParams(dimension_semantics=("arbitrary",),
                                             vmem_limit_bytes=VMEM_LIMIT_BYTES),
    )(x1, g2, w_up)


def _row(v):
    return v.reshape(1, -1).astype(F32)


def _pair_block_diag(w):
    nb, n, _ = w.shape
    w = w.reshape(nb // 2, 2, n, n)
    eye = jnp.eye(2, dtype=w.dtype)
    return (eye[None, :, None, :, None] * w[:, :, :, None, :]).reshape(nb // 2, 2 * n, 2 * n)


def _wkv_to_pairs(s):
    b = s.shape[0]
    s = s.astype(F32).reshape(b, PAIRS, 2, TM_HEAD, TM_HEAD)
    return jnp.transpose(s, (0, 1, 3, 2, 4)).reshape(b, PAIRS, TM_HEAD, LANES)


def _wkv_from_pairs(s):
    b = s.shape[0]
    s = s.reshape(b, PAIRS, TM_HEAD, 2, TM_HEAD)
    return jnp.transpose(s, (0, 1, 3, 2, 4)).reshape(b, TM_HEADS, TM_HEAD, TM_HEAD)


def _layer(x, states, mixer_w, ffn_w, *, mixer_bb, ffn_bb, mixer_chunk, ffn_chunk, pos0):
    st_shift, st_wkv, st_conv, st_h, st_fconv = states
    x1, o_shift, o_wkv, o_conv, o_h = _mixer(x, st_shift, st_wkv, st_conv, st_h, mixer_w,
                                             bb=mixer_bb, chunk=mixer_chunk, pos0=pos0)
    y, o_fconv = _ffn(x1, st_fconv, ffn_w, bb=ffn_bb, chunk=ffn_chunk)
    return y, (o_shift, o_wkv, o_conv, o_h, o_fconv)


def kernel(x_prompt, x_sample, state_tm_shift, state_tm_wkv, state_lru_conv, state_lru_h, state_ffn_conv, meta_tokens, norm1_g, w_in, tm_mu, tm_w0, tm_w_up, tm_a0, tm_a_up, tm_g_up, tm_k_k, tm_k_a, tm_r_k, tm_gn_g, tm_gn_b, lru_conv_w, lru_conv_b, lru_wa, lru_ba, lru_wx, lru_bx, lru_lambda, lru_out_g, w_out, norm2_g, ffn_w_up, ffn_w_gate, ffn_conv_w, ffn_conv_b, ffn_w_down, norm_f_g):
    depth = w_in.shape[0]
    assert depth == 1
    l = 0
    zeros_lora = jnp.zeros((DECAY_RANK, D_TM), F32)
    head_id = jnp.arange(D_TM) // TM_HEAD
    seg = (head_id[:, None] == jnp.arange(LANES)[None, :]).astype(BF16)
    mixer_w = (
        _row(norm1_g[l]), w_in[l].astype(BF16), _row(tm_mu[l]), _row(tm_w0[l]),
        jnp.concatenate([tm_w_up[l], zeros_lora], axis=0).astype(BF16),
        _row(tm_a0[l]),
        jnp.concatenate([zeros_lora, tm_a_up[l]], axis=0).astype(BF16),
        tm_g_up[l].astype(BF16),
        _row(tm_k_k[l]), _row(tm_k_a[l]), _row(tm_r_k[l]), _row(tm_gn_g[l]), _row(tm_gn_b[l]),
        lru_conv_w[l].astype(F32), _row(lru_conv_b[l]),
        jnp.concatenate([_pair_block_diag(lru_wa[l]), _pair_block_diag(lru_wx[l])], axis=-1).astype(BF16),
        _row(lru_ba[l]), _row(lru_bx[l]),
        _row(lru_lambda[l]), _row(lru_out_g[l]),
        seg, seg.T, w_out[l].astype(BF16),
    )
    ffn_w = (
        _row(norm2_g[l]), ffn_w_up[l].astype(BF16), ffn_w_gate[l].astype(BF16),
        ffn_conv_w[l].astype(F32), _row(ffn_conv_b[l]), ffn_w_down[l].astype(BF16), _row(norm_f_g),
    )

    bsz, seq = x_prompt.shape[0], x_prompt.shape[1]
    x_meta = jnp.broadcast_to(meta_tokens[None].astype(F32), (bsz, N_META, D_MODEL))
    x1_meta, *meta_st = _mixer(x_meta, jnp.zeros((bsz, D_TM_PROJ), F32),
                               jnp.zeros((bsz, PAIRS, TM_HEAD, LANES), F32),
                               jnp.zeros((LRU_CONV_W - 1, bsz, D_LRU), F32),
                               jnp.zeros((bsz, D_LRU), F32), mixer_w,
                               bb=bsz, chunk=N_META, pos0=0)
    p_init = tuple(meta_st) + (_ffn_state(x1_meta, ffn_w[0], ffn_w[1]),)

    y_prompt, p_st = _layer(x_prompt, p_init, mixer_w, ffn_w,
                            mixer_bb=bsz, ffn_bb=bsz, mixer_chunk=64, ffn_chunk=64,
                            pos0=N_META)

    dec_b, dec_seq = x_sample.shape[0], x_sample.shape[1]
    s_init = (state_tm_shift[l].astype(F32), _wkv_to_pairs(state_tm_wkv[l]),
              jnp.transpose(state_lru_conv[l].astype(F32), (1, 0, 2)), state_lru_h[l].astype(F32),
              jnp.transpose(state_ffn_conv[l].astype(F32), (1, 0, 2)))
    y_sample, s_st = _layer(x_sample, s_init, mixer_w, ffn_w,
                            mixer_bb=32, ffn_bb=64, mixer_chunk=dec_seq, ffn_chunk=dec_seq,
                            pos0=PAST_LEN)

    def unpack(st):
        o_shift, o_wkv, o_conv, o_h, o_fconv = st
        return (o_shift[None], _wkv_from_pairs(o_wkv)[None], jnp.transpose(o_conv, (1, 0, 2))[None],
                o_h[None], jnp.transpose(o_fconv, (1, 0, 2))[None])

    return (y_prompt, y_sample) + unpack(p_st) + unpack(s_st)
```

```python
import functools
import math

import jax
import jax.numpy as jnp
from jax import lax
from jax.experimental import pallas as pl
from jax.experimental.pallas import tpu as pltpu

F32 = jnp.float32
BF16 = jnp.bfloat16

D_MODEL = 1024
N_META = 16
PAST_LEN = 16384
D_TM = 512
TM_HEAD = 64
TM_HEADS = 8
DECAY_RANK = 64
AAA_RANK = 64
GATE_RANK = 128
D_TM_PROJ = 3 * D_TM + DECAY_RANK + AAA_RANK + GATE_RANK
D_LRU = 512
LRU_CONV_W = 4
LRU_C = 8.0
D_IN_PROJ = D_TM_PROJ + 2 * D_LRU
D_FF = 3 * D_MODEL
FFN_CONV_W = 3
EPS = 1e-6
GN_EPS = 64e-5
LOG2_E = math.log2(math.e)

SUBLANES = 8
LANES = 128
PAIRS = TM_HEADS // 2
LORA_OFF = 3 * D_TM
GATE_OFF = LORA_OFF + DECAY_RANK + AAA_RANK
FF_COL_TILE = 1024
VMEM_LIMIT_BYTES = 60 * 1024 * 1024


def _dot(a, b):
    return jnp.dot(a.astype(BF16), b.astype(BF16), preferred_element_type=F32)


def _bdot(a, b):
    return lax.dot_general(a.astype(BF16), b.astype(BF16), (((2,), (1,)), ((0,), (0,))),
                           preferred_element_type=F32)


def _bdot_nt(a, b):
    return lax.dot_general(a.astype(BF16), b.astype(BF16), (((2,), (2,)), ((0,), (0,))),
                           preferred_element_type=F32)


def _bdot_tn(a, b):
    return lax.dot_general(a.astype(BF16), b.astype(BF16), (((1,), (1,)), ((0,), (0,))),
                           preferred_element_type=F32)


def _rms(x, g):
    return x * lax.rsqrt(jnp.mean(x * x, axis=-1, keepdims=True) + EPS) * g


def _sigmoid(z):
    return 0.5 * jnp.tanh(0.5 * z) + 0.5


def _softplus(z):
    return jnp.maximum(z, 0.0) + jnp.log(1.0 + jnp.exp(-jnp.abs(z)))


def _block_diag_rows(x, left):
    return jnp.concatenate([jnp.where(left, x, 0.0), jnp.where(left, 0.0, x)], axis=1)


def _shift_time(carry, x, steps, bb):
    n = steps * bb
    return jnp.concatenate([carry[carry.shape[0] - n:], x[:x.shape[0] - n]], axis=0)


class _TimeMajorStream:
    def __init__(self, hbm, buf, sem, *, bb, chunk, n_blocks, n_chunks, to_hbm):
        self.hbm, self.buf, self.sem = hbm, buf, sem
        self.bb, self.chunk, self.to_hbm = bb, chunk, to_hbm
        self.n_blocks, self.n_chunks = n_blocks, n_chunks

    def _copy(self, g, slot, t):
        if self.n_chunks == 1:
            bi, ti = g, 0
        elif self.n_blocks == 1:
            bi, ti = 0, g
        else:
            bi, ti = lax.div(g, self.n_chunks), lax.rem(g, self.n_chunks)
        seqs = pl.ds(pl.multiple_of(bi * self.bb, SUBLANES), self.bb)
        hbm_rows = self.hbm.at[seqs, ti * self.chunk + t, :]
        tile = self.buf.at[slot, t]
        src, dst = (tile, hbm_rows) if self.to_hbm else (hbm_rows, tile)
        return pltpu.make_async_copy(src, dst, self.sem.at[slot])

    def start(self, g, slot):
        for t in range(self.chunk):
            self._copy(g, slot, t).start()

    def wait(self, g, slot):
        for t in range(self.chunk):
            self._copy(g, slot, t).wait()


READ_SLOTS = 3


def _read_chunk_begin(stream, g, n_steps):
    last = n_steps - 1

    @pl.when(g == 0)
    def _():
        stream.start(0, 0)
        stream.start(jnp.minimum(1, last), 1)

    slot = lax.rem(g, READ_SLOTS)
    stream.wait(g, slot)
    return slot


def _read_chunk_end(stream, g, n_steps):
    last = n_steps - 1
    stream.start(jnp.minimum(g + 2, last), lax.rem(g + 2, READ_SLOTS))

    @pl.when(g == last)
    def _():
        stream.wait(last, lax.rem(g + 1, READ_SLOTS))
        stream.wait(last, lax.rem(g + 2, READ_SLOTS))


def _mixer_kernel(x_hbm, st_shift_ref, st_wkv_ref, st_conv_ref, st_h_ref,
                  g1_ref, w_in_ref, mu_ref, w0_ref, wdec_ref, a0_ref, waaa_ref, wgate_ref,
                  kk_ref, ka_ref, rk_ref, gng_ref, gnb_ref,
                  cw_ref, cb_ref, wgates_ref, ba_ref, bx_ref, lam_ref, og_ref,
                  seg_ref, segt_ref, w_out_ref,
                  x1_ref, o_shift_ref, o_wkv_ref, o_conv_ref, o_h_ref,
                  tm_carry, xb_carry, kap_s, rt_s, bt_s, kt_s, v_s, yt_s, pe_s, x_buf, x_sem,
                  *, bb, chunk, pos0, n_blocks, n_chunks):
    rows = bb * chunk
    conv_rows = (LRU_CONV_W - 1) * bb
    wkv_chunk = max(chunk, SUBLANES)
    ti = pl.program_id(1)
    step = pl.program_id(0) * n_chunks + ti
    n_steps = n_blocks * n_chunks
    x_stream = _TimeMajorStream(x_hbm, x_buf, x_sem, bb=bb, chunk=chunk, n_blocks=n_blocks,
                                n_chunks=n_chunks, to_hbm=False)
    x_slot = _read_chunk_begin(x_stream, step, n_steps)

    @pl.when(ti == 0)
    def _():
        tm_carry[...] = st_shift_ref[...]
        xb_carry[...] = st_conv_ref[...].reshape(conv_rows, D_LRU)
        o_wkv_ref[...] = st_wkv_ref[...]
        o_h_ref[...] = st_h_ref[...]
        if wkv_chunk > chunk:
            for ref in (kap_s, rt_s, bt_s, kt_s, v_s):
                ref[:, rows:wkv_chunk * bb, :] = jnp.zeros((PAIRS, wkv_chunk * bb - rows, LANES), F32)

    seg = seg_ref[...]
    segt = segt_ref[...]

    def head_sum(x):
        s = jnp.dot(x.astype(BF16), seg, preferred_element_type=F32)
        return jnp.dot(s.astype(BF16), segt, preferred_element_type=F32)

    x = x_buf[x_slot].reshape(rows, D_MODEL)
    u = _dot(_rms(x, g1_ref[...]), w_in_ref[...])
    u_tm = u[:, :D_TM_PROJ]
    xb = u[:, D_TM_PROJ:D_TM_PROJ + D_LRU]
    gate_lru = u[:, D_TM_PROJ + D_LRU:D_IN_PROJ]

    row = lax.broadcasted_iota(jnp.int32, (rows, D_TM), 0)

    um = u_tm + (_shift_time(tm_carry[...], u_tm, 1, bb) - u_tm) * mu_ref[...]
    r = um[:, 0:D_TM]
    k = um[:, D_TM:2 * D_TM]
    v = um[:, 2 * D_TM:3 * D_TM]
    x_lora = um[:, LORA_OFF:GATE_OFF]
    x_gate = um[:, GATE_OFF:D_TM_PROJ]
    log2_decay = (-math.exp(-0.5) * LOG2_E) * _sigmoid(w0_ref[...] + _dot(jnp.tanh(x_lora), wdec_ref[...]))
    a = _sigmoid(a0_ref[...] + _dot(x_lora, waaa_ref[...]))
    gate_tm = _dot(_sigmoid(x_gate), wgate_ref[...])
    kk = k * kk_ref[...]
    k = k * (a * ka_ref[...] + (1.0 - ka_ref[...]))
    kk = kk * lax.rsqrt(jnp.maximum(head_sum(kk * kk), 1e-24))
    bonus = head_sum(r * k * rk_ref[...]) * v
    acc = log2_decay[0:bb]
    cum = [acc]
    for t in range(1, chunk):
        acc = acc + log2_decay[t * bb:(t + 1) * bb]
        cum.append(acc)
    c = jnp.concatenate(cum, axis=0)
    inv_p = jnp.exp2(-c)
    kap = kk * jnp.exp2(c - log2_decay)
    rt = r * jnp.exp2(c)
    bt = kk * a * inv_p
    kt = k * inv_p
    p_end = jnp.exp2(cum[chunk - 1])
    for p in range(PAIRS):
        ls = slice(p * LANES, (p + 1) * LANES)
        kap_s[p, 0:rows, :] = kap[:, ls]
        rt_s[p, 0:rows, :] = rt[:, ls]
        bt_s[p, 0:rows, :] = bt[:, ls]
        kt_s[p, 0:rows, :] = kt[:, ls]
        v_s[p, 0:rows, :] = v[:, ls]
        pe_s[p] = p_end[:, ls]

    xb_prev = xb_carry[...]
    xc = cb_ref[...] + _shift_time(xb_prev, xb, 3, bb) * cw_ref[0:1, :]
    xc = xc + _shift_time(xb_prev, xb, 2, bb) * cw_ref[1:2, :]
    xc = xc + _shift_time(xb_prev, xb, 1, bb) * cw_ref[2:3, :]
    xc = xc + xb * cw_ref[3:4, :]
    xc_bf = xc.astype(BF16)

    gates = [jnp.dot(xc_bf[:, p * LANES:(p + 1) * LANES], wgates_ref[p], preferred_element_type=F32)
             for p in range(D_LRU // LANES)]
    r_g = _sigmoid(jnp.concatenate([g[:, :LANES] for g in gates], axis=-1) + ba_ref[...])
    i_g = _sigmoid(jnp.concatenate([g[:, LANES:] for g in gates], axis=-1) + bx_ref[...])
    la = jnp.exp2(r_g * ((-LRU_C * LOG2_E) * _softplus(-lam_ref[...])))
    gap = 1.0 - la * la
    mult = jnp.where(gap > 0.0, gap * lax.rsqrt(gap), 0.0)
    if pos0 == 0:
        mult = jnp.where(jnp.logical_and(row < bb, ti == 0), 1.0, mult)
    lb = xc * i_g * mult
    h = o_h_ref[...]
    hs = []
    for t in range(chunk):
        ts = slice(t * bb, (t + 1) * bb)
        h = la[ts] * h + lb[ts]
        hs.append(h)
    o_h_ref[...] = h
    y_lru = _rms(jnp.concatenate(hs, axis=0) * jax.nn.gelu(gate_lru), og_ref[...])

    wc = wkv_chunk
    n_lv = max(1, (chunk - 1).bit_length())
    cw2 = 2 * wc
    left = lax.broadcasted_iota(jnp.int32, (1, 1, LANES), 2) < TM_HEAD
    left_c = lax.broadcasted_iota(jnp.int32, (1, 1, cw2), 2) < wc
    ri = lax.broadcasted_iota(jnp.int32, (wc, cw2), 0)
    ci = lax.broadcasted_iota(jnp.int32, (wc, cw2), 1) & (wc - 1)
    strict = ri > ci
    incl = ri >= ci
    eye = (ri == ci).astype(F32)

    tsel = [pl.ds(b, wc, stride=bb) for b in range(bb)]
    gather = lambda ref: jnp.stack([ref.at[p][ts, :] for ts in tsel for p in range(PAIRS)])
    kap_g = gather(kap_s)
    rt_g = gather(rt_s)
    bt_g = gather(bt_s)
    kt_g = gather(kt_s)
    vv = gather(v_s)
    pe = jnp.stack([pe_s.at[p][pl.ds(b, 1), :] for b in range(bb) for p in range(PAIRS)])
    s0 = o_wkv_ref[...].reshape(bb * PAIRS, TM_HEAD, LANES)
    lhs = jnp.concatenate([kap_g, rt_g], axis=1)
    fused = _bdot_nt(lhs, jnp.concatenate([_block_diag_rows(bt_g, left), _block_diag_rows(kt_g, left),
                                           _block_diag_rows(s0, left)], axis=1))
    g_b = fused[:, :, 0:cw2]
    g_k = fused[:, :, cw2:2 * cw2]
    z = fused[:, :, 2 * cw2:]
    m_ab = jnp.where(strict, g_b[:, :wc], 0.0)
    m_ak = jnp.where(strict, g_k[:, :wc], 0.0)
    m_rb = jnp.where(incl, g_b[:, wc:], 0.0)
    m_rk = jnp.where(incl, g_k[:, wc:], 0.0)
    t_inv = eye - m_ab
    m_pow = _bdot(m_ab, _block_diag_rows(m_ab, left_c))
    for lv in range(1, n_lv):
        if lv < n_lv - 1:
            prod = _bdot(jnp.concatenate([t_inv, m_pow], axis=1), _block_diag_rows(m_pow, left_c))
            t_inv = t_inv + prod[:, :wc]
            m_pow = prod[:, wc:]
        else:
            t_inv = t_inv + _bdot(t_inv, _block_diag_rows(m_pow, left_c))
    xv = _bdot(jnp.concatenate([m_ak, m_rk], axis=1), _block_diag_rows(vv, left))
    uu = -_bdot(t_inv, _block_diag_rows(z[:, :wc] + xv[:, :wc], left))
    yh = z[:, wc:] + _bdot(m_rb, _block_diag_rows(uu, left)) + xv[:, wc:]
    full = _bdot_tn(jnp.concatenate([uu, vv], axis=1), jnp.concatenate([bt_g, kt_g], axis=1))
    upd = jnp.where(left, full[:, :TM_HEAD], full[:, TM_HEAD:])
    o_wkv_ref[...] = ((s0 + upd) * pe).reshape(bb, PAIRS, TM_HEAD, LANES)
    for b, ts in enumerate(tsel):
        for p in range(PAIRS):
            yt_s.at[p][ts, :] = yh[b * PAIRS + p]

    yv = jnp.concatenate([yt_s[p, 0:rows, :] for p in range(PAIRS)], axis=-1)
    mean = head_sum(yv) * (1.0 / TM_HEAD)
    cen = yv - mean
    var = head_sum(cen * cen) * (1.0 / TM_HEAD)
    y_tm = ((cen * lax.rsqrt(var + GN_EPS)) * gng_ref[...] + gnb_ref[...] + bonus) * gate_tm
    mixed = jnp.concatenate([y_tm.astype(BF16), y_lru.astype(BF16)], axis=-1)
    x1 = x_buf[x_slot].reshape(rows, D_MODEL) + jnp.dot(mixed, w_out_ref[...], preferred_element_type=F32)
    x1_ref[...] = x1.reshape(chunk, bb, D_MODEL)

    o_shift_ref[...] = u_tm[rows - bb:]
    o_conv_ref[...] = xb[rows - conv_rows:].reshape(LRU_CONV_W - 1, bb, D_LRU)
    tm_carry[...] = u_tm[rows - bb:]
    xb_carry[...] = xb[rows - conv_rows:]
    _read_chunk_end(x_stream, step, n_steps)


def _const_spec(shape):
    zeros = (0,) * len(shape)
    return pl.BlockSpec(shape, lambda bi, ti: zeros, pipeline_mode=pl.Buffered(1))


def _mixer(x, st_shift, st_wkv, st_conv, st_h, weights, *, bb, chunk, pos0):
    batch, length, _ = x.shape
    assert batch % bb == 0 and length % chunk == 0 and bb % SUBLANES == 0
    assert chunk & (chunk - 1) == 0 and chunk >= LRU_CONV_W - 1
    rows = bb * chunk
    grid = (batch // bb, length // chunk)
    act = pl.BlockSpec((chunk, bb, D_MODEL), lambda bi, ti: (ti, bi, 0))
    vec = lambda w: pl.BlockSpec((bb, w), lambda bi, ti: (bi, 0))
    wkv_spec = pl.BlockSpec((bb, PAIRS, TM_HEAD, LANES), lambda bi, ti: (bi, 0, 0, 0))
    conv_spec = pl.BlockSpec((LRU_CONV_W - 1, bb, D_LRU), lambda bi, ti: (0, bi, 0))
    in_specs = [pl.BlockSpec(memory_space=pl.ANY), vec(D_TM_PROJ), wkv_spec, conv_spec, vec(D_LRU)]
    in_specs += [_const_spec(w.shape) for w in weights]
    out_specs = [act, vec(D_TM_PROJ), wkv_spec, conv_spec, vec(D_LRU)]
    out_shape = [jax.ShapeDtypeStruct((length, batch, D_MODEL), F32),
                 jax.ShapeDtypeStruct((batch, D_TM_PROJ), F32),
                 jax.ShapeDtypeStruct((batch, PAIRS, TM_HEAD, LANES), F32),
                 jax.ShapeDtypeStruct((LRU_CONV_W - 1, batch, D_LRU), F32),
                 jax.ShapeDtypeStruct((batch, D_LRU), F32)]
    pair_rows = lambda n: pltpu.VMEM((PAIRS, n, LANES), F32)
    scratch = [pltpu.VMEM((bb, D_TM_PROJ), F32),
               pltpu.VMEM(((LRU_CONV_W - 1) * bb, D_LRU), F32)]
    scratch += [pair_rows(max(chunk, SUBLANES) * bb)] * 6 + [pair_rows(bb)]
    scratch += [pltpu.VMEM((READ_SLOTS, chunk, bb, D_MODEL), F32), pltpu.SemaphoreType.DMA((READ_SLOTS,))]
    return pl.pallas_call(
        functools.partial(_mixer_kernel, bb=bb, chunk=chunk, pos0=pos0,
                          n_blocks=grid[0], n_chunks=grid[1]),
        out_shape=out_shape, grid=grid, in_specs=in_specs, out_specs=out_specs,
        scratch_shapes=scratch, name="mixer",
        compiler_params=pltpu.CompilerParams(dimension_semantics=("arbitrary", "arbitrary"),
                                             vmem_limit_bytes=VMEM_LIMIT_BYTES),
    )(x, st_shift, st_wkv, st_conv, st_h, *weights)


def _ffn_kernel(x1_ref, st_conv_ref,
                g2_ref, w_up_ref, w_gate_ref, cw_ref, cb_ref, w_down_ref, gf_ref,
                y_hbm, o_conv_ref,
                up_carry, y_buf, y_sem,
                *, bb, chunk, n_blocks, n_chunks):
    rows = bb * chunk
    conv_rows = (FFN_CONV_W - 1) * bb
    ti = pl.program_id(1)
    step = pl.program_id(0) * n_chunks + ti
    n_steps = n_blocks * n_chunks
    y_stream = _TimeMajorStream(y_hbm, y_buf, y_sem, bb=bb, chunk=chunk, n_blocks=n_blocks,
                                n_chunks=n_chunks, to_hbm=True)
    slot = lax.rem(step, 2)

    @pl.when(ti == 0)
    def _():
        up_carry[...] = st_conv_ref[...].reshape(conv_rows, D_FF)

    x1 = x1_ref[...].reshape(rows, D_MODEL)
    xn = _rms(x1, g2_ref[...]).astype(BF16)
    acc = x1
    for n in range(D_FF // FF_COL_TILE):
        cols = slice(n * FF_COL_TILE, (n + 1) * FF_COL_TILE)
        up = jnp.dot(xn, w_up_ref[:, cols], preferred_element_type=F32)
        gate = jnp.dot(xn, w_gate_ref[:, cols], preferred_element_type=F32)
        up_prev = up_carry[:, cols]
        upc = cb_ref[:, cols] + _shift_time(up_prev, up, 2, bb) * cw_ref[0:1, cols]
        upc = upc + _shift_time(up_prev, up, 1, bb) * cw_ref[1:2, cols]
        upc = upc + up * cw_ref[2:3, cols]
        up_carry[:, cols] = up[rows - conv_rows:]
        o_conv_ref[:, :, cols] = up[rows - conv_rows:].reshape(FFN_CONV_W - 1, bb, FF_COL_TILE)
        hid = (jax.nn.gelu(upc) * gate).astype(BF16)
        acc = acc + jnp.dot(hid, w_down_ref[cols, :], preferred_element_type=F32)

    y_buf[slot] = _rms(acc, gf_ref[...]).reshape(chunk, bb, D_MODEL)
    y_stream.start(step, slot)

    @pl.when(step > 0)
    def _():
        y_stream.wait(step - 1, 1 - slot)

    @pl.when(step == n_steps - 1)
    def _():
        y_stream.wait(step, slot)


def _ffn(x1, st_conv, weights, *, bb, chunk):
    length, batch, _ = x1.shape
    assert batch % bb == 0 and length % chunk == 0 and bb % SUBLANES == 0
    assert chunk >= FFN_CONV_W - 1
    rows = bb * chunk
    grid = (batch // bb, length // chunk)
    act = pl.BlockSpec((chunk, bb, D_MODEL), lambda bi, ti: (ti, bi, 0))
    conv_spec = pl.BlockSpec((FFN_CONV_W - 1, bb, D_FF), lambda bi, ti: (0, bi, 0))
    in_specs = [act, conv_spec] + [_const_spec(w.shape) for w in weights]
    out_shape = [jax.ShapeDtypeStruct((batch, length, D_MODEL), F32),
                 jax.ShapeDtypeStruct((FFN_CONV_W - 1, batch, D_FF), F32)]
    scratch = [pltpu.VMEM(((FFN_CONV_W - 1) * bb, D_FF), F32),
               pltpu.VMEM((2, chunk, bb, D_MODEL), F32), pltpu.SemaphoreType.DMA((2,))]
    return pl.pallas_call(
        functools.partial(_ffn_kernel, bb=bb, chunk=chunk, n_blocks=grid[0], n_chunks=grid[1]),
        out_shape=out_shape, grid=grid, in_specs=in_specs,
        out_specs=[pl.BlockSpec(memory_space=pl.ANY), conv_spec],
        scratch_shapes=scratch, name="ffn",
        compiler_params=pltpu.CompilerParams(dimension_semantics=("arbitrary", "arbitrary"),
                                             vmem_limit_bytes=VMEM_LIMIT_BYTES),
    )(x1, st_conv, *weights)


def _ffn_state_kernel(x1_ref, g2_ref, w_up_ref, o_conv_ref, *, bb):
    tail = FFN_CONV_W - 1
    x1 = x1_ref[...].reshape(tail * bb, D_MODEL)
    up = _dot(_rms(x1, g2_ref[...]), w_up_ref[...])
    o_conv_ref[...] = up.reshape(tail, bb, D_FF)


def _ffn_state(x1, g2, w_up):
    length, batch, _ = x1.shape
    tail = FFN_CONV_W - 1
    assert length % tail == 0 and batch % SUBLANES == 0
    return pl.pallas_call(
        functools.partial(_ffn_state_kernel, bb=batch),
        out_shape=jax.ShapeDtypeStruct((tail, batch, D_FF), F32), grid=(1,),
        in_specs=[pl.BlockSpec((tail, batch, D_MODEL), lambda i: (length // tail - 1, 0, 0)),
                  pl.BlockSpec(g2.shape, lambda i: (0, 0)), pl.BlockSpec(w_up.shape, lambda i: (0, 0))],
        out_specs=pl.BlockSpec((tail, batch, D_FF), lambda i: (0, 0, 0)), name="ffn_state",
        compiler_params=pltpu.CompilerParams(dimension_semantics=("arbitrary",),
                                             vmem_limit_bytes=VMEM_LIMIT_BYTES),
    )(x1, g2, w_up)


def _row(v):
    return v.reshape(1, -1).astype(F32)


def _pair_block_diag(w):
    nb, n, _ = w.shape
    w = w.reshape(nb // 2, 2, n, n)
    eye = jnp.eye(2, dtype=w.dtype)
    return (eye[None, :, None, :, None] * w[:, :, :, None, :]).reshape(nb // 2, 2 * n, 2 * n)


def _wkv_to_pairs(s):
    s = s.astype(F32)
    return jnp.concatenate([s[:, 0::2], s[:, 1::2]], axis=-1)


def _wkv_from_pairs(s):
    b = s.shape[0]
    halves = jnp.stack([s[..., :TM_HEAD], s[..., TM_HEAD:]], axis=2)
    return halves.reshape(b, TM_HEADS, TM_HEAD, TM_HEAD)


def _layer(x, states, mixer_w, ffn_w, *, mixer_bb, ffn_bb, mixer_chunk, ffn_chunk, pos0):
    st_shift, st_wkv, st_conv, st_h, st_fconv = states
    x1, o_shift, o_wkv, o_conv, o_h = _mixer(x, st_shift, st_wkv, st_conv, st_h, mixer_w,
                                             bb=mixer_bb, chunk=mixer_chunk, pos0=pos0)
    y, o_fconv = _ffn(x1, st_fconv, ffn_w, bb=ffn_bb, chunk=ffn_chunk)
    return y, (o_shift, o_wkv, o_conv, o_h, o_fconv)


def kernel(x_prompt, x_sample, state_tm_shift, state_tm_wkv, state_lru_conv, state_lru_h, state_ffn_conv, meta_tokens, norm1_g, w_in, tm_mu, tm_w0, tm_w_up, tm_a0, tm_a_up, tm_g_up, tm_k_k, tm_k_a, tm_r_k, tm_gn_g, tm_gn_b, lru_conv_w, lru_conv_b, lru_wa, lru_ba, lru_wx, lru_bx, lru_lambda, lru_out_g, w_out, norm2_g, ffn_w_up, ffn_w_gate, ffn_conv_w, ffn_conv_b, ffn_w_down, norm_f_g):
    depth = w_in.shape[0]
    assert depth == 1
    l = 0
    zeros_lora = jnp.zeros((DECAY_RANK, D_TM), F32)
    head_id = jnp.arange(D_TM) // TM_HEAD
    seg = (head_id[:, None] == jnp.arange(LANES)[None, :]).astype(BF16)
    mixer_w = (
        _row(norm1_g[l]), w_in[l].astype(BF16), _row(tm_mu[l]), _row(tm_w0[l]),
        jnp.concatenate([tm_w_up[l], zeros_lora], axis=0).astype(BF16),
        _row(tm_a0[l]),
        jnp.concatenate([zeros_lora, tm_a_up[l]], axis=0).astype(BF16),
        tm_g_up[l].astype(BF16),
        _row(tm_k_k[l]), _row(tm_k_a[l]), _row(tm_r_k[l]), _row(tm_gn_g[l]), _row(tm_gn_b[l]),
        lru_conv_w[l].astype(F32), _row(lru_conv_b[l]),
        jnp.concatenate([_pair_block_diag(lru_wa[l]), _pair_block_diag(lru_wx[l])], axis=-1).astype(BF16),
        _row(lru_ba[l]), _row(lru_bx[l]),
        _row(lru_lambda[l]), _row(lru_out_g[l]),
        seg, seg.T, w_out[l].astype(BF16),
    )
    ffn_w = (
        _row(norm2_g[l]), ffn_w_up[l].astype(BF16), ffn_w_gate[l].astype(BF16),
        ffn_conv_w[l].astype(F32), _row(ffn_conv_b[l]), ffn_w_down[l].astype(BF16), _row(norm_f_g),
    )

    bsz, seq = x_prompt.shape[0], x_prompt.shape[1]
    x_meta = jnp.broadcast_to(meta_tokens[None].astype(F32), (bsz, N_META, D_MODEL))
    x1_meta, *meta_st = _mixer(x_meta, jnp.zeros((bsz, D_TM_PROJ), F32),
                               jnp.zeros((bsz, PAIRS, TM_HEAD, LANES), F32),
                               jnp.zeros((LRU_CONV_W - 1, bsz, D_LRU), F32),
                               jnp.zeros((bsz, D_LRU), F32), mixer_w,
                               bb=bsz, chunk=N_META, pos0=0)
    p_init = tuple(meta_st) + (_ffn_state(x1_meta, ffn_w[0], ffn_w[1]),)

    y_prompt, p_st = _layer(x_prompt, p_init, mixer_w, ffn_w,
                            mixer_bb=bsz, ffn_bb=bsz, mixer_chunk=64, ffn_chunk=64,
                            pos0=N_META)

    dec_b, dec_seq = x_sample.shape[0], x_sample.shape[1]
    s_init = (state_tm_shift[l].astype(F32), _wkv_to_pairs(state_tm_wkv[l]),
              jnp.transpose(state_lru_conv[l].astype(F32), (1, 0, 2)), state_lru_h[l].astype(F32),
              jnp.transpose(state_ffn_conv[l].astype(F32), (1, 0, 2)))
    y_sample, s_st = _layer(x_sample, s_init, mixer_w, ffn_w,
                            mixer_bb=32, ffn_bb=64, mixer_chunk=dec_seq, ffn_chunk=dec_seq,
                            pos0=PAST_LEN)

    def unpack(st):
        o_shift, o_wkv, o_conv, o_h, o_fconv = st
        return (o_shift[None], _wkv_from_pairs(o_wkv)[None], jnp.transpose(o_conv, (1, 0, 2))[None],
                o_h[None], jnp.transpose(o_fconv, (1, 0, 2))[None])

    return (y_prompt, y_sample) + unpack(p_st) + unpack(s_st)
```

```python
import functools
import math

import jax
import jax.numpy as jnp
from jax import lax
from jax.experimental import pallas as pl
from jax.experimental.pallas import tpu as pltpu

F32 = jnp.float32
BF16 = jnp.bfloat16

D_MODEL = 1024
N_META = 16
PAST_LEN = 16384
D_TM = 512
TM_HEAD = 64
TM_HEADS = 8
DECAY_RANK = 64
AAA_RANK = 64
GATE_RANK = 128
D_TM_PROJ = 3 * D_TM + DECAY_RANK + AAA_RANK + GATE_RANK
D_LRU = 512
LRU_CONV_W = 4
LRU_C = 8.0
D_IN_PROJ = D_TM_PROJ + 2 * D_LRU
D_FF = 3 * D_MODEL
FFN_CONV_W = 3
EPS = 1e-6
GN_EPS = 64e-5
LOG2_E = math.log2(math.e)

SUBLANES = 8
LANES = 128
PAIRS = TM_HEADS // 2
LORA_OFF = 3 * D_TM
GATE_OFF = LORA_OFF + DECAY_RANK + AAA_RANK
FF_COL_TILE = 1024
VMEM_LIMIT_BYTES = 60 * 1024 * 1024


def _dot(a, b):
    return jnp.dot(a.astype(BF16), b.astype(BF16), preferred_element_type=F32)


def _bdot(a, b):
    return lax.dot_general(a.astype(BF16), b.astype(BF16), (((2,), (1,)), ((0,), (0,))),
                           preferred_element_type=F32)


def _bdot_nt(a, b):
    return lax.dot_general(a.astype(BF16), b.astype(BF16), (((2,), (2,)), ((0,), (0,))),
                           preferred_element_type=F32)


def _bdot_tn(a, b):
    return lax.dot_general(a.astype(BF16), b.astype(BF16), (((1,), (1,)), ((0,), (0,))),
                           preferred_element_type=F32)


def _rms(x, g):
    return x * lax.rsqrt(jnp.mean(x * x, axis=-1, keepdims=True) + EPS) * g


def _sigmoid(z):
    return 0.5 * jnp.tanh(0.5 * z) + 0.5


def _softplus(z):
    return jnp.maximum(z, 0.0) + jnp.log(1.0 + jnp.exp(-jnp.abs(z)))


def _block_diag_rows(x, left):
    return jnp.concatenate([jnp.where(left, x, 0.0), jnp.where(left, 0.0, x)], axis=1)


def _shift_time(carry, x, steps, bb):
    n = steps * bb
    return jnp.concatenate([carry[carry.shape[0] - n:], x[:x.shape[0] - n]], axis=0)


class _TimeMajorStream:
    def __init__(self, hbm, buf, sem, *, bb, chunk, n_blocks, n_chunks, to_hbm):
        self.hbm, self.buf, self.sem = hbm, buf, sem
        self.bb, self.chunk, self.to_hbm = bb, chunk, to_hbm
        self.n_blocks, self.n_chunks = n_blocks, n_chunks

    def _copy(self, g, slot, i):
        if self.n_chunks == 1:
            bi, ti = g, 0
        elif self.n_blocks == 1:
            bi, ti = 0, g
        else:
            bi, ti = lax.div(g, self.n_chunks), lax.rem(g, self.n_chunks)
        if self.bb < self.chunk:
            hbm_rows = self.hbm.at[bi * self.bb + i, pl.ds(ti * self.chunk, self.chunk), :]
            tile = self.buf.at[slot, :, i, :]
        else:
            seqs = pl.ds(pl.multiple_of(bi * self.bb, SUBLANES), self.bb)
            hbm_rows = self.hbm.at[seqs, ti * self.chunk + i, :]
            tile = self.buf.at[slot, i]
        src, dst = (tile, hbm_rows) if self.to_hbm else (hbm_rows, tile)
        return pltpu.make_async_copy(src, dst, self.sem.at[slot])

    def start(self, g, slot):
        for i in range(min(self.bb, self.chunk)):
            self._copy(g, slot, i).start()

    def wait(self, g, slot):
        for i in range(min(self.bb, self.chunk)):
            self._copy(g, slot, i).wait()


READ_SLOTS = 3


def _read_chunk_begin(stream, g, n_steps):
    last = n_steps - 1

    @pl.when(g == 0)
    def _():
        stream.start(0, 0)
        stream.start(jnp.minimum(1, last), 1)

    slot = lax.rem(g, READ_SLOTS)
    stream.wait(g, slot)
    return slot


def _read_chunk_end(stream, g, n_steps):
    last = n_steps - 1
    stream.start(jnp.minimum(g + 2, last), lax.rem(g + 2, READ_SLOTS))

    @pl.when(g == last)
    def _():
        stream.wait(last, lax.rem(g + 1, READ_SLOTS))
        stream.wait(last, lax.rem(g + 2, READ_SLOTS))


def _mixer_kernel(x_hbm, st_shift_ref, st_wkv_ref, st_conv_ref, st_h_ref,
                  g1_ref, w_in_ref, mu_ref, w0_ref, wdec_ref, a0_ref, waaa_ref, wgate_ref,
                  kk_ref, ka_ref, rk_ref, gng_ref, gnb_ref,
                  cw_ref, cb_ref, wgates_ref, ba_ref, bx_ref, lam_ref, og_ref,
                  seg_ref, segt_ref, w_out_ref,
                  x1_ref, o_shift_ref, o_wkv_ref, o_conv_ref, o_h_ref,
                  tm_carry, xb_carry, kap_s, rt_s, bt_s, kt_s, v_s, yt_s, pe_s, x_buf, x_sem,
                  *, bb, chunk, pos0, n_blocks, n_chunks):
    rows = bb * chunk
    conv_rows = (LRU_CONV_W - 1) * bb
    wkv_chunk = max(chunk, SUBLANES)
    ti = pl.program_id(1)
    step = pl.program_id(0) * n_chunks + ti
    n_steps = n_blocks * n_chunks
    x_stream = _TimeMajorStream(x_hbm, x_buf, x_sem, bb=bb, chunk=chunk, n_blocks=n_blocks,
                                n_chunks=n_chunks, to_hbm=False)
    x_slot = _read_chunk_begin(x_stream, step, n_steps)

    @pl.when(ti == 0)
    def _():
        tm_carry[...] = st_shift_ref[...]
        xb_carry[...] = st_conv_ref[...].reshape(conv_rows, D_LRU)
        o_wkv_ref[...] = st_wkv_ref[...]
        o_h_ref[...] = st_h_ref[...]
        if wkv_chunk > chunk:
            for ref in (kap_s, rt_s, bt_s, kt_s, v_s):
                ref[:, rows:wkv_chunk * bb, :] = jnp.zeros((PAIRS, wkv_chunk * bb - rows, LANES), F32)

    seg = seg_ref[...]
    segt = segt_ref[...]

    def head_sum(x):
        s = jnp.dot(x.astype(BF16), seg, preferred_element_type=F32)
        return jnp.dot(s.astype(BF16), segt, preferred_element_type=F32)

    x = x_buf[x_slot].reshape(rows, D_MODEL)
    u = _dot(_rms(x, g1_ref[...]), w_in_ref[...])
    u_tm = u[:, :D_TM_PROJ]
    xb = u[:, D_TM_PROJ:D_TM_PROJ + D_LRU]
    gate_lru = u[:, D_TM_PROJ + D_LRU:D_IN_PROJ]

    row = lax.broadcasted_iota(jnp.int32, (rows, D_TM), 0)

    um = u_tm + (_shift_time(tm_carry[...], u_tm, 1, bb) - u_tm) * mu_ref[...]
    r = um[:, 0:D_TM]
    k = um[:, D_TM:2 * D_TM]
    v = um[:, 2 * D_TM:3 * D_TM]
    x_lora = um[:, LORA_OFF:GATE_OFF]
    x_gate = um[:, GATE_OFF:D_TM_PROJ]
    log2_decay = (-math.exp(-0.5) * LOG2_E) * _sigmoid(w0_ref[...] + _dot(jnp.tanh(x_lora), wdec_ref[...]))
    a = _sigmoid(a0_ref[...] + _dot(x_lora, waaa_ref[...]))
    gate_tm = _dot(_sigmoid(x_gate), wgate_ref[...])
    kk = k * kk_ref[...]
    k = k * (a * ka_ref[...] + (1.0 - ka_ref[...]))
    kk = kk * lax.rsqrt(jnp.maximum(head_sum(kk * kk), 1e-24))
    bonus = head_sum(r * k * rk_ref[...]) * v
    acc = log2_decay[0:bb]
    cum = [acc]
    for t in range(1, chunk):
        acc = acc + log2_decay[t * bb:(t + 1) * bb]
        cum.append(acc)
    c = jnp.concatenate(cum, axis=0)
    inv_p = jnp.exp2(-c)
    kap = kk * jnp.exp2(c - log2_decay)
    rt = r * jnp.exp2(c)
    bt = kk * a * inv_p
    kt = k * inv_p
    p_end = jnp.exp2(cum[chunk - 1])
    for p in range(PAIRS):
        ls = slice(p * LANES, (p + 1) * LANES)
        kap_s[p, 0:rows, :] = kap[:, ls]
        rt_s[p, 0:rows, :] = rt[:, ls]
        bt_s[p, 0:rows, :] = bt[:, ls]
        kt_s[p, 0:rows, :] = kt[:, ls]
        v_s[p, 0:rows, :] = v[:, ls]
        pe_s[p] = p_end[:, ls]

    xb_prev = xb_carry[...]
    xc = cb_ref[...] + _shift_time(xb_prev, xb, 3, bb) * cw_ref[0:1, :]
    xc = xc + _shift_time(xb_prev, xb, 2, bb) * cw_ref[1:2, :]
    xc = xc + _shift_time(xb_prev, xb, 1, bb) * cw_ref[2:3, :]
    xc = xc + xb * cw_ref[3:4, :]
    xc_bf = xc.astype(BF16)

    gates = [jnp.dot(xc_bf[:, p * LANES:(p + 1) * LANES], wgates_ref[p], preferred_element_type=F32)
             for p in range(D_LRU // LANES)]
    r_g = _sigmoid(jnp.concatenate([g[:, :LANES] for g in gates], axis=-1) + ba_ref[...])
    i_g = _sigmoid(jnp.concatenate([g[:, LANES:] for g in gates], axis=-1) + bx_ref[...])
    la = jnp.exp2(r_g * ((-LRU_C * LOG2_E) * _softplus(-lam_ref[...])))
    gap = 1.0 - la * la
    mult = jnp.where(gap > 0.0, gap * lax.rsqrt(gap), 0.0)
    if pos0 == 0:
        mult = jnp.where(jnp.logical_and(row < bb, ti == 0), 1.0, mult)
    lb = xc * i_g * mult
    h = o_h_ref[...]
    hs = []
    for t in range(chunk):
        ts = slice(t * bb, (t + 1) * bb)
        h = la[ts] * h + lb[ts]
        hs.append(h)
    o_h_ref[...] = h
    y_lru = _rms(jnp.concatenate(hs, axis=0) * jax.nn.gelu(gate_lru), og_ref[...])

    wc = wkv_chunk
    n_lv = max(1, (chunk - 1).bit_length())
    cw2 = 2 * wc
    left = lax.broadcasted_iota(jnp.int32, (1, 1, LANES), 2) < TM_HEAD
    left_c = lax.broadcasted_iota(jnp.int32, (1, 1, cw2), 2) < wc
    ri = lax.broadcasted_iota(jnp.int32, (wc, cw2), 0)
    ci = lax.broadcasted_iota(jnp.int32, (wc, cw2), 1) & (wc - 1)
    strict = ri > ci
    incl = ri >= ci
    eye = (ri == ci).astype(F32)

    tsel = [pl.ds(b, wc, stride=bb) for b in range(bb)]
    gather = lambda ref: jnp.stack([ref.at[p][ts, :] for ts in tsel for p in range(PAIRS)])
    kap_g = gather(kap_s)
    rt_g = gather(rt_s)
    bt_g = gather(bt_s)
    kt_g = gather(kt_s)
    vv = gather(v_s)
    pe = jnp.stack([pe_s.at[p][pl.ds(b, 1), :] for b in range(bb) for p in range(PAIRS)])
    s0 = o_wkv_ref[...].reshape(bb * PAIRS, TM_HEAD, LANES)
    lhs = jnp.concatenate([kap_g, rt_g], axis=1)
    fused = _bdot_nt(lhs, jnp.concatenate([_block_diag_rows(bt_g, left), _block_diag_rows(kt_g, left),
                                           _block_diag_rows(s0, left)], axis=1))
    g_b = fused[:, :, 0:cw2]
    g_k = fused[:, :, cw2:2 * cw2]
    z = fused[:, :, 2 * cw2:]
    m_ab = jnp.where(strict, g_b[:, :wc], 0.0)
    m_ak = jnp.where(strict, g_k[:, :wc], 0.0)
    m_rb = jnp.where(incl, g_b[:, wc:], 0.0)
    m_rk = jnp.where(incl, g_k[:, wc:], 0.0)
    t_inv = eye - m_ab
    m_pow = _bdot(m_ab, _block_diag_rows(m_ab, left_c))
    for lv in range(1, n_lv):
        if lv < n_lv - 1:
            prod = _bdot(jnp.concatenate([t_inv, m_pow], axis=1), _block_diag_rows(m_pow, left_c))
            t_inv = t_inv + prod[:, :wc]
            m_pow = prod[:, wc:]
        else:
            t_inv = t_inv + _bdot(t_inv, _block_diag_rows(m_pow, left_c))
    xv = _bdot(jnp.concatenate([m_ak, m_rk], axis=1), _block_diag_rows(vv, left))
    uu = -_bdot(t_inv, _block_diag_rows(z[:, :wc] + xv[:, :wc], left))
    yh = z[:, wc:] + _bdot(m_rb, _block_diag_rows(uu, left)) + xv[:, wc:]
    full = _bdot_tn(jnp.concatenate([uu, vv], axis=1), jnp.concatenate([bt_g, kt_g], axis=1))
    upd = jnp.where(left, full[:, :TM_HEAD], full[:, TM_HEAD:])
    o_wkv_ref[...] = ((s0 + upd) * pe).reshape(bb, PAIRS, TM_HEAD, LANES)
    for b, ts in enumerate(tsel):
        for p in range(PAIRS):
            yt_s.at[p][ts, :] = yh[b * PAIRS + p]

    yv = jnp.concatenate([yt_s[p, 0:rows, :] for p in range(PAIRS)], axis=-1)
    mean = head_sum(yv) * (1.0 / TM_HEAD)
    cen = yv - mean
    var = head_sum(cen * cen) * (1.0 / TM_HEAD)
    y_tm = ((cen * lax.rsqrt(var + GN_EPS)) * gng_ref[...] + gnb_ref[...] + bonus) * gate_tm
    mixed = jnp.concatenate([y_tm.astype(BF16), y_lru.astype(BF16)], axis=-1)
    x1 = x_buf[x_slot].reshape(rows, D_MODEL) + jnp.dot(mixed, w_out_ref[...], preferred_element_type=F32)
    x1_ref[...] = x1.reshape(chunk, bb, D_MODEL)

    o_shift_ref[...] = u_tm[rows - bb:]
    o_conv_ref[...] = xb[rows - conv_rows:].reshape(LRU_CONV_W - 1, bb, D_LRU)
    tm_carry[...] = u_tm[rows - bb:]
    xb_carry[...] = xb[rows - conv_rows:]
    _read_chunk_end(x_stream, step, n_steps)


def _const_spec(shape):
    zeros = (0,) * len(shape)
    return pl.BlockSpec(shape, lambda bi, ti: zeros, pipeline_mode=pl.Buffered(1))


def _mixer(x, st_shift, st_wkv, st_conv, st_h, weights, *, bb, chunk, pos0):
    batch, length, _ = x.shape
    assert batch % bb == 0 and length % chunk == 0 and bb % SUBLANES == 0
    assert chunk & (chunk - 1) == 0 and chunk >= LRU_CONV_W - 1
    rows = bb * chunk
    grid = (batch // bb, length // chunk)
    act = pl.BlockSpec((chunk, bb, D_MODEL), lambda bi, ti: (ti, bi, 0))
    vec = lambda w: pl.BlockSpec((bb, w), lambda bi, ti: (bi, 0))
    wkv_spec = pl.BlockSpec((bb, PAIRS, TM_HEAD, LANES), lambda bi, ti: (bi, 0, 0, 0))
    conv_spec = pl.BlockSpec((LRU_CONV_W - 1, bb, D_LRU), lambda bi, ti: (0, bi, 0))
    in_specs = [pl.BlockSpec(memory_space=pl.ANY), vec(D_TM_PROJ), wkv_spec, conv_spec, vec(D_LRU)]
    in_specs += [_const_spec(w.shape) for w in weights]
    out_specs = [act, vec(D_TM_PROJ), wkv_spec, conv_spec, vec(D_LRU)]
    out_shape = [jax.ShapeDtypeStruct((length, batch, D_MODEL), F32),
                 jax.ShapeDtypeStruct((batch, D_TM_PROJ), F32),
                 jax.ShapeDtypeStruct((batch, PAIRS, TM_HEAD, LANES), F32),
                 jax.ShapeDtypeStruct((LRU_CONV_W - 1, batch, D_LRU), F32),
                 jax.ShapeDtypeStruct((batch, D_LRU), F32)]
    pair_rows = lambda n: pltpu.VMEM((PAIRS, n, LANES), F32)
    scratch = [pltpu.VMEM((bb, D_TM_PROJ), F32),
               pltpu.VMEM(((LRU_CONV_W - 1) * bb, D_LRU), F32)]
    scratch += [pair_rows(max(chunk, SUBLANES) * bb)] * 6 + [pair_rows(bb)]
    scratch += [pltpu.VMEM((READ_SLOTS, chunk, bb, D_MODEL), F32), pltpu.SemaphoreType.DMA((READ_SLOTS,))]
    return pl.pallas_call(
        functools.partial(_mixer_kernel, bb=bb, chunk=chunk, pos0=pos0,
                          n_blocks=grid[0], n_chunks=grid[1]),
        out_shape=out_shape, grid=grid, in_specs=in_specs, out_specs=out_specs,
        scratch_shapes=scratch, name="mixer",
        compiler_params=pltpu.CompilerParams(dimension_semantics=("arbitrary", "arbitrary"),
                                             vmem_limit_bytes=VMEM_LIMIT_BYTES),
    )(x, st_shift, st_wkv, st_conv, st_h, *weights)


def _ffn_kernel(x1_ref, st_conv_ref,
                g2_ref, w_up_ref, w_gate_ref, cw_ref, cb_ref, w_down_ref, gf_ref,
                y_hbm, o_conv_ref,
                up_carry, y_buf, y_sem,
                *, bb, chunk, n_blocks, n_chunks):
    rows = bb * chunk
    conv_rows = (FFN_CONV_W - 1) * bb
    ti = pl.program_id(1)
    step = pl.program_id(0) * n_chunks + ti
    n_steps = n_blocks * n_chunks
    y_stream = _TimeMajorStream(y_hbm, y_buf, y_sem, bb=bb, chunk=chunk, n_blocks=n_blocks,
                                n_chunks=n_chunks, to_hbm=True)
    slot = lax.rem(step, 2)

    @pl.when(ti == 0)
    def _():
        up_carry[...] = st_conv_ref[...].reshape(conv_rows, D_FF)

    x1 = x1_ref[...].reshape(rows, D_MODEL)
    xn = _rms(x1, g2_ref[...]).astype(BF16)
    acc = x1
    for n in range(D_FF // FF_COL_TILE):
        cols = slice(n * FF_COL_TILE, (n + 1) * FF_COL_TILE)
        up = jnp.dot(xn, w_up_ref[:, cols], preferred_element_type=F32)
        gate = jnp.dot(xn, w_gate_ref[:, cols], preferred_element_type=F32)
        up_prev = up_carry[:, cols]
        upc = cb_ref[:, cols] + _shift_time(up_prev, up, 2, bb) * cw_ref[0:1, cols]
        upc = upc + _shift_time(up_prev, up, 1, bb) * cw_ref[1:2, cols]
        upc = upc + up * cw_ref[2:3, cols]
        up_carry[:, cols] = up[rows - conv_rows:]
        o_conv_ref[:, :, cols] = up[rows - conv_rows:].reshape(FFN_CONV_W - 1, bb, FF_COL_TILE)
        hid = (jax.nn.gelu(upc) * gate).astype(BF16)
        acc = acc + jnp.dot(hid, w_down_ref[cols, :], preferred_element_type=F32)

    y_buf[slot] = _rms(acc, gf_ref[...]).reshape(chunk, bb, D_MODEL)
    y_stream.start(step, slot)

    @pl.when(step > 0)
    def _():
        y_stream.wait(step - 1, 1 - slot)

    @pl.when(step == n_steps - 1)
    def _():
        y_stream.wait(step, slot)


def _ffn(x1, st_conv, weights, *, bb, chunk):
    length, batch, _ = x1.shape
    assert batch % bb == 0 and length % chunk == 0 and bb % SUBLANES == 0
    assert chunk >= FFN_CONV_W - 1
    rows = bb * chunk
    grid = (batch // bb, length // chunk)
    act = pl.BlockSpec((chunk, bb, D_MODEL), lambda bi, ti: (ti, bi, 0))
    conv_spec = pl.BlockSpec((FFN_CONV_W - 1, bb, D_FF), lambda bi, ti: (0, bi, 0))
    in_specs = [act, conv_spec] + [_const_spec(w.shape) for w in weights]
    out_shape = [jax.ShapeDtypeStruct((batch, length, D_MODEL), F32),
                 jax.ShapeDtypeStruct((FFN_CONV_W - 1, batch, D_FF), F32)]
    scratch = [pltpu.VMEM(((FFN_CONV_W - 1) * bb, D_FF), F32),
               pltpu.VMEM((2, chunk, bb, D_MODEL), F32), pltpu.SemaphoreType.DMA((2,))]
    return pl.pallas_call(
        functools.partial(_ffn_kernel, bb=bb, chunk=chunk, n_blocks=grid[0], n_chunks=grid[1]),
        out_shape=out_shape, grid=grid, in_specs=in_specs,
        out_specs=[pl.BlockSpec(memory_space=pl.ANY), conv_spec],
        scratch_shapes=scratch, name="ffn",
        compiler_params=pltpu.CompilerParams(dimension_semantics=("arbitrary", "arbitrary"),
                                             vmem_limit_bytes=VMEM_LIMIT_BYTES),
    )(x1, st_conv, *weights)


def _ffn_state_kernel(x1_ref, g2_ref, w_up_ref, o_conv_ref, *, bb):
    tail = FFN_CONV_W - 1
    x1 = x1_ref[...].reshape(tail * bb, D_MODEL)
    up = _dot(_rms(x1, g2_ref[...]), w_up_ref[...])
    o_conv_ref[...] = up.reshape(tail, bb, D_FF)


def _ffn_state(x1, g2, w_up):
    length, batch, _ = x1.shape
    tail = FFN_CONV_W - 1
    assert length % tail == 0 and batch % SUBLANES == 0
    return pl.pallas_call(
        functools.partial(_ffn_state_kernel, bb=batch),
        out_shape=jax.ShapeDtypeStruct((tail, batch, D_FF), F32), grid=(1,),
        in_specs=[pl.BlockSpec((tail, batch, D_MODEL), lambda i: (length // tail - 1, 0, 0)),
                  pl.BlockSpec(g2.shape, lambda i: (0, 0)), pl.BlockSpec(w_up.shape, lambda i: (0, 0))],
        out_specs=pl.BlockSpec((tail, batch, D_FF), lambda i: (0, 0, 0)), name="ffn_state",
        compiler_params=pltpu.CompilerParams(dimension_semantics=("arbitrary",),
                                             vmem_limit_bytes=VMEM_LIMIT_BYTES),
    )(x1, g2, w_up)


def _row(v):
    return v.reshape(1, -1).astype(F32)


def _pair_block_diag(w):
    nb, n, _ = w.shape
    w = w.reshape(nb // 2, 2, n, n)
    eye = jnp.eye(2, dtype=w.dtype)
    return (eye[None, :, None, :, None] * w[:, :, :, None, :]).reshape(nb // 2, 2 * n, 2 * n)


def _wkv_to_pairs(s):
    b = s.shape[0]
    s = s.astype(F32).reshape(b, PAIRS, 2, TM_HEAD, TM_HEAD)
    return jnp.transpose(s, (0, 1, 3, 2, 4)).reshape(b, PAIRS, TM_HEAD, LANES)


def _wkv_from_pairs(s):
    b = s.shape[0]
    s = s.reshape(b, PAIRS, TM_HEAD, 2, TM_HEAD)
    return jnp.transpose(s, (0, 1, 3, 2, 4)).reshape(b, TM_HEADS, TM_HEAD, TM_HEAD)


def _layer(x, states, mixer_w, ffn_w, *, mixer_bb, ffn_bb, mixer_chunk, ffn_chunk, pos0):
    st_shift, st_wkv, st_conv, st_h, st_fconv = states
    x1, o_shift, o_wkv, o_conv, o_h = _mixer(x, st_shift, st_wkv, st_conv, st_h, mixer_w,
                                             bb=mixer_bb, chunk=mixer_chunk, pos0=pos0)
    y, o_fconv = _ffn(x1, st_fconv, ffn_w, bb=ffn_bb, chunk=ffn_chunk)
    return y, (o_shift, o_wkv, o_conv, o_h, o_fconv)


def kernel(x_prompt, x_sample, state_tm_shift, state_tm_wkv, state_lru_conv, state_lru_h, state_ffn_conv, meta_tokens, norm1_g, w_in, tm_mu, tm_w0, tm_w_up, tm_a0, tm_a_up, tm_g_up, tm_k_k, tm_k_a, tm_r_k, tm_gn_g, tm_gn_b, lru_conv_w, lru_conv_b, lru_wa, lru_ba, lru_wx, lru_bx, lru_lambda, lru_out_g, w_out, norm2_g, ffn_w_up, ffn_w_gate, ffn_conv_w, ffn_conv_b, ffn_w_down, norm_f_g):
    depth = w_in.shape[0]
    assert depth == 1
    l = 0
    zeros_lora = jnp.zeros((DECAY_RANK, D_TM), F32)
    head_id = jnp.arange(D_TM) // TM_HEAD
    seg = (head_id[:, None] == jnp.arange(LANES)[None, :]).astype(BF16)
    mixer_w = (
        _row(norm1_g[l]), w_in[l].astype(BF16), _row(tm_mu[l]), _row(tm_w0[l]),
        jnp.concatenate([tm_w_up[l], zeros_lora], axis=0).astype(BF16),
        _row(tm_a0[l]),
        jnp.concatenate([zeros_lora, tm_a_up[l]], axis=0).astype(BF16),
        tm_g_up[l].astype(BF16),
        _row(tm_k_k[l]), _row(tm_k_a[l]), _row(tm_r_k[l]), _row(tm_gn_g[l]), _row(tm_gn_b[l]),
        lru_conv_w[l].astype(F32), _row(lru_conv_b[l]),
        jnp.concatenate([_pair_block_diag(lru_wa[l]), _pair_block_diag(lru_wx[l])], axis=-1).astype(BF16),
        _row(lru_ba[l]), _row(lru_bx[l]),
        _row(lru_lambda[l]), _row(lru_out_g[l]),
        seg, seg.T, w_out[l].astype(BF16),
    )
    ffn_w = (
        _row(norm2_g[l]), ffn_w_up[l].astype(BF16), ffn_w_gate[l].astype(BF16),
        ffn_conv_w[l].astype(F32), _row(ffn_conv_b[l]), ffn_w_down[l].astype(BF16), _row(norm_f_g),
    )

    bsz, seq = x_prompt.shape[0], x_prompt.shape[1]
    x_meta = jnp.broadcast_to(meta_tokens[None].astype(F32), (bsz, N_META, D_MODEL))
    x1_meta, *meta_st = _mixer(x_meta, jnp.zeros((bsz, D_TM_PROJ), F32),
                               jnp.zeros((bsz, PAIRS, TM_HEAD, LANES), F32),
                               jnp.zeros((LRU_CONV_W - 1, bsz, D_LRU), F32),
                               jnp.zeros((bsz, D_LRU), F32), mixer_w,
                               bb=bsz, chunk=N_META, pos0=0)
    p_init = tuple(meta_st) + (_ffn_state(x1_meta, ffn_w[0], ffn_w[1]),)

    y_prompt, p_st = _layer(x_prompt, p_init, mixer_w, ffn_w,
                            mixer_bb=bsz, ffn_bb=bsz, mixer_chunk=64, ffn_chunk=64,
                            pos0=N_META)

    dec_b, dec_seq = x_sample.shape[0], x_sample.shape[1]
    s_init = (state_tm_shift[l].astype(F32), _wkv_to_pairs(state_tm_wkv[l]),
              jnp.transpose(state_lru_conv[l].astype(F32), (1, 0, 2)), state_lru_h[l].astype(F32),
              jnp.transpose(state_ffn_conv[l].astype(F32), (1, 0, 2)))
    y_sample, s_st = _layer(x_sample, s_init, mixer_w, ffn_w,
                            mixer_bb=32, ffn_bb=64, mixer_chunk=dec_seq, ffn_chunk=dec_seq,
                            pos0=PAST_LEN)

    def unpack(st):
        o_shift, o_wkv, o_conv, o_h, o_fconv = st
        return (o_shift[None], _wkv_from_pairs(o_wkv)[None], jnp.transpose(o_conv, (1, 0, 2))[None],
                o_h[None], jnp.transpose(o_fconv, (1, 0, 2))[None])

    return (y_prompt, y_sample) + unpack(p_st) + unpack(s_st)
```

```python
import functools
import math

import jax
import jax.numpy as jnp
from jax import lax
from jax.experimental import pallas as pl
from jax.experimental.pallas import tpu as pltpu

F32 = jnp.float32
BF16 = jnp.bfloat16

D_MODEL = 1024
N_META = 16
PAST_LEN = 16384
D_TM = 512
TM_HEAD = 64
TM_HEADS = 8
DECAY_RANK = 64
AAA_RANK = 64
GATE_RANK = 128
D_TM_PROJ = 3 * D_TM + DECAY_RANK + AAA_RANK + GATE_RANK
D_LRU = 512
LRU_CONV_W = 4
LRU_C = 8.0
D_IN_PROJ = D_TM_PROJ + 2 * D_LRU
D_FF = 3 * D_MODEL
FFN_CONV_W = 3
EPS = 1e-6
GN_EPS = 64e-5
LOG2_E = math.log2(math.e)

SUBLANES = 8
LANES = 128
PAIRS = TM_HEADS // 2
LORA_OFF = 3 * D_TM
GATE_OFF = LORA_OFF + DECAY_RANK + AAA_RANK
FF_COL_TILE = 1024
VMEM_LIMIT_BYTES = 60 * 1024 * 1024


def _dot(a, b):
    return jnp.dot(a.astype(BF16), b.astype(BF16), preferred_element_type=F32)


def _bdot(a, b):
    return lax.dot_general(a.astype(BF16), b.astype(BF16), (((2,), (1,)), ((0,), (0,))),
                           preferred_element_type=F32)


def _bdot_nt(a, b):
    return lax.dot_general(a.astype(BF16), b.astype(BF16), (((2,), (2,)), ((0,), (0,))),
                           preferred_element_type=F32)


def _bdot_tn(a, b):
    return lax.dot_general(a.astype(BF16), b.astype(BF16), (((1,), (1,)), ((0,), (0,))),
                           preferred_element_type=F32)


def _rms(x, g):
    return x * lax.rsqrt(jnp.mean(x * x, axis=-1, keepdims=True) + EPS) * g


def _sigmoid(z):
    return 0.5 * jnp.tanh(0.5 * z) + 0.5


def _softplus(z):
    return jnp.maximum(z, 0.0) + jnp.log(1.0 + jnp.exp(-jnp.abs(z)))


def _block_diag_rows(x, left):
    return jnp.concatenate([jnp.where(left, x, 0.0), jnp.where(left, 0.0, x)], axis=1)


def _shift_time(carry, x, steps, bb):
    n = steps * bb
    return jnp.concatenate([carry[carry.shape[0] - n:], x[:x.shape[0] - n]], axis=0)


class _TimeMajorStream:
    def __init__(self, hbm, buf, sem, *, bb, chunk, n_blocks, n_chunks, to_hbm):
        self.hbm, self.buf, self.sem = hbm, buf, sem
        self.bb, self.chunk, self.to_hbm = bb, chunk, to_hbm
        self.n_blocks, self.n_chunks = n_blocks, n_chunks

    def _copy(self, g, slot, i):
        if self.n_chunks == 1:
            bi, ti = g, 0
        elif self.n_blocks == 1:
            bi, ti = 0, g
        else:
            bi, ti = lax.div(g, self.n_chunks), lax.rem(g, self.n_chunks)
        if self.bb < self.chunk:
            hbm_rows = self.hbm.at[bi * self.bb + i, pl.ds(ti * self.chunk, self.chunk), :]
            tile = self.buf.at[slot, :, i, :]
        else:
            seqs = pl.ds(pl.multiple_of(bi * self.bb, SUBLANES), self.bb)
            hbm_rows = self.hbm.at[seqs, ti * self.chunk + i, :]
            tile = self.buf.at[slot, i]
        src, dst = (tile, hbm_rows) if self.to_hbm else (hbm_rows, tile)
        return pltpu.make_async_copy(src, dst, self.sem.at[slot])

    def start(self, g, slot):
        for i in range(min(self.bb, self.chunk)):
            self._copy(g, slot, i).start()

    def wait(self, g, slot):
        for i in range(min(self.bb, self.chunk)):
            self._copy(g, slot, i).wait()


READ_SLOTS = 3


def _read_chunk_begin(stream, g, n_steps):
    last = n_steps - 1

    @pl.when(g == 0)
    def _():
        stream.start(0, 0)
        stream.start(jnp.minimum(1, last), 1)

    slot = lax.rem(g, READ_SLOTS)
    stream.wait(g, slot)
    return slot


def _read_chunk_end(stream, g, n_steps):
    last = n_steps - 1
    stream.start(jnp.minimum(g + 2, last), lax.rem(g + 2, READ_SLOTS))

    @pl.when(g == last)
    def _():
        stream.wait(last, lax.rem(g + 1, READ_SLOTS))
        stream.wait(last, lax.rem(g + 2, READ_SLOTS))


def _mixer_kernel(x_hbm, st_shift_ref, st_wkv_ref, st_conv_ref, st_h_ref,
                  g1_ref, w_in_ref, mu_ref, w0_ref, wdec_ref, a0_ref, waaa_ref, wgate_ref,
                  kk_ref, ka_ref, rk_ref, gng_ref, gnb_ref,
                  cw_ref, cb_ref, wgates_ref, ba_ref, bx_ref, lam_ref, og_ref,
                  seg_ref, segt_ref, w_out_ref,
                  x1_ref, o_shift_ref, o_wkv_ref, o_conv_ref, o_h_ref,
                  tm_carry, xb_carry, kap_s, rt_s, bt_s, kt_s, v_s, yt_s, pe_s, x_buf, x_sem,
                  *, bb, chunk, pos0, n_blocks, n_chunks):
    rows = bb * chunk
    conv_rows = (LRU_CONV_W - 1) * bb
    wkv_chunk = max(chunk, SUBLANES)
    ti = pl.program_id(1)
    step = pl.program_id(0) * n_chunks + ti
    n_steps = n_blocks * n_chunks
    x_stream = _TimeMajorStream(x_hbm, x_buf, x_sem, bb=bb, chunk=chunk, n_blocks=n_blocks,
                                n_chunks=n_chunks, to_hbm=False)
    x_slot = _read_chunk_begin(x_stream, step, n_steps)

    @pl.when(ti == 0)
    def _():
        tm_carry[...] = st_shift_ref[...]
        xb_carry[...] = st_conv_ref[...].reshape(conv_rows, D_LRU)
        o_wkv_ref[...] = st_wkv_ref[...]
        o_h_ref[...] = st_h_ref[...]
        if wkv_chunk > chunk:
            for ref in (kap_s, rt_s, bt_s, kt_s, v_s):
                ref[:, rows:wkv_chunk * bb, :] = jnp.zeros((PAIRS, wkv_chunk * bb - rows, LANES), F32)

    seg = seg_ref[...]
    segt = segt_ref[...]

    def head_sum(x):
        s = jnp.dot(x.astype(BF16), seg, preferred_element_type=F32)
        return jnp.dot(s.astype(BF16), segt, preferred_element_type=F32)

    x = x_buf[x_slot].reshape(rows, D_MODEL)
    u = _dot(_rms(x, g1_ref[...]), w_in_ref[...])
    u_tm = u[:, :D_TM_PROJ]
    xb = u[:, D_TM_PROJ:D_TM_PROJ + D_LRU]
    gate_lru = u[:, D_TM_PROJ + D_LRU:D_IN_PROJ]

    row = lax.broadcasted_iota(jnp.int32, (rows, D_TM), 0)

    um = u_tm + (_shift_time(tm_carry[...], u_tm, 1, bb) - u_tm) * mu_ref[...]
    r = um[:, 0:D_TM]
    k = um[:, D_TM:2 * D_TM]
    v = um[:, 2 * D_TM:3 * D_TM]
    x_lora = um[:, LORA_OFF:GATE_OFF]
    x_gate = um[:, GATE_OFF:D_TM_PROJ]
    log2_decay = (-math.exp(-0.5) * LOG2_E) * _sigmoid(w0_ref[...] + _dot(jnp.tanh(x_lora), wdec_ref[...]))
    a = _sigmoid(a0_ref[...] + _dot(x_lora, waaa_ref[...]))
    gate_tm = _dot(_sigmoid(x_gate), wgate_ref[...])
    kk = k * kk_ref[...]
    k = k * (a * ka_ref[...] + (1.0 - ka_ref[...]))
    kk = kk * lax.rsqrt(jnp.maximum(head_sum(kk * kk), 1e-24))
    bonus = head_sum(r * k * rk_ref[...]) * v
    acc = log2_decay[0:bb]
    cum = [acc]
    for t in range(1, chunk):
        acc = acc + log2_decay[t * bb:(t + 1) * bb]
        cum.append(acc)
    c = jnp.concatenate(cum, axis=0)
    inv_p = jnp.exp2(-c)
    kap = kk * jnp.exp2(c - log2_decay)
    rt = r * jnp.exp2(c)
    bt = kk * a * inv_p
    kt = k * inv_p
    p_end = jnp.exp2(cum[chunk - 1])
    for p in range(PAIRS):
        ls = slice(p * LANES, (p + 1) * LANES)
        kap_s[p, 0:rows, :] = kap[:, ls]
        rt_s[p, 0:rows, :] = rt[:, ls]
        bt_s[p, 0:rows, :] = bt[:, ls]
        kt_s[p, 0:rows, :] = kt[:, ls]
        v_s[p, 0:rows, :] = v[:, ls]
        pe_s[p] = p_end[:, ls]

    xb_prev = xb_carry[...]
    xc = cb_ref[...] + _shift_time(xb_prev, xb, 3, bb) * cw_ref[0:1, :]
    xc = xc + _shift_time(xb_prev, xb, 2, bb) * cw_ref[1:2, :]
    xc = xc + _shift_time(xb_prev, xb, 1, bb) * cw_ref[2:3, :]
    xc = xc + xb * cw_ref[3:4, :]
    xc_bf = xc.astype(BF16)

    gates = [jnp.dot(xc_bf[:, p * LANES:(p + 1) * LANES], wgates_ref[p], preferred_element_type=F32)
             for p in range(D_LRU // LANES)]
    r_g = _sigmoid(jnp.concatenate([g[:, :LANES] for g in gates], axis=-1) + ba_ref[...])
    i_g = _sigmoid(jnp.concatenate([g[:, LANES:] for g in gates], axis=-1) + bx_ref[...])
    la = jnp.exp2(r_g * ((-LRU_C * LOG2_E) * _softplus(-lam_ref[...])))
    gap = 1.0 - la * la
    mult = jnp.where(gap > 0.0, gap * lax.rsqrt(gap), 0.0)
    if pos0 == 0:
        mult = jnp.where(jnp.logical_and(row < bb, ti == 0), 1.0, mult)
    lb = xc * i_g * mult
    h = o_h_ref[...]
    hs = []
    for t in range(chunk):
        ts = slice(t * bb, (t + 1) * bb)
        h = la[ts] * h + lb[ts]
        hs.append(h)
    o_h_ref[...] = h
    y_lru = _rms(jnp.concatenate(hs, axis=0) * jax.nn.gelu(gate_lru), og_ref[...])

    wc = wkv_chunk
    n_lv = max(1, (chunk - 1).bit_length())
    cw2 = 2 * wc
    left = lax.broadcasted_iota(jnp.int32, (1, 1, LANES), 2) < TM_HEAD
    left_c = lax.broadcasted_iota(jnp.int32, (1, 1, cw2), 2) < wc
    ri = lax.broadcasted_iota(jnp.int32, (wc, cw2), 0)
    ci = lax.broadcasted_iota(jnp.int32, (wc, cw2), 1) & (wc - 1)
    strict = ri > ci
    incl = ri >= ci
    eye = (ri == ci).astype(F32)

    tsel = [pl.ds(b, wc, stride=bb) for b in range(bb)]
    gather = lambda ref: jnp.stack([ref.at[p][ts, :] for ts in tsel for p in range(PAIRS)])
    kap_g = gather(kap_s)
    rt_g = gather(rt_s)
    bt_g = gather(bt_s)
    kt_g = gather(kt_s)
    vv = gather(v_s)
    pe = jnp.stack([pe_s.at[p][pl.ds(b, 1), :] for b in range(bb) for p in range(PAIRS)])
    s0 = o_wkv_ref[...].reshape(bb * PAIRS, TM_HEAD, LANES)
    lhs = jnp.concatenate([kap_g, rt_g], axis=1)
    fused = _bdot_nt(lhs, jnp.concatenate([_block_diag_rows(bt_g, left), _block_diag_rows(kt_g, left),
                                           _block_diag_rows(s0, left)], axis=1))
    g_b = fused[:, :, 0:cw2]
    g_k = fused[:, :, cw2:2 * cw2]
    z = fused[:, :, 2 * cw2:]
    m_ab = jnp.where(strict, g_b[:, :wc], 0.0)
    m_ak = jnp.where(strict, g_k[:, :wc], 0.0)
    m_rb = jnp.where(incl, g_b[:, wc:], 0.0)
    m_rk = jnp.where(incl, g_k[:, wc:], 0.0)
    t_inv = eye - m_ab
    m_pow = _bdot(m_ab, _block_diag_rows(m_ab, left_c))
    for lv in range(1, n_lv):
        if lv < n_lv - 1:
            prod = _bdot(jnp.concatenate([t_inv, m_pow], axis=1), _block_diag_rows(m_pow, left_c))
            t_inv = t_inv + prod[:, :wc]
            m_pow = prod[:, wc:]
        else:
            t_inv = t_inv + _bdot(t_inv, _block_diag_rows(m_pow, left_c))
    xv = _bdot(jnp.concatenate([m_ak, m_rk], axis=1), _block_diag_rows(vv, left))
    uu = -_bdot(t_inv, _block_diag_rows(z[:, :wc] + xv[:, :wc], left))
    yh = z[:, wc:] + _bdot(m_rb, _block_diag_rows(uu, left)) + xv[:, wc:]
    full = _bdot_tn(jnp.concatenate([uu, vv], axis=1), jnp.concatenate([bt_g, kt_g], axis=1))
    upd = jnp.where(left, full[:, :TM_HEAD], full[:, TM_HEAD:])
    o_wkv_ref[...] = ((s0 + upd) * pe).reshape(bb, PAIRS, TM_HEAD, LANES)
    for b, ts in enumerate(tsel):
        for p in range(PAIRS):
            yt_s.at[p][ts, :] = yh[b * PAIRS + p]

    yv = jnp.concatenate([yt_s[p, 0:rows, :] for p in range(PAIRS)], axis=-1)
    mean = head_sum(yv) * (1.0 / TM_HEAD)
    cen = yv - mean
    var = head_sum(cen * cen) * (1.0 / TM_HEAD)
    y_tm = ((cen * lax.rsqrt(var + GN_EPS)) * gng_ref[...] + gnb_ref[...] + bonus) * gate_tm
    mixed = jnp.concatenate([y_tm.astype(BF16), y_lru.astype(BF16)], axis=-1)
    x1 = x_buf[x_slot].reshape(rows, D_MODEL) + jnp.dot(mixed, w_out_ref[...], preferred_element_type=F32)
    x1_ref[...] = x1.reshape(chunk, bb, D_MODEL)

    o_shift_ref[...] = u_tm[rows - bb:]
    o_conv_ref[...] = xb[rows - conv_rows:].reshape(LRU_CONV_W - 1, bb, D_LRU)
    tm_carry[...] = u_tm[rows - bb:]
    xb_carry[...] = xb[rows - conv_rows:]
    _read_chunk_end(x_stream, step, n_steps)


def _const_spec(shape):
    zeros = (0,) * len(shape)
    return pl.BlockSpec(shape, lambda bi, ti: zeros, pipeline_mode=pl.Buffered(1))


def _mixer(x, st_shift, st_wkv, st_conv, st_h, weights, *, bb, chunk, pos0):
    batch, length, _ = x.shape
    assert batch % bb == 0 and length % chunk == 0 and bb % SUBLANES == 0
    assert chunk & (chunk - 1) == 0 and chunk >= LRU_CONV_W - 1
    rows = bb * chunk
    grid = (batch // bb, length // chunk)
    act = pl.BlockSpec((chunk, bb, D_MODEL), lambda bi, ti: (ti, bi, 0))
    vec = lambda w: pl.BlockSpec((bb, w), lambda bi, ti: (bi, 0))
    wkv_spec = pl.BlockSpec((bb, PAIRS, TM_HEAD, LANES), lambda bi, ti: (bi, 0, 0, 0))
    conv_spec = pl.BlockSpec((LRU_CONV_W - 1, bb, D_LRU), lambda bi, ti: (0, bi, 0))
    in_specs = [pl.BlockSpec(memory_space=pl.ANY), vec(D_TM_PROJ), wkv_spec, conv_spec, vec(D_LRU)]
    in_specs += [_const_spec(w.shape) for w in weights]
    out_specs = [act, vec(D_TM_PROJ), wkv_spec, conv_spec, vec(D_LRU)]
    out_shape = [jax.ShapeDtypeStruct((length, batch, D_MODEL), F32),
                 jax.ShapeDtypeStruct((batch, D_TM_PROJ), F32),
                 jax.ShapeDtypeStruct((batch, PAIRS, TM_HEAD, LANES), F32),
                 jax.ShapeDtypeStruct((LRU_CONV_W - 1, batch, D_LRU), F32),
                 jax.ShapeDtypeStruct((batch, D_LRU), F32)]
    pair_rows = lambda n: pltpu.VMEM((PAIRS, n, LANES), F32)
    scratch = [pltpu.VMEM((bb, D_TM_PROJ), F32),
               pltpu.VMEM(((LRU_CONV_W - 1) * bb, D_LRU), F32)]
    scratch += [pair_rows(max(chunk, SUBLANES) * bb)] * 6 + [pair_rows(bb)]
    scratch += [pltpu.VMEM((READ_SLOTS, chunk, bb, D_MODEL), F32), pltpu.SemaphoreType.DMA((READ_SLOTS,))]
    return pl.pallas_call(
        functools.partial(_mixer_kernel, bb=bb, chunk=chunk, pos0=pos0,
                          n_blocks=grid[0], n_chunks=grid[1]),
        out_shape=out_shape, grid=grid, in_specs=in_specs, out_specs=out_specs,
        scratch_shapes=scratch, name="mixer",
        compiler_params=pltpu.CompilerParams(dimension_semantics=("arbitrary", "arbitrary"),
                                             vmem_limit_bytes=VMEM_LIMIT_BYTES),
    )(x, st_shift, st_wkv, st_conv, st_h, *weights)


def _ffn_kernel(x1_ref, st_conv_ref,
                g2_ref, w_up_ref, w_gate_ref, cw_ref, cb_ref, w_down_ref, gf_ref,
                y_hbm, o_conv_ref,
                up_carry, y_buf, y_sem,
                *, bb, chunk, n_blocks, n_chunks):
    rows = bb * chunk
    conv_rows = (FFN_CONV_W - 1) * bb
    ti = pl.program_id(1)
    step = pl.program_id(0) * n_chunks + ti
    n_steps = n_blocks * n_chunks
    y_stream = _TimeMajorStream(y_hbm, y_buf, y_sem, bb=bb, chunk=chunk, n_blocks=n_blocks,
                                n_chunks=n_chunks, to_hbm=True)
    slot = lax.rem(step, 2)

    @pl.when(ti == 0)
    def _():
        up_carry[...] = st_conv_ref[...].reshape(conv_rows, D_FF)

    x1 = x1_ref[...].reshape(rows, D_MODEL)
    xn = _rms(x1, g2_ref[...]).astype(BF16)
    acc = x1
    for n in range(D_FF // FF_COL_TILE):
        cols = slice(n * FF_COL_TILE, (n + 1) * FF_COL_TILE)
        up = jnp.dot(xn, w_up_ref[:, cols], preferred_element_type=F32)
        gate = jnp.dot(xn, w_gate_ref[:, cols], preferred_element_type=F32)
        up_prev = up_carry[:, cols]
        upc = cb_ref[:, cols] + _shift_time(up_prev, up, 2, bb) * cw_ref[0:1, cols]
        upc = upc + _shift_time(up_prev, up, 1, bb) * cw_ref[1:2, cols]
        upc = upc + up * cw_ref[2:3, cols]
        up_carry[:, cols] = up[rows - conv_rows:]
        o_conv_ref[:, :, cols] = up[rows - conv_rows:].reshape(FFN_CONV_W - 1, bb, FF_COL_TILE)
        hid = (jax.nn.gelu(upc) * gate).astype(BF16)
        acc = acc + jnp.dot(hid, w_down_ref[cols, :], preferred_element_type=F32)

    y_buf[slot] = _rms(acc, gf_ref[...]).reshape(chunk, bb, D_MODEL)
    y_stream.start(step, slot)

    @pl.when(step > 0)
    def _():
        y_stream.wait(step - 1, 1 - slot)

    @pl.when(step == n_steps - 1)
    def _():
        y_stream.wait(step, slot)


def _ffn(x1, st_conv, weights, *, bb, chunk):
    length, batch, _ = x1.shape
    assert batch % bb == 0 and length % chunk == 0 and bb % SUBLANES == 0
    assert chunk >= FFN_CONV_W - 1
    rows = bb * chunk
    grid = (batch // bb, length // chunk)
    act = pl.BlockSpec((chunk, bb, D_MODEL), lambda bi, ti: (ti, bi, 0))
    conv_spec = pl.BlockSpec((FFN_CONV_W - 1, bb, D_FF), lambda bi, ti: (0, bi, 0))
    in_specs = [act, conv_spec] + [_const_spec(w.shape) for w in weights]
    out_shape = [jax.ShapeDtypeStruct((batch, length, D_MODEL), F32),
                 jax.ShapeDtypeStruct((FFN_CONV_W - 1, batch, D_FF), F32)]
    scratch = [pltpu.VMEM(((FFN_CONV_W - 1) * bb, D_FF), F32),
               pltpu.VMEM((2, chunk, bb, D_MODEL), F32), pltpu.SemaphoreType.DMA((2,))]
    return pl.pallas_call(
        functools.partial(_ffn_kernel, bb=bb, chunk=chunk, n_blocks=grid[0], n_chunks=grid[1]),
        out_shape=out_shape, grid=grid, in_specs=in_specs,
        out_specs=[pl.BlockSpec(memory_space=pl.ANY), conv_spec],
        scratch_shapes=scratch, name="ffn",
        compiler_params=pltpu.CompilerParams(dimension_semantics=("arbitrary", "arbitrary"),
                                             vmem_limit_bytes=VMEM_LIMIT_BYTES),
    )(x1, st_conv, *weights)


def _ffn_state_kernel(x1_ref, g2_ref, w_up_ref, o_conv_ref, *, bb):
    tail = FFN_CONV_W - 1
    x1 = x1_ref[...].reshape(tail * bb, D_MODEL)
    up = _dot(_rms(x1, g2_ref[...]), w_up_ref[...])
    o_conv_ref[...] = up.reshape(tail, bb, D_FF)


def _ffn_state(x1, g2, w_up):
    length, batch, _ = x1.shape
    tail = FFN_CONV_W - 1
    assert length % tail == 0 and batch % SUBLANES == 0
    return pl.pallas_call(
        functools.partial(_ffn_state_kernel, bb=batch),
        out_shape=jax.ShapeDtypeStruct((tail, batch, D_FF), F32), grid=(1,),
        in_specs=[pl.BlockSpec((tail, batch, D_MODEL), lambda i: (length // tail - 1, 0, 0)),
                  pl.BlockSpec(g2.shape, lambda i: (0, 0)), pl.BlockSpec(w_up.shape, lambda i: (0, 0))],
        out_specs=pl.BlockSpec((tail, batch, D_FF), lambda i: (0, 0, 0)), name="ffn_state",
        compiler_params=pltpu.CompilerParams(dimension_semantics=("arbitrary",),
                                             vmem_limit_bytes=VMEM_LIMIT_BYTES),
    )(x1, g2, w_up)


RELAYOUT_ROWS = 32


def _pair_rows_copy(hbm, buf, sem, step, slot, i, *, n_vt, to_hbm):
    p, vi = lax.div(step, n_vt), lax.rem(step, n_vt)
    rows = hbm.at[:, p, vi * RELAYOUT_ROWS + i, :]
    tile = buf.at[slot, i]
    src, dst = (tile, rows) if to_hbm else (rows, tile)
    return pltpu.make_async_copy(src, dst, sem.at[slot])


def _pairs_from_lanes_kernel(a_ref, o_hbm, buf, sem, *, n_vt, n_steps):
    step = pl.program_id(0) * n_vt + pl.program_id(1)
    slot = lax.rem(step, 2)
    copy = functools.partial(_pair_rows_copy, o_hbm, buf, sem, n_vt=n_vt, to_hbm=True)
    batch = a_ref.shape[-1]
    for i in range(RELAYOUT_ROWS):
        buf[slot, i] = a_ref[:, i].reshape(LANES, batch).T
    for i in range(RELAYOUT_ROWS):
        copy(step, slot, i).start()

    @pl.when(step > 0)
    def _():
        for i in range(RELAYOUT_ROWS):
            copy(step - 1, 1 - slot, i).wait()

    @pl.when(step == n_steps - 1)
    def _():
        for i in range(RELAYOUT_ROWS):
            copy(step, slot, i).wait()


def _lanes_from_pairs_kernel(s_hbm, o_ref, buf, sem, *, n_vt, n_steps):
    step = pl.program_id(0) * n_vt + pl.program_id(1)
    slot = lax.rem(step, 2)
    last = n_steps - 1
    copy = functools.partial(_pair_rows_copy, s_hbm, buf, sem, n_vt=n_vt, to_hbm=False)

    @pl.when(step == 0)
    def _():
        for i in range(RELAYOUT_ROWS):
            copy(0, 0, i).start()

    for i in range(RELAYOUT_ROWS):
        copy(jnp.minimum(step + 1, last), 1 - slot, i).start()
    for i in range(RELAYOUT_ROWS):
        copy(step, slot, i).wait()
    batch = o_ref.shape[-1]
    for i in range(RELAYOUT_ROWS):
        o_ref[:, i] = buf[slot, i].T.reshape(2, TM_HEAD, batch)

    @pl.when(step == last)
    def _():
        for i in range(RELAYOUT_ROWS):
            copy(last, 1 - slot, i).wait()


def _wkv_relayout(x, *, to_pairs):
    batch = x.shape[-1] if to_pairs else x.shape[0]
    assert batch == LANES and TM_HEAD % RELAYOUT_ROWS == 0
    n_vt = TM_HEAD // RELAYOUT_ROWS
    lanes_shape = (TM_HEADS, TM_HEAD, TM_HEAD, batch)
    pairs_shape = (batch, PAIRS, TM_HEAD, LANES)
    lanes_spec = pl.BlockSpec((2, RELAYOUT_ROWS, TM_HEAD, batch), lambda p, vi: (p, vi, 0, 0))
    hbm_spec = pl.BlockSpec(memory_space=pl.ANY)
    body = _pairs_from_lanes_kernel if to_pairs else _lanes_from_pairs_kernel
    return pl.pallas_call(
        functools.partial(body, n_vt=n_vt, n_steps=PAIRS * n_vt),
        out_shape=jax.ShapeDtypeStruct(pairs_shape if to_pairs else lanes_shape, F32),
        grid=(PAIRS, n_vt),
        in_specs=[lanes_spec if to_pairs else hbm_spec],
        out_specs=hbm_spec if to_pairs else lanes_spec,
        scratch_shapes=[pltpu.VMEM((2, RELAYOUT_ROWS, batch, LANES), F32), pltpu.SemaphoreType.DMA((2,))],
        name="wkv_relayout",
        compiler_params=pltpu.CompilerParams(dimension_semantics=("arbitrary", "arbitrary")),
    )(x)


def _row(v):
    return v.reshape(1, -1).astype(F32)


def _pair_block_diag(w):
    nb, n, _ = w.shape
    w = w.reshape(nb // 2, 2, n, n)
    eye = jnp.eye(2, dtype=w.dtype)
    return (eye[None, :, None, :, None] * w[:, :, :, None, :]).reshape(nb // 2, 2 * n, 2 * n)


def _wkv_to_pairs(s):
    b = s.shape[0]
    if b == LANES:
        return _wkv_relayout(jnp.transpose(s.astype(F32), (1, 2, 3, 0)), to_pairs=True)
    s = s.astype(F32).reshape(b, PAIRS, 2, TM_HEAD, TM_HEAD)
    return jnp.transpose(s, (0, 1, 3, 2, 4)).reshape(b, PAIRS, TM_HEAD, LANES)


def _wkv_from_pairs(s):
    b = s.shape[0]
    if b == LANES:
        return jnp.transpose(_wkv_relayout(s, to_pairs=False), (3, 0, 1, 2))
    s = s.reshape(b, PAIRS, TM_HEAD, 2, TM_HEAD)
    return jnp.transpose(s, (0, 1, 3, 2, 4)).reshape(b, TM_HEADS, TM_HEAD, TM_HEAD)


def _layer(x, states, mixer_w, ffn_w, *, mixer_bb, ffn_bb, mixer_chunk, ffn_chunk, pos0):
    st_shift, st_wkv, st_conv, st_h, st_fconv = states
    x1, o_shift, o_wkv, o_conv, o_h = _mixer(x, st_shift, st_wkv, st_conv, st_h, mixer_w,
                                             bb=mixer_bb, chunk=mixer_chunk, pos0=pos0)
    y, o_fconv = _ffn(x1, st_fconv, ffn_w, bb=ffn_bb, chunk=ffn_chunk)
    return y, (o_shift, o_wkv, o_conv, o_h, o_fconv)


def kernel(x_prompt, x_sample, state_tm_shift, state_tm_wkv, state_lru_conv, state_lru_h, state_ffn_conv, meta_tokens, norm1_g, w_in, tm_mu, tm_w0, tm_w_up, tm_a0, tm_a_up, tm_g_up, tm_k_k, tm_k_a, tm_r_k, tm_gn_g, tm_gn_b, lru_conv_w, lru_conv_b, lru_wa, lru_ba, lru_wx, lru_bx, lru_lambda, lru_out_g, w_out, norm2_g, ffn_w_up, ffn_w_gate, ffn_conv_w, ffn_conv_b, ffn_w_down, norm_f_g):
    depth = w_in.shape[0]
    assert depth == 1
    l = 0
    zeros_lora = jnp.zeros((DECAY_RANK, D_TM), F32)
    head_id = jnp.arange(D_TM) // TM_HEAD
    seg = (head_id[:, None] == jnp.arange(LANES)[None, :]).astype(BF16)
    mixer_w = (
        _row(norm1_g[l]), w_in[l].astype(BF16), _row(tm_mu[l]), _row(tm_w0[l]),
        jnp.concatenate([tm_w_up[l], zeros_lora], axis=0).astype(BF16),
        _row(tm_a0[l]),
        jnp.concatenate([zeros_lora, tm_a_up[l]], axis=0).astype(BF16),
        tm_g_up[l].astype(BF16),
        _row(tm_k_k[l]), _row(tm_k_a[l]), _row(tm_r_k[l]), _row(tm_gn_g[l]), _row(tm_gn_b[l]),
        lru_conv_w[l].astype(F32), _row(lru_conv_b[l]),
        jnp.concatenate([_pair_block_diag(lru_wa[l]), _pair_block_diag(lru_wx[l])], axis=-1).astype(BF16),
        _row(lru_ba[l]), _row(lru_bx[l]),
        _row(lru_lambda[l]), _row(lru_out_g[l]),
        seg, seg.T, w_out[l].astype(BF16),
    )
    ffn_w = (
        _row(norm2_g[l]), ffn_w_up[l].astype(BF16), ffn_w_gate[l].astype(BF16),
        ffn_conv_w[l].astype(F32), _row(ffn_conv_b[l]), ffn_w_down[l].astype(BF16), _row(norm_f_g),
    )

    bsz, seq = x_prompt.shape[0], x_prompt.shape[1]
    x_meta = jnp.broadcast_to(meta_tokens[None].astype(F32), (bsz, N_META, D_MODEL))
    x1_meta, *meta_st = _mixer(x_meta, jnp.zeros((bsz, D_TM_PROJ), F32),
                               jnp.zeros((bsz, PAIRS, TM_HEAD, LANES), F32),
                               jnp.zeros((LRU_CONV_W - 1, bsz, D_LRU), F32),
                               jnp.zeros((bsz, D_LRU), F32), mixer_w,
                               bb=bsz, chunk=N_META, pos0=0)
    p_init = tuple(meta_st) + (_ffn_state(x1_meta, ffn_w[0], ffn_w[1]),)

    y_prompt, p_st = _layer(x_prompt, p_init, mixer_w, ffn_w,
                            mixer_bb=bsz, ffn_bb=bsz, mixer_chunk=64, ffn_chunk=64,
                            pos0=N_META)

    dec_b, dec_seq = x_sample.shape[0], x_sample.shape[1]
    s_init = (state_tm_shift[l].astype(F32), _wkv_to_pairs(state_tm_wkv[l]),
              jnp.transpose(state_lru_conv[l].astype(F32), (1, 0, 2)), state_lru_h[l].astype(F32),
              jnp.transpose(state_ffn_conv[l].astype(F32), (1, 0, 2)))
    y_sample, s_st = _layer(x_sample, s_init, mixer_w, ffn_w,
                            mixer_bb=32, ffn_bb=64, mixer_chunk=dec_seq, ffn_chunk=dec_seq,
                            pos0=PAST_LEN)

    def unpack(st):
        o_shift, o_wkv, o_conv, o_h, o_fconv = st
        return (o_shift[None], _wkv_from_pairs(o_wkv)[None], jnp.transpose(o_conv, (1, 0, 2))[None],
                o_h[None], jnp.transpose(o_fconv, (1, 0, 2))[None])

    return (y_prompt, y_sample) + unpack(p_st) + unpack(s_st)
```

```python
import functools
import math

import jax
import jax.numpy as jnp
from jax import lax
from jax.experimental import pallas as pl
from jax.experimental.pallas import tpu as pltpu

F32 = jnp.float32
BF16 = jnp.bfloat16

D_MODEL = 1024
N_META = 16
PAST_LEN = 16384
D_TM = 512
TM_HEAD = 64
TM_HEADS = 8
DECAY_RANK = 64
AAA_RANK = 64
GATE_RANK = 128
D_TM_PROJ = 3 * D_TM + DECAY_RANK + AAA_RANK + GATE_RANK
D_LRU = 512
LRU_CONV_W = 4
LRU_C = 8.0
D_IN_PROJ = D_TM_PROJ + 2 * D_LRU
D_FF = 3 * D_MODEL
FFN_CONV_W = 3
EPS = 1e-6
GN_EPS = 64e-5
LOG2_E = math.log2(math.e)

SUBLANES = 8
LANES = 128
PAIRS = TM_HEADS // 2
LORA_OFF = 3 * D_TM
GATE_OFF = LORA_OFF + DECAY_RANK + AAA_RANK
FF_COL_TILE = 1024
VMEM_LIMIT_BYTES = 60 * 1024 * 1024


def _dot(a, b):
    return jnp.dot(a.astype(BF16), b.astype(BF16), preferred_element_type=F32)


def _bdot(a, b):
    return lax.dot_general(a.astype(BF16), b.astype(BF16), (((2,), (1,)), ((0,), (0,))),
                           preferred_element_type=F32)


def _bdot_nt(a, b):
    return lax.dot_general(a.astype(BF16), b.astype(BF16), (((2,), (2,)), ((0,), (0,))),
                           preferred_element_type=F32)


def _bdot_tn(a, b):
    return lax.dot_general(a.astype(BF16), b.astype(BF16), (((1,), (1,)), ((0,), (0,))),
                           preferred_element_type=F32)


def _rms(x, g):
    return x * lax.rsqrt(jnp.mean(x * x, axis=-1, keepdims=True) + EPS) * g


def _sigmoid(z):
    return 0.5 * jnp.tanh(0.5 * z) + 0.5


def _softplus(z):
    return jnp.maximum(z, 0.0) + jnp.log(1.0 + jnp.exp(-jnp.abs(z)))


def _block_diag_rows(x, left):
    return jnp.concatenate([jnp.where(left, x, 0.0), jnp.where(left, 0.0, x)], axis=1)


def _shift_time(carry, x, steps, bb):
    n = steps * bb
    return jnp.concatenate([carry[carry.shape[0] - n:], x[:x.shape[0] - n]], axis=0)


class _TimeMajorStream:
    def __init__(self, hbm, buf, sem, *, bb, chunk, n_blocks, n_chunks, to_hbm):
        self.hbm, self.buf, self.sem = hbm, buf, sem
        self.bb, self.chunk, self.to_hbm = bb, chunk, to_hbm
        self.n_blocks, self.n_chunks = n_blocks, n_chunks

    def _copy(self, g, slot, i):
        if self.n_chunks == 1:
            bi, ti = g, 0
        elif self.n_blocks == 1:
            bi, ti = 0, g
        else:
            bi, ti = lax.div(g, self.n_chunks), lax.rem(g, self.n_chunks)
        if self.bb < self.chunk:
            hbm_rows = self.hbm.at[bi * self.bb + i, pl.ds(ti * self.chunk, self.chunk), :]
            tile = self.buf.at[slot, :, i, :]
        else:
            seqs = pl.ds(pl.multiple_of(bi * self.bb, SUBLANES), self.bb)
            hbm_rows = self.hbm.at[seqs, ti * self.chunk + i, :]
            tile = self.buf.at[slot, i]
        src, dst = (tile, hbm_rows) if self.to_hbm else (hbm_rows, tile)
        return pltpu.make_async_copy(src, dst, self.sem.at[slot])

    def start(self, g, slot):
        for i in range(min(self.bb, self.chunk)):
            self._copy(g, slot, i).start()

    def wait(self, g, slot):
        for i in range(min(self.bb, self.chunk)):
            self._copy(g, slot, i).wait()


READ_SLOTS = 3


def _read_chunk_begin(stream, g, n_steps):
    last = n_steps - 1

    @pl.when(g == 0)
    def _():
        stream.start(0, 0)
        stream.start(jnp.minimum(1, last), 1)

    slot = lax.rem(g, READ_SLOTS)
    stream.wait(g, slot)
    return slot


def _read_chunk_end(stream, g, n_steps):
    last = n_steps - 1
    stream.start(jnp.minimum(g + 2, last), lax.rem(g + 2, READ_SLOTS))

    @pl.when(g == last)
    def _():
        stream.wait(last, lax.rem(g + 1, READ_SLOTS))
        stream.wait(last, lax.rem(g + 2, READ_SLOTS))


def _mixer_kernel(x_hbm, st_shift_ref, st_wkv_ref, st_conv_ref, st_h_ref,
                  g1_ref, w_in_ref, mu_ref, w0_ref, wdec_ref, a0_ref, waaa_ref, wgate_ref,
                  kk_ref, ka_ref, rk_ref, gng_ref, gnb_ref,
                  cw_ref, cb_ref, wgates_ref, ba_ref, bx_ref, lam_ref, og_ref,
                  seg_ref, segt_ref, w_out_ref,
                  x1_ref, o_shift_ref, o_wkv_ref, o_conv_ref, o_h_ref,
                  tm_carry, xb_carry, kap_s, rt_s, bt_s, kt_s, v_s, yt_s, pe_s, x_buf, x_sem,
                  *, bb, chunk, pos0, n_blocks, n_chunks):
    rows = bb * chunk
    conv_rows = (LRU_CONV_W - 1) * bb
    wkv_chunk = max(chunk, SUBLANES)
    ti = pl.program_id(1)
    step = pl.program_id(0) * n_chunks + ti
    n_steps = n_blocks * n_chunks
    x_stream = _TimeMajorStream(x_hbm, x_buf, x_sem, bb=bb, chunk=chunk, n_blocks=n_blocks,
                                n_chunks=n_chunks, to_hbm=False)
    x_slot = _read_chunk_begin(x_stream, step, n_steps)

    @pl.when(ti == 0)
    def _():
        tm_carry[...] = st_shift_ref[...]
        xb_carry[...] = st_conv_ref[...].reshape(conv_rows, D_LRU)
        o_wkv_ref[...] = st_wkv_ref[...]
        o_h_ref[...] = st_h_ref[...]
        if wkv_chunk > chunk:
            for ref in (kap_s, rt_s, bt_s, kt_s, v_s):
                ref[:, rows:wkv_chunk * bb, :] = jnp.zeros((PAIRS, wkv_chunk * bb - rows, LANES), F32)

    seg = seg_ref[...]
    segt = segt_ref[...]

    def head_sum(x):
        s = jnp.dot(x.astype(BF16), seg, preferred_element_type=F32)
        return jnp.dot(s.astype(BF16), segt, preferred_element_type=F32)

    x = x_buf[x_slot].reshape(rows, D_MODEL)
    u = _dot(_rms(x, g1_ref[...]), w_in_ref[...])
    u_tm = u[:, :D_TM_PROJ]
    xb = u[:, D_TM_PROJ:D_TM_PROJ + D_LRU]
    gate_lru = u[:, D_TM_PROJ + D_LRU:D_IN_PROJ]

    row = lax.broadcasted_iota(jnp.int32, (rows, D_TM), 0)

    um = u_tm + (_shift_time(tm_carry[...], u_tm, 1, bb) - u_tm) * mu_ref[...]
    r = um[:, 0:D_TM]
    k = um[:, D_TM:2 * D_TM]
    v = um[:, 2 * D_TM:3 * D_TM]
    x_lora = um[:, LORA_OFF:GATE_OFF]
    x_gate = um[:, GATE_OFF:D_TM_PROJ]
    log2_decay = (-math.exp(-0.5) * LOG2_E) * _sigmoid(w0_ref[...] + _dot(jnp.tanh(x_lora), wdec_ref[...]))
    a = _sigmoid(a0_ref[...] + _dot(x_lora, waaa_ref[...]))
    gate_tm = _dot(_sigmoid(x_gate), wgate_ref[...])
    kk = k * kk_ref[...]
    k = k * (a * ka_ref[...] + (1.0 - ka_ref[...]))
    kk = kk * lax.rsqrt(jnp.maximum(head_sum(kk * kk), 1e-24))
    bonus = head_sum(r * k * rk_ref[...]) * v
    acc = log2_decay[0:bb]
    cum = [acc]
    for t in range(1, chunk):
        acc = acc + log2_decay[t * bb:(t + 1) * bb]
        cum.append(acc)
    c = jnp.concatenate(cum, axis=0)
    inv_p = jnp.exp2(-c)
    kap = kk * jnp.exp2(c - log2_decay)
    rt = r * jnp.exp2(c)
    bt = kk * a * inv_p
    kt = k * inv_p
    p_end = jnp.exp2(cum[chunk - 1])
    for p in range(PAIRS):
        ls = slice(p * LANES, (p + 1) * LANES)
        kap_s[p, 0:rows, :] = kap[:, ls]
        rt_s[p, 0:rows, :] = rt[:, ls]
        bt_s[p, 0:rows, :] = bt[:, ls]
        kt_s[p, 0:rows, :] = kt[:, ls]
        v_s[p, 0:rows, :] = v[:, ls]
        pe_s[p] = p_end[:, ls]

    xb_prev = xb_carry[...]
    xc = cb_ref[...] + _shift_time(xb_prev, xb, 3, bb) * cw_ref[0:1, :]
    xc = xc + _shift_time(xb_prev, xb, 2, bb) * cw_ref[1:2, :]
    xc = xc + _shift_time(xb_prev, xb, 1, bb) * cw_ref[2:3, :]
    xc = xc + xb * cw_ref[3:4, :]
    xc_bf = xc.astype(BF16)

    gates = [jnp.dot(xc_bf[:, p * LANES:(p + 1) * LANES], wgates_ref[p], preferred_element_type=F32)
             for p in range(D_LRU // LANES)]
    r_g = _sigmoid(jnp.concatenate([g[:, :LANES] for g in gates], axis=-1) + ba_ref[...])
    i_g = _sigmoid(jnp.concatenate([g[:, LANES:] for g in gates], axis=-1) + bx_ref[...])
    la = jnp.exp2(r_g * ((-LRU_C * LOG2_E) * _softplus(-lam_ref[...])))
    gap = 1.0 - la * la
    mult = jnp.where(gap > 0.0, gap * lax.rsqrt(gap), 0.0)
    if pos0 == 0:
        mult = jnp.where(jnp.logical_and(row < bb, ti == 0), 1.0, mult)
    lb = xc * i_g * mult
    h = o_h_ref[...]
    hs = []
    for t in range(chunk):
        ts = slice(t * bb, (t + 1) * bb)
        h = la[ts] * h + lb[ts]
        hs.append(h)
    o_h_ref[...] = h
    y_lru = _rms(jnp.concatenate(hs, axis=0) * jax.nn.gelu(gate_lru), og_ref[...])

    wc = wkv_chunk
    n_lv = max(1, (chunk - 1).bit_length())
    cw2 = 2 * wc
    left = lax.broadcasted_iota(jnp.int32, (1, 1, LANES), 2) < TM_HEAD
    left_c = lax.broadcasted_iota(jnp.int32, (1, 1, cw2), 2) < wc
    ri = lax.broadcasted_iota(jnp.int32, (wc, cw2), 0)
    ci = lax.broadcasted_iota(jnp.int32, (wc, cw2), 1) & (wc - 1)
    strict = ri > ci
    incl = ri >= ci
    eye = (ri == ci).astype(F32)

    tsel = [pl.ds(b, wc, stride=bb) for b in range(bb)]
    gather = lambda ref: jnp.stack([ref.at[p][ts, :] for ts in tsel for p in range(PAIRS)])
    kap_g = gather(kap_s)
    rt_g = gather(rt_s)
    bt_g = gather(bt_s)
    kt_g = gather(kt_s)
    vv = gather(v_s)
    pe = jnp.stack([pe_s.at[p][pl.ds(b, 1), :] for b in range(bb) for p in range(PAIRS)])
    s0 = o_wkv_ref[...].reshape(bb * PAIRS, TM_HEAD, LANES)
    lhs = jnp.concatenate([kap_g, rt_g], axis=1)
    fused = _bdot_nt(lhs, jnp.concatenate([_block_diag_rows(bt_g, left), _block_diag_rows(kt_g, left),
                                           _block_diag_rows(s0, left)], axis=1))
    g_b = fused[:, :, 0:cw2]
    g_k = fused[:, :, cw2:2 * cw2]
    z = fused[:, :, 2 * cw2:]
    m_ab = jnp.where(strict, g_b[:, :wc], 0.0)
    m_ak = jnp.where(strict, g_k[:, :wc], 0.0)
    m_rb = jnp.where(incl, g_b[:, wc:], 0.0)
    m_rk = jnp.where(incl, g_k[:, wc:], 0.0)
    t_inv = eye - m_ab
    m_pow = _bdot(m_ab, _block_diag_rows(m_ab, left_c))
    for lv in range(1, n_lv):
        if lv < n_lv - 1:
            prod = _bdot(jnp.concatenate([t_inv, m_pow], axis=1), _block_diag_rows(m_pow, left_c))
            t_inv = t_inv + prod[:, :wc]
            m_pow = prod[:, wc:]
        else:
            t_inv = t_inv + _bdot(t_inv, _block_diag_rows(m_pow, left_c))
    xv = _bdot(jnp.concatenate([m_ak, m_rk], axis=1), _block_diag_rows(vv, left))
    uu = -_bdot(t_inv, _block_diag_rows(z[:, :wc] + xv[:, :wc], left))
    yh = z[:, wc:] + _bdot(m_rb, _block_diag_rows(uu, left)) + xv[:, wc:]
    full = _bdot_tn(jnp.concatenate([uu, vv], axis=1), jnp.concatenate([bt_g, kt_g], axis=1))
    upd = jnp.where(left, full[:, :TM_HEAD], full[:, TM_HEAD:])
    o_wkv_ref[...] = ((s0 + upd) * pe).reshape(bb, PAIRS, TM_HEAD, LANES)
    for b, ts in enumerate(tsel):
        for p in range(PAIRS):
            yt_s.at[p][ts, :] = yh[b * PAIRS + p]

    yv = jnp.concatenate([yt_s[p, 0:rows, :] for p in range(PAIRS)], axis=-1)
    mean = head_sum(yv) * (1.0 / TM_HEAD)
    cen = yv - mean
    var = head_sum(cen * cen) * (1.0 / TM_HEAD)
    y_tm = ((cen * lax.rsqrt(var + GN_EPS)) * gng_ref[...] + gnb_ref[...] + bonus) * gate_tm
    mixed = jnp.concatenate([y_tm.astype(BF16), y_lru.astype(BF16)], axis=-1)
    x1 = x_buf[x_slot].reshape(rows, D_MODEL) + jnp.dot(mixed, w_out_ref[...], preferred_element_type=F32)
    x1_ref[...] = x1.reshape(chunk, bb, D_MODEL)

    o_shift_ref[...] = u_tm[rows - bb:]
    o_conv_ref[...] = xb[rows - conv_rows:].reshape(LRU_CONV_W - 1, bb, D_LRU)
    tm_carry[...] = u_tm[rows - bb:]
    xb_carry[...] = xb[rows - conv_rows:]
    _read_chunk_end(x_stream, step, n_steps)


def _const_spec(shape):
    zeros = (0,) * len(shape)
    return pl.BlockSpec(shape, lambda bi, ti: zeros, pipeline_mode=pl.Buffered(1))


def _mixer(x, st_shift, st_wkv, st_conv, st_h, weights, *, bb, chunk, pos0):
    batch, length, _ = x.shape
    assert batch % bb == 0 and length % chunk == 0 and bb % SUBLANES == 0
    assert chunk & (chunk - 1) == 0 and chunk >= LRU_CONV_W - 1
    rows = bb * chunk
    grid = (batch // bb, length // chunk)
    act = pl.BlockSpec((chunk, bb, D_MODEL), lambda bi, ti: (ti, bi, 0))
    vec = lambda w: pl.BlockSpec((bb, w), lambda bi, ti: (bi, 0))
    wkv_spec = pl.BlockSpec((bb, PAIRS, TM_HEAD, LANES), lambda bi, ti: (bi, 0, 0, 0))
    conv_spec = pl.BlockSpec((LRU_CONV_W - 1, bb, D_LRU), lambda bi, ti: (0, bi, 0))
    in_specs = [pl.BlockSpec(memory_space=pl.ANY), vec(D_TM_PROJ), wkv_spec, conv_spec, vec(D_LRU)]
    in_specs += [_const_spec(w.shape) for w in weights]
    out_specs = [act, vec(D_TM_PROJ), wkv_spec, conv_spec, vec(D_LRU)]
    out_shape = [jax.ShapeDtypeStruct((length, batch, D_MODEL), F32),
                 jax.ShapeDtypeStruct((batch, D_TM_PROJ), F32),
                 jax.ShapeDtypeStruct((batch, PAIRS, TM_HEAD, LANES), F32),
                 jax.ShapeDtypeStruct((LRU_CONV_W - 1, batch, D_LRU), F32),
                 jax.ShapeDtypeStruct((batch, D_LRU), F32)]
    pair_rows = lambda n: pltpu.VMEM((PAIRS, n, LANES), F32)
    scratch = [pltpu.VMEM((bb, D_TM_PROJ), F32),
               pltpu.VMEM(((LRU_CONV_W - 1) * bb, D_LRU), F32)]
    scratch += [pair_rows(max(chunk, SUBLANES) * bb)] * 6 + [pair_rows(bb)]
    scratch += [pltpu.VMEM((READ_SLOTS, chunk, bb, D_MODEL), F32), pltpu.SemaphoreType.DMA((READ_SLOTS,))]
    return pl.pallas_call(
        functools.partial(_mixer_kernel, bb=bb, chunk=chunk, pos0=pos0,
                          n_blocks=grid[0], n_chunks=grid[1]),
        out_shape=out_shape, grid=grid, in_specs=in_specs, out_specs=out_specs,
        scratch_shapes=scratch, name="mixer",
        compiler_params=pltpu.CompilerParams(dimension_semantics=("arbitrary", "arbitrary"),
                                             vmem_limit_bytes=VMEM_LIMIT_BYTES),
    )(x, st_shift, st_wkv, st_conv, st_h, *weights)


def _ffn_kernel(x1_ref, st_conv_ref,
                g2_ref, w_up_ref, w_gate_ref, cw_ref, cb_ref, w_down_ref, gf_ref,
                y_hbm, o_conv_ref,
                up_carry, y_buf, y_sem,
                *, bb, chunk, n_blocks, n_chunks):
    rows = bb * chunk
    conv_rows = (FFN_CONV_W - 1) * bb
    ti = pl.program_id(1)
    step = pl.program_id(0) * n_chunks + ti
    n_steps = n_blocks * n_chunks
    y_stream = _TimeMajorStream(y_hbm, y_buf, y_sem, bb=bb, chunk=chunk, n_blocks=n_blocks,
                                n_chunks=n_chunks, to_hbm=True)
    slot = lax.rem(step, 2)

    @pl.when(ti == 0)
    def _():
        up_carry[...] = st_conv_ref[...].reshape(conv_rows, D_FF)

    x1 = x1_ref[...].reshape(rows, D_MODEL)
    xn = _rms(x1, g2_ref[...]).astype(BF16)
    acc = x1
    for n in range(D_FF // FF_COL_TILE):
        cols = slice(n * FF_COL_TILE, (n + 1) * FF_COL_TILE)
        up = jnp.dot(xn, w_up_ref[:, cols], preferred_element_type=F32)
        gate = jnp.dot(xn, w_gate_ref[:, cols], preferred_element_type=F32)
        up_prev = up_carry[:, cols]
        upc = cb_ref[:, cols] + _shift_time(up_prev, up, 2, bb) * cw_ref[0:1, cols]
        upc = upc + _shift_time(up_prev, up, 1, bb) * cw_ref[1:2, cols]
        upc = upc + up * cw_ref[2:3, cols]
        up_carry[:, cols] = up[rows - conv_rows:]
        o_conv_ref[:, :, cols] = up[rows - conv_rows:].reshape(FFN_CONV_W - 1, bb, FF_COL_TILE)
        hid = (jax.nn.gelu(upc) * gate).astype(BF16)
        acc = acc + jnp.dot(hid, w_down_ref[cols, :], preferred_element_type=F32)

    y_buf[slot] = _rms(acc, gf_ref[...]).reshape(chunk, bb, D_MODEL)
    y_stream.start(step, slot)

    @pl.when(step > 0)
    def _():
        y_stream.wait(step - 1, 1 - slot)

    @pl.when(step == n_steps - 1)
    def _():
        y_stream.wait(step, slot)


def _ffn(x1, st_conv, weights, *, bb, chunk):
    length, batch, _ = x1.shape
    assert batch % bb == 0 and length % chunk == 0 and bb % SUBLANES == 0
    assert chunk >= FFN_CONV_W - 1
    rows = bb * chunk
    grid = (batch // bb, length // chunk)
    act = pl.BlockSpec((chunk, bb, D_MODEL), lambda bi, ti: (ti, bi, 0))
    conv_spec = pl.BlockSpec((FFN_CONV_W - 1, bb, D_FF), lambda bi, ti: (0, bi, 0))
    in_specs = [act, conv_spec] + [_const_spec(w.shape) for w in weights]
    out_shape = [jax.ShapeDtypeStruct((batch, length, D_MODEL), F32),
                 jax.ShapeDtypeStruct((FFN_CONV_W - 1, batch, D_FF), F32)]
    scratch = [pltpu.VMEM(((FFN_CONV_W - 1) * bb, D_FF), F32),
               pltpu.VMEM((2, chunk, bb, D_MODEL), F32), pltpu.SemaphoreType.DMA((2,))]
    return pl.pallas_call(
        functools.partial(_ffn_kernel, bb=bb, chunk=chunk, n_blocks=grid[0], n_chunks=grid[1]),
        out_shape=out_shape, grid=grid, in_specs=in_specs,
        out_specs=[pl.BlockSpec(memory_space=pl.ANY), conv_spec],
        scratch_shapes=scratch, name="ffn",
        compiler_params=pltpu.CompilerParams(dimension_semantics=("arbitrary", "arbitrary"),
                                             vmem_limit_bytes=VMEM_LIMIT_BYTES),
    )(x1, st_conv, *weights)


def _ffn_state_kernel(x1_ref, g2_ref, w_up_ref, o_conv_ref, *, bb):
    tail = FFN_CONV_W - 1
    x1 = x1_ref[...].reshape(tail * bb, D_MODEL)
    up = _dot(_rms(x1, g2_ref[...]), w_up_ref[...])
    o_conv_ref[...] = up.reshape(tail, bb, D_FF)


def _ffn_state(x1, g2, w_up):
    length, batch, _ = x1.shape
    tail = FFN_CONV_W - 1
    assert length % tail == 0 and batch % SUBLANES == 0
    return pl.pallas_call(
        functools.partial(_ffn_state_kernel, bb=batch),
        out_shape=jax.ShapeDtypeStruct((tail, batch, D_FF), F32), grid=(1,),
        in_specs=[pl.BlockSpec((tail, batch, D_MODEL), lambda i: (length // tail - 1, 0, 0)),
                  pl.BlockSpec(g2.shape, lambda i: (0, 0)), pl.BlockSpec(w_up.shape, lambda i: (0, 0))],
        out_specs=pl.BlockSpec((tail, batch, D_FF), lambda i: (0, 0, 0)), name="ffn_state",
        compiler_params=pltpu.CompilerParams(dimension_semantics=("arbitrary",),
                                             vmem_limit_bytes=VMEM_LIMIT_BYTES),
    )(x1, g2, w_up)


RELAYOUT_ROWS = 32


def _pair_rows_copy(hbm, buf, sem, step, slot, *, n_vt, to_hbm):
    p, vi = lax.div(step, n_vt), lax.rem(step, n_vt)
    rows = hbm.at[:, p, pl.ds(vi * RELAYOUT_ROWS, RELAYOUT_ROWS), :]
    src, dst = (buf.at[slot], rows) if to_hbm else (rows, buf.at[slot])
    return pltpu.make_async_copy(src, dst, sem.at[slot])


def _pairs_from_lanes_kernel(a_ref, o_hbm, buf, sem, *, n_vt, n_steps):
    step = pl.program_id(0) * n_vt + pl.program_id(1)
    slot = lax.rem(step, 2)
    copy = functools.partial(_pair_rows_copy, o_hbm, buf, sem, n_vt=n_vt, to_hbm=True)
    batch = a_ref.shape[-1]
    for i in range(RELAYOUT_ROWS):
        buf[slot, :, i, :] = a_ref[:, i].reshape(LANES, batch).T
    copy(step, slot).start()

    @pl.when(step > 0)
    def _():
        copy(step - 1, 1 - slot).wait()

    @pl.when(step == n_steps - 1)
    def _():
        copy(step, slot).wait()


def _lanes_from_pairs_kernel(s_hbm, o_ref, buf, sem, *, n_vt, n_steps):
    step = pl.program_id(0) * n_vt + pl.program_id(1)
    slot = lax.rem(step, 2)
    last = n_steps - 1
    copy = functools.partial(_pair_rows_copy, s_hbm, buf, sem, n_vt=n_vt, to_hbm=False)

    @pl.when(step == 0)
    def _():
        copy(0, 0).start()

    copy(jnp.minimum(step + 1, last), 1 - slot).start()
    copy(step, slot).wait()
    batch = o_ref.shape[-1]
    for i in range(RELAYOUT_ROWS):
        o_ref[:, i] = buf[slot, :, i, :].T.reshape(2, TM_HEAD, batch)

    @pl.when(step == last)
    def _():
        copy(last, 1 - slot).wait()


def _wkv_relayout(x, *, to_pairs):
    batch = x.shape[-1] if to_pairs else x.shape[0]
    assert batch == LANES and TM_HEAD % RELAYOUT_ROWS == 0
    n_vt = TM_HEAD // RELAYOUT_ROWS
    lanes_shape = (TM_HEADS, TM_HEAD, TM_HEAD, batch)
    pairs_shape = (batch, PAIRS, TM_HEAD, LANES)
    lanes_spec = pl.BlockSpec((2, RELAYOUT_ROWS, TM_HEAD, batch), lambda p, vi: (p, vi, 0, 0))
    hbm_spec = pl.BlockSpec(memory_space=pl.ANY)
    body = _pairs_from_lanes_kernel if to_pairs else _lanes_from_pairs_kernel
    return pl.pallas_call(
        functools.partial(body, n_vt=n_vt, n_steps=PAIRS * n_vt),
        out_shape=jax.ShapeDtypeStruct(pairs_shape if to_pairs else lanes_shape, F32),
        grid=(PAIRS, n_vt),
        in_specs=[lanes_spec if to_pairs else hbm_spec],
        out_specs=hbm_spec if to_pairs else lanes_spec,
        scratch_shapes=[pltpu.VMEM((2, batch, RELAYOUT_ROWS, LANES), F32), pltpu.SemaphoreType.DMA((2,))],
        name="wkv_relayout",
        compiler_params=pltpu.CompilerParams(dimension_semantics=("arbitrary", "arbitrary")),
    )(x)


def _row(v):
    return v.reshape(1, -1).astype(F32)


def _pair_block_diag(w):
    nb, n, _ = w.shape
    w = w.reshape(nb // 2, 2, n, n)
    eye = jnp.eye(2, dtype=w.dtype)
    return (eye[None, :, None, :, None] * w[:, :, :, None, :]).reshape(nb // 2, 2 * n, 2 * n)


def _wkv_to_pairs(s):
    b = s.shape[0]
    if b == LANES:
        return _wkv_relayout(jnp.transpose(s.astype(F32), (1, 2, 3, 0)), to_pairs=True)
    s = s.astype(F32).reshape(b, PAIRS, 2, TM_HEAD, TM_HEAD)
    return jnp.transpose(s, (0, 1, 3, 2, 4)).reshape(b, PAIRS, TM_HEAD, LANES)


def _wkv_from_pairs(s):
    b = s.shape[0]
    if b == LANES:
        return jnp.transpose(_wkv_relayout(s, to_pairs=False), (3, 0, 1, 2))
    s = s.reshape(b, PAIRS, TM_HEAD, 2, TM_HEAD)
    return jnp.transpose(s, (0, 1, 3, 2, 4)).reshape(b, TM_HEADS, TM_HEAD, TM_HEAD)


def _layer(x, states, mixer_w, ffn_w, *, mixer_bb, ffn_bb, mixer_chunk, ffn_chunk, pos0):
    st_shift, st_wkv, st_conv, st_h, st_fconv = states
    x1, o_shift, o_wkv, o_conv, o_h = _mixer(x, st_shift, st_wkv, st_conv, st_h, mixer_w,
                                             bb=mixer_bb, chunk=mixer_chunk, pos0=pos0)
    y, o_fconv = _ffn(x1, st_fconv, ffn_w, bb=ffn_bb, chunk=ffn_chunk)
    return y, (o_shift, o_wkv, o_conv, o_h, o_fconv)


def kernel(x_prompt, x_sample, state_tm_shift, state_tm_wkv, state_lru_conv, state_lru_h, state_ffn_conv, meta_tokens, norm1_g, w_in, tm_mu, tm_w0, tm_w_up, tm_a0, tm_a_up, tm_g_up, tm_k_k, tm_k_a, tm_r_k, tm_gn_g, tm_gn_b, lru_conv_w, lru_conv_b, lru_wa, lru_ba, lru_wx, lru_bx, lru_lambda, lru_out_g, w_out, norm2_g, ffn_w_up, ffn_w_gate, ffn_conv_w, ffn_conv_b, ffn_w_down, norm_f_g):
    depth = w_in.shape[0]
    assert depth == 1
    l = 0
    zeros_lora = jnp.zeros((DECAY_RANK, D_TM), F32)
    head_id = jnp.arange(D_TM) // TM_HEAD
    seg = (head_id[:, None] == jnp.arange(LANES)[None, :]).astype(BF16)
    mixer_w = (
        _row(norm1_g[l]), w_in[l].astype(BF16), _row(tm_mu[l]), _row(tm_w0[l]),
        jnp.concatenate([tm_w_up[l], zeros_lora], axis=0).astype(BF16),
        _row(tm_a0[l]),
        jnp.concatenate([zeros_lora, tm_a_up[l]], axis=0).astype(BF16),
        tm_g_up[l].astype(BF16),
        _row(tm_k_k[l]), _row(tm_k_a[l]), _row(tm_r_k[l]), _row(tm_gn_g[l]), _row(tm_gn_b[l]),
        lru_conv_w[l].astype(F32), _row(lru_conv_b[l]),
        jnp.concatenate([_pair_block_diag(lru_wa[l]), _pair_block_diag(lru_wx[l])], axis=-1).astype(BF16),
        _row(lru_ba[l]), _row(lru_bx[l]),
        _row(lru_lambda[l]), _row(lru_out_g[l]),
        seg, seg.T, w_out[l].astype(BF16),
    )
    ffn_w = (
        _row(norm2_g[l]), ffn_w_up[l].astype(BF16), ffn_w_gate[l].astype(BF16),
        ffn_conv_w[l].astype(F32), _row(ffn_conv_b[l]), ffn_w_down[l].astype(BF16), _row(norm_f_g),
    )

    bsz, seq = x_prompt.shape[0], x_prompt.shape[1]
    x_meta = jnp.broadcast_to(meta_tokens[None].astype(F32), (bsz, N_META, D_MODEL))
    x1_meta, *meta_st = _mixer(x_meta, jnp.zeros((bsz, D_TM_PROJ), F32),
                               jnp.zeros((bsz, PAIRS, TM_HEAD, LANES), F32),
                               jnp.zeros((LRU_CONV_W - 1, bsz, D_LRU), F32),
                               jnp.zeros((bsz, D_LRU), F32), mixer_w,
                               bb=bsz, chunk=N_META, pos0=0)
    p_init = tuple(meta_st) + (_ffn_state(x1_meta, ffn_w[0], ffn_w[1]),)

    y_prompt, p_st = _layer(x_prompt, p_init, mixer_w, ffn_w,
                            mixer_bb=bsz, ffn_bb=bsz, mixer_chunk=64, ffn_chunk=64,
                            pos0=N_META)

    dec_b, dec_seq = x_sample.shape[0], x_sample.shape[1]
    s_init = (state_tm_shift[l].astype(F32), _wkv_to_pairs(state_tm_wkv[l]),
              jnp.transpose(state_lru_conv[l].astype(F32), (1, 0, 2)), state_lru_h[l].astype(F32),
              jnp.transpose(state_ffn_conv[l].astype(F32), (1, 0, 2)))
    y_sample, s_st = _layer(x_sample, s_init, mixer_w, ffn_w,
                            mixer_bb=32, ffn_bb=64, mixer_chunk=dec_seq, ffn_chunk=dec_seq,
                            pos0=PAST_LEN)

    def unpack(st):
        o_shift, o_wkv, o_conv, o_h, o_fconv = st
        return (o_shift[None], _wkv_from_pairs(o_wkv)[None], jnp.transpose(o_conv, (1, 0, 2))[None],
                o_h[None], jnp.transpose(o_fconv, (1, 0, 2))[None])

    return (y_prompt, y_sample) + unpack(p_st) + unpack(s_st)
```

```python
import functools
import math

import jax
import jax.numpy as jnp
from jax import lax
from jax.experimental import pallas as pl
from jax.experimental.pallas import tpu as pltpu

F32 = jnp.float32
BF16 = jnp.bfloat16

D_MODEL = 1024
N_META = 16
PAST_LEN = 16384
D_TM = 512
TM_HEAD = 64
TM_HEADS = 8
DECAY_RANK = 64
AAA_RANK = 64
GATE_RANK = 128
D_TM_PROJ = 3 * D_TM + DECAY_RANK + AAA_RANK + GATE_RANK
D_LRU = 512
LRU_CONV_W = 4
LRU_C = 8.0
D_IN_PROJ = D_TM_PROJ + 2 * D_LRU
D_FF = 3 * D_MODEL
FFN_CONV_W = 3
EPS = 1e-6
GN_EPS = 64e-5
LOG2_E = math.log2(math.e)

SUBLANES = 8
LANES = 128
PAIRS = TM_HEADS // 2
LORA_OFF = 3 * D_TM
GATE_OFF = LORA_OFF + DECAY_RANK + AAA_RANK
FF_COL_TILE = 1024
VMEM_LIMIT_BYTES = 60 * 1024 * 1024


def _dot(a, b):
    return jnp.dot(a.astype(BF16), b.astype(BF16), preferred_element_type=F32)


def _bdot(a, b):
    return lax.dot_general(a.astype(BF16), b.astype(BF16), (((2,), (1,)), ((0,), (0,))),
                           preferred_element_type=F32)


def _bdot_nt(a, b):
    return lax.dot_general(a.astype(BF16), b.astype(BF16), (((2,), (2,)), ((0,), (0,))),
                           preferred_element_type=F32)


def _bdot_tn(a, b):
    return lax.dot_general(a.astype(BF16), b.astype(BF16), (((1,), (1,)), ((0,), (0,))),
                           preferred_element_type=F32)


def _rms(x, g):
    return x * lax.rsqrt(jnp.mean(x * x, axis=-1, keepdims=True) + EPS) * g


def _sigmoid(z):
    return 0.5 * jnp.tanh(0.5 * z) + 0.5


def _softplus(z):
    return jnp.maximum(z, 0.0) + jnp.log(1.0 + jnp.exp(-jnp.abs(z)))


def _block_diag_rows(x, left):
    return jnp.concatenate([jnp.where(left, x, 0.0), jnp.where(left, 0.0, x)], axis=1)


def _shift_time(carry, x, steps, bb):
    n = steps * bb
    return jnp.concatenate([carry[carry.shape[0] - n:], x[:x.shape[0] - n]], axis=0)


class _TimeMajorStream:
    def __init__(self, hbm, buf, sem, *, bb, chunk, n_blocks, n_chunks, to_hbm):
        self.hbm, self.buf, self.sem = hbm, buf, sem
        self.bb, self.chunk, self.to_hbm = bb, chunk, to_hbm
        self.n_blocks, self.n_chunks = n_blocks, n_chunks

    def _copy(self, g, slot, i):
        if self.n_chunks == 1:
            bi, ti = g, 0
        elif self.n_blocks == 1:
            bi, ti = 0, g
        else:
            bi, ti = lax.div(g, self.n_chunks), lax.rem(g, self.n_chunks)
        if self.bb < self.chunk:
            hbm_rows = self.hbm.at[bi * self.bb + i, pl.ds(ti * self.chunk, self.chunk), :]
            tile = self.buf.at[slot, :, i, :]
        else:
            seqs = pl.ds(pl.multiple_of(bi * self.bb, SUBLANES), self.bb)
            hbm_rows = self.hbm.at[seqs, ti * self.chunk + i, :]
            tile = self.buf.at[slot, i]
        src, dst = (tile, hbm_rows) if self.to_hbm else (hbm_rows, tile)
        return pltpu.make_async_copy(src, dst, self.sem.at[slot])

    def start(self, g, slot):
        for i in range(min(self.bb, self.chunk)):
            self._copy(g, slot, i).start()

    def wait(self, g, slot):
        for i in range(min(self.bb, self.chunk)):
            self._copy(g, slot, i).wait()


READ_SLOTS = 3


def _read_chunk_begin(stream, g, n_steps):
    last = n_steps - 1

    @pl.when(g == 0)
    def _():
        stream.start(0, 0)
        stream.start(jnp.minimum(1, last), 1)

    slot = lax.rem(g, READ_SLOTS)
    stream.wait(g, slot)
    return slot


def _read_chunk_end(stream, g, n_steps):
    last = n_steps - 1
    stream.start(jnp.minimum(g + 2, last), lax.rem(g + 2, READ_SLOTS))

    @pl.when(g == last)
    def _():
        stream.wait(last, lax.rem(g + 1, READ_SLOTS))
        stream.wait(last, lax.rem(g + 2, READ_SLOTS))


def _mixer_kernel(x_hbm, st_shift_ref, st_wkv_ref, st_conv_ref, st_h_ref,
                  g1_ref, w_in_ref, mu_ref, w0_ref, wdec_ref, a0_ref, waaa_ref, wgate_ref,
                  kk_ref, ka_ref, rk_ref, gng_ref, gnb_ref,
                  cw_ref, cb_ref, wgates_ref, ba_ref, bx_ref, lam_ref, og_ref,
                  seg_ref, segt_ref, w_out_ref,
                  x1_ref, o_shift_ref, o_wkv_ref, o_conv_ref, o_h_ref,
                  tm_carry, xb_carry, kap_s, rt_s, bt_s, kt_s, v_s, yt_s, pe_s, x_buf, x_sem,
                  *, bb, chunk, pos0, n_blocks, n_chunks):
    rows = bb * chunk
    conv_rows = (LRU_CONV_W - 1) * bb
    wkv_chunk = max(chunk, SUBLANES)
    ti = pl.program_id(1)
    step = pl.program_id(0) * n_chunks + ti
    n_steps = n_blocks * n_chunks
    x_stream = _TimeMajorStream(x_hbm, x_buf, x_sem, bb=bb, chunk=chunk, n_blocks=n_blocks,
                                n_chunks=n_chunks, to_hbm=False)
    x_slot = _read_chunk_begin(x_stream, step, n_steps)

    @pl.when(ti == 0)
    def _():
        tm_carry[...] = st_shift_ref[...]
        xb_carry[...] = st_conv_ref[...].reshape(conv_rows, D_LRU)
        o_wkv_ref[...] = st_wkv_ref[...]
        o_h_ref[...] = st_h_ref[...]
        if wkv_chunk > chunk:
            for ref in (kap_s, rt_s, bt_s, kt_s, v_s):
                ref[:, rows:wkv_chunk * bb, :] = jnp.zeros((PAIRS, wkv_chunk * bb - rows, LANES), F32)

    seg = seg_ref[...]
    segt = segt_ref[...]

    def head_sum(x):
        s = jnp.dot(x.astype(BF16), seg, preferred_element_type=F32)
        return jnp.dot(s.astype(BF16), segt, preferred_element_type=F32)

    x = x_buf[x_slot].reshape(rows, D_MODEL)
    u = _dot(_rms(x, g1_ref[...]), w_in_ref[...])
    u_tm = u[:, :D_TM_PROJ]
    xb = u[:, D_TM_PROJ:D_TM_PROJ + D_LRU]
    gate_lru = u[:, D_TM_PROJ + D_LRU:D_IN_PROJ]

    row = lax.broadcasted_iota(jnp.int32, (rows, D_TM), 0)

    um = u_tm + (_shift_time(tm_carry[...], u_tm, 1, bb) - u_tm) * mu_ref[...]
    r = um[:, 0:D_TM]
    k = um[:, D_TM:2 * D_TM]
    v = um[:, 2 * D_TM:3 * D_TM]
    x_lora = um[:, LORA_OFF:GATE_OFF]
    x_gate = um[:, GATE_OFF:D_TM_PROJ]
    log2_decay = (-math.exp(-0.5) * LOG2_E) * _sigmoid(w0_ref[...] + _dot(jnp.tanh(x_lora), wdec_ref[...]))
    a = _sigmoid(a0_ref[...] + _dot(x_lora, waaa_ref[...]))
    gate_tm = _dot(_sigmoid(x_gate), wgate_ref[...])
    kk = k * kk_ref[...]
    k = k * (a * ka_ref[...] + (1.0 - ka_ref[...]))
    kk = kk * lax.rsqrt(jnp.maximum(head_sum(kk * kk), 1e-24))
    bonus = head_sum(r * k * rk_ref[...]) * v
    acc = log2_decay[0:bb]
    cum = [acc]
    for t in range(1, chunk):
        acc = acc + log2_decay[t * bb:(t + 1) * bb]
        cum.append(acc)
    c = jnp.concatenate(cum, axis=0)
    inv_p = jnp.exp2(-c)
    kap = kk * jnp.exp2(c - log2_decay)
    rt = r * jnp.exp2(c)
    bt = kk * a * inv_p
    kt = k * inv_p
    p_end = jnp.exp2(cum[chunk - 1])
    for p in range(PAIRS):
        ls = slice(p * LANES, (p + 1) * LANES)
        kap_s[p, 0:rows, :] = kap[:, ls]
        rt_s[p, 0:rows, :] = rt[:, ls]
        bt_s[p, 0:rows, :] = bt[:, ls]
        kt_s[p, 0:rows, :] = kt[:, ls]
        v_s[p, 0:rows, :] = v[:, ls]
        pe_s[p] = p_end[:, ls]

    xb_prev = xb_carry[...]
    xc = cb_ref[...] + _shift_time(xb_prev, xb, 3, bb) * cw_ref[0:1, :]
    xc = xc + _shift_time(xb_prev, xb, 2, bb) * cw_ref[1:2, :]
    xc = xc + _shift_time(xb_prev, xb, 1, bb) * cw_ref[2:3, :]
    xc = xc + xb * cw_ref[3:4, :]
    xc_bf = xc.astype(BF16)

    gates = [jnp.dot(xc_bf[:, p * LANES:(p + 1) * LANES], wgates_ref[p], preferred_element_type=F32)
             for p in range(D_LRU // LANES)]
    r_g = _sigmoid(jnp.concatenate([g[:, :LANES] for g in gates], axis=-1) + ba_ref[...])
    i_g = _sigmoid(jnp.concatenate([g[:, LANES:] for g in gates], axis=-1) + bx_ref[...])
    la = jnp.exp2(r_g * ((-LRU_C * LOG2_E) * _softplus(-lam_ref[...])))
    gap = 1.0 - la * la
    mult = jnp.where(gap > 0.0, gap * lax.rsqrt(gap), 0.0)
    if pos0 == 0:
        mult = jnp.where(jnp.logical_and(row < bb, ti == 0), 1.0, mult)
    lb = xc * i_g * mult
    h = o_h_ref[...]
    hs = []
    for t in range(chunk):
        ts = slice(t * bb, (t + 1) * bb)
        h = la[ts] * h + lb[ts]
        hs.append(h)
    o_h_ref[...] = h
    y_lru = _rms(jnp.concatenate(hs, axis=0) * jax.nn.gelu(gate_lru), og_ref[...])

    wc = wkv_chunk
    n_lv = max(1, (chunk - 1).bit_length())
    cw2 = 2 * wc
    left = lax.broadcasted_iota(jnp.int32, (1, 1, LANES), 2) < TM_HEAD
    left_c = lax.broadcasted_iota(jnp.int32, (1, 1, cw2), 2) < wc
    ri = lax.broadcasted_iota(jnp.int32, (wc, cw2), 0)
    ci = lax.broadcasted_iota(jnp.int32, (wc, cw2), 1) & (wc - 1)
    strict = ri > ci
    incl = ri >= ci
    eye = (ri == ci).astype(F32)

    tsel = [pl.ds(b, wc, stride=bb) for b in range(bb)]
    gather = lambda ref: jnp.stack([ref.at[p][ts, :] for ts in tsel for p in range(PAIRS)])
    kap_g = gather(kap_s)
    rt_g = gather(rt_s)
    bt_g = gather(bt_s)
    kt_g = gather(kt_s)
    vv = gather(v_s)
    pe = jnp.stack([pe_s.at[p][pl.ds(b, 1), :] for b in range(bb) for p in range(PAIRS)])
    s0 = o_wkv_ref[...].reshape(bb * PAIRS, TM_HEAD, LANES)
    lhs = jnp.concatenate([kap_g, rt_g], axis=1)
    fused = _bdot_nt(lhs, jnp.concatenate([_block_diag_rows(bt_g, left), _block_diag_rows(kt_g, left),
                                           _block_diag_rows(s0, left)], axis=1))
    g_b = fused[:, :, 0:cw2]
    g_k = fused[:, :, cw2:2 * cw2]
    z = fused[:, :, 2 * cw2:]
    m_ab = jnp.where(strict, g_b[:, :wc], 0.0)
    m_ak = jnp.where(strict, g_k[:, :wc], 0.0)
    m_rb = jnp.where(incl, g_b[:, wc:], 0.0)
    m_rk = jnp.where(incl, g_k[:, wc:], 0.0)
    t_inv = eye - m_ab
    m_pow = _bdot(m_ab, _block_diag_rows(m_ab, left_c))
    for lv in range(1, n_lv):
        if lv < n_lv - 1:
            prod = _bdot(jnp.concatenate([t_inv, m_pow], axis=1), _block_diag_rows(m_pow, left_c))
            t_inv = t_inv + prod[:, :wc]
            m_pow = prod[:, wc:]
        else:
            t_inv = t_inv + _bdot(t_inv, _block_diag_rows(m_pow, left_c))
    xv = _bdot(jnp.concatenate([m_ak, m_rk], axis=1), _block_diag_rows(vv, left))
    uu = -_bdot(t_inv, _block_diag_rows(z[:, :wc] + xv[:, :wc], left))
    yh = z[:, wc:] + _bdot(m_rb, _block_diag_rows(uu, left)) + xv[:, wc:]
    full = _bdot_tn(jnp.concatenate([uu, vv], axis=1), jnp.concatenate([bt_g, kt_g], axis=1))
    upd = jnp.where(left, full[:, :TM_HEAD], full[:, TM_HEAD:])
    o_wkv_ref[...] = ((s0 + upd) * pe).reshape(bb, PAIRS, TM_HEAD, LANES)
    for b, ts in enumerate(tsel):
        for p in range(PAIRS):
            yt_s.at[p][ts, :] = yh[b * PAIRS + p]

    yv = jnp.concatenate([yt_s[p, 0:rows, :] for p in range(PAIRS)], axis=-1)
    mean = head_sum(yv) * (1.0 / TM_HEAD)
    cen = yv - mean
    var = head_sum(cen * cen) * (1.0 / TM_HEAD)
    y_tm = ((cen * lax.rsqrt(var + GN_EPS)) * gng_ref[...] + gnb_ref[...] + bonus) * gate_tm
    mixed = jnp.concatenate([y_tm.astype(BF16), y_lru.astype(BF16)], axis=-1)
    x1 = x_buf[x_slot].reshape(rows, D_MODEL) + jnp.dot(mixed, w_out_ref[...], preferred_element_type=F32)
    x1_ref[...] = x1.reshape(chunk, bb, D_MODEL)

    o_shift_ref[...] = u_tm[rows - bb:]
    o_conv_ref[...] = xb[rows - conv_rows:].reshape(LRU_CONV_W - 1, bb, D_LRU)
    tm_carry[...] = u_tm[rows - bb:]
    xb_carry[...] = xb[rows - conv_rows:]
    _read_chunk_end(x_stream, step, n_steps)


def _const_spec(shape):
    zeros = (0,) * len(shape)
    return pl.BlockSpec(shape, lambda bi, ti: zeros, pipeline_mode=pl.Buffered(1))


def _mixer(x, st_shift, st_wkv, st_conv, st_h, weights, *, bb, chunk, pos0):
    batch, length, _ = x.shape
    assert batch % bb == 0 and length % chunk == 0 and bb % SUBLANES == 0
    assert chunk & (chunk - 1) == 0 and chunk >= LRU_CONV_W - 1
    rows = bb * chunk
    grid = (batch // bb, length // chunk)
    act = pl.BlockSpec((chunk, bb, D_MODEL), lambda bi, ti: (ti, bi, 0))
    vec = lambda w: pl.BlockSpec((bb, w), lambda bi, ti: (bi, 0))
    wkv_spec = pl.BlockSpec((bb, PAIRS, TM_HEAD, LANES), lambda bi, ti: (bi, 0, 0, 0))
    conv_spec = pl.BlockSpec((LRU_CONV_W - 1, bb, D_LRU), lambda bi, ti: (0, bi, 0))
    in_specs = [pl.BlockSpec(memory_space=pl.ANY), vec(D_TM_PROJ), wkv_spec, conv_spec, vec(D_LRU)]
    in_specs += [_const_spec(w.shape) for w in weights]
    out_specs = [act, vec(D_TM_PROJ), wkv_spec, conv_spec, vec(D_LRU)]
    out_shape = [jax.ShapeDtypeStruct((length, batch, D_MODEL), F32),
                 jax.ShapeDtypeStruct((batch, D_TM_PROJ), F32),
                 jax.ShapeDtypeStruct((batch, PAIRS, TM_HEAD, LANES), F32),
                 jax.ShapeDtypeStruct((LRU_CONV_W - 1, batch, D_LRU), F32),
                 jax.ShapeDtypeStruct((batch, D_LRU), F32)]
    pair_rows = lambda n: pltpu.VMEM((PAIRS, n, LANES), F32)
    scratch = [pltpu.VMEM((bb, D_TM_PROJ), F32),
               pltpu.VMEM(((LRU_CONV_W - 1) * bb, D_LRU), F32)]
    scratch += [pair_rows(max(chunk, SUBLANES) * bb)] * 6 + [pair_rows(bb)]
    scratch += [pltpu.VMEM((READ_SLOTS, chunk, bb, D_MODEL), F32), pltpu.SemaphoreType.DMA((READ_SLOTS,))]
    return pl.pallas_call(
        functools.partial(_mixer_kernel, bb=bb, chunk=chunk, pos0=pos0,
                          n_blocks=grid[0], n_chunks=grid[1]),
        out_shape=out_shape, grid=grid, in_specs=in_specs, out_specs=out_specs,
        scratch_shapes=scratch, name="mixer",
        compiler_params=pltpu.CompilerParams(dimension_semantics=("arbitrary", "arbitrary"),
                                             vmem_limit_bytes=VMEM_LIMIT_BYTES),
    )(x, st_shift, st_wkv, st_conv, st_h, *weights)


def _ffn_kernel(x1_ref, st_conv_ref,
                g2_ref, w_up_ref, w_gate_ref, cw_ref, cb_ref, w_down_ref, gf_ref,
                y_hbm, o_conv_ref,
                up_carry, y_buf, y_sem,
                *, bb, chunk, n_blocks, n_chunks):
    rows = bb * chunk
    conv_rows = (FFN_CONV_W - 1) * bb
    ti = pl.program_id(1)
    step = pl.program_id(0) * n_chunks + ti
    n_steps = n_blocks * n_chunks
    y_stream = _TimeMajorStream(y_hbm, y_buf, y_sem, bb=bb, chunk=chunk, n_blocks=n_blocks,
                                n_chunks=n_chunks, to_hbm=True)
    slot = lax.rem(step, 2)

    @pl.when(ti == 0)
    def _():
        up_carry[...] = st_conv_ref[...].reshape(conv_rows, D_FF)

    x1 = x1_ref[...].reshape(rows, D_MODEL)
    xn = _rms(x1, g2_ref[...]).astype(BF16)
    acc = x1
    for n in range(D_FF // FF_COL_TILE):
        cols = slice(n * FF_COL_TILE, (n + 1) * FF_COL_TILE)
        up = jnp.dot(xn, w_up_ref[:, cols], preferred_element_type=F32)
        gate = jnp.dot(xn, w_gate_ref[:, cols], preferred_element_type=F32)
        up_prev = up_carry[:, cols]
        upc = cb_ref[:, cols] + _shift_time(up_prev, up, 2, bb) * cw_ref[0:1, cols]
        upc = upc + _shift_time(up_prev, up, 1, bb) * cw_ref[1:2, cols]
        upc = upc + up * cw_ref[2:3, cols]
        up_carry[:, cols] = up[rows - conv_rows:]
        o_conv_ref[:, :, cols] = up[rows - conv_rows:].reshape(FFN_CONV_W - 1, bb, FF_COL_TILE)
        hid = (jax.nn.gelu(upc) * gate).astype(BF16)
        acc = acc + jnp.dot(hid, w_down_ref[cols, :], preferred_element_type=F32)

    y_buf[slot] = _rms(acc, gf_ref[...]).reshape(chunk, bb, D_MODEL)
    y_stream.start(step, slot)

    @pl.when(step > 0)
    def _():
        y_stream.wait(step - 1, 1 - slot)

    @pl.when(step == n_steps - 1)
    def _():
        y_stream.wait(step, slot)


def _ffn(x1, st_conv, weights, *, bb, chunk):
    length, batch, _ = x1.shape
    assert batch % bb == 0 and length % chunk == 0 and bb % SUBLANES == 0
    assert chunk >= FFN_CONV_W - 1
    rows = bb * chunk
    grid = (batch // bb, length // chunk)
    act = pl.BlockSpec((chunk, bb, D_MODEL), lambda bi, ti: (ti, bi, 0))
    conv_spec = pl.BlockSpec((FFN_CONV_W - 1, bb, D_FF), lambda bi, ti: (0, bi, 0))
    in_specs = [act, conv_spec] + [_const_spec(w.shape) for w in weights]
    out_shape = [jax.ShapeDtypeStruct((batch, length, D_MODEL), F32),
                 jax.ShapeDtypeStruct((FFN_CONV_W - 1, batch, D_FF), F32)]
    scratch = [pltpu.VMEM(((FFN_CONV_W - 1) * bb, D_FF), F32),
               pltpu.VMEM((2, chunk, bb, D_MODEL), F32), pltpu.SemaphoreType.DMA((2,))]
    return pl.pallas_call(
        functools.partial(_ffn_kernel, bb=bb, chunk=chunk, n_blocks=grid[0], n_chunks=grid[1]),
        out_shape=out_shape, grid=grid, in_specs=in_specs,
        out_specs=[pl.BlockSpec(memory_space=pl.ANY), conv_spec],
        scratch_shapes=scratch, name="ffn",
        compiler_params=pltpu.CompilerParams(dimension_semantics=("arbitrary", "arbitrary"),
                                             vmem_limit_bytes=VMEM_LIMIT_BYTES),
    )(x1, st_conv, *weights)


def _ffn_state_kernel(x1_ref, g2_ref, w_up_ref, o_conv_ref, *, bb):
    tail = FFN_CONV_W - 1
    x1 = x1_ref[...].reshape(tail * bb, D_MODEL)
    up = _dot(_rms(x1, g2_ref[...]), w_up_ref[...])
    o_conv_ref[...] = up.reshape(tail, bb, D_FF)


def _ffn_state(x1, g2, w_up):
    length, batch, _ = x1.shape
    tail = FFN_CONV_W - 1
    assert length % tail == 0 and batch % SUBLANES == 0
    return pl.pallas_call(
        functools.partial(_ffn_state_kernel, bb=batch),
        out_shape=jax.ShapeDtypeStruct((tail, batch, D_FF), F32), grid=(1,),
        in_specs=[pl.BlockSpec((tail, batch, D_MODEL), lambda i: (length // tail - 1, 0, 0)),
                  pl.BlockSpec(g2.shape, lambda i: (0, 0)), pl.BlockSpec(w_up.shape, lambda i: (0, 0))],
        out_specs=pl.BlockSpec((tail, batch, D_FF), lambda i: (0, 0, 0)), name="ffn_state",
        compiler_params=pltpu.CompilerParams(dimension_semantics=("arbitrary",),
                                             vmem_limit_bytes=VMEM_LIMIT_BYTES),
    )(x1, g2, w_up)


RELAYOUT_ROWS = 32


def _pair_rows_copy(hbm, buf, sem, step, slot, b, *, n_vt, to_hbm):
    p, vi = lax.div(step, n_vt), lax.rem(step, n_vt)
    rows = hbm.at[b, p, pl.ds(vi * RELAYOUT_ROWS, RELAYOUT_ROWS), :]
    tile = buf.at[slot, :, b, :]
    src, dst = (tile, rows) if to_hbm else (rows, tile)
    return pltpu.make_async_copy(src, dst, sem.at[slot])


def _pairs_from_lanes_kernel(a_ref, o_hbm, buf, sem, *, n_vt, n_steps):
    step = pl.program_id(0) * n_vt + pl.program_id(1)
    slot = lax.rem(step, 2)
    copy = functools.partial(_pair_rows_copy, o_hbm, buf, sem, n_vt=n_vt, to_hbm=True)
    batch = a_ref.shape[-1]
    for i in range(RELAYOUT_ROWS):
        buf[slot, i] = a_ref[:, i].reshape(LANES, batch).T
    for b in range(batch):
        copy(step, slot, b).start()

    @pl.when(step > 0)
    def _():
        for b in range(batch):
            copy(step - 1, 1 - slot, b).wait()

    @pl.when(step == n_steps - 1)
    def _():
        for b in range(batch):
            copy(step, slot, b).wait()


def _lanes_from_pairs_kernel(s_hbm, o_ref, buf, sem, *, n_vt, n_steps):
    step = pl.program_id(0) * n_vt + pl.program_id(1)
    slot = lax.rem(step, 2)
    last = n_steps - 1
    copy = functools.partial(_pair_rows_copy, s_hbm, buf, sem, n_vt=n_vt, to_hbm=False)

    batch = o_ref.shape[-1]

    @pl.when(step == 0)
    def _():
        for b in range(batch):
            copy(0, 0, b).start()

    for b in range(batch):
        copy(jnp.minimum(step + 1, last), 1 - slot, b).start()
    for b in range(batch):
        copy(step, slot, b).wait()
    for i in range(RELAYOUT_ROWS):
        o_ref[:, i] = buf[slot, i].T.reshape(2, TM_HEAD, batch)

    @pl.when(step == last)
    def _():
        for b in range(batch):
            copy(last, 1 - slot, b).wait()


def _wkv_relayout(x, *, to_pairs):
    batch = x.shape[-1] if to_pairs else x.shape[0]
    assert batch == LANES and TM_HEAD % RELAYOUT_ROWS == 0
    n_vt = TM_HEAD // RELAYOUT_ROWS
    lanes_shape = (TM_HEADS, TM_HEAD, TM_HEAD, batch)
    pairs_shape = (batch, PAIRS, TM_HEAD, LANES)
    lanes_spec = pl.BlockSpec((2, RELAYOUT_ROWS, TM_HEAD, batch), lambda p, vi: (p, vi, 0, 0))
    hbm_spec = pl.BlockSpec(memory_space=pl.ANY)
    body = _pairs_from_lanes_kernel if to_pairs else _lanes_from_pairs_kernel
    return pl.pallas_call(
        functools.partial(body, n_vt=n_vt, n_steps=PAIRS * n_vt),
        out_shape=jax.ShapeDtypeStruct(pairs_shape if to_pairs else lanes_shape, F32),
        grid=(PAIRS, n_vt),
        in_specs=[lanes_spec if to_pairs else hbm_spec],
        out_specs=hbm_spec if to_pairs else lanes_spec,
        scratch_shapes=[pltpu.VMEM((2, RELAYOUT_ROWS, batch, LANES), F32), pltpu.SemaphoreType.DMA((2,))],
        name="wkv_relayout",
        compiler_params=pltpu.CompilerParams(dimension_semantics=("arbitrary", "arbitrary")),
    )(x)


def _row(v):
    return v.reshape(1, -1).astype(F32)


def _pair_block_diag(w):
    nb, n, _ = w.shape
    w = w.reshape(nb // 2, 2, n, n)
    eye = jnp.eye(2, dtype=w.dtype)
    return (eye[None, :, None, :, None] * w[:, :, :, None, :]).reshape(nb // 2, 2 * n, 2 * n)


def _wkv_to_pairs(s):
    b = s.shape[0]
    if b == LANES:
        return _wkv_relayout(jnp.transpose(s.astype(F32), (1, 2, 3, 0)), to_pairs=True)
    s = s.astype(F32).reshape(b, PAIRS, 2, TM_HEAD, TM_HEAD)
    return jnp.transpose(s, (0, 1, 3, 2, 4)).reshape(b, PAIRS, TM_HEAD, LANES)


def _wkv_from_pairs(s):
    b = s.shape[0]
    if b == LANES:
        return jnp.transpose(_wkv_relayout(s, to_pairs=False), (3, 0, 1, 2))
    s = s.reshape(b, PAIRS, TM_HEAD, 2, TM_HEAD)
    return jnp.transpose(s, (0, 1, 3, 2, 4)).reshape(b, TM_HEADS, TM_HEAD, TM_HEAD)


def _layer(x, states, mixer_w, ffn_w, *, mixer_bb, ffn_bb, mixer_chunk, ffn_chunk, pos0):
    st_shift, st_wkv, st_conv, st_h, st_fconv = states
    x1, o_shift, o_wkv, o_conv, o_h = _mixer(x, st_shift, st_wkv, st_conv, st_h, mixer_w,
                                             bb=mixer_bb, chunk=mixer_chunk, pos0=pos0)
    y, o_fconv = _ffn(x1, st_fconv, ffn_w, bb=ffn_bb, chunk=ffn_chunk)
    return y, (o_shift, o_wkv, o_conv, o_h, o_fconv)


def kernel(x_prompt, x_sample, state_tm_shift, state_tm_wkv, state_lru_conv, state_lru_h, state_ffn_conv, meta_tokens, norm1_g, w_in, tm_mu, tm_w0, tm_w_up, tm_a0, tm_a_up, tm_g_up, tm_k_k, tm_k_a, tm_r_k, tm_gn_g, tm_gn_b, lru_conv_w, lru_conv_b, lru_wa, lru_ba, lru_wx, lru_bx, lru_lambda, lru_out_g, w_out, norm2_g, ffn_w_up, ffn_w_gate, ffn_conv_w, ffn_conv_b, ffn_w_down, norm_f_g):
    depth = w_in.shape[0]
    assert depth == 1
    l = 0
    zeros_lora = jnp.zeros((DECAY_RANK, D_TM), F32)
    head_id = jnp.arange(D_TM) // TM_HEAD
    seg = (head_id[:, None] == jnp.arange(LANES)[None, :]).astype(BF16)
    mixer_w = (
        _row(norm1_g[l]), w_in[l].astype(BF16), _row(tm_mu[l]), _row(tm_w0[l]),
        jnp.concatenate([tm_w_up[l], zeros_lora], axis=0).astype(BF16),
        _row(tm_a0[l]),
        jnp.concatenate([zeros_lora, tm_a_up[l]], axis=0).astype(BF16),
        tm_g_up[l].astype(BF16),
        _row(tm_k_k[l]), _row(tm_k_a[l]), _row(tm_r_k[l]), _row(tm_gn_g[l]), _row(tm_gn_b[l]),
        lru_conv_w[l].astype(F32), _row(lru_conv_b[l]),
        jnp.concatenate([_pair_block_diag(lru_wa[l]), _pair_block_diag(lru_wx[l])], axis=-1).astype(BF16),
        _row(lru_ba[l]), _row(lru_bx[l]),
        _row(lru_lambda[l]), _row(lru_out_g[l]),
        seg, seg.T, w_out[l].astype(BF16),
    )
    ffn_w = (
        _row(norm2_g[l]), ffn_w_up[l].astype(BF16), ffn_w_gate[l].astype(BF16),
        ffn_conv_w[l].astype(F32), _row(ffn_conv_b[l]), ffn_w_down[l].astype(BF16), _row(norm_f_g),
    )

    bsz, seq = x_prompt.shape[0], x_prompt.shape[1]
    x_meta = jnp.broadcast_to(meta_tokens[None].astype(F32), (bsz, N_META, D_MODEL))
    x1_meta, *meta_st = _mixer(x_meta, jnp.zeros((bsz, D_TM_PROJ), F32),
                               jnp.zeros((bsz, PAIRS, TM_HEAD, LANES), F32),
                               jnp.zeros((LRU_CONV_W - 1, bsz, D_LRU), F32),
                               jnp.zeros((bsz, D_LRU), F32), mixer_w,
                               bb=bsz, chunk=N_META, pos0=0)
    p_init = tuple(meta_st) + (_ffn_state(x1_meta, ffn_w[0], ffn_w[1]),)

    y_prompt, p_st = _layer(x_prompt, p_init, mixer_w, ffn_w,
                            mixer_bb=bsz, ffn_bb=bsz, mixer_chunk=64, ffn_chunk=64,
                            pos0=N_META)

    dec_b, dec_seq = x_sample.shape[0], x_sample.shape[1]
    s_init = (state_tm_shift[l].astype(F32), _wkv_to_pairs(state_tm_wkv[l]),
              jnp.transpose(state_lru_conv[l].astype(F32), (1, 0, 2)), state_lru_h[l].astype(F32),
              jnp.transpose(state_ffn_conv[l].astype(F32), (1, 0, 2)))
    y_sample, s_st = _layer(x_sample, s_init, mixer_w, ffn_w,
                            mixer_bb=32, ffn_bb=64, mixer_chunk=dec_seq, ffn_chunk=dec_seq,
                            pos0=PAST_LEN)

    def unpack(st):
        o_shift, o_wkv, o_conv, o_h, o_fconv = st
        return (o_shift[None], _wkv_from_pairs(o_wkv)[None], jnp.transpose(o_conv, (1, 0, 2))[None],
                o_h[None], jnp.transpose(o_fconv, (1, 0, 2))[None])

    return (y_prompt, y_sample) + unpack(p_st) + unpack(s_st)
```

```python
import functools
import math

import jax
import jax.numpy as jnp
from jax import lax
from jax.experimental import pallas as pl
from jax.experimental.pallas import tpu as pltpu

F32 = jnp.float32
BF16 = jnp.bfloat16

D_MODEL = 1024
N_META = 16
PAST_LEN = 16384
D_TM = 512
TM_HEAD = 64
TM_HEADS = 8
DECAY_RANK = 64
AAA_RANK = 64
GATE_RANK = 128
D_TM_PROJ = 3 * D_TM + DECAY_RANK + AAA_RANK + GATE_RANK
D_LRU = 512
LRU_CONV_W = 4
LRU_C = 8.0
D_IN_PROJ = D_TM_PROJ + 2 * D_LRU
D_FF = 3 * D_MODEL
FFN_CONV_W = 3
EPS = 1e-6
GN_EPS = 64e-5
LOG2_E = math.log2(math.e)

SUBLANES = 8
LANES = 128
PAIRS = TM_HEADS // 2
LORA_OFF = 3 * D_TM
GATE_OFF = LORA_OFF + DECAY_RANK + AAA_RANK
FF_COL_TILE = 1024
VMEM_LIMIT_BYTES = 60 * 1024 * 1024


def _dot(a, b):
    return jnp.dot(a.astype(BF16), b.astype(BF16), preferred_element_type=F32)


def _bdot(a, b):
    return lax.dot_general(a.astype(BF16), b.astype(BF16), (((2,), (1,)), ((0,), (0,))),
                           preferred_element_type=F32)


def _bdot_nt(a, b):
    return lax.dot_general(a.astype(BF16), b.astype(BF16), (((2,), (2,)), ((0,), (0,))),
                           preferred_element_type=F32)


def _bdot_tn(a, b):
    return lax.dot_general(a.astype(BF16), b.astype(BF16), (((1,), (1,)), ((0,), (0,))),
                           preferred_element_type=F32)


def _rms(x, g):
    return x * lax.rsqrt(jnp.mean(x * x, axis=-1, keepdims=True) + EPS) * g


def _sigmoid(z):
    return 0.5 * jnp.tanh(0.5 * z) + 0.5


def _softplus(z):
    return jnp.maximum(z, 0.0) + jnp.log(1.0 + jnp.exp(-jnp.abs(z)))


def _block_diag_rows(x, left):
    return jnp.concatenate([jnp.where(left, x, 0.0), jnp.where(left, 0.0, x)], axis=1)


def _shift_time(carry, x, steps, bb):
    n = steps * bb
    return jnp.concatenate([carry[carry.shape[0] - n:], x[:x.shape[0] - n]], axis=0)


class _TimeMajorStream:
    def __init__(self, hbm, buf, sem, *, bb, chunk, n_blocks, n_chunks, to_hbm):
        self.hbm, self.buf, self.sem = hbm, buf, sem
        self.bb, self.chunk, self.to_hbm = bb, chunk, to_hbm
        self.n_blocks, self.n_chunks = n_blocks, n_chunks

    def _copy(self, g, slot, i):
        if self.n_chunks == 1:
            bi, ti = g, 0
        elif self.n_blocks == 1:
            bi, ti = 0, g
        else:
            bi, ti = lax.div(g, self.n_chunks), lax.rem(g, self.n_chunks)
        if self.bb < self.chunk:
            hbm_rows = self.hbm.at[bi * self.bb + i, pl.ds(ti * self.chunk, self.chunk), :]
            tile = self.buf.at[slot, :, i, :]
        else:
            seqs = pl.ds(pl.multiple_of(bi * self.bb, SUBLANES), self.bb)
            hbm_rows = self.hbm.at[seqs, ti * self.chunk + i, :]
            tile = self.buf.at[slot, i]
        src, dst = (tile, hbm_rows) if self.to_hbm else (hbm_rows, tile)
        return pltpu.make_async_copy(src, dst, self.sem.at[slot])

    def start(self, g, slot):
        for i in range(min(self.bb, self.chunk)):
            self._copy(g, slot, i).start()

    def wait(self, g, slot):
        for i in range(min(self.bb, self.chunk)):
            self._copy(g, slot, i).wait()


READ_SLOTS = 3


def _read_chunk_begin(stream, g, n_steps):
    last = n_steps - 1

    @pl.when(g == 0)
    def _():
        stream.start(0, 0)
        stream.start(jnp.minimum(1, last), 1)

    slot = lax.rem(g, READ_SLOTS)
    stream.wait(g, slot)
    return slot


def _read_chunk_end(stream, g, n_steps):
    last = n_steps - 1
    stream.start(jnp.minimum(g + 2, last), lax.rem(g + 2, READ_SLOTS))

    @pl.when(g == last)
    def _():
        stream.wait(last, lax.rem(g + 1, READ_SLOTS))
        stream.wait(last, lax.rem(g + 2, READ_SLOTS))


def _mixer_kernel(x_hbm, st_shift_ref, st_wkv_ref, st_conv_ref, st_h_ref,
                  g1_ref, w_in_ref, mu_ref, w0_ref, wdec_ref, a0_ref, waaa_ref, wgate_ref,
                  kk_ref, ka_ref, rk_ref, gng_ref, gnb_ref,
                  cw_ref, cb_ref, wgates_ref, ba_ref, bx_ref, lam_ref, og_ref,
                  seg_ref, segt_ref, w_out_ref,
                  x1_ref, o_shift_ref, o_wkv_ref, o_conv_ref, o_h_ref,
                  tm_carry, xb_carry, kap_s, rt_s, bt_s, kt_s, v_s, yt_s, pe_s, x_buf, x_sem,
                  *, bb, chunk, pos0, n_blocks, n_chunks):
    rows = bb * chunk
    conv_rows = (LRU_CONV_W - 1) * bb
    wkv_chunk = max(chunk, SUBLANES)
    ti = pl.program_id(1)
    step = pl.program_id(0) * n_chunks + ti
    n_steps = n_blocks * n_chunks
    x_stream = _TimeMajorStream(x_hbm, x_buf, x_sem, bb=bb, chunk=chunk, n_blocks=n_blocks,
                                n_chunks=n_chunks, to_hbm=False)
    x_slot = _read_chunk_begin(x_stream, step, n_steps)

    @pl.when(ti == 0)
    def _():
        tm_carry[...] = st_shift_ref[...]
        xb_carry[...] = st_conv_ref[...].reshape(conv_rows, D_LRU)
        o_wkv_ref[...] = st_wkv_ref[...]
        o_h_ref[...] = st_h_ref[...]
        if wkv_chunk > chunk:
            for ref in (kap_s, rt_s, bt_s, kt_s, v_s):
                ref[:, rows:wkv_chunk * bb, :] = jnp.zeros((PAIRS, wkv_chunk * bb - rows, LANES), F32)

    seg = seg_ref[...]
    segt = segt_ref[...]

    def head_sum(x):
        s = jnp.dot(x.astype(BF16), seg, preferred_element_type=F32)
        return jnp.dot(s.astype(BF16), segt, preferred_element_type=F32)

    x = x_buf[x_slot].reshape(rows, D_MODEL)
    u = _dot(_rms(x, g1_ref[...]), w_in_ref[...])
    u_tm = u[:, :D_TM_PROJ]
    xb = u[:, D_TM_PROJ:D_TM_PROJ + D_LRU]
    gate_lru = u[:, D_TM_PROJ + D_LRU:D_IN_PROJ]

    row = lax.broadcasted_iota(jnp.int32, (rows, D_TM), 0)

    um = u_tm + (_shift_time(tm_carry[...], u_tm, 1, bb) - u_tm) * mu_ref[...]
    r = um[:, 0:D_TM]
    k = um[:, D_TM:2 * D_TM]
    v = um[:, 2 * D_TM:3 * D_TM]
    x_lora = um[:, LORA_OFF:GATE_OFF]
    x_gate = um[:, GATE_OFF:D_TM_PROJ]
    log2_decay = (-math.exp(-0.5) * LOG2_E) * _sigmoid(w0_ref[...] + _dot(jnp.tanh(x_lora), wdec_ref[...]))
    a = _sigmoid(a0_ref[...] + _dot(x_lora, waaa_ref[...]))
    gate_tm = _dot(_sigmoid(x_gate), wgate_ref[...])
    kk = k * kk_ref[...]
    k = k * (a * ka_ref[...] + (1.0 - ka_ref[...]))
    kk = kk * lax.rsqrt(jnp.maximum(head_sum(kk * kk), 1e-24))
    bonus = head_sum(r * k * rk_ref[...]) * v
    acc = log2_decay[0:bb]
    cum = [acc]
    for t in range(1, chunk):
        acc = acc + log2_decay[t * bb:(t + 1) * bb]
        cum.append(acc)
    c = jnp.concatenate(cum, axis=0)
    inv_p = jnp.exp2(-c)
    kap = kk * jnp.exp2(c - log2_decay)
    rt = r * jnp.exp2(c)
    bt = kk * a * inv_p
    kt = k * inv_p
    p_end = jnp.exp2(cum[chunk - 1])
    for p in range(PAIRS):
        ls = slice(p * LANES, (p + 1) * LANES)
        kap_s[p, 0:rows, :] = kap[:, ls]
        rt_s[p, 0:rows, :] = rt[:, ls]
        bt_s[p, 0:rows, :] = bt[:, ls]
        kt_s[p, 0:rows, :] = kt[:, ls]
        v_s[p, 0:rows, :] = v[:, ls]
        pe_s[p] = p_end[:, ls]

    xb_prev = xb_carry[...]
    xc = cb_ref[...] + _shift_time(xb_prev, xb, 3, bb) * cw_ref[0:1, :]
    xc = xc + _shift_time(xb_prev, xb, 2, bb) * cw_ref[1:2, :]
    xc = xc + _shift_time(xb_prev, xb, 1, bb) * cw_ref[2:3, :]
    xc = xc + xb * cw_ref[3:4, :]
    xc_bf = xc.astype(BF16)

    gates = [jnp.dot(xc_bf[:, p * LANES:(p + 1) * LANES], wgates_ref[p], preferred_element_type=F32)
             for p in range(D_LRU // LANES)]
    r_g = _sigmoid(jnp.concatenate([g[:, :LANES] for g in gates], axis=-1) + ba_ref[...])
    i_g = _sigmoid(jnp.concatenate([g[:, LANES:] for g in gates], axis=-1) + bx_ref[...])
    la = jnp.exp2(r_g * ((-LRU_C * LOG2_E) * _softplus(-lam_ref[...])))
    gap = 1.0 - la * la
    mult = jnp.where(gap > 0.0, gap * lax.rsqrt(gap), 0.0)
    if pos0 == 0:
        mult = jnp.where(jnp.logical_and(row < bb, ti == 0), 1.0, mult)
    lb = xc * i_g * mult
    h = o_h_ref[...]
    hs = []
    for t in range(chunk):
        ts = slice(t * bb, (t + 1) * bb)
        h = la[ts] * h + lb[ts]
        hs.append(h)
    o_h_ref[...] = h
    y_lru = _rms(jnp.concatenate(hs, axis=0) * jax.nn.gelu(gate_lru), og_ref[...])

    wc = wkv_chunk
    n_lv = max(1, (chunk - 1).bit_length())
    cw2 = 2 * wc
    left = lax.broadcasted_iota(jnp.int32, (1, 1, LANES), 2) < TM_HEAD
    left_c = lax.broadcasted_iota(jnp.int32, (1, 1, cw2), 2) < wc
    ri = lax.broadcasted_iota(jnp.int32, (wc, cw2), 0)
    ci = lax.broadcasted_iota(jnp.int32, (wc, cw2), 1) & (wc - 1)
    strict = ri > ci
    incl = ri >= ci
    eye = (ri == ci).astype(F32)

    tsel = [pl.ds(b, wc, stride=bb) for b in range(bb)]
    gather = lambda ref: jnp.stack([ref.at[p][ts, :] for ts in tsel for p in range(PAIRS)])
    kap_g = gather(kap_s)
    rt_g = gather(rt_s)
    bt_g = gather(bt_s)
    kt_g = gather(kt_s)
    vv = gather(v_s)
    pe = jnp.stack([pe_s.at[p][pl.ds(b, 1), :] for b in range(bb) for p in range(PAIRS)])
    s0 = o_wkv_ref[...].reshape(bb * PAIRS, TM_HEAD, LANES)
    lhs = jnp.concatenate([kap_g, rt_g], axis=1)
    fused = _bdot_nt(lhs, jnp.concatenate([_block_diag_rows(bt_g, left), _block_diag_rows(kt_g, left),
                                           _block_diag_rows(s0, left)], axis=1))
    g_b = fused[:, :, 0:cw2]
    g_k = fused[:, :, cw2:2 * cw2]
    z = fused[:, :, 2 * cw2:]
    m_ab = jnp.where(strict, g_b[:, :wc], 0.0)
    m_ak = jnp.where(strict, g_k[:, :wc], 0.0)
    m_rb = jnp.where(incl, g_b[:, wc:], 0.0)
    m_rk = jnp.where(incl, g_k[:, wc:], 0.0)
    t_inv = eye - m_ab
    m_pow = _bdot(m_ab, _block_diag_rows(m_ab, left_c))
    for lv in range(1, n_lv):
        if lv < n_lv - 1:
            prod = _bdot(jnp.concatenate([t_inv, m_pow], axis=1), _block_diag_rows(m_pow, left_c))
            t_inv = t_inv + prod[:, :wc]
            m_pow = prod[:, wc:]
        else:
            t_inv = t_inv + _bdot(t_inv, _block_diag_rows(m_pow, left_c))
    xv = _bdot(jnp.concatenate([m_ak, m_rk], axis=1), _block_diag_rows(vv, left))
    uu = -_bdot(t_inv, _block_diag_rows(z[:, :wc] + xv[:, :wc], left))
    yh = z[:, wc:] + _bdot(m_rb, _block_diag_rows(uu, left)) + xv[:, wc:]
    full = _bdot_tn(jnp.concatenate([uu, vv], axis=1), jnp.concatenate([bt_g, kt_g], axis=1))
    upd = jnp.where(left, full[:, :TM_HEAD], full[:, TM_HEAD:])
    o_wkv_ref[...] = ((s0 + upd) * pe).reshape(bb, PAIRS, TM_HEAD, LANES)
    for b, ts in enumerate(tsel):
        for p in range(PAIRS):
            yt_s.at[p][ts, :] = yh[b * PAIRS + p]

    yv = jnp.concatenate([yt_s[p, 0:rows, :] for p in range(PAIRS)], axis=-1)
    mean = head_sum(yv) * (1.0 / TM_HEAD)
    cen = yv - mean
    var = head_sum(cen * cen) * (1.0 / TM_HEAD)
    y_tm = ((cen * lax.rsqrt(var + GN_EPS)) * gng_ref[...] + gnb_ref[...] + bonus) * gate_tm
    mixed = jnp.concatenate([y_tm.astype(BF16), y_lru.astype(BF16)], axis=-1)
    x1 = x_buf[x_slot].reshape(rows, D_MODEL) + jnp.dot(mixed, w_out_ref[...], preferred_element_type=F32)
    x1_ref[...] = x1.reshape(chunk, bb, D_MODEL)

    o_shift_ref[...] = u_tm[rows - bb:]
    o_conv_ref[...] = xb[rows - conv_rows:].reshape(LRU_CONV_W - 1, bb, D_LRU)
    tm_carry[...] = u_tm[rows - bb:]
    xb_carry[...] = xb[rows - conv_rows:]
    _read_chunk_end(x_stream, step, n_steps)


def _const_spec(shape):
    zeros = (0,) * len(shape)
    return pl.BlockSpec(shape, lambda bi, ti: zeros, pipeline_mode=pl.Buffered(1))


def _mixer(x, st_shift, st_wkv, st_conv, st_h, weights, *, bb, chunk, pos0):
    batch, length, _ = x.shape
    assert batch % bb == 0 and length % chunk == 0 and bb % SUBLANES == 0
    assert chunk & (chunk - 1) == 0 and chunk >= LRU_CONV_W - 1
    rows = bb * chunk
    grid = (batch // bb, length // chunk)
    act = pl.BlockSpec((chunk, bb, D_MODEL), lambda bi, ti: (ti, bi, 0))
    vec = lambda w: pl.BlockSpec((bb, w), lambda bi, ti: (bi, 0))
    wkv_spec = pl.BlockSpec((bb, PAIRS, TM_HEAD, LANES), lambda bi, ti: (bi, 0, 0, 0))
    conv_spec = pl.BlockSpec((LRU_CONV_W - 1, bb, D_LRU), lambda bi, ti: (0, bi, 0))
    in_specs = [pl.BlockSpec(memory_space=pl.ANY), vec(D_TM_PROJ), wkv_spec, conv_spec, vec(D_LRU)]
    in_specs += [_const_spec(w.shape) for w in weights]
    out_specs = [act, vec(D_TM_PROJ), wkv_spec, conv_spec, vec(D_LRU)]
    out_shape = [jax.ShapeDtypeStruct((length, batch, D_MODEL), F32),
                 jax.ShapeDtypeStruct((batch, D_TM_PROJ), F32),
                 jax.ShapeDtypeStruct((batch, PAIRS, TM_HEAD, LANES), F32),
                 jax.ShapeDtypeStruct((LRU_CONV_W - 1, batch, D_LRU), F32),
                 jax.ShapeDtypeStruct((batch, D_LRU), F32)]
    pair_rows = lambda n: pltpu.VMEM((PAIRS, n, LANES), F32)
    scratch = [pltpu.VMEM((bb, D_TM_PROJ), F32),
               pltpu.VMEM(((LRU_CONV_W - 1) * bb, D_LRU), F32)]
    scratch += [pair_rows(max(chunk, SUBLANES) * bb)] * 6 + [pair_rows(bb)]
    scratch += [pltpu.VMEM((READ_SLOTS, chunk, bb, D_MODEL), F32), pltpu.SemaphoreType.DMA((READ_SLOTS,))]
    return pl.pallas_call(
        functools.partial(_mixer_kernel, bb=bb, chunk=chunk, pos0=pos0,
                          n_blocks=grid[0], n_chunks=grid[1]),
        out_shape=out_shape, grid=grid, in_specs=in_specs, out_specs=out_specs,
        scratch_shapes=scratch, name="mixer",
        compiler_params=pltpu.CompilerParams(dimension_semantics=("arbitrary", "arbitrary"),
                                             vmem_limit_bytes=VMEM_LIMIT_BYTES),
    )(x, st_shift, st_wkv, st_conv, st_h, *weights)


def _ffn_kernel(x1_ref, st_conv_ref,
                g2_ref, w_up_ref, w_gate_ref, cw_ref, cb_ref, w_down_ref, gf_ref,
                y_hbm, o_conv_ref,
                up_carry, y_buf, y_sem,
                *, bb, chunk, n_blocks, n_chunks):
    rows = bb * chunk
    conv_rows = (FFN_CONV_W - 1) * bb
    ti = pl.program_id(1)
    step = pl.program_id(0) * n_chunks + ti
    n_steps = n_blocks * n_chunks
    y_stream = _TimeMajorStream(y_hbm, y_buf, y_sem, bb=bb, chunk=chunk, n_blocks=n_blocks,
                                n_chunks=n_chunks, to_hbm=True)
    slot = lax.rem(step, 2)

    @pl.when(ti == 0)
    def _():
        up_carry[...] = st_conv_ref[...].reshape(conv_rows, D_FF)

    x1 = x1_ref[...].reshape(rows, D_MODEL)
    xn = _rms(x1, g2_ref[...]).astype(BF16)
    acc = x1
    for n in range(D_FF // FF_COL_TILE):
        cols = slice(n * FF_COL_TILE, (n + 1) * FF_COL_TILE)
        up = jnp.dot(xn, w_up_ref[:, cols], preferred_element_type=F32)
        gate = jnp.dot(xn, w_gate_ref[:, cols], preferred_element_type=F32)
        up_prev = up_carry[:, cols]
        upc = cb_ref[:, cols] + _shift_time(up_prev, up, 2, bb) * cw_ref[0:1, cols]
        upc = upc + _shift_time(up_prev, up, 1, bb) * cw_ref[1:2, cols]
        upc = upc + up * cw_ref[2:3, cols]
        up_carry[:, cols] = up[rows - conv_rows:]
        o_conv_ref[:, :, cols] = up[rows - conv_rows:].reshape(FFN_CONV_W - 1, bb, FF_COL_TILE)
        hid = (jax.nn.gelu(upc) * gate).astype(BF16)
        acc = acc + jnp.dot(hid, w_down_ref[cols, :], preferred_element_type=F32)

    y_buf[slot] = _rms(acc, gf_ref[...]).reshape(chunk, bb, D_MODEL)
    y_stream.start(step, slot)

    @pl.when(step > 0)
    def _():
        y_stream.wait(step - 1, 1 - slot)

    @pl.when(step == n_steps - 1)
    def _():
        y_stream.wait(step, slot)


def _ffn(x1, st_conv, weights, *, bb, chunk):
    length, batch, _ = x1.shape
    assert batch % bb == 0 and length % chunk == 0 and bb % SUBLANES == 0
    assert chunk >= FFN_CONV_W - 1
    rows = bb * chunk
    grid = (batch // bb, length // chunk)
    act = pl.BlockSpec((chunk, bb, D_MODEL), lambda bi, ti: (ti, bi, 0))
    conv_spec = pl.BlockSpec((FFN_CONV_W - 1, bb, D_FF), lambda bi, ti: (0, bi, 0))
    in_specs = [act, conv_spec] + [_const_spec(w.shape) for w in weights]
    out_shape = [jax.ShapeDtypeStruct((batch, length, D_MODEL), F32),
                 jax.ShapeDtypeStruct((FFN_CONV_W - 1, batch, D_FF), F32)]
    scratch = [pltpu.VMEM(((FFN_CONV_W - 1) * bb, D_FF), F32),
               pltpu.VMEM((2, chunk, bb, D_MODEL), F32), pltpu.SemaphoreType.DMA((2,))]
    return pl.pallas_call(
        functools.partial(_ffn_kernel, bb=bb, chunk=chunk, n_blocks=grid[0], n_chunks=grid[1]),
        out_shape=out_shape, grid=grid, in_specs=in_specs,
        out_specs=[pl.BlockSpec(memory_space=pl.ANY), conv_spec],
        scratch_shapes=scratch, name="ffn",
        compiler_params=pltpu.CompilerParams(dimension_semantics=("arbitrary", "arbitrary"),
                                             vmem_limit_bytes=VMEM_LIMIT_BYTES),
    )(x1, st_conv, *weights)


def _ffn_state_kernel(x1_ref, g2_ref, w_up_ref, o_conv_ref, *, bb):
    tail = FFN_CONV_W - 1
    x1 = x1_ref[...].reshape(tail * bb, D_MODEL)
    up = _dot(_rms(x1, g2_ref[...]), w_up_ref[...])
    o_conv_ref[...] = up.reshape(tail, bb, D_FF)


def _ffn_state(x1, g2, w_up):
    length, batch, _ = x1.shape
    tail = FFN_CONV_W - 1
    assert length % tail == 0 and batch % SUBLANES == 0
    return pl.pallas_call(
        functools.partial(_ffn_state_kernel, bb=batch),
        out_shape=jax.ShapeDtypeStruct((tail, batch, D_FF), F32), grid=(1,),
        in_specs=[pl.BlockSpec((tail, batch, D_MODEL), lambda i: (length // tail - 1, 0, 0)),
                  pl.BlockSpec(g2.shape, lambda i: (0, 0)), pl.BlockSpec(w_up.shape, lambda i: (0, 0))],
        out_specs=pl.BlockSpec((tail, batch, D_FF), lambda i: (0, 0, 0)), name="ffn_state",
        compiler_params=pltpu.CompilerParams(dimension_semantics=("arbitrary",),
                                             vmem_limit_bytes=VMEM_LIMIT_BYTES),
    )(x1, g2, w_up)


RELAYOUT_ROWS = 64


def _pair_rows_copy(hbm, buf, sem, step, slot, i, *, n_vt, to_hbm):
    p, vi = lax.div(step, n_vt), lax.rem(step, n_vt)
    rows = hbm.at[:, p, vi * RELAYOUT_ROWS + i, :]
    tile = buf.at[slot, i]
    src, dst = (tile, rows) if to_hbm else (rows, tile)
    return pltpu.make_async_copy(src, dst, sem.at[slot])


def _pairs_from_lanes_kernel(a_ref, o_hbm, buf, sem, *, n_vt, n_steps):
    step = pl.program_id(0) * n_vt + pl.program_id(1)
    slot = lax.rem(step, 2)
    copy = functools.partial(_pair_rows_copy, o_hbm, buf, sem, n_vt=n_vt, to_hbm=True)
    batch = a_ref.shape[-1]
    for i in range(RELAYOUT_ROWS):
        buf[slot, i] = a_ref[:, i].reshape(LANES, batch).T
    for i in range(RELAYOUT_ROWS):
        copy(step, slot, i).start()

    @pl.when(step > 0)
    def _():
        for i in range(RELAYOUT_ROWS):
            copy(step - 1, 1 - slot, i).wait()

    @pl.when(step == n_steps - 1)
    def _():
        for i in range(RELAYOUT_ROWS):
            copy(step, slot, i).wait()


def _lanes_from_pairs_kernel(s_hbm, o_ref, buf, sem, *, n_vt, n_steps):
    step = pl.program_id(0) * n_vt + pl.program_id(1)
    slot = lax.rem(step, 2)
    last = n_steps - 1
    copy = functools.partial(_pair_rows_copy, s_hbm, buf, sem, n_vt=n_vt, to_hbm=False)

    @pl.when(step == 0)
    def _():
        for i in range(RELAYOUT_ROWS):
            copy(0, 0, i).start()

    for i in range(RELAYOUT_ROWS):
        copy(jnp.minimum(step + 1, last), 1 - slot, i).start()
    for i in range(RELAYOUT_ROWS):
        copy(step, slot, i).wait()
    batch = o_ref.shape[-1]
    for i in range(RELAYOUT_ROWS):
        o_ref[:, i] = buf[slot, i].T.reshape(2, TM_HEAD, batch)

    @pl.when(step == last)
    def _():
        for i in range(RELAYOUT_ROWS):
            copy(last, 1 - slot, i).wait()


def _wkv_relayout(x, *, to_pairs):
    batch = x.shape[-1] if to_pairs else x.shape[0]
    assert batch == LANES and TM_HEAD % RELAYOUT_ROWS == 0
    n_vt = TM_HEAD // RELAYOUT_ROWS
    lanes_shape = (TM_HEADS, TM_HEAD, TM_HEAD, batch)
    pairs_shape = (batch, PAIRS, TM_HEAD, LANES)
    lanes_spec = pl.BlockSpec((2, RELAYOUT_ROWS, TM_HEAD, batch), lambda p, vi: (p, vi, 0, 0))
    hbm_spec = pl.BlockSpec(memory_space=pl.ANY)
    body = _pairs_from_lanes_kernel if to_pairs else _lanes_from_pairs_kernel
    return pl.pallas_call(
        functools.partial(body, n_vt=n_vt, n_steps=PAIRS * n_vt),
        out_shape=jax.ShapeDtypeStruct(pairs_shape if to_pairs else lanes_shape, F32),
        grid=(PAIRS, n_vt),
        in_specs=[lanes_spec if to_pairs else hbm_spec],
        out_specs=hbm_spec if to_pairs else lanes_spec,
        scratch_shapes=[pltpu.VMEM((2, RELAYOUT_ROWS, batch, LANES), F32), pltpu.SemaphoreType.DMA((2,))],
        name="wkv_relayout",
        compiler_params=pltpu.CompilerParams(dimension_semantics=("arbitrary", "arbitrary")),
    )(x)


def _row(v):
    return v.reshape(1, -1).astype(F32)


def _pair_block_diag(w):
    nb, n, _ = w.shape
    w = w.reshape(nb // 2, 2, n, n)
    eye = jnp.eye(2, dtype=w.dtype)
    return (eye[None, :, None, :, None] * w[:, :, :, None, :]).reshape(nb // 2, 2 * n, 2 * n)


def _wkv_to_pairs(s):
    b = s.shape[0]
    if b == LANES:
        return _wkv_relayout(jnp.transpose(s.astype(F32), (1, 2, 3, 0)), to_pairs=True)
    s = s.astype(F32).reshape(b, PAIRS, 2, TM_HEAD, TM_HEAD)
    return jnp.transpose(s, (0, 1, 3, 2, 4)).reshape(b, PAIRS, TM_HEAD, LANES)


def _wkv_from_pairs(s):
    b = s.shape[0]
    if b == LANES:
        return jnp.transpose(_wkv_relayout(s, to_pairs=False), (3, 0, 1, 2))
    s = s.reshape(b, PAIRS, TM_HEAD, 2, TM_HEAD)
    return jnp.transpose(s, (0, 1, 3, 2, 4)).reshape(b, TM_HEADS, TM_HEAD, TM_HEAD)


def _layer(x, states, mixer_w, ffn_w, *, mixer_bb, ffn_bb, mixer_chunk, ffn_chunk, pos0):
    st_shift, st_wkv, st_conv, st_h, st_fconv = states
    x1, o_shift, o_wkv, o_conv, o_h = _mixer(x, st_shift, st_wkv, st_conv, st_h, mixer_w,
                                             bb=mixer_bb, chunk=mixer_chunk, pos0=pos0)
    y, o_fconv = _ffn(x1, st_fconv, ffn_w, bb=ffn_bb, chunk=ffn_chunk)
    return y, (o_shift, o_wkv, o_conv, o_h, o_fconv)


def kernel(x_prompt, x_sample, state_tm_shift, state_tm_wkv, state_lru_conv, state_lru_h, state_ffn_conv, meta_tokens, norm1_g, w_in, tm_mu, tm_w0, tm_w_up, tm_a0, tm_a_up, tm_g_up, tm_k_k, tm_k_a, tm_r_k, tm_gn_g, tm_gn_b, lru_conv_w, lru_conv_b, lru_wa, lru_ba, lru_wx, lru_bx, lru_lambda, lru_out_g, w_out, norm2_g, ffn_w_up, ffn_w_gate, ffn_conv_w, ffn_conv_b, ffn_w_down, norm_f_g):
    depth = w_in.shape[0]
    assert depth == 1
    l = 0
    zeros_lora = jnp.zeros((DECAY_RANK, D_TM), F32)
    head_id = jnp.arange(D_TM) // TM_HEAD
    seg = (head_id[:, None] == jnp.arange(LANES)[None, :]).astype(BF16)
    mixer_w = (
        _row(norm1_g[l]), w_in[l].astype(BF16), _row(tm_mu[l]), _row(tm_w0[l]),
        jnp.concatenate([tm_w_up[l], zeros_lora], axis=0).astype(BF16),
        _row(tm_a0[l]),
        jnp.concatenate([zeros_lora, tm_a_up[l]], axis=0).astype(BF16),
        tm_g_up[l].astype(BF16),
        _row(tm_k_k[l]), _row(tm_k_a[l]), _row(tm_r_k[l]), _row(tm_gn_g[l]), _row(tm_gn_b[l]),
        lru_conv_w[l].astype(F32), _row(lru_conv_b[l]),
        jnp.concatenate([_pair_block_diag(lru_wa[l]), _pair_block_diag(lru_wx[l])], axis=-1).astype(BF16),
        _row(lru_ba[l]), _row(lru_bx[l]),
        _row(lru_lambda[l]), _row(lru_out_g[l]),
        seg, seg.T, w_out[l].astype(BF16),
    )
    ffn_w = (
        _row(norm2_g[l]), ffn_w_up[l].astype(BF16), ffn_w_gate[l].astype(BF16),
        ffn_conv_w[l].astype(F32), _row(ffn_conv_b[l]), ffn_w_down[l].astype(BF16), _row(norm_f_g),
    )

    bsz, seq = x_prompt.shape[0], x_prompt.shape[1]
    x_meta = jnp.broadcast_to(meta_tokens[None].astype(F32), (bsz, N_META, D_MODEL))
    x1_meta, *meta_st = _mixer(x_meta, jnp.zeros((bsz, D_TM_PROJ), F32),
                               jnp.zeros((bsz, PAIRS, TM_HEAD, LANES), F32),
                               jnp.zeros((LRU_CONV_W - 1, bsz, D_LRU), F32),
                               jnp.zeros((bsz, D_LRU), F32), mixer_w,
                               bb=bsz, chunk=N_META, pos0=0)
    p_init = tuple(meta_st) + (_ffn_state(x1_meta, ffn_w[0], ffn_w[1]),)

    y_prompt, p_st = _layer(x_prompt, p_init, mixer_w, ffn_w,
                            mixer_bb=bsz, ffn_bb=bsz, mixer_chunk=64, ffn_chunk=64,
                            pos0=N_META)

    dec_b, dec_seq = x_sample.shape[0], x_sample.shape[1]
    s_init = (state_tm_shift[l].astype(F32), _wkv_to_pairs(state_tm_wkv[l]),
              jnp.transpose(state_lru_conv[l].astype(F32), (1, 0, 2)), state_lru_h[l].astype(F32),
              jnp.transpose(state_ffn_conv[l].astype(F32), (1, 0, 2)))
    y_sample, s_st = _layer(x_sample, s_init, mixer_w, ffn_w,
                            mixer_bb=32, ffn_bb=64, mixer_chunk=dec_seq, ffn_chunk=dec_seq,
                            pos0=PAST_LEN)

    def unpack(st):
        o_shift, o_wkv, o_conv, o_h, o_fconv = st
        return (o_shift[None], _wkv_from_pairs(o_wkv)[None], jnp.transpose(o_conv, (1, 0, 2))[None],
                o_h[None], jnp.transpose(o_fconv, (1, 0, 2))[None])

    return (y_prompt, y_sample) + unpack(p_st) + unpack(s_st)
```

```python
import functools
import math

import jax
import jax.numpy as jnp
from jax import lax
from jax.experimental import pallas as pl
from jax.experimental.pallas import tpu as pltpu

F32 = jnp.float32
BF16 = jnp.bfloat16

D_MODEL = 1024
N_META = 16
PAST_LEN = 16384
D_TM = 512
TM_HEAD = 64
TM_HEADS = 8
DECAY_RANK = 64
AAA_RANK = 64
GATE_RANK = 128
D_TM_PROJ = 3 * D_TM + DECAY_RANK + AAA_RANK + GATE_RANK
D_LRU = 512
LRU_CONV_W = 4
LRU_C = 8.0
D_IN_PROJ = D_TM_PROJ + 2 * D_LRU
D_FF = 3 * D_MODEL
FFN_CONV_W = 3
EPS = 1e-6
GN_EPS = 64e-5
LOG2_E = math.log2(math.e)

SUBLANES = 8
LANES = 128
PAIRS = TM_HEADS // 2
LORA_OFF = 3 * D_TM
GATE_OFF = LORA_OFF + DECAY_RANK + AAA_RANK
FF_COL_TILE = 1024
VMEM_LIMIT_BYTES = 60 * 1024 * 1024


def _dot(a, b):
    return jnp.dot(a.astype(BF16), b.astype(BF16), preferred_element_type=F32)


def _bdot(a, b):
    return lax.dot_general(a.astype(BF16), b.astype(BF16), (((2,), (1,)), ((0,), (0,))),
                           preferred_element_type=F32)


def _bdot_nt(a, b):
    return lax.dot_general(a.astype(BF16), b.astype(BF16), (((2,), (2,)), ((0,), (0,))),
                           preferred_element_type=F32)


def _bdot_tn(a, b):
    return lax.dot_general(a.astype(BF16), b.astype(BF16), (((1,), (1,)), ((0,), (0,))),
                           preferred_element_type=F32)


def _rms(x, g):
    return x * lax.rsqrt(jnp.mean(x * x, axis=-1, keepdims=True) + EPS) * g


def _sigmoid(z):
    return 0.5 * jnp.tanh(0.5 * z) + 0.5


def _softplus(z):
    return jnp.maximum(z, 0.0) + jnp.log(1.0 + jnp.exp(-jnp.abs(z)))


def _block_diag_rows(x, left):
    return jnp.concatenate([jnp.where(left, x, 0.0), jnp.where(left, 0.0, x)], axis=1)


def _shift_time(carry, x, steps, bb):
    n = steps * bb
    return jnp.concatenate([carry[carry.shape[0] - n:], x[:x.shape[0] - n]], axis=0)


class _TimeMajorStream:
    def __init__(self, hbm, buf, sem, *, bb, chunk, n_blocks, n_chunks, to_hbm):
        self.hbm, self.buf, self.sem = hbm, buf, sem
        self.bb, self.chunk, self.to_hbm = bb, chunk, to_hbm
        self.n_blocks, self.n_chunks = n_blocks, n_chunks

    def _copy(self, g, slot, i):
        if self.n_chunks == 1:
            bi, ti = g, 0
        elif self.n_blocks == 1:
            bi, ti = 0, g
        else:
            bi, ti = lax.div(g, self.n_chunks), lax.rem(g, self.n_chunks)
        if self.bb < self.chunk:
            hbm_rows = self.hbm.at[bi * self.bb + i, pl.ds(ti * self.chunk, self.chunk), :]
            tile = self.buf.at[slot, :, i, :]
        else:
            seqs = pl.ds(pl.multiple_of(bi * self.bb, SUBLANES), self.bb)
            hbm_rows = self.hbm.at[seqs, ti * self.chunk + i, :]
            tile = self.buf.at[slot, i]
        src, dst = (tile, hbm_rows) if self.to_hbm else (hbm_rows, tile)
        return pltpu.make_async_copy(src, dst, self.sem.at[slot])

    def start(self, g, slot):
        for i in range(min(self.bb, self.chunk)):
            self._copy(g, slot, i).start()

    def wait(self, g, slot):
        for i in range(min(self.bb, self.chunk)):
            self._copy(g, slot, i).wait()


READ_SLOTS = 3


def _read_chunk_begin(stream, g, n_steps):
    last = n_steps - 1

    @pl.when(g == 0)
    def _():
        stream.start(0, 0)
        stream.start(jnp.minimum(1, last), 1)

    slot = lax.rem(g, READ_SLOTS)
    stream.wait(g, slot)
    return slot


def _read_chunk_end(stream, g, n_steps):
    last = n_steps - 1
    stream.start(jnp.minimum(g + 2, last), lax.rem(g + 2, READ_SLOTS))

    @pl.when(g == last)
    def _():
        stream.wait(last, lax.rem(g + 1, READ_SLOTS))
        stream.wait(last, lax.rem(g + 2, READ_SLOTS))


def _mixer_kernel(x_hbm, st_shift_ref, st_wkv_ref, st_conv_ref, st_h_ref,
                  g1_ref, w_in_ref, mu_ref, w0_ref, wdec_ref, a0_ref, waaa_ref, wgate_ref,
                  kk_ref, ka_ref, rk_ref, gng_ref, gnb_ref,
                  cw_ref, cb_ref, wgates_ref, ba_ref, bx_ref, lam_ref, og_ref,
                  seg_ref, segt_ref, w_out_ref,
                  x1_ref, o_shift_ref, o_wkv_ref, o_conv_ref, o_h_ref,
                  tm_carry, xb_carry, kap_s, rt_s, bt_s, kt_s, v_s, yt_s, pe_s, x_buf, x_sem,
                  *, bb, chunk, pos0, n_blocks, n_chunks):
    rows = bb * chunk
    conv_rows = (LRU_CONV_W - 1) * bb
    wkv_chunk = max(chunk, SUBLANES)
    ti = pl.program_id(1)
    step = pl.program_id(0) * n_chunks + ti
    n_steps = n_blocks * n_chunks
    x_stream = _TimeMajorStream(x_hbm, x_buf, x_sem, bb=bb, chunk=chunk, n_blocks=n_blocks,
                                n_chunks=n_chunks, to_hbm=False)
    x_slot = _read_chunk_begin(x_stream, step, n_steps)

    @pl.when(ti == 0)
    def _():
        tm_carry[...] = st_shift_ref[...]
        xb_carry[...] = st_conv_ref[...].reshape(conv_rows, D_LRU)
        o_wkv_ref[...] = st_wkv_ref[...]
        o_h_ref[...] = st_h_ref[...]
        if wkv_chunk > chunk:
            for ref in (kap_s, rt_s, bt_s, kt_s, v_s):
                ref[:, rows:wkv_chunk * bb, :] = jnp.zeros((PAIRS, wkv_chunk * bb - rows, LANES), F32)

    seg = seg_ref[...]
    segt = segt_ref[...]

    def head_sum(x):
        s = jnp.dot(x.astype(BF16), seg, preferred_element_type=F32)
        return jnp.dot(s.astype(BF16), segt, preferred_element_type=F32)

    x = x_buf[x_slot].reshape(rows, D_MODEL)
    u = _dot(_rms(x, g1_ref[...]), w_in_ref[...])
    u_tm = u[:, :D_TM_PROJ]
    xb = u[:, D_TM_PROJ:D_TM_PROJ + D_LRU]
    gate_lru = u[:, D_TM_PROJ + D_LRU:D_IN_PROJ]

    row = lax.broadcasted_iota(jnp.int32, (rows, D_TM), 0)

    um = u_tm + (_shift_time(tm_carry[...], u_tm, 1, bb) - u_tm) * mu_ref[...]
    r = um[:, 0:D_TM]
    k = um[:, D_TM:2 * D_TM]
    v = um[:, 2 * D_TM:3 * D_TM]
    x_lora = um[:, LORA_OFF:GATE_OFF]
    x_gate = um[:, GATE_OFF:D_TM_PROJ]
    log2_decay = (-math.exp(-0.5) * LOG2_E) * _sigmoid(w0_ref[...] + _dot(jnp.tanh(x_lora), wdec_ref[...]))
    a = _sigmoid(a0_ref[...] + _dot(x_lora, waaa_ref[...]))
    gate_tm = _dot(_sigmoid(x_gate), wgate_ref[...])
    kk = k * kk_ref[...]
    k = k * (a * ka_ref[...] + (1.0 - ka_ref[...]))
    kk = kk * lax.rsqrt(jnp.maximum(head_sum(kk * kk), 1e-24))
    bonus = head_sum(r * k * rk_ref[...]) * v
    acc = log2_decay[0:bb]
    cum = [acc]
    for t in range(1, chunk):
        acc = acc + log2_decay[t * bb:(t + 1) * bb]
        cum.append(acc)
    c = jnp.concatenate(cum, axis=0)
    inv_p = jnp.exp2(-c)
    kap = kk * jnp.exp2(c - log2_decay)
    rt = r * jnp.exp2(c)
    bt = kk * a * inv_p
    kt = k * inv_p
    p_end = jnp.exp2(cum[chunk - 1])
    for p in range(PAIRS):
        ls = slice(p * LANES, (p + 1) * LANES)
        kap_s[p, 0:rows, :] = kap[:, ls]
        rt_s[p, 0:rows, :] = rt[:, ls]
        bt_s[p, 0:rows, :] = bt[:, ls]
        kt_s[p, 0:rows, :] = kt[:, ls]
        v_s[p, 0:rows, :] = v[:, ls]
        pe_s[p] = p_end[:, ls]

    xb_prev = xb_carry[...]
    xc = cb_ref[...] + _shift_time(xb_prev, xb, 3, bb) * cw_ref[0:1, :]
    xc = xc + _shift_time(xb_prev, xb, 2, bb) * cw_ref[1:2, :]
    xc = xc + _shift_time(xb_prev, xb, 1, bb) * cw_ref[2:3, :]
    xc = xc + xb * cw_ref[3:4, :]
    xc_bf = xc.astype(BF16)

    gates = [jnp.dot(xc_bf[:, p * LANES:(p + 1) * LANES], wgates_ref[p], preferred_element_type=F32)
             for p in range(D_LRU // LANES)]
    r_g = _sigmoid(jnp.concatenate([g[:, :LANES] for g in gates], axis=-1) + ba_ref[...])
    i_g = _sigmoid(jnp.concatenate([g[:, LANES:] for g in gates], axis=-1) + bx_ref[...])
    la = jnp.exp2(r_g * ((-LRU_C * LOG2_E) * _softplus(-lam_ref[...])))
    gap = 1.0 - la * la
    mult = jnp.where(gap > 0.0, gap * lax.rsqrt(gap), 0.0)
    if pos0 == 0:
        mult = jnp.where(jnp.logical_and(row < bb, ti == 0), 1.0, mult)
    lb = xc * i_g * mult
    h = o_h_ref[...]
    hs = []
    for t in range(chunk):
        ts = slice(t * bb, (t + 1) * bb)
        h = la[ts] * h + lb[ts]
        hs.append(h)
    o_h_ref[...] = h
    y_lru = _rms(jnp.concatenate(hs, axis=0) * jax.nn.gelu(gate_lru), og_ref[...])

    wc = wkv_chunk
    n_lv = max(1, (chunk - 1).bit_length())
    cw2 = 2 * wc
    left = lax.broadcasted_iota(jnp.int32, (1, 1, LANES), 2) < TM_HEAD
    left_c = lax.broadcasted_iota(jnp.int32, (1, 1, cw2), 2) < wc
    ri = lax.broadcasted_iota(jnp.int32, (wc, cw2), 0)
    ci = lax.broadcasted_iota(jnp.int32, (wc, cw2), 1) & (wc - 1)
    strict = ri > ci
    incl = ri >= ci
    eye = (ri == ci).astype(F32)

    tsel = [pl.ds(b, wc, stride=bb) for b in range(bb)]
    gather = lambda ref: jnp.stack([ref.at[p][ts, :] for ts in tsel for p in range(PAIRS)])
    kap_g = gather(kap_s)
    rt_g = gather(rt_s)
    bt_g = gather(bt_s)
    kt_g = gather(kt_s)
    vv = gather(v_s)
    pe = jnp.stack([pe_s.at[p][pl.ds(b, 1), :] for b in range(bb) for p in range(PAIRS)])
    s0 = o_wkv_ref[...].reshape(bb * PAIRS, TM_HEAD, LANES)
    lhs = jnp.concatenate([kap_g, rt_g], axis=1)
    fused = _bdot_nt(lhs, jnp.concatenate([_block_diag_rows(bt_g, left), _block_diag_rows(kt_g, left),
                                           _block_diag_rows(s0, left)], axis=1))
    g_b = fused[:, :, 0:cw2]
    g_k = fused[:, :, cw2:2 * cw2]
    z = fused[:, :, 2 * cw2:]
    m_ab = jnp.where(strict, g_b[:, :wc], 0.0)
    m_ak = jnp.where(strict, g_k[:, :wc], 0.0)
    m_rb = jnp.where(incl, g_b[:, wc:], 0.0)
    m_rk = jnp.where(incl, g_k[:, wc:], 0.0)
    t_inv = eye - m_ab
    m_pow = _bdot(m_ab, _block_diag_rows(m_ab, left_c))
    for lv in range(1, n_lv):
        if lv < n_lv - 1:
            prod = _bdot(jnp.concatenate([t_inv, m_pow], axis=1), _block_diag_rows(m_pow, left_c))
            t_inv = t_inv + prod[:, :wc]
            m_pow = prod[:, wc:]
        else:
            t_inv = t_inv + _bdot(t_inv, _block_diag_rows(m_pow, left_c))
    xv = _bdot(jnp.concatenate([m_ak, m_rk], axis=1), _block_diag_rows(vv, left))
    uu = -_bdot(t_inv, _block_diag_rows(z[:, :wc] + xv[:, :wc], left))
    yh = z[:, wc:] + _bdot(m_rb, _block_diag_rows(uu, left)) + xv[:, wc:]
    full = _bdot_tn(jnp.concatenate([uu, vv], axis=1), jnp.concatenate([bt_g, kt_g], axis=1))
    upd = jnp.where(left, full[:, :TM_HEAD], full[:, TM_HEAD:])
    o_wkv_ref[...] = ((s0 + upd) * pe).reshape(bb, PAIRS, TM_HEAD, LANES)
    for b, ts in enumerate(tsel):
        for p in range(PAIRS):
            yt_s.at[p][ts, :] = yh[b * PAIRS + p]

    yv = jnp.concatenate([yt_s[p, 0:rows, :] for p in range(PAIRS)], axis=-1)
    mean = head_sum(yv) * (1.0 / TM_HEAD)
    cen = yv - mean
    var = head_sum(cen * cen) * (1.0 / TM_HEAD)
    y_tm = ((cen * lax.rsqrt(var + GN_EPS)) * gng_ref[...] + gnb_ref[...] + bonus) * gate_tm
    mixed = jnp.concatenate([y_tm.astype(BF16), y_lru.astype(BF16)], axis=-1)
    x1 = x_buf[x_slot].reshape(rows, D_MODEL) + jnp.dot(mixed, w_out_ref[...], preferred_element_type=F32)
    x1_ref[...] = x1.reshape(chunk, bb, D_MODEL)

    o_shift_ref[...] = u_tm[rows - bb:]
    o_conv_ref[...] = xb[rows - conv_rows:].reshape(LRU_CONV_W - 1, bb, D_LRU)
    tm_carry[...] = u_tm[rows - bb:]
    xb_carry[...] = xb[rows - conv_rows:]
    _read_chunk_end(x_stream, step, n_steps)


def _const_spec(shape):
    zeros = (0,) * len(shape)
    return pl.BlockSpec(shape, lambda bi, ti: zeros, pipeline_mode=pl.Buffered(1))


def _mixer(x, st_shift, st_wkv, st_conv, st_h, weights, *, bb, chunk, pos0):
    batch, length, _ = x.shape
    assert batch % bb == 0 and length % chunk == 0 and bb % SUBLANES == 0
    assert chunk & (chunk - 1) == 0 and chunk >= LRU_CONV_W - 1
    rows = bb * chunk
    grid = (batch // bb, length // chunk)
    act = pl.BlockSpec((chunk, bb, D_MODEL), lambda bi, ti: (ti, bi, 0))
    vec = lambda w: pl.BlockSpec((bb, w), lambda bi, ti: (bi, 0))
    wkv_spec = pl.BlockSpec((bb, PAIRS, TM_HEAD, LANES), lambda bi, ti: (bi, 0, 0, 0))
    conv_spec = pl.BlockSpec((LRU_CONV_W - 1, bb, D_LRU), lambda bi, ti: (0, bi, 0))
    in_specs = [pl.BlockSpec(memory_space=pl.ANY), vec(D_TM_PROJ), wkv_spec, conv_spec, vec(D_LRU)]
    in_specs += [_const_spec(w.shape) for w in weights]
    out_specs = [act, vec(D_TM_PROJ), wkv_spec, conv_spec, vec(D_LRU)]
    out_shape = [jax.ShapeDtypeStruct((length, batch, D_MODEL), F32),
                 jax.ShapeDtypeStruct((batch, D_TM_PROJ), F32),
                 jax.ShapeDtypeStruct((batch, PAIRS, TM_HEAD, LANES), F32),
                 jax.ShapeDtypeStruct((LRU_CONV_W - 1, batch, D_LRU), F32),
                 jax.ShapeDtypeStruct((batch, D_LRU), F32)]
    pair_rows = lambda n: pltpu.VMEM((PAIRS, n, LANES), F32)
    scratch = [pltpu.VMEM((bb, D_TM_PROJ), F32),
               pltpu.VMEM(((LRU_CONV_W - 1) * bb, D_LRU), F32)]
    scratch += [pair_rows(max(chunk, SUBLANES) * bb)] * 6 + [pair_rows(bb)]
    scratch += [pltpu.VMEM((READ_SLOTS, chunk, bb, D_MODEL), F32), pltpu.SemaphoreType.DMA((READ_SLOTS,))]
    return pl.pallas_call(
        functools.partial(_mixer_kernel, bb=bb, chunk=chunk, pos0=pos0,
                          n_blocks=grid[0], n_chunks=grid[1]),
        out_shape=out_shape, grid=grid, in_specs=in_specs, out_specs=out_specs,
        scratch_shapes=scratch, name="mixer",
        compiler_params=pltpu.CompilerParams(dimension_semantics=("arbitrary", "arbitrary"),
                                             vmem_limit_bytes=VMEM_LIMIT_BYTES),
    )(x, st_shift, st_wkv, st_conv, st_h, *weights)


def _ffn_kernel(x1_ref, st_conv_ref,
                g2_ref, w_up_ref, w_gate_ref, cw_ref, cb_ref, w_down_ref, gf_ref,
                y_hbm, o_conv_ref,
                up_carry, y_buf, y_sem,
                *, bb, chunk, n_blocks, n_chunks):
    rows = bb * chunk
    conv_rows = (FFN_CONV_W - 1) * bb
    ti = pl.program_id(1)
    step = pl.program_id(0) * n_chunks + ti
    n_steps = n_blocks * n_chunks
    y_stream = _TimeMajorStream(y_hbm, y_buf, y_sem, bb=bb, chunk=chunk, n_blocks=n_blocks,
                                n_chunks=n_chunks, to_hbm=True)
    slot = lax.rem(step, 2)

    @pl.when(ti == 0)
    def _():
        up_carry[...] = st_conv_ref[...].reshape(conv_rows, D_FF)

    x1 = x1_ref[...].reshape(rows, D_MODEL)
    xn = _rms(x1, g2_ref[...]).astype(BF16)
    acc = x1
    for n in range(D_FF // FF_COL_TILE):
        cols = slice(n * FF_COL_TILE, (n + 1) * FF_COL_TILE)
        up = jnp.dot(xn, w_up_ref[:, cols], preferred_element_type=F32)
        gate = jnp.dot(xn, w_gate_ref[:, cols], preferred_element_type=F32)
        up_prev = up_carry[:, cols]
        upc = cb_ref[:, cols] + _shift_time(up_prev, up, 2, bb) * cw_ref[0:1, cols]
        upc = upc + _shift_time(up_prev, up, 1, bb) * cw_ref[1:2, cols]
        upc = upc + up * cw_ref[2:3, cols]
        up_carry[:, cols] = up[rows - conv_rows:]
        o_conv_ref[:, :, cols] = up[rows - conv_rows:].reshape(FFN_CONV_W - 1, bb, FF_COL_TILE)
        hid = (jax.nn.gelu(upc) * gate).astype(BF16)
        acc = acc + jnp.dot(hid, w_down_ref[cols, :], preferred_element_type=F32)

    y_buf[slot] = _rms(acc, gf_ref[...]).reshape(chunk, bb, D_MODEL)
    y_stream.start(step, slot)

    @pl.when(step > 0)
    def _():
        y_stream.wait(step - 1, 1 - slot)

    @pl.when(step == n_steps - 1)
    def _():
        y_stream.wait(step, slot)


def _ffn(x1, st_conv, weights, *, bb, chunk):
    length, batch, _ = x1.shape
    assert batch % bb == 0 and length % chunk == 0 and bb % SUBLANES == 0
    assert chunk >= FFN_CONV_W - 1
    rows = bb * chunk
    grid = (batch // bb, length // chunk)
    act = pl.BlockSpec((chunk, bb, D_MODEL), lambda bi, ti: (ti, bi, 0))
    conv_spec = pl.BlockSpec((FFN_CONV_W - 1, bb, D_FF), lambda bi, ti: (0, bi, 0))
    in_specs = [act, conv_spec] + [_const_spec(w.shape) for w in weights]
    out_shape = [jax.ShapeDtypeStruct((batch, length, D_MODEL), F32),
                 jax.ShapeDtypeStruct((FFN_CONV_W - 1, batch, D_FF), F32)]
    scratch = [pltpu.VMEM(((FFN_CONV_W - 1) * bb, D_FF), F32),
               pltpu.VMEM((2, chunk, bb, D_MODEL), F32), pltpu.SemaphoreType.DMA((2,))]
    return pl.pallas_call(
        functools.partial(_ffn_kernel, bb=bb, chunk=chunk, n_blocks=grid[0], n_chunks=grid[1]),
        out_shape=out_shape, grid=grid, in_specs=in_specs,
        out_specs=[pl.BlockSpec(memory_space=pl.ANY), conv_spec],
        scratch_shapes=scratch, name="ffn",
        compiler_params=pltpu.CompilerParams(dimension_semantics=("arbitrary", "arbitrary"),
                                             vmem_limit_bytes=VMEM_LIMIT_BYTES),
    )(x1, st_conv, *weights)


def _ffn_state_kernel(x1_ref, g2_ref, w_up_ref, o_conv_ref, *, bb):
    tail = FFN_CONV_W - 1
    x1 = x1_ref[...].reshape(tail * bb, D_MODEL)
    up = _dot(_rms(x1, g2_ref[...]), w_up_ref[...])
    o_conv_ref[...] = up.reshape(tail, bb, D_FF)


def _ffn_state(x1, g2, w_up):
    length, batch, _ = x1.shape
    tail = FFN_CONV_W - 1
    assert length % tail == 0 and batch % SUBLANES == 0
    return pl.pallas_call(
        functools.partial(_ffn_state_kernel, bb=batch),
        out_shape=jax.ShapeDtypeStruct((tail, batch, D_FF), F32), grid=(1,),
        in_specs=[pl.BlockSpec((tail, batch, D_MODEL), lambda i: (length // tail - 1, 0, 0)),
                  pl.BlockSpec(g2.shape, lambda i: (0, 0)), pl.BlockSpec(w_up.shape, lambda i: (0, 0))],
        out_specs=pl.BlockSpec((tail, batch, D_FF), lambda i: (0, 0, 0)), name="ffn_state",
        compiler_params=pltpu.CompilerParams(dimension_semantics=("arbitrary",),
                                             vmem_limit_bytes=VMEM_LIMIT_BYTES),
    )(x1, g2, w_up)


RELAYOUT_ROWS = 32
RELAYOUT_AHEAD = 3
RELAYOUT_CHUNKS = tuple((p, v0) for p in range(PAIRS) for v0 in range(0, TM_HEAD, RELAYOUT_ROWS))


def _lanes_copy(hbm, buf, sem, c, *, to_hbm):
    p, v0 = RELAYOUT_CHUNKS[c]
    region = hbm.at[pl.ds(2 * p, 2), pl.ds(v0, RELAYOUT_ROWS)]
    src, dst = (buf.at[c], region) if to_hbm else (region, buf.at[c])
    return pltpu.make_async_copy(src, dst, sem.at[c])


def _pair_row_copy(hbm, buf, sem, c, i, *, to_hbm):
    p, v0 = RELAYOUT_CHUNKS[c]
    rows = hbm.at[:, p, v0 + i, :]
    src, dst = (buf.at[c, i], rows) if to_hbm else (rows, buf.at[c, i])
    return pltpu.make_async_copy(src, dst, sem.at[c])


def _pairs_from_lanes_kernel(a_hbm, o_hbm, lanes_buf, pairs_buf, lanes_sem, pairs_sem):
    batch = lanes_buf.shape[-1]
    n = len(RELAYOUT_CHUNKS)
    for c in range(n):
        _lanes_copy(a_hbm, lanes_buf, lanes_sem, c, to_hbm=False).start(priority=c % 2)
    for c in range(n):
        _lanes_copy(a_hbm, lanes_buf, lanes_sem, c, to_hbm=False).wait()
        for i in range(RELAYOUT_ROWS):
            pairs_buf[c, i] = lanes_buf[c, :, i].reshape(LANES, batch).T
        for i in range(RELAYOUT_ROWS):
            _pair_row_copy(o_hbm, pairs_buf, pairs_sem, c, i, to_hbm=True).start(priority=i % 2)
    for c in range(n):
        for i in range(RELAYOUT_ROWS):
            _pair_row_copy(o_hbm, pairs_buf, pairs_sem, c, i, to_hbm=True).wait()


def _lanes_from_pairs_kernel(s_hbm, o_hbm, lanes_buf, pairs_buf, lanes_sem, pairs_sem):
    batch = lanes_buf.shape[-1]
    n = len(RELAYOUT_CHUNKS)

    def gather(c):
        for i in range(RELAYOUT_ROWS):
            _pair_row_copy(s_hbm, pairs_buf, pairs_sem, c, i, to_hbm=False).start(priority=i % 2)

    for c in range(min(RELAYOUT_AHEAD, n)):
        gather(c)
    for c in range(n):
        if c + RELAYOUT_AHEAD < n:
            gather(c + RELAYOUT_AHEAD)
        for i in range(RELAYOUT_ROWS):
            _pair_row_copy(s_hbm, pairs_buf, pairs_sem, c, i, to_hbm=False).wait()
        for i in range(RELAYOUT_ROWS):
            lanes_buf[c, :, i] = pairs_buf[c, i].T.reshape(2, TM_HEAD, batch)
        _lanes_copy(o_hbm, lanes_buf, lanes_sem, c, to_hbm=True).start(priority=c % 2)
    for c in range(n):
        _lanes_copy(o_hbm, lanes_buf, lanes_sem, c, to_hbm=True).wait()


def _wkv_relayout(x, *, to_pairs):
    batch = x.shape[-1] if to_pairs else x.shape[0]
    assert batch == LANES
    n = len(RELAYOUT_CHUNKS)
    lanes_shape = (TM_HEADS, TM_HEAD, TM_HEAD, batch)
    pairs_shape = (batch, PAIRS, TM_HEAD, LANES)
    state_bytes = TM_HEADS * TM_HEAD * TM_HEAD * batch * 4
    hbm_spec = pl.BlockSpec(memory_space=pl.ANY)
    return pl.pallas_call(
        _pairs_from_lanes_kernel if to_pairs else _lanes_from_pairs_kernel,
        out_shape=jax.ShapeDtypeStruct(pairs_shape if to_pairs else lanes_shape, F32),
        in_specs=[hbm_spec], out_specs=hbm_spec,
        scratch_shapes=[pltpu.VMEM((n, 2, RELAYOUT_ROWS, TM_HEAD, batch), F32),
                        pltpu.VMEM((n, RELAYOUT_ROWS, batch, LANES), F32),
                        pltpu.SemaphoreType.DMA((n,)), pltpu.SemaphoreType.DMA((n,))],
        name="wkv_relayout",
        compiler_params=pltpu.CompilerParams(vmem_limit_bytes=2 * state_bytes + 8 * 1024 * 1024),
    )(x)


def _row(v):
    return v.reshape(1, -1).astype(F32)


def _pair_block_diag(w):
    nb, n, _ = w.shape
    w = w.reshape(nb // 2, 2, n, n)
    eye = jnp.eye(2, dtype=w.dtype)
    return (eye[None, :, None, :, None] * w[:, :, :, None, :]).reshape(nb // 2, 2 * n, 2 * n)


def _wkv_to_pairs(s):
    b = s.shape[0]
    if b == LANES:
        return _wkv_relayout(jnp.transpose(s.astype(F32), (1, 2, 3, 0)), to_pairs=True)
    s = s.astype(F32).reshape(b, PAIRS, 2, TM_HEAD, TM_HEAD)
    return jnp.transpose(s, (0, 1, 3, 2, 4)).reshape(b, PAIRS, TM_HEAD, LANES)


def _wkv_from_pairs(s):
    b = s.shape[0]
    if b == LANES:
        return jnp.transpose(_wkv_relayout(s, to_pairs=False), (3, 0, 1, 2))
    s = s.reshape(b, PAIRS, TM_HEAD, 2, TM_HEAD)
    return jnp.transpose(s, (0, 1, 3, 2, 4)).reshape(b, TM_HEADS, TM_HEAD, TM_HEAD)


def _layer(x, states, mixer_w, ffn_w, *, mixer_bb, ffn_bb, mixer_chunk, ffn_chunk, pos0):
    st_shift, st_wkv, st_conv, st_h, st_fconv = states
    x1, o_shift, o_wkv, o_conv, o_h = _mixer(x, st_shift, st_wkv, st_conv, st_h, mixer_w,
                                             bb=mixer_bb, chunk=mixer_chunk, pos0=pos0)
    y, o_fconv = _ffn(x1, st_fconv, ffn_w, bb=ffn_bb, chunk=ffn_chunk)
    return y, (o_shift, o_wkv, o_conv, o_h, o_fconv)


def kernel(x_prompt, x_sample, state_tm_shift, state_tm_wkv, state_lru_conv, state_lru_h, state_ffn_conv, meta_tokens, norm1_g, w_in, tm_mu, tm_w0, tm_w_up, tm_a0, tm_a_up, tm_g_up, tm_k_k, tm_k_a, tm_r_k, tm_gn_g, tm_gn_b, lru_conv_w, lru_conv_b, lru_wa, lru_ba, lru_wx, lru_bx, lru_lambda, lru_out_g, w_out, norm2_g, ffn_w_up, ffn_w_gate, ffn_conv_w, ffn_conv_b, ffn_w_down, norm_f_g):
    depth = w_in.shape[0]
    assert depth == 1
    l = 0
    zeros_lora = jnp.zeros((DECAY_RANK, D_TM), F32)
    head_id = jnp.arange(D_TM) // TM_HEAD
    seg = (head_id[:, None] == jnp.arange(LANES)[None, :]).astype(BF16)
    mixer_w = (
        _row(norm1_g[l]), w_in[l].astype(BF16), _row(tm_mu[l]), _row(tm_w0[l]),
        jnp.concatenate([tm_w_up[l], zeros_lora], axis=0).astype(BF16),
        _row(tm_a0[l]),
        jnp.concatenate([zeros_lora, tm_a_up[l]], axis=0).astype(BF16),
        tm_g_up[l].astype(BF16),
        _row(tm_k_k[l]), _row(tm_k_a[l]), _row(tm_r_k[l]), _row(tm_gn_g[l]), _row(tm_gn_b[l]),
        lru_conv_w[l].astype(F32), _row(lru_conv_b[l]),
        jnp.concatenate([_pair_block_diag(lru_wa[l]), _pair_block_diag(lru_wx[l])], axis=-1).astype(BF16),
        _row(lru_ba[l]), _row(lru_bx[l]),
        _row(lru_lambda[l]), _row(lru_out_g[l]),
        seg, seg.T, w_out[l].astype(BF16),
    )
    ffn_w = (
        _row(norm2_g[l]), ffn_w_up[l].astype(BF16), ffn_w_gate[l].astype(BF16),
        ffn_conv_w[l].astype(F32), _row(ffn_conv_b[l]), ffn_w_down[l].astype(BF16), _row(norm_f_g),
    )

    bsz, seq = x_prompt.shape[0], x_prompt.shape[1]
    x_meta = jnp.broadcast_to(meta_tokens[None].astype(F32), (bsz, N_META, D_MODEL))
    x1_meta, *meta_st = _mixer(x_meta, jnp.zeros((bsz, D_TM_PROJ), F32),
                               jnp.zeros((bsz, PAIRS, TM_HEAD, LANES), F32),
                               jnp.zeros((LRU_CONV_W - 1, bsz, D_LRU), F32),
                               jnp.zeros((bsz, D_LRU), F32), mixer_w,
                               bb=bsz, chunk=N_META, pos0=0)
    p_init = tuple(meta_st) + (_ffn_state(x1_meta, ffn_w[0], ffn_w[1]),)

    y_prompt, p_st = _layer(x_prompt, p_init, mixer_w, ffn_w,
                            mixer_bb=bsz, ffn_bb=bsz, mixer_chunk=64, ffn_chunk=64,
                            pos0=N_META)

    dec_b, dec_seq = x_sample.shape[0], x_sample.shape[1]
    s_init = (state_tm_shift[l].astype(F32), _wkv_to_pairs(state_tm_wkv[l]),
              jnp.transpose(state_lru_conv[l].astype(F32), (1, 0, 2)), state_lru_h[l].astype(F32),
              jnp.transpose(state_ffn_conv[l].astype(F32), (1, 0, 2)))
    y_sample, s_st = _layer(x_sample, s_init, mixer_w, ffn_w,
                            mixer_bb=32, ffn_bb=64, mixer_chunk=dec_seq, ffn_chunk=dec_seq,
                            pos0=PAST_LEN)

    def unpack(st):
        o_shift, o_wkv, o_conv, o_h, o_fconv = st
        return (o_shift[None], _wkv_from_pairs(o_wkv)[None], jnp.transpose(o_conv, (1, 0, 2))[None],
                o_h[None], jnp.transpose(o_fconv, (1, 0, 2))[None])

    return (y_prompt, y_sample) + unpack(p_st) + unpack(s_st)
```

```python
import functools
import math

import jax
import jax.numpy as jnp
from jax import lax
from jax.experimental import pallas as pl
from jax.experimental.pallas import tpu as pltpu

F32 = jnp.float32
BF16 = jnp.bfloat16

D_MODEL = 1024
N_META = 16
PAST_LEN = 16384
D_TM = 512
TM_HEAD = 64
TM_HEADS = 8
DECAY_RANK = 64
AAA_RANK = 64
GATE_RANK = 128
D_TM_PROJ = 3 * D_TM + DECAY_RANK + AAA_RANK + GATE_RANK
D_LRU = 512
LRU_CONV_W = 4
LRU_C = 8.0
D_IN_PROJ = D_TM_PROJ + 2 * D_LRU
D_FF = 3 * D_MODEL
FFN_CONV_W = 3
EPS = 1e-6
GN_EPS = 64e-5
LOG2_E = math.log2(math.e)

SUBLANES = 8
LANES = 128
PAIRS = TM_HEADS // 2
LORA_OFF = 3 * D_TM
GATE_OFF = LORA_OFF + DECAY_RANK + AAA_RANK
FF_COL_TILE = 1024
VMEM_LIMIT_BYTES = 60 * 1024 * 1024


def _dot(a, b):
    return jnp.dot(a.astype(BF16), b.astype(BF16), preferred_element_type=F32)


def _bdot(a, b):
    return lax.dot_general(a.astype(BF16), b.astype(BF16), (((2,), (1,)), ((0,), (0,))),
                           preferred_element_type=F32)


def _bdot_nt(a, b):
    return lax.dot_general(a.astype(BF16), b.astype(BF16), (((2,), (2,)), ((0,), (0,))),
                           preferred_element_type=F32)


def _bdot_tn(a, b):
    return lax.dot_general(a.astype(BF16), b.astype(BF16), (((1,), (1,)), ((0,), (0,))),
                           preferred_element_type=F32)


def _rms(x, g):
    return x * lax.rsqrt(jnp.mean(x * x, axis=-1, keepdims=True) + EPS) * g


def _sigmoid(z):
    return 0.5 * jnp.tanh(0.5 * z) + 0.5


def _softplus(z):
    return jnp.maximum(z, 0.0) + jnp.log(1.0 + jnp.exp(-jnp.abs(z)))


def _block_diag_rows(x, left):
    return jnp.concatenate([jnp.where(left, x, 0.0), jnp.where(left, 0.0, x)], axis=1)


def _shift_time(carry, x, steps, bb):
    n = steps * bb
    return jnp.concatenate([carry[carry.shape[0] - n:], x[:x.shape[0] - n]], axis=0)


class _TimeMajorStream:
    def __init__(self, hbm, buf, sem, *, bb, chunk, n_blocks, n_chunks, to_hbm):
        self.hbm, self.buf, self.sem = hbm, buf, sem
        self.bb, self.chunk, self.to_hbm = bb, chunk, to_hbm
        self.n_blocks, self.n_chunks = n_blocks, n_chunks

    def _copy(self, g, slot, i):
        if self.n_chunks == 1:
            bi, ti = g, 0
        elif self.n_blocks == 1:
            bi, ti = 0, g
        else:
            bi, ti = lax.div(g, self.n_chunks), lax.rem(g, self.n_chunks)
        if self.bb < self.chunk:
            hbm_rows = self.hbm.at[bi * self.bb + i, pl.ds(ti * self.chunk, self.chunk), :]
            tile = self.buf.at[slot, :, i, :]
        else:
            seqs = pl.ds(pl.multiple_of(bi * self.bb, SUBLANES), self.bb)
            hbm_rows = self.hbm.at[seqs, ti * self.chunk + i, :]
            tile = self.buf.at[slot, i]
        src, dst = (tile, hbm_rows) if self.to_hbm else (hbm_rows, tile)
        return pltpu.make_async_copy(src, dst, self.sem.at[slot])

    def start(self, g, slot):
        for i in range(min(self.bb, self.chunk)):
            self._copy(g, slot, i).start()

    def wait(self, g, slot):
        for i in range(min(self.bb, self.chunk)):
            self._copy(g, slot, i).wait()


READ_SLOTS = 3


def _read_chunk_begin(stream, g, n_steps):
    last = n_steps - 1

    @pl.when(g == 0)
    def _():
        stream.start(0, 0)
        stream.start(jnp.minimum(1, last), 1)

    slot = lax.rem(g, READ_SLOTS)
    stream.wait(g, slot)
    return slot


def _read_chunk_end(stream, g, n_steps):
    last = n_steps - 1
    stream.start(jnp.minimum(g + 2, last), lax.rem(g + 2, READ_SLOTS))

    @pl.when(g == last)
    def _():
        stream.wait(last, lax.rem(g + 1, READ_SLOTS))
        stream.wait(last, lax.rem(g + 2, READ_SLOTS))


def _mixer_kernel(x_hbm, st_shift_ref, st_wkv_ref, st_conv_ref, st_h_ref,
                  g1_ref, w_in_hbm, mu_ref, w0_ref, wdec_ref, a0_ref, waaa_ref, wgate_ref,
                  kk_ref, ka_ref, rk_ref, gng_ref, gnb_ref,
                  cw_ref, cb_ref, wgates_ref, ba_ref, bx_ref, lam_ref, og_ref,
                  seg_ref, segt_ref, w_out_hbm,
                  x1_ref, o_shift_ref, o_wkv_ref, o_conv_ref, o_h_ref,
                  tm_carry, xb_carry, kap_s, rt_s, bt_s, kt_s, v_s, yt_s, pe_s, x_buf, x_sem,
                  w_in_ref, w_out_ref, w_sem,
                  *, bb, chunk, pos0, n_blocks, n_chunks):
    rows = bb * chunk
    conv_rows = (LRU_CONV_W - 1) * bb
    wkv_chunk = max(chunk, SUBLANES)
    ti = pl.program_id(1)
    step = pl.program_id(0) * n_chunks + ti
    n_steps = n_blocks * n_chunks

    @pl.when(step == 0)
    def _():
        _load_matrices((w_in_hbm, w_out_hbm), (w_in_ref, w_out_ref), w_sem)

    x_stream = _TimeMajorStream(x_hbm, x_buf, x_sem, bb=bb, chunk=chunk, n_blocks=n_blocks,
                                n_chunks=n_chunks, to_hbm=False)
    x_slot = _read_chunk_begin(x_stream, step, n_steps)

    @pl.when(ti == 0)
    def _():
        tm_carry[...] = st_shift_ref[...]
        xb_carry[...] = st_conv_ref[...].reshape(conv_rows, D_LRU)
        o_wkv_ref[...] = st_wkv_ref[...]
        o_h_ref[...] = st_h_ref[...]
        if wkv_chunk > chunk:
            for ref in (kap_s, rt_s, bt_s, kt_s, v_s):
                ref[:, rows:wkv_chunk * bb, :] = jnp.zeros((PAIRS, wkv_chunk * bb - rows, LANES), F32)

    seg = seg_ref[...]
    segt = segt_ref[...]

    def head_sum(x):
        s = jnp.dot(x.astype(BF16), seg, preferred_element_type=F32)
        return jnp.dot(s.astype(BF16), segt, preferred_element_type=F32)

    x = x_buf[x_slot].reshape(rows, D_MODEL)
    u = _dot(_rms(x, g1_ref[...]), w_in_ref[...])
    u_tm = u[:, :D_TM_PROJ]
    xb = u[:, D_TM_PROJ:D_TM_PROJ + D_LRU]
    gate_lru = u[:, D_TM_PROJ + D_LRU:D_IN_PROJ]

    row = lax.broadcasted_iota(jnp.int32, (rows, D_TM), 0)

    um = u_tm + (_shift_time(tm_carry[...], u_tm, 1, bb) - u_tm) * mu_ref[...]
    r = um[:, 0:D_TM]
    k = um[:, D_TM:2 * D_TM]
    v = um[:, 2 * D_TM:3 * D_TM]
    x_lora = um[:, LORA_OFF:GATE_OFF]
    x_gate = um[:, GATE_OFF:D_TM_PROJ]
    log2_decay = (-math.exp(-0.5) * LOG2_E) * _sigmoid(w0_ref[...] + _dot(jnp.tanh(x_lora), wdec_ref[...]))
    a = _sigmoid(a0_ref[...] + _dot(x_lora, waaa_ref[...]))
    gate_tm = _dot(_sigmoid(x_gate), wgate_ref[...])
    kk = k * kk_ref[...]
    k = k * (a * ka_ref[...] + (1.0 - ka_ref[...]))
    kk = kk * lax.rsqrt(jnp.maximum(head_sum(kk * kk), 1e-24))
    bonus = head_sum(r * k * rk_ref[...]) * v
    acc = log2_decay[0:bb]
    cum = [acc]
    for t in range(1, chunk):
        acc = acc + log2_decay[t * bb:(t + 1) * bb]
        cum.append(acc)
    c = jnp.concatenate(cum, axis=0)
    inv_p = jnp.exp2(-c)
    kap = kk * jnp.exp2(c - log2_decay)
    rt = r * jnp.exp2(c)
    bt = kk * a * inv_p
    kt = k * inv_p
    p_end = jnp.exp2(cum[chunk - 1])
    for p in range(PAIRS):
        ls = slice(p * LANES, (p + 1) * LANES)
        kap_s[p, 0:rows, :] = kap[:, ls]
        rt_s[p, 0:rows, :] = rt[:, ls]
        bt_s[p, 0:rows, :] = bt[:, ls]
        kt_s[p, 0:rows, :] = kt[:, ls]
        v_s[p, 0:rows, :] = v[:, ls]
        pe_s[p] = p_end[:, ls]

    xb_prev = xb_carry[...]
    xc = cb_ref[...] + _shift_time(xb_prev, xb, 3, bb) * cw_ref[0:1, :]
    xc = xc + _shift_time(xb_prev, xb, 2, bb) * cw_ref[1:2, :]
    xc = xc + _shift_time(xb_prev, xb, 1, bb) * cw_ref[2:3, :]
    xc = xc + xb * cw_ref[3:4, :]
    xc_bf = xc.astype(BF16)

    gates = [jnp.dot(xc_bf[:, p * LANES:(p + 1) * LANES], wgates_ref[p], preferred_element_type=F32)
             for p in range(D_LRU // LANES)]
    r_g = _sigmoid(jnp.concatenate([g[:, :LANES] for g in gates], axis=-1) + ba_ref[...])
    i_g = _sigmoid(jnp.concatenate([g[:, LANES:] for g in gates], axis=-1) + bx_ref[...])
    la = jnp.exp2(r_g * ((-LRU_C * LOG2_E) * _softplus(-lam_ref[...])))
    gap = 1.0 - la * la
    mult = jnp.where(gap > 0.0, gap * lax.rsqrt(gap), 0.0)
    if pos0 == 0:
        mult = jnp.where(jnp.logical_and(row < bb, ti == 0), 1.0, mult)
    lb = xc * i_g * mult
    h = o_h_ref[...]
    hs = []
    for t in range(chunk):
        ts = slice(t * bb, (t + 1) * bb)
        h = la[ts] * h + lb[ts]
        hs.append(h)
    o_h_ref[...] = h
    y_lru = _rms(jnp.concatenate(hs, axis=0) * jax.nn.gelu(gate_lru), og_ref[...])

    wc = wkv_chunk
    n_lv = max(1, (chunk - 1).bit_length())
    cw2 = 2 * wc
    left = lax.broadcasted_iota(jnp.int32, (1, 1, LANES), 2) < TM_HEAD
    left_c = lax.broadcasted_iota(jnp.int32, (1, 1, cw2), 2) < wc
    ri = lax.broadcasted_iota(jnp.int32, (wc, cw2), 0)
    ci = lax.broadcasted_iota(jnp.int32, (wc, cw2), 1) & (wc - 1)
    strict = ri > ci
    incl = ri >= ci
    eye = (ri == ci).astype(F32)

    tsel = [pl.ds(b, wc, stride=bb) for b in range(bb)]
    gather = lambda ref: jnp.stack([ref.at[p][ts, :] for ts in tsel for p in range(PAIRS)])
    kap_g = gather(kap_s)
    rt_g = gather(rt_s)
    bt_g = gather(bt_s)
    kt_g = gather(kt_s)
    vv = gather(v_s)
    pe = jnp.stack([pe_s.at[p][pl.ds(b, 1), :] for b in range(bb) for p in range(PAIRS)])
    s0 = o_wkv_ref[...].reshape(bb * PAIRS, TM_HEAD, LANES)
    lhs = jnp.concatenate([kap_g, rt_g], axis=1)
    fused = _bdot_nt(lhs, jnp.concatenate([_block_diag_rows(bt_g, left), _block_diag_rows(kt_g, left),
                                           _block_diag_rows(s0, left)], axis=1))
    g_b = fused[:, :, 0:cw2]
    g_k = fused[:, :, cw2:2 * cw2]
    z = fused[:, :, 2 * cw2:]
    m_ab = jnp.where(strict, g_b[:, :wc], 0.0)
    m_ak = jnp.where(strict, g_k[:, :wc], 0.0)
    m_rb = jnp.where(incl, g_b[:, wc:], 0.0)
    m_rk = jnp.where(incl, g_k[:, wc:], 0.0)
    t_inv = eye - m_ab
    m_pow = _bdot(m_ab, _block_diag_rows(m_ab, left_c))
    for lv in range(1, n_lv):
        if lv < n_lv - 1:
            prod = _bdot(jnp.concatenate([t_inv, m_pow], axis=1), _block_diag_rows(m_pow, left_c))
            t_inv = t_inv + prod[:, :wc]
            m_pow = prod[:, wc:]
        else:
            t_inv = t_inv + _bdot(t_inv, _block_diag_rows(m_pow, left_c))
    xv = _bdot(jnp.concatenate([m_ak, m_rk], axis=1), _block_diag_rows(vv, left))
    uu = -_bdot(t_inv, _block_diag_rows(z[:, :wc] + xv[:, :wc], left))
    yh = z[:, wc:] + _bdot(m_rb, _block_diag_rows(uu, left)) + xv[:, wc:]
    full = _bdot_tn(jnp.concatenate([uu, vv], axis=1), jnp.concatenate([bt_g, kt_g], axis=1))
    upd = jnp.where(left, full[:, :TM_HEAD], full[:, TM_HEAD:])
    o_wkv_ref[...] = ((s0 + upd) * pe).reshape(bb, PAIRS, TM_HEAD, LANES)
    for b, ts in enumerate(tsel):
        for p in range(PAIRS):
            yt_s.at[p][ts, :] = yh[b * PAIRS + p]

    yv = jnp.concatenate([yt_s[p, 0:rows, :] for p in range(PAIRS)], axis=-1)
    mean = head_sum(yv) * (1.0 / TM_HEAD)
    cen = yv - mean
    var = head_sum(cen * cen) * (1.0 / TM_HEAD)
    y_tm = ((cen * lax.rsqrt(var + GN_EPS)) * gng_ref[...] + gnb_ref[...] + bonus) * gate_tm
    mixed = jnp.concatenate([y_tm.astype(BF16), y_lru.astype(BF16)], axis=-1)
    x1 = x_buf[x_slot].reshape(rows, D_MODEL) + jnp.dot(mixed, w_out_ref[...], preferred_element_type=F32)
    x1_ref[...] = x1.reshape(chunk, bb, D_MODEL)

    o_shift_ref[...] = u_tm[rows - bb:]
    o_conv_ref[...] = xb[rows - conv_rows:].reshape(LRU_CONV_W - 1, bb, D_LRU)
    tm_carry[...] = u_tm[rows - bb:]
    xb_carry[...] = xb[rows - conv_rows:]
    _read_chunk_end(x_stream, step, n_steps)


def _const_spec(shape):
    zeros = (0,) * len(shape)
    return pl.BlockSpec(shape, lambda bi, ti: zeros, pipeline_mode=pl.Buffered(1))


MATRIX_PIECES = 4
MATRIX_MIN_SIZE = 1 << 20


def _is_matrix(w):
    return w.ndim == 2 and w.size >= MATRIX_MIN_SIZE


def _weight_specs(weights):
    big = [_is_matrix(w) for w in weights]
    specs = [pl.BlockSpec(memory_space=pl.ANY) if b else _const_spec(w.shape) for w, b in zip(weights, big)]
    scratch = [pltpu.VMEM(w.shape, BF16) for w, b in zip(weights, big) if b]
    return specs, scratch + [pltpu.SemaphoreType.DMA((1,))]


def _load_matrices(hbm_refs, vmem_refs, sem):
    copies = []
    for hbm, vmem in zip(hbm_refs, vmem_refs):
        rows = hbm.shape[0] // MATRIX_PIECES
        for j in range(MATRIX_PIECES):
            piece = pl.ds(j * rows, rows)
            copies.append(pltpu.make_async_copy(hbm.at[piece], vmem.at[piece], sem.at[0]))
    for j, copy in enumerate(copies):
        copy.start(priority=j % 2)
    for copy in copies:
        copy.wait()


def _mixer(x, st_shift, st_wkv, st_conv, st_h, weights, *, bb, chunk, pos0):
    batch, length, _ = x.shape
    assert batch % bb == 0 and length % chunk == 0 and bb % SUBLANES == 0
    assert chunk & (chunk - 1) == 0 and chunk >= LRU_CONV_W - 1
    rows = bb * chunk
    grid = (batch // bb, length // chunk)
    act = pl.BlockSpec((chunk, bb, D_MODEL), lambda bi, ti: (ti, bi, 0))
    vec = lambda w: pl.BlockSpec((bb, w), lambda bi, ti: (bi, 0))
    wkv_spec = pl.BlockSpec((bb, PAIRS, TM_HEAD, LANES), lambda bi, ti: (bi, 0, 0, 0))
    conv_spec = pl.BlockSpec((LRU_CONV_W - 1, bb, D_LRU), lambda bi, ti: (0, bi, 0))
    in_specs = [pl.BlockSpec(memory_space=pl.ANY), vec(D_TM_PROJ), wkv_spec, conv_spec, vec(D_LRU)]
    weight_specs, weight_scratch = _weight_specs(weights)
    in_specs += weight_specs
    out_specs = [act, vec(D_TM_PROJ), wkv_spec, conv_spec, vec(D_LRU)]
    out_shape = [jax.ShapeDtypeStruct((length, batch, D_MODEL), F32),
                 jax.ShapeDtypeStruct((batch, D_TM_PROJ), F32),
                 jax.ShapeDtypeStruct((batch, PAIRS, TM_HEAD, LANES), F32),
                 jax.ShapeDtypeStruct((LRU_CONV_W - 1, batch, D_LRU), F32),
                 jax.ShapeDtypeStruct((batch, D_LRU), F32)]
    pair_rows = lambda n: pltpu.VMEM((PAIRS, n, LANES), F32)
    scratch = [pltpu.VMEM((bb, D_TM_PROJ), F32),
               pltpu.VMEM(((LRU_CONV_W - 1) * bb, D_LRU), F32)]
    scratch += [pair_rows(max(chunk, SUBLANES) * bb)] * 6 + [pair_rows(bb)]
    scratch += [pltpu.VMEM((READ_SLOTS, chunk, bb, D_MODEL), F32), pltpu.SemaphoreType.DMA((READ_SLOTS,))]
    scratch += weight_scratch
    return pl.pallas_call(
        functools.partial(_mixer_kernel, bb=bb, chunk=chunk, pos0=pos0,
                          n_blocks=grid[0], n_chunks=grid[1]),
        out_shape=out_shape, grid=grid, in_specs=in_specs, out_specs=out_specs,
        scratch_shapes=scratch, name="mixer",
        compiler_params=pltpu.CompilerParams(dimension_semantics=("arbitrary", "arbitrary"),
                                             vmem_limit_bytes=VMEM_LIMIT_BYTES),
    )(x, st_shift, st_wkv, st_conv, st_h, *weights)


def _ffn_kernel(x1_ref, st_conv_ref,
                g2_ref, w_up_hbm, w_gate_hbm, cw_ref, cb_ref, w_down_hbm, gf_ref,
                y_hbm, o_conv_ref,
                up_carry, y_buf, y_sem, w_up_ref, w_gate_ref, w_down_ref, w_sem,
                *, bb, chunk, n_blocks, n_chunks):
    rows = bb * chunk
    conv_rows = (FFN_CONV_W - 1) * bb
    ti = pl.program_id(1)
    step = pl.program_id(0) * n_chunks + ti
    n_steps = n_blocks * n_chunks

    @pl.when(step == 0)
    def _():
        _load_matrices((w_up_hbm, w_gate_hbm, w_down_hbm), (w_up_ref, w_gate_ref, w_down_ref), w_sem)

    y_stream = _TimeMajorStream(y_hbm, y_buf, y_sem, bb=bb, chunk=chunk, n_blocks=n_blocks,
                                n_chunks=n_chunks, to_hbm=True)
    slot = lax.rem(step, 2)

    @pl.when(ti == 0)
    def _():
        up_carry[...] = st_conv_ref[...].reshape(conv_rows, D_FF)

    x1 = x1_ref[...].reshape(rows, D_MODEL)
    xn = _rms(x1, g2_ref[...]).astype(BF16)
    acc = x1
    for n in range(D_FF // FF_COL_TILE):
        cols = slice(n * FF_COL_TILE, (n + 1) * FF_COL_TILE)
        up = jnp.dot(xn, w_up_ref[:, cols], preferred_element_type=F32)
        gate = jnp.dot(xn, w_gate_ref[:, cols], preferred_element_type=F32)
        up_prev = up_carry[:, cols]
        upc = cb_ref[:, cols] + _shift_time(up_prev, up, 2, bb) * cw_ref[0:1, cols]
        upc = upc + _shift_time(up_prev, up, 1, bb) * cw_ref[1:2, cols]
        upc = upc + up * cw_ref[2:3, cols]
        up_carry[:, cols] = up[rows - conv_rows:]
        o_conv_ref[:, :, cols] = up[rows - conv_rows:].reshape(FFN_CONV_W - 1, bb, FF_COL_TILE)
        hid = (jax.nn.gelu(upc) * gate).astype(BF16)
        acc = acc + jnp.dot(hid, w_down_ref[cols, :], preferred_element_type=F32)

    y_buf[slot] = _rms(acc, gf_ref[...]).reshape(chunk, bb, D_MODEL)
    y_stream.start(step, slot)

    @pl.when(step > 0)
    def _():
        y_stream.wait(step - 1, 1 - slot)

    @pl.when(step == n_steps - 1)
    def _():
        y_stream.wait(step, slot)


def _ffn(x1, st_conv, weights, *, bb, chunk):
    length, batch, _ = x1.shape
    assert batch % bb == 0 and length % chunk == 0 and bb % SUBLANES == 0
    assert chunk >= FFN_CONV_W - 1
    rows = bb * chunk
    grid = (batch // bb, length // chunk)
    act = pl.BlockSpec((chunk, bb, D_MODEL), lambda bi, ti: (ti, bi, 0))
    conv_spec = pl.BlockSpec((FFN_CONV_W - 1, bb, D_FF), lambda bi, ti: (0, bi, 0))
    weight_specs, weight_scratch = _weight_specs(weights)
    in_specs = [act, conv_spec] + weight_specs
    out_shape = [jax.ShapeDtypeStruct((batch, length, D_MODEL), F32),
                 jax.ShapeDtypeStruct((FFN_CONV_W - 1, batch, D_FF), F32)]
    scratch = [pltpu.VMEM(((FFN_CONV_W - 1) * bb, D_FF), F32),
               pltpu.VMEM((2, chunk, bb, D_MODEL), F32), pltpu.SemaphoreType.DMA((2,))]
    scratch += weight_scratch
    return pl.pallas_call(
        functools.partial(_ffn_kernel, bb=bb, chunk=chunk, n_blocks=grid[0], n_chunks=grid[1]),
        out_shape=out_shape, grid=grid, in_specs=in_specs,
        out_specs=[pl.BlockSpec(memory_space=pl.ANY), conv_spec],
        scratch_shapes=scratch, name="ffn",
        compiler_params=pltpu.CompilerParams(dimension_semantics=("arbitrary", "arbitrary"),
                                             vmem_limit_bytes=VMEM_LIMIT_BYTES),
    )(x1, st_conv, *weights)


def _ffn_state_kernel(x1_ref, g2_ref, w_up_ref, o_conv_ref, *, bb):
    tail = FFN_CONV_W - 1
    x1 = x1_ref[...].reshape(tail * bb, D_MODEL)
    up = _dot(_rms(x1, g2_ref[...]), w_up_ref[...])
    o_conv_ref[...] = up.reshape(tail, bb, D_FF)


def _ffn_state(x1, g2, w_up):
    length, batch, _ = x1.shape
    tail = FFN_CONV_W - 1
    assert length % tail == 0 and batch % SUBLANES == 0
    return pl.pallas_call(
        functools.partial(_ffn_state_kernel, bb=batch),
        out_shape=jax.ShapeDtypeStruct((tail, batch, D_FF), F32), grid=(1,),
        in_specs=[pl.BlockSpec((tail, batch, D_MODEL), lambda i: (length // tail - 1, 0, 0)),
                  pl.BlockSpec(g2.shape, lambda i: (0, 0)), pl.BlockSpec(w_up.shape, lambda i: (0, 0))],
        out_specs=pl.BlockSpec((tail, batch, D_FF), lambda i: (0, 0, 0)), name="ffn_state",
        compiler_params=pltpu.CompilerParams(dimension_semantics=("arbitrary",),
                                             vmem_limit_bytes=VMEM_LIMIT_BYTES),
    )(x1, g2, w_up)


RELAYOUT_ROWS = 32
RELAYOUT_AHEAD = 3
RELAYOUT_CHUNKS = tuple((p, v0) for p in range(PAIRS) for v0 in range(0, TM_HEAD, RELAYOUT_ROWS))


def _lanes_copy(hbm, buf, sem, c, *, to_hbm):
    p, v0 = RELAYOUT_CHUNKS[c]
    region = hbm.at[pl.ds(2 * p, 2), pl.ds(v0, RELAYOUT_ROWS)]
    src, dst = (buf.at[c], region) if to_hbm else (region, buf.at[c])
    return pltpu.make_async_copy(src, dst, sem.at[c])


def _pair_row_copy(hbm, buf, sem, c, i, *, to_hbm):
    p, v0 = RELAYOUT_CHUNKS[c]
    rows = hbm.at[:, p, v0 + i, :]
    src, dst = (buf.at[c, i], rows) if to_hbm else (rows, buf.at[c, i])
    return pltpu.make_async_copy(src, dst, sem.at[c])


def _pairs_from_lanes_kernel(a_hbm, o_hbm, lanes_buf, pairs_buf, lanes_sem, pairs_sem):
    batch = lanes_buf.shape[-1]
    n = len(RELAYOUT_CHUNKS)
    for c in range(n):
        _lanes_copy(a_hbm, lanes_buf, lanes_sem, c, to_hbm=False).start(priority=c % 2)
    for c in range(n):
        _lanes_copy(a_hbm, lanes_buf, lanes_sem, c, to_hbm=False).wait()
        for i in range(RELAYOUT_ROWS):
            pairs_buf[c, i] = lanes_buf[c, :, i].reshape(LANES, batch).T
        for i in range(RELAYOUT_ROWS):
            _pair_row_copy(o_hbm, pairs_buf, pairs_sem, c, i, to_hbm=True).start(priority=i % 2)
    for c in range(n):
        for i in range(RELAYOUT_ROWS):
            _pair_row_copy(o_hbm, pairs_buf, pairs_sem, c, i, to_hbm=True).wait()


def _lanes_from_pairs_kernel(s_hbm, o_hbm, lanes_buf, pairs_buf, lanes_sem, pairs_sem):
    batch = lanes_buf.shape[-1]
    n = len(RELAYOUT_CHUNKS)

    def gather(c):
        for i in range(RELAYOUT_ROWS):
            _pair_row_copy(s_hbm, pairs_buf, pairs_sem, c, i, to_hbm=False).start(priority=i % 2)

    for c in range(min(RELAYOUT_AHEAD, n)):
        gather(c)
    for c in range(n):
        if c + RELAYOUT_AHEAD < n:
            gather(c + RELAYOUT_AHEAD)
        for i in range(RELAYOUT_ROWS):
            _pair_row_copy(s_hbm, pairs_buf, pairs_sem, c, i, to_hbm=False).wait()
        for i in range(RELAYOUT_ROWS):
            lanes_buf[c, :, i] = pairs_buf[c, i].T.reshape(2, TM_HEAD, batch)
        _lanes_copy(o_hbm, lanes_buf, lanes_sem, c, to_hbm=True).start(priority=c % 2)
    for c in range(n):
        _lanes_copy(o_hbm, lanes_buf, lanes_sem, c, to_hbm=True).wait()


def _wkv_relayout(x, *, to_pairs):
    batch = x.shape[-1] if to_pairs else x.shape[0]
    assert batch == LANES
    n = len(RELAYOUT_CHUNKS)
    lanes_shape = (TM_HEADS, TM_HEAD, TM_HEAD, batch)
    pairs_shape = (batch, PAIRS, TM_HEAD, LANES)
    state_bytes = TM_HEADS * TM_HEAD * TM_HEAD * batch * 4
    hbm_spec = pl.BlockSpec(memory_space=pl.ANY)
    return pl.pallas_call(
        _pairs_from_lanes_kernel if to_pairs else _lanes_from_pairs_kernel,
        out_shape=jax.ShapeDtypeStruct(pairs_shape if to_pairs else lanes_shape, F32),
        in_specs=[hbm_spec], out_specs=hbm_spec,
        scratch_shapes=[pltpu.VMEM((n, 2, RELAYOUT_ROWS, TM_HEAD, batch), F32),
                        pltpu.VMEM((n, RELAYOUT_ROWS, batch, LANES), F32),
                        pltpu.SemaphoreType.DMA((n,)), pltpu.SemaphoreType.DMA((n,))],
        name="wkv_relayout",
        compiler_params=pltpu.CompilerParams(vmem_limit_bytes=2 * state_bytes + 8 * 1024 * 1024),
    )(x)


def _row(v):
    return v.reshape(1, -1).astype(F32)


def _pair_block_diag(w):
    nb, n, _ = w.shape
    w = w.reshape(nb // 2, 2, n, n)
    eye = jnp.eye(2, dtype=w.dtype)
    return (eye[None, :, None, :, None] * w[:, :, :, None, :]).reshape(nb // 2, 2 * n, 2 * n)


def _wkv_to_pairs(s):
    b = s.shape[0]
    if b == LANES:
        return _wkv_relayout(jnp.transpose(s.astype(F32), (1, 2, 3, 0)), to_pairs=True)
    s = s.astype(F32).reshape(b, PAIRS, 2, TM_HEAD, TM_HEAD)
    return jnp.transpose(s, (0, 1, 3, 2, 4)).reshape(b, PAIRS, TM_HEAD, LANES)


def _wkv_from_pairs(s):
    b = s.shape[0]
    if b == LANES:
        return jnp.transpose(_wkv_relayout(s, to_pairs=False), (3, 0, 1, 2))
    s = s.reshape(b, PAIRS, TM_HEAD, 2, TM_HEAD)
    return jnp.transpose(s, (0, 1, 3, 2, 4)).reshape(b, TM_HEADS, TM_HEAD, TM_HEAD)


def _layer(x, states, mixer_w, ffn_w, *, mixer_bb, ffn_bb, mixer_chunk, ffn_chunk, pos0):
    st_shift, st_wkv, st_conv, st_h, st_fconv = states
    x1, o_shift, o_wkv, o_conv, o_h = _mixer(x, st_shift, st_wkv, st_conv, st_h, mixer_w,
                                             bb=mixer_bb, chunk=mixer_chunk, pos0=pos0)
    y, o_fconv = _ffn(x1, st_fconv, ffn_w, bb=ffn_bb, chunk=ffn_chunk)
    return y, (o_shift, o_wkv, o_conv, o_h, o_fconv)


def kernel(x_prompt, x_sample, state_tm_shift, state_tm_wkv, state_lru_conv, state_lru_h, state_ffn_conv, meta_tokens, norm1_g, w_in, tm_mu, tm_w0, tm_w_up, tm_a0, tm_a_up, tm_g_up, tm_k_k, tm_k_a, tm_r_k, tm_gn_g, tm_gn_b, lru_conv_w, lru_conv_b, lru_wa, lru_ba, lru_wx, lru_bx, lru_lambda, lru_out_g, w_out, norm2_g, ffn_w_up, ffn_w_gate, ffn_conv_w, ffn_conv_b, ffn_w_down, norm_f_g):
    depth = w_in.shape[0]
    assert depth == 1
    l = 0
    zeros_lora = jnp.zeros((DECAY_RANK, D_TM), F32)
    head_id = jnp.arange(D_TM) // TM_HEAD
    seg = (head_id[:, None] == jnp.arange(LANES)[None, :]).astype(BF16)
    mixer_w = (
        _row(norm1_g[l]), w_in[l].astype(BF16), _row(tm_mu[l]), _row(tm_w0[l]),
        jnp.concatenate([tm_w_up[l], zeros_lora], axis=0).astype(BF16),
        _row(tm_a0[l]),
        jnp.concatenate([zeros_lora, tm_a_up[l]], axis=0).astype(BF16),
        tm_g_up[l].astype(BF16),
        _row(tm_k_k[l]), _row(tm_k_a[l]), _row(tm_r_k[l]), _row(tm_gn_g[l]), _row(tm_gn_b[l]),
        lru_conv_w[l].astype(F32), _row(lru_conv_b[l]),
        jnp.concatenate([_pair_block_diag(lru_wa[l]), _pair_block_diag(lru_wx[l])], axis=-1).astype(BF16),
        _row(lru_ba[l]), _row(lru_bx[l]),
        _row(lru_lambda[l]), _row(lru_out_g[l]),
        seg, seg.T, w_out[l].astype(BF16),
    )
    ffn_w = (
        _row(norm2_g[l]), ffn_w_up[l].astype(BF16), ffn_w_gate[l].astype(BF16),
        ffn_conv_w[l].astype(F32), _row(ffn_conv_b[l]), ffn_w_down[l].astype(BF16), _row(norm_f_g),
    )

    bsz, seq = x_prompt.shape[0], x_prompt.shape[1]
    x_meta = jnp.broadcast_to(meta_tokens[None].astype(F32), (bsz, N_META, D_MODEL))
    x1_meta, *meta_st = _mixer(x_meta, jnp.zeros((bsz, D_TM_PROJ), F32),
                               jnp.zeros((bsz, PAIRS, TM_HEAD, LANES), F32),
                               jnp.zeros((LRU_CONV_W - 1, bsz, D_LRU), F32),
                               jnp.zeros((bsz, D_LRU), F32), mixer_w,
                               bb=bsz, chunk=N_META, pos0=0)
    p_init = tuple(meta_st) + (_ffn_state(x1_meta, ffn_w[0], ffn_w[1]),)

    y_prompt, p_st = _layer(x_prompt, p_init, mixer_w, ffn_w,
                            mixer_bb=bsz, ffn_bb=bsz, mixer_chunk=64, ffn_chunk=64,
                            pos0=N_META)

    dec_b, dec_seq = x_sample.shape[0], x_sample.shape[1]
    s_init = (state_tm_shift[l].astype(F32), _wkv_to_pairs(state_tm_wkv[l]),
              jnp.transpose(state_lru_conv[l].astype(F32), (1, 0, 2)), state_lru_h[l].astype(F32),
              jnp.transpose(state_ffn_conv[l].astype(F32), (1, 0, 2)))
    y_sample, s_st = _layer(x_sample, s_init, mixer_w, ffn_w,
                            mixer_bb=32, ffn_bb=64, mixer_chunk=dec_seq, ffn_chunk=dec_seq,
                            pos0=PAST_LEN)

    def unpack(st):
        o_shift, o_wkv, o_conv, o_h, o_fconv = st
        return (o_shift[None], _wkv_from_pairs(o_wkv)[None], jnp.transpose(o_conv, (1, 0, 2))[None],
                o_h[None], jnp.transpose(o_fconv, (1, 0, 2))[None])

    return (y_prompt, y_sample) + unpack(p_st) + unpack(s_st)
```

```python
import functools
import math

import jax
import jax.numpy as jnp
from jax import lax
from jax.experimental import pallas as pl
from jax.experimental.pallas import tpu as pltpu

F32 = jnp.float32
BF16 = jnp.bfloat16

D_MODEL = 1024
N_META = 16
PAST_LEN = 16384
D_TM = 512
TM_HEAD = 64
TM_HEADS = 8
DECAY_RANK = 64
AAA_RANK = 64
GATE_RANK = 128
D_TM_PROJ = 3 * D_TM + DECAY_RANK + AAA_RANK + GATE_RANK
D_LRU = 512
LRU_CONV_W = 4
LRU_C = 8.0
D_IN_PROJ = D_TM_PROJ + 2 * D_LRU
D_FF = 3 * D_MODEL
FFN_CONV_W = 3
EPS = 1e-6
GN_EPS = 64e-5
LOG2_E = math.log2(math.e)

SUBLANES = 8
LANES = 128
PAIRS = TM_HEADS // 2
LORA_OFF = 3 * D_TM
GATE_OFF = LORA_OFF + DECAY_RANK + AAA_RANK
FF_COL_TILE = 1024
VMEM_LIMIT_BYTES = 60 * 1024 * 1024


def _dot(a, b):
    return jnp.dot(a.astype(BF16), b.astype(BF16), preferred_element_type=F32)


def _bdot(a, b):
    return lax.dot_general(a.astype(BF16), b.astype(BF16), (((2,), (1,)), ((0,), (0,))),
                           preferred_element_type=F32)


def _bdot_nt(a, b):
    return lax.dot_general(a.astype(BF16), b.astype(BF16), (((2,), (2,)), ((0,), (0,))),
                           preferred_element_type=F32)


def _bdot_tn(a, b):
    return lax.dot_general(a.astype(BF16), b.astype(BF16), (((1,), (1,)), ((0,), (0,))),
                           preferred_element_type=F32)


def _rms(x, g):
    return x * lax.rsqrt(jnp.mean(x * x, axis=-1, keepdims=True) + EPS) * g


def _sigmoid(z):
    return 0.5 * jnp.tanh(0.5 * z) + 0.5


def _softplus(z):
    return jnp.maximum(z, 0.0) + jnp.log(1.0 + jnp.exp(-jnp.abs(z)))


def _block_diag_rows(x, left):
    return jnp.concatenate([jnp.where(left, x, 0.0), jnp.where(left, 0.0, x)], axis=1)


def _shift_time(carry, x, steps, bb):
    n = steps * bb
    return jnp.concatenate([carry[carry.shape[0] - n:], x[:x.shape[0] - n]], axis=0)


class _TimeMajorStream:
    def __init__(self, hbm, buf, sem, *, bb, chunk, n_blocks, n_chunks, to_hbm):
        self.hbm, self.buf, self.sem = hbm, buf, sem
        self.bb, self.chunk, self.to_hbm = bb, chunk, to_hbm
        self.n_blocks, self.n_chunks = n_blocks, n_chunks

    def _copy(self, g, slot, i):
        if self.n_chunks == 1:
            bi, ti = g, 0
        elif self.n_blocks == 1:
            bi, ti = 0, g
        else:
            bi, ti = lax.div(g, self.n_chunks), lax.rem(g, self.n_chunks)
        if self.bb < self.chunk:
            hbm_rows = self.hbm.at[bi * self.bb + i, pl.ds(ti * self.chunk, self.chunk), :]
            tile = self.buf.at[slot, :, i, :]
        else:
            seqs = pl.ds(pl.multiple_of(bi * self.bb, SUBLANES), self.bb)
            hbm_rows = self.hbm.at[seqs, ti * self.chunk + i, :]
            tile = self.buf.at[slot, i]
        src, dst = (tile, hbm_rows) if self.to_hbm else (hbm_rows, tile)
        return pltpu.make_async_copy(src, dst, self.sem.at[slot])

    def start(self, g, slot):
        for i in range(min(self.bb, self.chunk)):
            self._copy(g, slot, i).start()

    def wait(self, g, slot):
        for i in range(min(self.bb, self.chunk)):
            self._copy(g, slot, i).wait()


READ_SLOTS = 3


def _read_chunk_begin(stream, g, n_steps):
    last = n_steps - 1

    @pl.when(g == 0)
    def _():
        stream.start(0, 0)
        stream.start(jnp.minimum(1, last), 1)

    slot = lax.rem(g, READ_SLOTS)
    stream.wait(g, slot)
    return slot


def _read_chunk_end(stream, g, n_steps):
    last = n_steps - 1
    stream.start(jnp.minimum(g + 2, last), lax.rem(g + 2, READ_SLOTS))

    @pl.when(g == last)
    def _():
        stream.wait(last, lax.rem(g + 1, READ_SLOTS))
        stream.wait(last, lax.rem(g + 2, READ_SLOTS))


def _mixer_kernel(x_hbm, st_shift_ref, st_wkv_ref, st_conv_ref, st_h_ref,
                  g1_ref, w_in_ref, mu_ref, w0_ref, wdec_ref, a0_ref, waaa_ref, wgate_ref,
                  kk_ref, ka_ref, rk_ref, gng_ref, gnb_ref,
                  cw_ref, cb_ref, wgates_ref, ba_ref, bx_ref, lam_ref, og_ref,
                  seg_ref, segt_ref, w_out_ref,
                  x1_ref, o_shift_ref, o_wkv_ref, o_conv_ref, o_h_ref,
                  tm_carry, xb_carry, kap_s, rt_s, bt_s, kt_s, v_s, yt_s, pe_s, x_buf, x_sem,
                  *, bb, chunk, pos0, n_blocks, n_chunks):
    rows = bb * chunk
    conv_rows = (LRU_CONV_W - 1) * bb
    wkv_chunk = max(chunk, SUBLANES)
    ti = pl.program_id(1)
    step = pl.program_id(0) * n_chunks + ti
    n_steps = n_blocks * n_chunks
    x_stream = _TimeMajorStream(x_hbm, x_buf, x_sem, bb=bb, chunk=chunk, n_blocks=n_blocks,
                                n_chunks=n_chunks, to_hbm=False)
    x_slot = _read_chunk_begin(x_stream, step, n_steps)

    @pl.when(ti == 0)
    def _():
        tm_carry[...] = st_shift_ref[...]
        xb_carry[...] = st_conv_ref[...].reshape(conv_rows, D_LRU)
        o_wkv_ref[...] = st_wkv_ref[...]
        o_h_ref[...] = st_h_ref[...]
        if wkv_chunk > chunk:
            for ref in (kap_s, rt_s, bt_s, kt_s, v_s):
                ref[:, rows:wkv_chunk * bb, :] = jnp.zeros((PAIRS, wkv_chunk * bb - rows, LANES), F32)

    seg = seg_ref[...]
    segt = segt_ref[...]

    def head_sum(x):
        s = jnp.dot(x.astype(BF16), seg, preferred_element_type=F32)
        return jnp.dot(s.astype(BF16), segt, preferred_element_type=F32)

    x = x_buf[x_slot].reshape(rows, D_MODEL)
    u = _dot(_rms(x, g1_ref[...]), w_in_ref[...])
    u_tm = u[:, :D_TM_PROJ]
    xb = u[:, D_TM_PROJ:D_TM_PROJ + D_LRU]
    gate_lru = u[:, D_TM_PROJ + D_LRU:D_IN_PROJ]

    row = lax.broadcasted_iota(jnp.int32, (rows, D_TM), 0)

    um = u_tm + (_shift_time(tm_carry[...], u_tm, 1, bb) - u_tm) * mu_ref[...]
    r = um[:, 0:D_TM]
    k = um[:, D_TM:2 * D_TM]
    v = um[:, 2 * D_TM:3 * D_TM]
    x_lora = um[:, LORA_OFF:GATE_OFF]
    x_gate = um[:, GATE_OFF:D_TM_PROJ]
    log2_decay = (-math.exp(-0.5) * LOG2_E) * _sigmoid(w0_ref[...] + _dot(jnp.tanh(x_lora), wdec_ref[...]))
    a = _sigmoid(a0_ref[...] + _dot(x_lora, waaa_ref[...]))
    gate_tm = _dot(_sigmoid(x_gate), wgate_ref[...])
    kk = k * kk_ref[...]
    k = k * (a * ka_ref[...] + (1.0 - ka_ref[...]))
    kk = kk * lax.rsqrt(jnp.maximum(head_sum(kk * kk), 1e-24))
    bonus = head_sum(r * k * rk_ref[...]) * v
    acc = log2_decay[0:bb]
    cum = [acc]
    for t in range(1, chunk):
        acc = acc + log2_decay[t * bb:(t + 1) * bb]
        cum.append(acc)
    c = jnp.concatenate(cum, axis=0)
    inv_p = jnp.exp2(-c)
    kap = kk * jnp.exp2(c - log2_decay)
    rt = r * jnp.exp2(c)
    bt = kk * a * inv_p
    kt = k * inv_p
    p_end = jnp.exp2(cum[chunk - 1])
    for p in range(PAIRS):
        ls = slice(p * LANES, (p + 1) * LANES)
        kap_s[p, 0:rows, :] = kap[:, ls]
        rt_s[p, 0:rows, :] = rt[:, ls]
        bt_s[p, 0:rows, :] = bt[:, ls]
        kt_s[p, 0:rows, :] = kt[:, ls]
        v_s[p, 0:rows, :] = v[:, ls]
        pe_s[p] = p_end[:, ls]

    xb_prev = xb_carry[...]
    xc = cb_ref[...] + _shift_time(xb_prev, xb, 3, bb) * cw_ref[0:1, :]
    xc = xc + _shift_time(xb_prev, xb, 2, bb) * cw_ref[1:2, :]
    xc = xc + _shift_time(xb_prev, xb, 1, bb) * cw_ref[2:3, :]
    xc = xc + xb * cw_ref[3:4, :]
    xc_bf = xc.astype(BF16)

    gates = [jnp.dot(xc_bf[:, p * LANES:(p + 1) * LANES], wgates_ref[p], preferred_element_type=F32)
             for p in range(D_LRU // LANES)]
    r_g = _sigmoid(jnp.concatenate([g[:, :LANES] for g in gates], axis=-1) + ba_ref[...])
    i_g = _sigmoid(jnp.concatenate([g[:, LANES:] for g in gates], axis=-1) + bx_ref[...])
    la = jnp.exp2(r_g * ((-LRU_C * LOG2_E) * _softplus(-lam_ref[...])))
    gap = 1.0 - la * la
    mult = jnp.where(gap > 0.0, gap * lax.rsqrt(gap), 0.0)
    if pos0 == 0:
        mult = jnp.where(jnp.logical_and(row < bb, ti == 0), 1.0, mult)
    lb = xc * i_g * mult
    h = o_h_ref[...]
    hs = []
    for t in range(chunk):
        ts = slice(t * bb, (t + 1) * bb)
        h = la[ts] * h + lb[ts]
        hs.append(h)
    o_h_ref[...] = h
    y_lru = _rms(jnp.concatenate(hs, axis=0) * jax.nn.gelu(gate_lru), og_ref[...])

    wc = wkv_chunk
    n_lv = max(1, (chunk - 1).bit_length())
    cw2 = 2 * wc
    left = lax.broadcasted_iota(jnp.int32, (1, 1, LANES), 2) < TM_HEAD
    left_c = lax.broadcasted_iota(jnp.int32, (1, 1, cw2), 2) < wc
    ri = lax.broadcasted_iota(jnp.int32, (wc, cw2), 0)
    ci = lax.broadcasted_iota(jnp.int32, (wc, cw2), 1) & (wc - 1)
    strict = ri > ci
    incl = ri >= ci
    eye = (ri == ci).astype(F32)

    tsel = [pl.ds(b, wc, stride=bb) for b in range(bb)]
    gather = lambda ref: jnp.stack([ref.at[p][ts, :] for ts in tsel for p in range(PAIRS)])
    kap_g = gather(kap_s)
    rt_g = gather(rt_s)
    bt_g = gather(bt_s)
    kt_g = gather(kt_s)
    vv = gather(v_s)
    pe = jnp.stack([pe_s.at[p][pl.ds(b, 1), :] for b in range(bb) for p in range(PAIRS)])
    s0 = o_wkv_ref[...].reshape(bb * PAIRS, TM_HEAD, LANES)
    lhs = jnp.concatenate([kap_g, rt_g], axis=1)
    fused = _bdot_nt(lhs, jnp.concatenate([_block_diag_rows(bt_g, left), _block_diag_rows(kt_g, left),
                                           _block_diag_rows(s0, left)], axis=1))
    g_b = fused[:, :, 0:cw2]
    g_k = fused[:, :, cw2:2 * cw2]
    z = fused[:, :, 2 * cw2:]
    m_ab = jnp.where(strict, g_b[:, :wc], 0.0)
    m_ak = jnp.where(strict, g_k[:, :wc], 0.0)
    m_rb = jnp.where(incl, g_b[:, wc:], 0.0)
    m_rk = jnp.where(incl, g_k[:, wc:], 0.0)
    t_inv = eye - m_ab
    m_pow = _bdot(m_ab, _block_diag_rows(m_ab, left_c))
    for lv in range(1, n_lv):
        if lv < n_lv - 1:
            prod = _bdot(jnp.concatenate([t_inv, m_pow], axis=1), _block_diag_rows(m_pow, left_c))
            t_inv = t_inv + prod[:, :wc]
            m_pow = prod[:, wc:]
        else:
            t_inv = t_inv + _bdot(t_inv, _block_diag_rows(m_pow, left_c))
    xv = _bdot(jnp.concatenate([m_ak, m_rk], axis=1), _block_diag_rows(vv, left))
    uu = -_bdot(t_inv, _block_diag_rows(z[:, :wc] + xv[:, :wc], left))
    yh = z[:, wc:] + _bdot(m_rb, _block_diag_rows(uu, left)) + xv[:, wc:]
    full = _bdot_tn(jnp.concatenate([uu, vv], axis=1), jnp.concatenate([bt_g, kt_g], axis=1))
    upd = jnp.where(left, full[:, :TM_HEAD], full[:, TM_HEAD:])
    o_wkv_ref[...] = ((s0 + upd) * pe).reshape(bb, PAIRS, TM_HEAD, LANES)
    for b, ts in enumerate(tsel):
        for p in range(PAIRS):
            yt_s.at[p][ts, :] = yh[b * PAIRS + p]

    yv = jnp.concatenate([yt_s[p, 0:rows, :] for p in range(PAIRS)], axis=-1)
    mean = head_sum(yv) * (1.0 / TM_HEAD)
    cen = yv - mean
    var = head_sum(cen * cen) * (1.0 / TM_HEAD)
    y_tm = ((cen * lax.rsqrt(var + GN_EPS)) * gng_ref[...] + gnb_ref[...] + bonus) * gate_tm
    mixed = jnp.concatenate([y_tm.astype(BF16), y_lru.astype(BF16)], axis=-1)
    x1 = x_buf[x_slot].reshape(rows, D_MODEL) + jnp.dot(mixed, w_out_ref[...], preferred_element_type=F32)
    x1_ref[...] = x1.reshape(chunk, bb, D_MODEL)

    o_shift_ref[...] = u_tm[rows - bb:]
    o_conv_ref[...] = xb[rows - conv_rows:].reshape(LRU_CONV_W - 1, bb, D_LRU)
    tm_carry[...] = u_tm[rows - bb:]
    xb_carry[...] = xb[rows - conv_rows:]
    _read_chunk_end(x_stream, step, n_steps)


def _const_spec(shape):
    zeros = (0,) * len(shape)
    return pl.BlockSpec(shape, lambda bi, ti: zeros, pipeline_mode=pl.Buffered(1))


def _mixer(x, st_shift, st_wkv, st_conv, st_h, weights, *, bb, chunk, pos0):
    batch, length, _ = x.shape
    assert batch % bb == 0 and length % chunk == 0 and bb % SUBLANES == 0
    assert chunk & (chunk - 1) == 0 and chunk >= LRU_CONV_W - 1
    rows = bb * chunk
    grid = (batch // bb, length // chunk)
    act = pl.BlockSpec((chunk, bb, D_MODEL), lambda bi, ti: (ti, bi, 0))
    vec = lambda w: pl.BlockSpec((bb, w), lambda bi, ti: (bi, 0))
    wkv_spec = pl.BlockSpec((bb, PAIRS, TM_HEAD, LANES), lambda bi, ti: (bi, 0, 0, 0))
    conv_spec = pl.BlockSpec((LRU_CONV_W - 1, bb, D_LRU), lambda bi, ti: (0, bi, 0))
    in_specs = [pl.BlockSpec(memory_space=pl.ANY), vec(D_TM_PROJ), wkv_spec, conv_spec, vec(D_LRU)]
    in_specs += [_const_spec(w.shape) for w in weights]
    out_specs = [act, vec(D_TM_PROJ), wkv_spec, conv_spec, vec(D_LRU)]
    out_shape = [jax.ShapeDtypeStruct((length, batch, D_MODEL), F32),
                 jax.ShapeDtypeStruct((batch, D_TM_PROJ), F32),
                 jax.ShapeDtypeStruct((batch, PAIRS, TM_HEAD, LANES), F32),
                 jax.ShapeDtypeStruct((LRU_CONV_W - 1, batch, D_LRU), F32),
                 jax.ShapeDtypeStruct((batch, D_LRU), F32)]
    pair_rows = lambda n: pltpu.VMEM((PAIRS, n, LANES), F32)
    scratch = [pltpu.VMEM((bb, D_TM_PROJ), F32),
               pltpu.VMEM(((LRU_CONV_W - 1) * bb, D_LRU), F32)]
    scratch += [pair_rows(max(chunk, SUBLANES) * bb)] * 6 + [pair_rows(bb)]
    scratch += [pltpu.VMEM((READ_SLOTS, chunk, bb, D_MODEL), F32), pltpu.SemaphoreType.DMA((READ_SLOTS,))]
    return pl.pallas_call(
        functools.partial(_mixer_kernel, bb=bb, chunk=chunk, pos0=pos0,
                          n_blocks=grid[0], n_chunks=grid[1]),
        out_shape=out_shape, grid=grid, in_specs=in_specs, out_specs=out_specs,
        scratch_shapes=scratch, name="mixer",
        compiler_params=pltpu.CompilerParams(dimension_semantics=("arbitrary", "arbitrary"),
                                             vmem_limit_bytes=VMEM_LIMIT_BYTES),
    )(x, st_shift, st_wkv, st_conv, st_h, *weights)


def _ffn_kernel(x1_ref, st_conv_ref,
                g2_ref, w_up_ref, w_gate_ref, cw_ref, cb_ref, w_down_ref, gf_ref,
                y_hbm, o_conv_ref,
                up_carry, y_buf, y_sem,
                *, bb, chunk, n_blocks, n_chunks):
    rows = bb * chunk
    conv_rows = (FFN_CONV_W - 1) * bb
    ti = pl.program_id(1)
    step = pl.program_id(0) * n_chunks + ti
    n_steps = n_blocks * n_chunks
    y_stream = _TimeMajorStream(y_hbm, y_buf, y_sem, bb=bb, chunk=chunk, n_blocks=n_blocks,
                                n_chunks=n_chunks, to_hbm=True)
    slot = lax.rem(step, 2)

    @pl.when(ti == 0)
    def _():
        up_carry[...] = st_conv_ref[...].reshape(conv_rows, D_FF)

    x1 = x1_ref[...].reshape(rows, D_MODEL)
    xn = _rms(x1, g2_ref[...]).astype(BF16)
    acc = x1
    for n in range(D_FF // FF_COL_TILE):
        cols = slice(n * FF_COL_TILE, (n + 1) * FF_COL_TILE)
        up = jnp.dot(xn, w_up_ref[:, cols], preferred_element_type=F32)
        gate = jnp.dot(xn, w_gate_ref[:, cols], preferred_element_type=F32)
        up_prev = up_carry[:, cols]
        upc = cb_ref[:, cols] + _shift_time(up_prev, up, 2, bb) * cw_ref[0:1, cols]
        upc = upc + _shift_time(up_prev, up, 1, bb) * cw_ref[1:2, cols]
        upc = upc + up * cw_ref[2:3, cols]
        up_carry[:, cols] = up[rows - conv_rows:]
        o_conv_ref[:, :, cols] = up[rows - conv_rows:].reshape(FFN_CONV_W - 1, bb, FF_COL_TILE)
        hid = (jax.nn.gelu(upc) * gate).astype(BF16)
        acc = acc + jnp.dot(hid, w_down_ref[cols, :], preferred_element_type=F32)

    y_buf[slot] = _rms(acc, gf_ref[...]).reshape(chunk, bb, D_MODEL)
    y_stream.start(step, slot)

    @pl.when(step > 0)
    def _():
        y_stream.wait(step - 1, 1 - slot)

    @pl.when(step == n_steps - 1)
    def _():
        y_stream.wait(step, slot)


def _ffn(x1, st_conv, weights, *, bb, chunk):
    length, batch, _ = x1.shape
    assert batch % bb == 0 and length % chunk == 0 and bb % SUBLANES == 0
    assert chunk >= FFN_CONV_W - 1
    rows = bb * chunk
    grid = (batch // bb, length // chunk)
    act = pl.BlockSpec((chunk, bb, D_MODEL), lambda bi, ti: (ti, bi, 0))
    conv_spec = pl.BlockSpec((FFN_CONV_W - 1, bb, D_FF), lambda bi, ti: (0, bi, 0))
    in_specs = [act, conv_spec] + [_const_spec(w.shape) for w in weights]
    out_shape = [jax.ShapeDtypeStruct((batch, length, D_MODEL), F32),
                 jax.ShapeDtypeStruct((FFN_CONV_W - 1, batch, D_FF), F32)]
    scratch = [pltpu.VMEM(((FFN_CONV_W - 1) * bb, D_FF), F32),
               pltpu.VMEM((2, chunk, bb, D_MODEL), F32), pltpu.SemaphoreType.DMA((2,))]
    return pl.pallas_call(
        functools.partial(_ffn_kernel, bb=bb, chunk=chunk, n_blocks=grid[0], n_chunks=grid[1]),
        out_shape=out_shape, grid=grid, in_specs=in_specs,
        out_specs=[pl.BlockSpec(memory_space=pl.ANY), conv_spec],
        scratch_shapes=scratch, name="ffn",
        compiler_params=pltpu.CompilerParams(dimension_semantics=("arbitrary", "arbitrary"),
                                             vmem_limit_bytes=VMEM_LIMIT_BYTES),
    )(x1, st_conv, *weights)


def _ffn_state_kernel(x1_ref, g2_ref, w_up_ref, o_conv_ref, *, bb):
    tail = FFN_CONV_W - 1
    x1 = x1_ref[...].reshape(tail * bb, D_MODEL)
    up = _dot(_rms(x1, g2_ref[...]), w_up_ref[...])
    o_conv_ref[...] = up.reshape(tail, bb, D_FF)


def _ffn_state(x1, g2, w_up):
    length, batch, _ = x1.shape
    tail = FFN_CONV_W - 1
    assert length % tail == 0 and batch % SUBLANES == 0
    return pl.pallas_call(
        functools.partial(_ffn_state_kernel, bb=batch),
        out_shape=jax.ShapeDtypeStruct((tail, batch, D_FF), F32), grid=(1,),
        in_specs=[pl.BlockSpec((tail, batch, D_MODEL), lambda i: (length // tail - 1, 0, 0)),
                  pl.BlockSpec(g2.shape, lambda i: (0, 0)), pl.BlockSpec(w_up.shape, lambda i: (0, 0))],
        out_specs=pl.BlockSpec((tail, batch, D_FF), lambda i: (0, 0, 0)), name="ffn_state",
        compiler_params=pltpu.CompilerParams(dimension_semantics=("arbitrary",),
                                             vmem_limit_bytes=VMEM_LIMIT_BYTES),
    )(x1, g2, w_up)


RELAYOUT_ROWS = 16
RELAYOUT_AHEAD = 3
RELAYOUT_CHUNKS = tuple((p, v0) for p in range(PAIRS) for v0 in range(0, TM_HEAD, RELAYOUT_ROWS))


def _lanes_copy(hbm, buf, sem, c, *, to_hbm):
    p, v0 = RELAYOUT_CHUNKS[c]
    region = hbm.at[pl.ds(2 * p, 2), pl.ds(v0, RELAYOUT_ROWS)]
    src, dst = (buf.at[c], region) if to_hbm else (region, buf.at[c])
    return pltpu.make_async_copy(src, dst, sem.at[c])


def _pair_row_copy(hbm, buf, sem, c, i, *, to_hbm):
    p, v0 = RELAYOUT_CHUNKS[c]
    rows = hbm.at[:, p, v0 + i, :]
    src, dst = (buf.at[c, i], rows) if to_hbm else (rows, buf.at[c, i])
    return pltpu.make_async_copy(src, dst, sem.at[c])


def _pairs_from_lanes_kernel(a_hbm, o_hbm, lanes_buf, pairs_buf, lanes_sem, pairs_sem):
    batch = lanes_buf.shape[-1]
    n = len(RELAYOUT_CHUNKS)
    for c in range(n):
        _lanes_copy(a_hbm, lanes_buf, lanes_sem, c, to_hbm=False).start(priority=c % 2)
    for c in range(n):
        _lanes_copy(a_hbm, lanes_buf, lanes_sem, c, to_hbm=False).wait()
        for i in range(RELAYOUT_ROWS):
            pairs_buf[c, i] = lanes_buf[c, :, i].reshape(LANES, batch).T
        for i in range(RELAYOUT_ROWS):
            _pair_row_copy(o_hbm, pairs_buf, pairs_sem, c, i, to_hbm=True).start(priority=i % 2)
    for c in range(n):
        for i in range(RELAYOUT_ROWS):
            _pair_row_copy(o_hbm, pairs_buf, pairs_sem, c, i, to_hbm=True).wait()


def _lanes_from_pairs_kernel(s_hbm, o_hbm, lanes_buf, pairs_buf, lanes_sem, pairs_sem):
    batch = lanes_buf.shape[-1]
    n = len(RELAYOUT_CHUNKS)

    def gather(c):
        for i in range(RELAYOUT_ROWS):
            _pair_row_copy(s_hbm, pairs_buf, pairs_sem, c, i, to_hbm=False).start(priority=i % 2)

    for c in range(min(RELAYOUT_AHEAD, n)):
        gather(c)
    for c in range(n):
        if c + RELAYOUT_AHEAD < n:
            gather(c + RELAYOUT_AHEAD)
        for i in range(RELAYOUT_ROWS):
            _pair_row_copy(s_hbm, pairs_buf, pairs_sem, c, i, to_hbm=False).wait()
        for i in range(RELAYOUT_ROWS):
            lanes_buf[c, :, i] = pairs_buf[c, i].T.reshape(2, TM_HEAD, batch)
        _lanes_copy(o_hbm, lanes_buf, lanes_sem, c, to_hbm=True).start(priority=c % 2)
    for c in range(n):
        _lanes_copy(o_hbm, lanes_buf, lanes_sem, c, to_hbm=True).wait()


def _wkv_relayout(x, *, to_pairs):
    batch = x.shape[-1] if to_pairs else x.shape[0]
    assert batch == LANES
    n = len(RELAYOUT_CHUNKS)
    lanes_shape = (TM_HEADS, TM_HEAD, TM_HEAD, batch)
    pairs_shape = (batch, PAIRS, TM_HEAD, LANES)
    state_bytes = TM_HEADS * TM_HEAD * TM_HEAD * batch * 4
    hbm_spec = pl.BlockSpec(memory_space=pl.ANY)
    return pl.pallas_call(
        _pairs_from_lanes_kernel if to_pairs else _lanes_from_pairs_kernel,
        out_shape=jax.ShapeDtypeStruct(pairs_shape if to_pairs else lanes_shape, F32),
        in_specs=[hbm_spec], out_specs=hbm_spec,
        scratch_shapes=[pltpu.VMEM((n, 2, RELAYOUT_ROWS, TM_HEAD, batch), F32),
                        pltpu.VMEM((n, RELAYOUT_ROWS, batch, LANES), F32),
                        pltpu.SemaphoreType.DMA((n,)), pltpu.SemaphoreType.DMA((n,))],
        name="wkv_relayout",
        compiler_params=pltpu.CompilerParams(vmem_limit_bytes=2 * state_bytes + 8 * 1024 * 1024),
    )(x)


def _row(v):
    return v.reshape(1, -1).astype(F32)


def _pair_block_diag(w):
    nb, n, _ = w.shape
    w = w.reshape(nb // 2, 2, n, n)
    eye = jnp.eye(2, dtype=w.dtype)
    return (eye[None, :, None, :, None] * w[:, :, :, None, :]).reshape(nb // 2, 2 * n, 2 * n)


def _wkv_to_pairs(s):
    b = s.shape[0]
    if b == LANES:
        return _wkv_relayout(jnp.transpose(s.astype(F32), (1, 2, 3, 0)), to_pairs=True)
    s = s.astype(F32).reshape(b, PAIRS, 2, TM_HEAD, TM_HEAD)
    return jnp.transpose(s, (0, 1, 3, 2, 4)).reshape(b, PAIRS, TM_HEAD, LANES)


def _wkv_from_pairs(s):
    b = s.shape[0]
    if b == LANES:
        return jnp.transpose(_wkv_relayout(s, to_pairs=False), (3, 0, 1, 2))
    s = s.reshape(b, PAIRS, TM_HEAD, 2, TM_HEAD)
    return jnp.transpose(s, (0, 1, 3, 2, 4)).reshape(b, TM_HEADS, TM_HEAD, TM_HEAD)


def _layer(x, states, mixer_w, ffn_w, *, mixer_bb, ffn_bb, mixer_chunk, ffn_chunk, pos0):
    st_shift, st_wkv, st_conv, st_h, st_fconv = states
    x1, o_shift, o_wkv, o_conv, o_h = _mixer(x, st_shift, st_wkv, st_conv, st_h, mixer_w,
                                             bb=mixer_bb, chunk=mixer_chunk, pos0=pos0)
    y, o_fconv = _ffn(x1, st_fconv, ffn_w, bb=ffn_bb, chunk=ffn_chunk)
    return y, (o_shift, o_wkv, o_conv, o_h, o_fconv)


def kernel(x_prompt, x_sample, state_tm_shift, state_tm_wkv, state_lru_conv, state_lru_h, state_ffn_conv, meta_tokens, norm1_g, w_in, tm_mu, tm_w0, tm_w_up, tm_a0, tm_a_up, tm_g_up, tm_k_k, tm_k_a, tm_r_k, tm_gn_g, tm_gn_b, lru_conv_w, lru_conv_b, lru_wa, lru_ba, lru_wx, lru_bx, lru_lambda, lru_out_g, w_out, norm2_g, ffn_w_up, ffn_w_gate, ffn_conv_w, ffn_conv_b, ffn_w_down, norm_f_g):
    depth = w_in.shape[0]
    assert depth == 1
    l = 0
    zeros_lora = jnp.zeros((DECAY_RANK, D_TM), F32)
    head_id = jnp.arange(D_TM) // TM_HEAD
    seg = (head_id[:, None] == jnp.arange(LANES)[None, :]).astype(BF16)
    mixer_w = (
        _row(norm1_g[l]), w_in[l].astype(BF16), _row(tm_mu[l]), _row(tm_w0[l]),
        jnp.concatenate([tm_w_up[l], zeros_lora], axis=0).astype(BF16),
        _row(tm_a0[l]),
        jnp.concatenate([zeros_lora, tm_a_up[l]], axis=0).astype(BF16),
        tm_g_up[l].astype(BF16),
        _row(tm_k_k[l]), _row(tm_k_a[l]), _row(tm_r_k[l]), _row(tm_gn_g[l]), _row(tm_gn_b[l]),
        lru_conv_w[l].astype(F32), _row(lru_conv_b[l]),
        jnp.concatenate([_pair_block_diag(lru_wa[l]), _pair_block_diag(lru_wx[l])], axis=-1).astype(BF16),
        _row(lru_ba[l]), _row(lru_bx[l]),
        _row(lru_lambda[l]), _row(lru_out_g[l]),
        seg, seg.T, w_out[l].astype(BF16),
    )
    ffn_w = (
        _row(norm2_g[l]), ffn_w_up[l].astype(BF16), ffn_w_gate[l].astype(BF16),
        ffn_conv_w[l].astype(F32), _row(ffn_conv_b[l]), ffn_w_down[l].astype(BF16), _row(norm_f_g),
    )

    bsz, seq = x_prompt.shape[0], x_prompt.shape[1]
    x_meta = jnp.broadcast_to(meta_tokens[None].astype(F32), (bsz, N_META, D_MODEL))
    x1_meta, *meta_st = _mixer(x_meta, jnp.zeros((bsz, D_TM_PROJ), F32),
                               jnp.zeros((bsz, PAIRS, TM_HEAD, LANES), F32),
                               jnp.zeros((LRU_CONV_W - 1, bsz, D_LRU), F32),
                               jnp.zeros((bsz, D_LRU), F32), mixer_w,
                               bb=bsz, chunk=N_META, pos0=0)
    p_init = tuple(meta_st) + (_ffn_state(x1_meta, ffn_w[0], ffn_w[1]),)

    y_prompt, p_st = _layer(x_prompt, p_init, mixer_w, ffn_w,
                            mixer_bb=bsz, ffn_bb=bsz, mixer_chunk=64, ffn_chunk=64,
                            pos0=N_META)

    dec_b, dec_seq = x_sample.shape[0], x_sample.shape[1]
    s_init = (state_tm_shift[l].astype(F32), _wkv_to_pairs(state_tm_wkv[l]),
              jnp.transpose(state_lru_conv[l].astype(F32), (1, 0, 2)), state_lru_h[l].astype(F32),
              jnp.transpose(state_ffn_conv[l].astype(F32), (1, 0, 2)))
    y_sample, s_st = _layer(x_sample, s_init, mixer_w, ffn_w,
                            mixer_bb=32, ffn_bb=64, mixer_chunk=dec_seq, ffn_chunk=dec_seq,
                            pos0=PAST_LEN)

    def unpack(st):
        o_shift, o_wkv, o_conv, o_h, o_fconv = st
        return (o_shift[None], _wkv_from_pairs(o_wkv)[None], jnp.transpose(o_conv, (1, 0, 2))[None],
                o_h[None], jnp.transpose(o_fconv, (1, 0, 2))[None])

    return (y_prompt, y_sample) + unpack(p_st) + unpack(s_st)
```

```python
import functools
import math

import jax
import jax.numpy as jnp
from jax import lax
from jax.experimental import pallas as pl
from jax.experimental.pallas import tpu as pltpu

F32 = jnp.float32
BF16 = jnp.bfloat16

D_MODEL = 1024
N_META = 16
PAST_LEN = 16384
D_TM = 512
TM_HEAD = 64
TM_HEADS = 8
DECAY_RANK = 64
AAA_RANK = 64
GATE_RANK = 128
D_TM_PROJ = 3 * D_TM + DECAY_RANK + AAA_RANK + GATE_RANK
D_LRU = 512
LRU_CONV_W = 4
LRU_C = 8.0
D_IN_PROJ = D_TM_PROJ + 2 * D_LRU
D_FF = 3 * D_MODEL
FFN_CONV_W = 3
EPS = 1e-6
GN_EPS = 64e-5
LOG2_E = math.log2(math.e)

SUBLANES = 8
LANES = 128
PAIRS = TM_HEADS // 2
LORA_OFF = 3 * D_TM
GATE_OFF = LORA_OFF + DECAY_RANK + AAA_RANK
FF_COL_TILE = 1024
VMEM_LIMIT_BYTES = 60 * 1024 * 1024


def _dot(a, b):
    return jnp.dot(a.astype(BF16), b.astype(BF16), preferred_element_type=F32)


def _bdot(a, b):
    return lax.dot_general(a.astype(BF16), b.astype(BF16), (((2,), (1,)), ((0,), (0,))),
                           preferred_element_type=F32)


def _bdot_nt(a, b):
    return lax.dot_general(a.astype(BF16), b.astype(BF16), (((2,), (2,)), ((0,), (0,))),
                           preferred_element_type=F32)


def _bdot_tn(a, b):
    return lax.dot_general(a.astype(BF16), b.astype(BF16), (((1,), (1,)), ((0,), (0,))),
                           preferred_element_type=F32)


def _rms(x, g):
    return x * lax.rsqrt(jnp.mean(x * x, axis=-1, keepdims=True) + EPS) * g


def _sigmoid(z):
    return 0.5 * jnp.tanh(0.5 * z) + 0.5


def _softplus(z):
    return jnp.maximum(z, 0.0) + jnp.log(1.0 + jnp.exp(-jnp.abs(z)))


def _block_diag_rows(x, left):
    return jnp.concatenate([jnp.where(left, x, 0.0), jnp.where(left, 0.0, x)], axis=1)


def _shift_time(carry, x, steps, bb):
    n = steps * bb
    return jnp.concatenate([carry[carry.shape[0] - n:], x[:x.shape[0] - n]], axis=0)


class _TimeMajorStream:
    def __init__(self, hbm, buf, sem, *, bb, chunk, n_blocks, n_chunks, to_hbm):
        self.hbm, self.buf, self.sem = hbm, buf, sem
        self.bb, self.chunk, self.to_hbm = bb, chunk, to_hbm
        self.n_blocks, self.n_chunks = n_blocks, n_chunks

    def _copy(self, g, slot, i):
        if self.n_chunks == 1:
            bi, ti = g, 0
        elif self.n_blocks == 1:
            bi, ti = 0, g
        else:
            bi, ti = lax.div(g, self.n_chunks), lax.rem(g, self.n_chunks)
        if self.bb < self.chunk:
            hbm_rows = self.hbm.at[bi * self.bb + i, pl.ds(ti * self.chunk, self.chunk), :]
            tile = self.buf.at[slot, :, i, :]
        else:
            seqs = pl.ds(pl.multiple_of(bi * self.bb, SUBLANES), self.bb)
            hbm_rows = self.hbm.at[seqs, ti * self.chunk + i, :]
            tile = self.buf.at[slot, i]
        src, dst = (tile, hbm_rows) if self.to_hbm else (hbm_rows, tile)
        return pltpu.make_async_copy(src, dst, self.sem.at[slot])

    def start(self, g, slot):
        for i in range(min(self.bb, self.chunk)):
            self._copy(g, slot, i).start(priority=i % 2)

    def wait(self, g, slot):
        for i in range(min(self.bb, self.chunk)):
            self._copy(g, slot, i).wait()


READ_SLOTS = 3


def _read_chunk_begin(stream, g, n_steps):
    last = n_steps - 1

    @pl.when(g == 0)
    def _():
        stream.start(0, 0)
        stream.start(jnp.minimum(1, last), 1)

    slot = lax.rem(g, READ_SLOTS)
    stream.wait(g, slot)
    return slot


def _read_chunk_end(stream, g, n_steps):
    last = n_steps - 1
    stream.start(jnp.minimum(g + 2, last), lax.rem(g + 2, READ_SLOTS))

    @pl.when(g == last)
    def _():
        stream.wait(last, lax.rem(g + 1, READ_SLOTS))
        stream.wait(last, lax.rem(g + 2, READ_SLOTS))


def _mixer_kernel(x_hbm, st_shift_ref, st_wkv_ref, st_conv_ref, st_h_ref,
                  g1_ref, w_in_ref, mu_ref, w0_ref, wdec_ref, a0_ref, waaa_ref, wgate_ref,
                  kk_ref, ka_ref, rk_ref, gng_ref, gnb_ref,
                  cw_ref, cb_ref, wgates_ref, ba_ref, bx_ref, lam_ref, og_ref,
                  seg_ref, segt_ref, w_out_ref,
                  x1_ref, o_shift_ref, o_wkv_ref, o_conv_ref, o_h_ref,
                  tm_carry, xb_carry, kap_s, rt_s, bt_s, kt_s, v_s, yt_s, pe_s, x_buf, x_sem,
                  *, bb, chunk, pos0, n_blocks, n_chunks):
    rows = bb * chunk
    conv_rows = (LRU_CONV_W - 1) * bb
    wkv_chunk = max(chunk, SUBLANES)
    ti = pl.program_id(1)
    step = pl.program_id(0) * n_chunks + ti
    n_steps = n_blocks * n_chunks
    x_stream = _TimeMajorStream(x_hbm, x_buf, x_sem, bb=bb, chunk=chunk, n_blocks=n_blocks,
                                n_chunks=n_chunks, to_hbm=False)
    x_slot = _read_chunk_begin(x_stream, step, n_steps)

    @pl.when(ti == 0)
    def _():
        tm_carry[...] = st_shift_ref[...]
        xb_carry[...] = st_conv_ref[...].reshape(conv_rows, D_LRU)
        o_wkv_ref[...] = st_wkv_ref[...]
        o_h_ref[...] = st_h_ref[...]
        if wkv_chunk > chunk:
            for ref in (kap_s, rt_s, bt_s, kt_s, v_s):
                ref[:, rows:wkv_chunk * bb, :] = jnp.zeros((PAIRS, wkv_chunk * bb - rows, LANES), F32)

    seg = seg_ref[...]
    segt = segt_ref[...]

    def head_sum(x):
        s = jnp.dot(x.astype(BF16), seg, preferred_element_type=F32)
        return jnp.dot(s.astype(BF16), segt, preferred_element_type=F32)

    x = x_buf[x_slot].reshape(rows, D_MODEL)
    u = _dot(_rms(x, g1_ref[...]), w_in_ref[...])
    u_tm = u[:, :D_TM_PROJ]
    xb = u[:, D_TM_PROJ:D_TM_PROJ + D_LRU]
    gate_lru = u[:, D_TM_PROJ + D_LRU:D_IN_PROJ]

    row = lax.broadcasted_iota(jnp.int32, (rows, D_TM), 0)

    um = u_tm + (_shift_time(tm_carry[...], u_tm, 1, bb) - u_tm) * mu_ref[...]
    r = um[:, 0:D_TM]
    k = um[:, D_TM:2 * D_TM]
    v = um[:, 2 * D_TM:3 * D_TM]
    x_lora = um[:, LORA_OFF:GATE_OFF]
    x_gate = um[:, GATE_OFF:D_TM_PROJ]
    log2_decay = (-math.exp(-0.5) * LOG2_E) * _sigmoid(w0_ref[...] + _dot(jnp.tanh(x_lora), wdec_ref[...]))
    a = _sigmoid(a0_ref[...] + _dot(x_lora, waaa_ref[...]))
    gate_tm = _dot(_sigmoid(x_gate), wgate_ref[...])
    kk = k * kk_ref[...]
    k = k * (a * ka_ref[...] + (1.0 - ka_ref[...]))
    kk = kk * lax.rsqrt(jnp.maximum(head_sum(kk * kk), 1e-24))
    bonus = head_sum(r * k * rk_ref[...]) * v
    acc = log2_decay[0:bb]
    cum = [acc]
    for t in range(1, chunk):
        acc = acc + log2_decay[t * bb:(t + 1) * bb]
        cum.append(acc)
    c = jnp.concatenate(cum, axis=0)
    inv_p = jnp.exp2(-c)
    kap = kk * jnp.exp2(c - log2_decay)
    rt = r * jnp.exp2(c)
    bt = kk * a * inv_p
    kt = k * inv_p
    p_end = jnp.exp2(cum[chunk - 1])
    for p in range(PAIRS):
        ls = slice(p * LANES, (p + 1) * LANES)
        kap_s[p, 0:rows, :] = kap[:, ls]
        rt_s[p, 0:rows, :] = rt[:, ls]
        bt_s[p, 0:rows, :] = bt[:, ls]
        kt_s[p, 0:rows, :] = kt[:, ls]
        v_s[p, 0:rows, :] = v[:, ls]
        pe_s[p] = p_end[:, ls]

    xb_prev = xb_carry[...]
    xc = cb_ref[...] + _shift_time(xb_prev, xb, 3, bb) * cw_ref[0:1, :]
    xc = xc + _shift_time(xb_prev, xb, 2, bb) * cw_ref[1:2, :]
    xc = xc + _shift_time(xb_prev, xb, 1, bb) * cw_ref[2:3, :]
    xc = xc + xb * cw_ref[3:4, :]
    xc_bf = xc.astype(BF16)

    gates = [jnp.dot(xc_bf[:, p * LANES:(p + 1) * LANES], wgates_ref[p], preferred_element_type=F32)
             for p in range(D_LRU // LANES)]
    r_g = _sigmoid(jnp.concatenate([g[:, :LANES] for g in gates], axis=-1) + ba_ref[...])
    i_g = _sigmoid(jnp.concatenate([g[:, LANES:] for g in gates], axis=-1) + bx_ref[...])
    la = jnp.exp2(r_g * ((-LRU_C * LOG2_E) * _softplus(-lam_ref[...])))
    gap = 1.0 - la * la
    mult = jnp.where(gap > 0.0, gap * lax.rsqrt(gap), 0.0)
    if pos0 == 0:
        mult = jnp.where(jnp.logical_and(row < bb, ti == 0), 1.0, mult)
    lb = xc * i_g * mult
    h = o_h_ref[...]
    hs = []
    for t in range(chunk):
        ts = slice(t * bb, (t + 1) * bb)
        h = la[ts] * h + lb[ts]
        hs.append(h)
    o_h_ref[...] = h
    y_lru = _rms(jnp.concatenate(hs, axis=0) * jax.nn.gelu(gate_lru), og_ref[...])

    wc = wkv_chunk
    n_lv = max(1, (chunk - 1).bit_length())
    cw2 = 2 * wc
    left = lax.broadcasted_iota(jnp.int32, (1, 1, LANES), 2) < TM_HEAD
    left_c = lax.broadcasted_iota(jnp.int32, (1, 1, cw2), 2) < wc
    ri = lax.broadcasted_iota(jnp.int32, (wc, cw2), 0)
    ci = lax.broadcasted_iota(jnp.int32, (wc, cw2), 1) & (wc - 1)
    strict = ri > ci
    incl = ri >= ci
    eye = (ri == ci).astype(F32)

    tsel = [pl.ds(b, wc, stride=bb) for b in range(bb)]
    gather = lambda ref: jnp.stack([ref.at[p][ts, :] for ts in tsel for p in range(PAIRS)])
    kap_g = gather(kap_s)
    rt_g = gather(rt_s)
    bt_g = gather(bt_s)
    kt_g = gather(kt_s)
    vv = gather(v_s)
    pe = jnp.stack([pe_s.at[p][pl.ds(b, 1), :] for b in range(bb) for p in range(PAIRS)])
    s0 = o_wkv_ref[...].reshape(bb * PAIRS, TM_HEAD, LANES)
    lhs = jnp.concatenate([kap_g, rt_g], axis=1)
    fused = _bdot_nt(lhs, jnp.concatenate([_block_diag_rows(bt_g, left), _block_diag_rows(kt_g, left),
                                           _block_diag_rows(s0, left)], axis=1))
    g_b = fused[:, :, 0:cw2]
    g_k = fused[:, :, cw2:2 * cw2]
    z = fused[:, :, 2 * cw2:]
    m_ab = jnp.where(strict, g_b[:, :wc], 0.0)
    m_ak = jnp.where(strict, g_k[:, :wc], 0.0)
    m_rb = jnp.where(incl, g_b[:, wc:], 0.0)
    m_rk = jnp.where(incl, g_k[:, wc:], 0.0)
    t_inv = eye - m_ab
    m_pow = _bdot(m_ab, _block_diag_rows(m_ab, left_c))
    for lv in range(1, n_lv):
        if lv < n_lv - 1:
            prod = _bdot(jnp.concatenate([t_inv, m_pow], axis=1), _block_diag_rows(m_pow, left_c))
            t_inv = t_inv + prod[:, :wc]
            m_pow = prod[:, wc:]
        else:
            t_inv = t_inv + _bdot(t_inv, _block_diag_rows(m_pow, left_c))
    xv = _bdot(jnp.concatenate([m_ak, m_rk], axis=1), _block_diag_rows(vv, left))
    uu = -_bdot(t_inv, _block_diag_rows(z[:, :wc] + xv[:, :wc], left))
    yh = z[:, wc:] + _bdot(m_rb, _block_diag_rows(uu, left)) + xv[:, wc:]
    full = _bdot_tn(jnp.concatenate([uu, vv], axis=1), jnp.concatenate([bt_g, kt_g], axis=1))
    upd = jnp.where(left, full[:, :TM_HEAD], full[:, TM_HEAD:])
    o_wkv_ref[...] = ((s0 + upd) * pe).reshape(bb, PAIRS, TM_HEAD, LANES)
    for b, ts in enumerate(tsel):
        for p in range(PAIRS):
            yt_s.at[p][ts, :] = yh[b * PAIRS + p]

    yv = jnp.concatenate([yt_s[p, 0:rows, :] for p in range(PAIRS)], axis=-1)
    mean = head_sum(yv) * (1.0 / TM_HEAD)
    cen = yv - mean
    var = head_sum(cen * cen) * (1.0 / TM_HEAD)
    y_tm = ((cen * lax.rsqrt(var + GN_EPS)) * gng_ref[...] + gnb_ref[...] + bonus) * gate_tm
    mixed = jnp.concatenate([y_tm.astype(BF16), y_lru.astype(BF16)], axis=-1)
    x1 = x_buf[x_slot].reshape(rows, D_MODEL) + jnp.dot(mixed, w_out_ref[...], preferred_element_type=F32)
    x1_ref[...] = x1.reshape(chunk, bb, D_MODEL)

    o_shift_ref[...] = u_tm[rows - bb:]
    o_conv_ref[...] = xb[rows - conv_rows:].reshape(LRU_CONV_W - 1, bb, D_LRU)
    tm_carry[...] = u_tm[rows - bb:]
    xb_carry[...] = xb[rows - conv_rows:]
    _read_chunk_end(x_stream, step, n_steps)


def _const_spec(shape):
    zeros = (0,) * len(shape)
    return pl.BlockSpec(shape, lambda bi, ti: zeros, pipeline_mode=pl.Buffered(1))


def _mixer(x, st_shift, st_wkv, st_conv, st_h, weights, *, bb, chunk, pos0):
    batch, length, _ = x.shape
    assert batch % bb == 0 and length % chunk == 0 and bb % SUBLANES == 0
    assert chunk & (chunk - 1) == 0 and chunk >= LRU_CONV_W - 1
    rows = bb * chunk
    grid = (batch // bb, length // chunk)
    act = pl.BlockSpec((chunk, bb, D_MODEL), lambda bi, ti: (ti, bi, 0))
    vec = lambda w: pl.BlockSpec((bb, w), lambda bi, ti: (bi, 0))
    wkv_spec = pl.BlockSpec((bb, PAIRS, TM_HEAD, LANES), lambda bi, ti: (bi, 0, 0, 0))
    conv_spec = pl.BlockSpec((LRU_CONV_W - 1, bb, D_LRU), lambda bi, ti: (0, bi, 0))
    in_specs = [pl.BlockSpec(memory_space=pl.ANY), vec(D_TM_PROJ), wkv_spec, conv_spec, vec(D_LRU)]
    in_specs += [_const_spec(w.shape) for w in weights]
    out_specs = [act, vec(D_TM_PROJ), wkv_spec, conv_spec, vec(D_LRU)]
    out_shape = [jax.ShapeDtypeStruct((length, batch, D_MODEL), F32),
                 jax.ShapeDtypeStruct((batch, D_TM_PROJ), F32),
                 jax.ShapeDtypeStruct((batch, PAIRS, TM_HEAD, LANES), F32),
                 jax.ShapeDtypeStruct((LRU_CONV_W - 1, batch, D_LRU), F32),
                 jax.ShapeDtypeStruct((batch, D_LRU), F32)]
    pair_rows = lambda n: pltpu.VMEM((PAIRS, n, LANES), F32)
    scratch = [pltpu.VMEM((bb, D_TM_PROJ), F32),
               pltpu.VMEM(((LRU_CONV_W - 1) * bb, D_LRU), F32)]
    scratch += [pair_rows(max(chunk, SUBLANES) * bb)] * 6 + [pair_rows(bb)]
    scratch += [pltpu.VMEM((READ_SLOTS, chunk, bb, D_MODEL), F32), pltpu.SemaphoreType.DMA((READ_SLOTS,))]
    return pl.pallas_call(
        functools.partial(_mixer_kernel, bb=bb, chunk=chunk, pos0=pos0,
                          n_blocks=grid[0], n_chunks=grid[1]),
        out_shape=out_shape, grid=grid, in_specs=in_specs, out_specs=out_specs,
        scratch_shapes=scratch, name="mixer",
        compiler_params=pltpu.CompilerParams(dimension_semantics=("arbitrary", "arbitrary"),
                                             vmem_limit_bytes=VMEM_LIMIT_BYTES),
    )(x, st_shift, st_wkv, st_conv, st_h, *weights)


def _ffn_kernel(x1_ref, st_conv_ref,
                g2_ref, w_up_ref, w_gate_ref, cw_ref, cb_ref, w_down_ref, gf_ref,
                y_hbm, o_conv_ref,
                up_carry, y_buf, y_sem,
                *, bb, chunk, n_blocks, n_chunks):
    rows = bb * chunk
    conv_rows = (FFN_CONV_W - 1) * bb
    ti = pl.program_id(1)
    step = pl.program_id(0) * n_chunks + ti
    n_steps = n_blocks * n_chunks
    y_stream = _TimeMajorStream(y_hbm, y_buf, y_sem, bb=bb, chunk=chunk, n_blocks=n_blocks,
                                n_chunks=n_chunks, to_hbm=True)
    slot = lax.rem(step, 2)

    @pl.when(ti == 0)
    def _():
        up_carry[...] = st_conv_ref[...].reshape(conv_rows, D_FF)

    x1 = x1_ref[...].reshape(rows, D_MODEL)
    xn = _rms(x1, g2_ref[...]).astype(BF16)
    acc = x1
    for n in range(D_FF // FF_COL_TILE):
        cols = slice(n * FF_COL_TILE, (n + 1) * FF_COL_TILE)
        up = jnp.dot(xn, w_up_ref[:, cols], preferred_element_type=F32)
        gate = jnp.dot(xn, w_gate_ref[:, cols], preferred_element_type=F32)
        up_prev = up_carry[:, cols]
        upc = cb_ref[:, cols] + _shift_time(up_prev, up, 2, bb) * cw_ref[0:1, cols]
        upc = upc + _shift_time(up_prev, up, 1, bb) * cw_ref[1:2, cols]
        upc = upc + up * cw_ref[2:3, cols]
        up_carry[:, cols] = up[rows - conv_rows:]
        o_conv_ref[:, :, cols] = up[rows - conv_rows:].reshape(FFN_CONV_W - 1, bb, FF_COL_TILE)
        hid = (jax.nn.gelu(upc) * gate).astype(BF16)
        acc = acc + jnp.dot(hid, w_down_ref[cols, :], preferred_element_type=F32)

    y_buf[slot] = _rms(acc, gf_ref[...]).reshape(chunk, bb, D_MODEL)
    y_stream.start(step, slot)

    @pl.when(step > 0)
    def _():
        y_stream.wait(step - 1, 1 - slot)

    @pl.when(step == n_steps - 1)
    def _():
        y_stream.wait(step, slot)


def _ffn(x1, st_conv, weights, *, bb, chunk):
    length, batch, _ = x1.shape
    assert batch % bb == 0 and length % chunk == 0 and bb % SUBLANES == 0
    assert chunk >= FFN_CONV_W - 1
    rows = bb * chunk
    grid = (batch // bb, length // chunk)
    act = pl.BlockSpec((chunk, bb, D_MODEL), lambda bi, ti: (ti, bi, 0))
    conv_spec = pl.BlockSpec((FFN_CONV_W - 1, bb, D_FF), lambda bi, ti: (0, bi, 0))
    in_specs = [act, conv_spec] + [_const_spec(w.shape) for w in weights]
    out_shape = [jax.ShapeDtypeStruct((batch, length, D_MODEL), F32),
                 jax.ShapeDtypeStruct((FFN_CONV_W - 1, batch, D_FF), F32)]
    scratch = [pltpu.VMEM(((FFN_CONV_W - 1) * bb, D_FF), F32),
               pltpu.VMEM((2, chunk, bb, D_MODEL), F32), pltpu.SemaphoreType.DMA((2,))]
    return pl.pallas_call(
        functools.partial(_ffn_kernel, bb=bb, chunk=chunk, n_blocks=grid[0], n_chunks=grid[1]),
        out_shape=out_shape, grid=grid, in_specs=in_specs,
        out_specs=[pl.BlockSpec(memory_space=pl.ANY), conv_spec],
        scratch_shapes=scratch, name="ffn",
        compiler_params=pltpu.CompilerParams(dimension_semantics=("arbitrary", "arbitrary"),
                                             vmem_limit_bytes=VMEM_LIMIT_BYTES),
    )(x1, st_conv, *weights)


def _ffn_state_kernel(x1_ref, g2_ref, w_up_ref, o_conv_ref, *, bb):
    tail = FFN_CONV_W - 1
    x1 = x1_ref[...].reshape(tail * bb, D_MODEL)
    up = _dot(_rms(x1, g2_ref[...]), w_up_ref[...])
    o_conv_ref[...] = up.reshape(tail, bb, D_FF)


def _ffn_state(x1, g2, w_up):
    length, batch, _ = x1.shape
    tail = FFN_CONV_W - 1
    assert length % tail == 0 and batch % SUBLANES == 0
    return pl.pallas_call(
        functools.partial(_ffn_state_kernel, bb=batch),
        out_shape=jax.ShapeDtypeStruct((tail, batch, D_FF), F32), grid=(1,),
        in_specs=[pl.BlockSpec((tail, batch, D_MODEL), lambda i: (length // tail - 1, 0, 0)),
                  pl.BlockSpec(g2.shape, lambda i: (0, 0)), pl.BlockSpec(w_up.shape, lambda i: (0, 0))],
        out_specs=pl.BlockSpec((tail, batch, D_FF), lambda i: (0, 0, 0)), name="ffn_state",
        compiler_params=pltpu.CompilerParams(dimension_semantics=("arbitrary",),
                                             vmem_limit_bytes=VMEM_LIMIT_BYTES),
    )(x1, g2, w_up)


RELAYOUT_ROWS = 32
RELAYOUT_AHEAD = 3
RELAYOUT_CHUNKS = tuple((p, v0) for p in range(PAIRS) for v0 in range(0, TM_HEAD, RELAYOUT_ROWS))


def _lanes_copy(hbm, buf, sem, c, *, to_hbm):
    p, v0 = RELAYOUT_CHUNKS[c]
    region = hbm.at[pl.ds(2 * p, 2), pl.ds(v0, RELAYOUT_ROWS)]
    src, dst = (buf.at[c], region) if to_hbm else (region, buf.at[c])
    return pltpu.make_async_copy(src, dst, sem.at[c])


def _pair_row_copy(hbm, buf, sem, c, i, *, to_hbm):
    p, v0 = RELAYOUT_CHUNKS[c]
    rows = hbm.at[:, p, v0 + i, :]
    src, dst = (buf.at[c, i], rows) if to_hbm else (rows, buf.at[c, i])
    return pltpu.make_async_copy(src, dst, sem.at[c])


def _pairs_from_lanes_kernel(a_hbm, o_hbm, lanes_buf, pairs_buf, lanes_sem, pairs_sem):
    batch = lanes_buf.shape[-1]
    n = len(RELAYOUT_CHUNKS)
    for c in range(n):
        _lanes_copy(a_hbm, lanes_buf, lanes_sem, c, to_hbm=False).start(priority=c % 2)
    for c in range(n):
        _lanes_copy(a_hbm, lanes_buf, lanes_sem, c, to_hbm=False).wait()
        for i in range(RELAYOUT_ROWS):
            pairs_buf[c, i] = lanes_buf[c, :, i].reshape(LANES, batch).T
        for i in range(RELAYOUT_ROWS):
            _pair_row_copy(o_hbm, pairs_buf, pairs_sem, c, i, to_hbm=True).start(priority=i % 2)
    for c in range(n):
        for i in range(RELAYOUT_ROWS):
            _pair_row_copy(o_hbm, pairs_buf, pairs_sem, c, i, to_hbm=True).wait()


def _lanes_from_pairs_kernel(s_hbm, o_hbm, lanes_buf, pairs_buf, lanes_sem, pairs_sem):
    batch = lanes_buf.shape[-1]
    n = len(RELAYOUT_CHUNKS)

    def gather(c):
        for i in range(RELAYOUT_ROWS):
            _pair_row_copy(s_hbm, pairs_buf, pairs_sem, c, i, to_hbm=False).start(priority=i % 2)

    for c in range(min(RELAYOUT_AHEAD, n)):
        gather(c)
    for c in range(n):
        if c + RELAYOUT_AHEAD < n:
            gather(c + RELAYOUT_AHEAD)
        for i in range(RELAYOUT_ROWS):
            _pair_row_copy(s_hbm, pairs_buf, pairs_sem, c, i, to_hbm=False).wait()
        for i in range(RELAYOUT_ROWS):
            lanes_buf[c, :, i] = pairs_buf[c, i].T.reshape(2, TM_HEAD, batch)
        _lanes_copy(o_hbm, lanes_buf, lanes_sem, c, to_hbm=True).start(priority=c % 2)
    for c in range(n):
        _lanes_copy(o_hbm, lanes_buf, lanes_sem, c, to_hbm=True).wait()


def _wkv_relayout(x, *, to_pairs):
    batch = x.shape[-1] if to_pairs else x.shape[0]
    assert batch == LANES
    n = len(RELAYOUT_CHUNKS)
    lanes_shape = (TM_HEADS, TM_HEAD, TM_HEAD, batch)
    pairs_shape = (batch, PAIRS, TM_HEAD, LANES)
    state_bytes = TM_HEADS * TM_HEAD * TM_HEAD * batch * 4
    hbm_spec = pl.BlockSpec(memory_space=pl.ANY)
    return pl.pallas_call(
        _pairs_from_lanes_kernel if to_pairs else _lanes_from_pairs_kernel,
        out_shape=jax.ShapeDtypeStruct(pairs_shape if to_pairs else lanes_shape, F32),
        in_specs=[hbm_spec], out_specs=hbm_spec,
        scratch_shapes=[pltpu.VMEM((n, 2, RELAYOUT_ROWS, TM_HEAD, batch), F32),
                        pltpu.VMEM((n, RELAYOUT_ROWS, batch, LANES), F32),
                        pltpu.SemaphoreType.DMA((n,)), pltpu.SemaphoreType.DMA((n,))],
        name="wkv_relayout",
        compiler_params=pltpu.CompilerParams(vmem_limit_bytes=2 * state_bytes + 8 * 1024 * 1024),
    )(x)


def _row(v):
    return v.reshape(1, -1).astype(F32)


def _pair_block_diag(w):
    nb, n, _ = w.shape
    w = w.reshape(nb // 2, 2, n, n)
    eye = jnp.eye(2, dtype=w.dtype)
    return (eye[None, :, None, :, None] * w[:, :, :, None, :]).reshape(nb // 2, 2 * n, 2 * n)


def _wkv_to_pairs(s):
    b = s.shape[0]
    if b == LANES:
        return _wkv_relayout(jnp.transpose(s.astype(F32), (1, 2, 3, 0)), to_pairs=True)
    s = s.astype(F32).reshape(b, PAIRS, 2, TM_HEAD, TM_HEAD)
    return jnp.transpose(s, (0, 1, 3, 2, 4)).reshape(b, PAIRS, TM_HEAD, LANES)


def _wkv_from_pairs(s):
    b = s.shape[0]
    if b == LANES:
        return jnp.transpose(_wkv_relayout(s, to_pairs=False), (3, 0, 1, 2))
    s = s.reshape(b, PAIRS, TM_HEAD, 2, TM_HEAD)
    return jnp.transpose(s, (0, 1, 3, 2, 4)).reshape(b, TM_HEADS, TM_HEAD, TM_HEAD)


def _layer(x, states, mixer_w, ffn_w, *, mixer_bb, ffn_bb, mixer_chunk, ffn_chunk, pos0):
    st_shift, st_wkv, st_conv, st_h, st_fconv = states
    x1, o_shift, o_wkv, o_conv, o_h = _mixer(x, st_shift, st_wkv, st_conv, st_h, mixer_w,
                                             bb=mixer_bb, chunk=mixer_chunk, pos0=pos0)
    y, o_fconv = _ffn(x1, st_fconv, ffn_w, bb=ffn_bb, chunk=ffn_chunk)
    return y, (o_shift, o_wkv, o_conv, o_h, o_fconv)


def kernel(x_prompt, x_sample, state_tm_shift, state_tm_wkv, state_lru_conv, state_lru_h, state_ffn_conv, meta_tokens, norm1_g, w_in, tm_mu, tm_w0, tm_w_up, tm_a0, tm_a_up, tm_g_up, tm_k_k, tm_k_a, tm_r_k, tm_gn_g, tm_gn_b, lru_conv_w, lru_conv_b, lru_wa, lru_ba, lru_wx, lru_bx, lru_lambda, lru_out_g, w_out, norm2_g, ffn_w_up, ffn_w_gate, ffn_conv_w, ffn_conv_b, ffn_w_down, norm_f_g):
    depth = w_in.shape[0]
    assert depth == 1
    l = 0
    zeros_lora = jnp.zeros((DECAY_RANK, D_TM), F32)
    head_id = jnp.arange(D_TM) // TM_HEAD
    seg = (head_id[:, None] == jnp.arange(LANES)[None, :]).astype(BF16)
    mixer_w = (
        _row(norm1_g[l]), w_in[l].astype(BF16), _row(tm_mu[l]), _row(tm_w0[l]),
        jnp.concatenate([tm_w_up[l], zeros_lora], axis=0).astype(BF16),
        _row(tm_a0[l]),
        jnp.concatenate([zeros_lora, tm_a_up[l]], axis=0).astype(BF16),
        tm_g_up[l].astype(BF16),
        _row(tm_k_k[l]), _row(tm_k_a[l]), _row(tm_r_k[l]), _row(tm_gn_g[l]), _row(tm_gn_b[l]),
        lru_conv_w[l].astype(F32), _row(lru_conv_b[l]),
        jnp.concatenate([_pair_block_diag(lru_wa[l]), _pair_block_diag(lru_wx[l])], axis=-1).astype(BF16),
        _row(lru_ba[l]), _row(lru_bx[l]),
        _row(lru_lambda[l]), _row(lru_out_g[l]),
        seg, seg.T, w_out[l].astype(BF16),
    )
    ffn_w = (
        _row(norm2_g[l]), ffn_w_up[l].astype(BF16), ffn_w_gate[l].astype(BF16),
        ffn_conv_w[l].astype(F32), _row(ffn_conv_b[l]), ffn_w_down[l].astype(BF16), _row(norm_f_g),
    )

    bsz, seq = x_prompt.shape[0], x_prompt.shape[1]
    x_meta = jnp.broadcast_to(meta_tokens[None].astype(F32), (bsz, N_META, D_MODEL))
    x1_meta, *meta_st = _mixer(x_meta, jnp.zeros((bsz, D_TM_PROJ), F32),
                               jnp.zeros((bsz, PAIRS, TM_HEAD, LANES), F32),
                               jnp.zeros((LRU_CONV_W - 1, bsz, D_LRU), F32),
                               jnp.zeros((bsz, D_LRU), F32), mixer_w,
                               bb=bsz, chunk=N_META, pos0=0)
    p_init = tuple(meta_st) + (_ffn_state(x1_meta, ffn_w[0], ffn_w[1]),)

    y_prompt, p_st = _layer(x_prompt, p_init, mixer_w, ffn_w,
                            mixer_bb=bsz, ffn_bb=bsz, mixer_chunk=64, ffn_chunk=64,
                            pos0=N_META)

    dec_b, dec_seq = x_sample.shape[0], x_sample.shape[1]
    s_init = (state_tm_shift[l].astype(F32), _wkv_to_pairs(state_tm_wkv[l]),
              jnp.transpose(state_lru_conv[l].astype(F32), (1, 0, 2)), state_lru_h[l].astype(F32),
              jnp.transpose(state_ffn_conv[l].astype(F32), (1, 0, 2)))
    y_sample, s_st = _layer(x_sample, s_init, mixer_w, ffn_w,
                            mixer_bb=32, ffn_bb=64, mixer_chunk=dec_seq, ffn_chunk=dec_seq,
                            pos0=PAST_LEN)

    def unpack(st):
        o_shift, o_wkv, o_conv, o_h, o_fconv = st
        return (o_shift[None], _wkv_from_pairs(o_wkv)[None], jnp.transpose(o_conv, (1, 0, 2))[None],
                o_h[None], jnp.transpose(o_fconv, (1, 0, 2))[None])

    return (y_prompt, y_sample) + unpack(p_st) + unpack(s_st)
```
